```python
import jax, jax.numpy as jnp
from jax import lax
import numpy as np

D_MODEL = 1024
BATCH = 8
SEQ = 16384
DEPTH = 2

N_META = 16
BLOCK = 128
META_PAD = BLOCK - N_META
CONV_CH = 256
CONV_K = 31
POOL_CH = 256
POOL_WINDOWS = (2, 4, 8, 16)
POOL_GROUPS = 4
POOL_GC = POOL_CH // POOL_GROUPS
POOL_OUT = D_MODEL // POOL_GROUPS
ATT_HEADS = 4
HEAD_DIM = 128
ATT_W = ATT_HEADS * HEAD_DIM
N_BRANCH = 3
D_IN = 2 * CONV_CH + POOL_CH + 3 * ATT_W + N_BRANCH * D_MODEL
D_FF = 3 * D_MODEL
FFN_K = 3
EPS = 1e-6

kernel_name = "hybrid_conv_pool_stickbreak_gated"


def rms_norm(x, g):
    x32 = x.astype(jnp.float32)
    y = x32 * lax.rsqrt(jnp.mean(x32 * x32, axis=-1, keepdims=True) + EPS)
    return y.astype(x.dtype) * g


def layer_norm(x, g, b):
    x32 = x.astype(jnp.float32)
    mu = jnp.mean(x32, axis=-1, keepdims=True)
    xc = x32 - mu
    y = xc * lax.rsqrt(jnp.mean(xc * xc, axis=-1, keepdims=True) + EPS)
    return y.astype(x.dtype) * g + b


def causal_dwconv(x, w, b):
    k, c = w.shape
    y = lax.conv_general_dilated(
        x, w[:, None, :].astype(x.dtype), window_strides=(1,), padding=[(k - 1, 0)],
        dimension_numbers=("NWC", "WIO", "NWC"), feature_group_count=c)
    return y + b


def conv_module(u, dw_w, dw_b, ln_g, ln_b, w_out, b_out):
    a, gate = jnp.split(u, 2, axis=-1)
    h = a * jax.nn.sigmoid(gate)
    h = causal_dwconv(h, dw_w, dw_b)
    h = layer_norm(h, ln_g, ln_b)
    h = jax.nn.silu(h)
    return h @ w_out + b_out


def pool_mixer(p, w_grp, scale):
    b, l, _ = p.shape
    p32 = p.astype(jnp.float32)
    csum = jnp.cumsum(p32, axis=1)
    pos = jnp.arange(l, dtype=jnp.float32)[:, None]
    groups = []
    for g, w in enumerate(POOL_WINDOWS):
        sl = slice(g * POOL_GC, (g + 1) * POOL_GC)
        cg = csum[..., sl]
        lag = jnp.pad(cg, ((0, 0), (w, 0), (0, 0)))[:, :l]
        mean = (cg - lag) / jnp.minimum(pos + 1.0, float(w))
        groups.append(mean - p32[..., sl])
    pooled = jnp.stack(groups, axis=2).astype(p.dtype)
    y = jnp.einsum("blgc,gcd->blgd", pooled, w_grp).reshape(b, l, D_MODEL)
    return y * scale


def stick_breaking_attention(q, k, v):
    b, l, h, dh = q.shape
    pad = ((0, 0), (META_PAD, 0), (0, 0), (0, 0))
    qp = jnp.pad(q.astype(jnp.float32) * (HEAD_DIM ** -0.5), pad).transpose(0, 2, 1, 3)
    kp = jnp.pad(k.astype(jnp.float32), pad).transpose(0, 2, 1, 3)
    vp = jnp.pad(v.astype(jnp.float32), pad).transpose(0, 2, 1, 3)
    lp = l + META_PAD
    n_blocks = lp // BLOCK
    tri = jnp.asarray(np.tril(np.ones((BLOCK, BLOCK), np.float32), -1))
    outs = []
    for i in range(n_blocks):
        nk = i + 1
        kl = nk * BLOCK
        qb = qp[:, :, i * BLOCK:(i + 1) * BLOCK]
        z = jnp.einsum("bhqd,bhkd->bhqk", qb, kp[:, :, :kl])
        qpos = i * BLOCK + jnp.arange(BLOCK)
        kpos = jnp.arange(kl)
        valid = (kpos[None, :] < qpos[:, None]) & (kpos[None, :] >= META_PAD)
        log_keep = jnp.where(valid, jax.nn.log_sigmoid(-z), 0.0)
        lk = log_keep.reshape(b, h, BLOCK, nk, BLOCK)
        within = jnp.einsum("bhqnj,js->bhqns", lk, tri)
        blk_sum = jnp.sum(lk, axis=-1)
        blk_tri = jnp.asarray(np.tril(np.ones((nk, nk), np.float32), -1))
        later = jnp.einsum("bhqm,mn->bhqn", blk_sum, blk_tri)
        log_keep_after = (within + later[..., None]).reshape(b, h, BLOCK, kl)
        a = jnp.where(valid, jnp.exp(z + log_keep + log_keep_after), 0.0)
        outs.append(jnp.einsum("bhqk,bhkd->bhqd", a, vp[:, :, :kl]))
    out = jnp.concatenate(outs, axis=2)
    out = out.transpose(0, 2, 1, 3).reshape(b, lp, h * dh)[:, META_PAD:]
    return out.astype(q.dtype)


def gated_conv_mlp(x, w_up, dw_w, dw_b, w_down):
    u = x @ w_up
    u = causal_dwconv(u, dw_w, dw_b)
    gate, val = jnp.split(u, 2, axis=-1)
    return (jax.nn.gelu(gate) * val) @ w_down


def _w(key, shape, fan_in):
    return jax.random.normal(key, shape, jnp.float32) * (fan_in ** -0.5)


def _fwd_setup_inputs(seed: int = 0) -> dict:
    key = jax.random.key(seed)
    ks = jax.random.split(key, 24)
    L = DEPTH
    return {
        "x": jax.random.normal(ks[0], (BATCH, SEQ, D_MODEL), jnp.float32),
        "meta": jax.random.normal(ks[1], (N_META, D_MODEL), jnp.float32),
        "norm1": 1.0 + 0.02 * jax.random.normal(ks[2], (L, D_MODEL), jnp.float32),
        "w_in": _w(ks[3], (L, D_MODEL, D_IN), D_MODEL),
        "conv_dw_w": _w(ks[4], (L, CONV_K, CONV_CH), CONV_K),
        "conv_dw_b": 0.02 * jax.random.normal(ks[5], (L, CONV_CH), jnp.float32),
        "conv_ln_g": 1.0 + 0.02 * jax.random.normal(ks[6], (L, CONV_CH), jnp.float32),
        "conv_ln_b": 0.02 * jax.random.normal(ks[7], (L, CONV_CH), jnp.float32),
        "w_conv_out": _w(ks[8], (L, CONV_CH, D_MODEL), CONV_CH),
        "b_conv_out": 0.02 * jax.random.normal(ks[9], (L, D_MODEL), jnp.float32),
        "w_pool_grp": _w(ks[10], (L, POOL_GROUPS, POOL_GC, POOL_OUT), POOL_GC),
        "pool_scale": 1.0 + 0.02 * jax.random.normal(ks[11], (L, D_MODEL), jnp.float32),
        "w_attn_out": _w(ks[12], (L, ATT_W, D_MODEL), ATT_W),
        "w_o": _w(ks[13], (L, D_MODEL, D_MODEL), D_MODEL),
        "norm2": 1.0 + 0.02 * jax.random.normal(ks[14], (L, D_MODEL), jnp.float32),
        "w_up": _w(ks[15], (L, D_MODEL, 2 * D_FF), D_MODEL),
        "ffn_dw_w": _w(ks[16], (L, FFN_K, 2 * D_FF), FFN_K),
        "ffn_dw_b": 0.02 * jax.random.normal(ks[17], (L, 2 * D_FF), jnp.float32),
        "w_down": _w(ks[18], (L, D_FF, D_MODEL), D_FF),
        "final_norm": 1.0 + 0.02 * jax.random.normal(ks[19], (D_MODEL,), jnp.float32),
    }


def _fwd_reference(x, meta, norm1, w_in, conv_dw_w, conv_dw_b, conv_ln_g, conv_ln_b, w_conv_out, b_conv_out,
              w_pool_grp, pool_scale, w_attn_out, w_o, norm2, w_up, ffn_dw_w, ffn_dw_b, w_down, final_norm):
    b = x.shape[0]
    h = jnp.concatenate([jnp.broadcast_to(meta[None].astype(x.dtype), (b, N_META, D_MODEL)), x], axis=1)
    l = h.shape[1]
    splits = [2 * CONV_CH, 2 * CONV_CH + POOL_CH, 2 * CONV_CH + POOL_CH + ATT_W,
              2 * CONV_CH + POOL_CH + 2 * ATT_W, 2 * CONV_CH + POOL_CH + 3 * ATT_W]
    for i in range(DEPTH):
        hn = rms_norm(h, norm1[i])
        proj = hn @ w_in[i]
        u_conv, u_pool, q, k, v, gates = jnp.split(proj, splits, axis=-1)
        y_a = conv_module(u_conv, conv_dw_w[i], conv_dw_b[i], conv_ln_g[i], conv_ln_b[i],
                          w_conv_out[i], b_conv_out[i])
        y_b = pool_mixer(u_pool, w_pool_grp[i], pool_scale[i])
        att = stick_breaking_attention(q.reshape(b, l, ATT_HEADS, HEAD_DIM),
                                       k.reshape(b, l, ATT_HEADS, HEAD_DIM),
                                       v.reshape(b, l, ATT_HEADS, HEAD_DIM))
        y_c = att @ w_attn_out[i]
        g = jax.nn.sigmoid(gates).reshape(b, l, N_BRANCH, D_MODEL)
        mixed = g[:, :, 0] * y_a + g[:, :, 1] * y_b + g[:, :, 2] * y_c
        h = h + mixed @ w_o[i]
        h = h + gated_conv_mlp(rms_norm(h, norm2[i]), w_up[i], ffn_dw_w[i], ffn_dw_b[i], w_down[i])
    return rms_norm(h, final_norm)[:, N_META:]


import jax as _jax
import jax.numpy as _jnp

TWIN_FORMAT = 'train_step'
FWD_PARAMS = ['x', 'meta', 'norm1', 'w_in', 'conv_dw_w', 'conv_dw_b', 'conv_ln_g', 'conv_ln_b', 'w_conv_out', 'b_conv_out', 'w_pool_grp', 'pool_scale', 'w_attn_out', 'w_o', 'norm2', 'w_up', 'ffn_dw_w', 'ffn_dw_b', 'w_down', 'final_norm']
TWIN_WEIGHTS = ['meta', 'norm1', 'w_in', 'conv_dw_w', 'conv_dw_b', 'conv_ln_g', 'conv_ln_b', 'w_conv_out', 'b_conv_out', 'w_pool_grp', 'pool_scale', 'w_attn_out', 'w_o', 'norm2', 'w_up', 'ffn_dw_w', 'ffn_dw_b', 'w_down', 'final_norm']
TWIN_DIFF_INPUT = 'x'
TWIN_INPUTS = ['x', 'meta', 'norm1', 'w_in', 'conv_dw_w', 'conv_dw_b', 'conv_ln_g', 'conv_ln_b', 'w_conv_out', 'b_conv_out', 'w_pool_grp', 'pool_scale', 'w_attn_out', 'w_o', 'norm2', 'w_up', 'ffn_dw_w', 'ffn_dw_b', 'w_down', 'final_norm', 'loss_target', 'm_meta', 'm_norm1', 'm_w_in', 'm_conv_dw_w', 'm_conv_dw_b', 'm_conv_ln_g', 'm_conv_ln_b', 'm_w_conv_out', 'm_b_conv_out', 'm_w_pool_grp', 'm_pool_scale', 'm_w_attn_out', 'm_w_o', 'm_norm2', 'm_w_up', 'm_ffn_dw_w', 'm_ffn_dw_b', 'm_w_down', 'm_final_norm', 'v_meta', 'v_norm1', 'v_w_in', 'v_conv_dw_w', 'v_conv_dw_b', 'v_conv_ln_g', 'v_conv_ln_b', 'v_w_conv_out', 'v_b_conv_out', 'v_w_pool_grp', 'v_pool_scale', 'v_w_attn_out', 'v_w_o', 'v_norm2', 'v_w_up', 'v_ffn_dw_w', 'v_ffn_dw_b', 'v_w_down', 'v_final_norm']
TWIN_OUTPUTS = ['loss', 'grad_x', 'grad_meta', 'grad_norm1', 'grad_w_in', 'grad_conv_dw_w', 'grad_conv_dw_b', 'grad_conv_ln_g', 'grad_conv_ln_b', 'grad_w_conv_out', 'grad_b_conv_out', 'grad_w_pool_grp', 'grad_pool_scale', 'grad_w_attn_out', 'grad_w_o', 'grad_norm2', 'grad_w_up', 'grad_ffn_dw_w', 'grad_ffn_dw_b', 'grad_w_down', 'grad_final_norm', 'delta_meta', 'delta_norm1', 'delta_w_in', 'delta_conv_dw_w', 'delta_conv_dw_b', 'delta_conv_ln_g', 'delta_conv_ln_b', 'delta_w_conv_out', 'delta_b_conv_out', 'delta_w_pool_grp', 'delta_pool_scale', 'delta_w_attn_out', 'delta_w_o', 'delta_norm2', 'delta_w_up', 'delta_ffn_dw_w', 'delta_ffn_dw_b', 'delta_w_down', 'delta_final_norm', 'new_m_meta', 'new_m_norm1', 'new_m_w_in', 'new_m_conv_dw_w', 'new_m_conv_dw_b', 'new_m_conv_ln_g', 'new_m_conv_ln_b', 'new_m_w_conv_out', 'new_m_b_conv_out', 'new_m_w_pool_grp', 'new_m_pool_scale', 'new_m_w_attn_out', 'new_m_w_o', 'new_m_norm2', 'new_m_w_up', 'new_m_ffn_dw_w', 'new_m_ffn_dw_b', 'new_m_w_down', 'new_m_final_norm', 'new_v_meta', 'new_v_norm1', 'new_v_w_in', 'new_v_conv_dw_w', 'new_v_conv_dw_b', 'new_v_conv_ln_g', 'new_v_conv_ln_b', 'new_v_w_conv_out', 'new_v_b_conv_out', 'new_v_w_pool_grp', 'new_v_pool_scale', 'new_v_w_attn_out', 'new_v_w_o', 'new_v_norm2', 'new_v_w_up', 'new_v_ffn_dw_w', 'new_v_ffn_dw_b', 'new_v_w_down', 'new_v_final_norm']
TWIN_LEAF_KINDS = {'loss': 'loss', 'grad_x': 'grad_x', 'grad_meta': 'grad_w', 'grad_norm1': 'grad_w', 'grad_w_in': 'grad_w', 'grad_conv_dw_w': 'grad_w', 'grad_conv_dw_b': 'grad_w', 'grad_conv_ln_g': 'grad_w', 'grad_conv_ln_b': 'grad_w', 'grad_w_conv_out': 'grad_w', 'grad_b_conv_out': 'grad_w', 'grad_w_pool_grp': 'grad_w', 'grad_pool_scale': 'grad_w', 'grad_w_attn_out': 'grad_w', 'grad_w_o': 'grad_w', 'grad_norm2': 'grad_w', 'grad_w_up': 'grad_w', 'grad_ffn_dw_w': 'grad_w', 'grad_ffn_dw_b': 'grad_w', 'grad_w_down': 'grad_w', 'grad_final_norm': 'grad_w', 'delta_meta': 'delta_w', 'delta_norm1': 'delta_w', 'delta_w_in': 'delta_w', 'delta_conv_dw_w': 'delta_w', 'delta_conv_dw_b': 'delta_w', 'delta_conv_ln_g': 'delta_w', 'delta_conv_ln_b': 'delta_w', 'delta_w_conv_out': 'delta_w', 'delta_b_conv_out': 'delta_w', 'delta_w_pool_grp': 'delta_w', 'delta_pool_scale': 'delta_w', 'delta_w_attn_out': 'delta_w', 'delta_w_o': 'delta_w', 'delta_norm2': 'delta_w', 'delta_w_up': 'delta_w', 'delta_ffn_dw_w': 'delta_w', 'delta_ffn_dw_b': 'delta_w', 'delta_w_down': 'delta_w', 'delta_final_norm': 'delta_w', 'new_m_meta': 'new_m', 'new_m_norm1': 'new_m', 'new_m_w_in': 'new_m', 'new_m_conv_dw_w': 'new_m', 'new_m_conv_dw_b': 'new_m', 'new_m_conv_ln_g': 'new_m', 'new_m_conv_ln_b': 'new_m', 'new_m_w_conv_out': 'new_m', 'new_m_b_conv_out': 'new_m', 'new_m_w_pool_grp': 'new_m', 'new_m_pool_scale': 'new_m', 'new_m_w_attn_out': 'new_m', 'new_m_w_o': 'new_m', 'new_m_norm2': 'new_m', 'new_m_w_up': 'new_m', 'new_m_ffn_dw_w': 'new_m', 'new_m_ffn_dw_b': 'new_m', 'new_m_w_down': 'new_m', 'new_m_final_norm': 'new_m', 'new_v_meta': 'new_v', 'new_v_norm1': 'new_v', 'new_v_w_in': 'new_v', 'new_v_conv_dw_w': 'new_v', 'new_v_conv_dw_b': 'new_v', 'new_v_conv_ln_g': 'new_v', 'new_v_conv_ln_b': 'new_v', 'new_v_w_conv_out': 'new_v', 'new_v_b_conv_out': 'new_v', 'new_v_w_pool_grp': 'new_v', 'new_v_pool_scale': 'new_v', 'new_v_w_attn_out': 'new_v', 'new_v_w_o': 'new_v', 'new_v_norm2': 'new_v', 'new_v_w_up': 'new_v', 'new_v_ffn_dw_w': 'new_v', 'new_v_ffn_dw_b': 'new_v', 'new_v_w_down': 'new_v', 'new_v_final_norm': 'new_v'}


def _forward(args):
    return _fwd_reference(*[args[k] for k in FWD_PARAMS])


def _output_shape():
    def fwd():
        inp = _fwd_setup_inputs(0)
        return _fwd_reference(*[inp[k] for k in FWD_PARAMS])
    out = _jax.eval_shape(fwd)
    return out.shape, out.dtype

N_MICROBATCH = 1
ADAM_LR = 0.001
ADAM_B1 = 0.9
ADAM_B2 = 0.999
ADAM_EPS = 1e-08
ADAM_WD = 0.01
ADAM_STEP = 10
PER_EXAMPLE_BATCH_AXIS = {'x': 0, 'loss_target': 0}
SHARED_INPUTS = []
_WEIGHT_DTYPES = {'meta': _jnp.float32, 'norm1': _jnp.float32, 'w_in': _jnp.float32, 'conv_dw_w': _jnp.float32, 'conv_dw_b': _jnp.float32, 'conv_ln_g': _jnp.float32, 'conv_ln_b': _jnp.float32, 'w_conv_out': _jnp.float32, 'b_conv_out': _jnp.float32, 'w_pool_grp': _jnp.float32, 'pool_scale': _jnp.float32, 'w_attn_out': _jnp.float32, 'w_o': _jnp.float32, 'norm2': _jnp.float32, 'w_up': _jnp.float32, 'ffn_dw_w': _jnp.float32, 'ffn_dw_b': _jnp.float32, 'w_down': _jnp.float32, 'final_norm': _jnp.float32}
MOMENT_SCALE = {'meta': 1.231370e-02, 'norm1': 2.403745e-01, 'w_in': 1.063650e-01, 'conv_dw_w': 2.087759e-01, 'conv_dw_b': 4.369447e-01, 'conv_ln_g': 2.367708e-01, 'conv_ln_b': 2.369914e-01, 'w_conv_out': 1.022377e-01, 'b_conv_out': 1.744715e-01, 'w_pool_grp': 1.516644e-01, 'pool_scale': 1.532353e-01, 'w_attn_out': 1.104452e-01, 'w_o': 2.120829e-01, 'norm2': 2.437906e-01, 'w_up': 9.755573e-02, 'ffn_dw_w': 1.017764e-01, 'ffn_dw_b': 9.940037e-02, 'w_down': 1.667500e-01, 'final_norm': 1.278453e+02}


def _to_microbatches(a, axis):
    t = _jnp.moveaxis(a, axis, 0)
    t = t.reshape((N_MICROBATCH, t.shape[0] // N_MICROBATCH) + t.shape[1:])
    return _jnp.moveaxis(t, 1, axis + 1)


def setup_inputs(seed: int = 0) -> dict:
    inp = _fwd_setup_inputs(seed)
    key = _jax.random.fold_in(_jax.random.key(seed), 7919)
    shape, _ = _output_shape()
    out = dict(inp)
    out["loss_target"] = _jax.random.normal(_jax.random.fold_in(key, 0), shape, _jnp.float32)
    for i, name in enumerate(TWIN_WEIGHTS):
        w = inp[name].astype(_jnp.float32)
        if MOMENT_SCALE is None:
            s = _jnp.sqrt(_jnp.mean(_jnp.square(w)) + 1e-30)
        else:
            s = MOMENT_SCALE[name]
        km, kv = _jax.random.split(_jax.random.fold_in(key, i + 1))
        out[name] = w
        out["m_" + name] = s * _jax.random.normal(km, w.shape, _jnp.float32)
        out["v_" + name] = (s * s) * _jax.random.uniform(kv, w.shape, _jnp.float32, 0.5, 1.5)
    if N_MICROBATCH > 1:
        for name, axis in PER_EXAMPLE_BATCH_AXIS.items():
            out[name] = _to_microbatches(out[name], axis)
    return {'x': out['x'], 'meta': out['meta'], 'norm1': out['norm1'], 'w_in': out['w_in'], 'conv_dw_w': out['conv_dw_w'], 'conv_dw_b': out['conv_dw_b'], 'conv_ln_g': out['conv_ln_g'], 'conv_ln_b': out['conv_ln_b'], 'w_conv_out': out['w_conv_out'], 'b_conv_out': out['b_conv_out'], 'w_pool_grp': out['w_pool_grp'], 'pool_scale': out['pool_scale'], 'w_attn_out': out['w_attn_out'], 'w_o': out['w_o'], 'norm2': out['norm2'], 'w_up': out['w_up'], 'ffn_dw_w': out['ffn_dw_w'], 'ffn_dw_b': out['ffn_dw_b'], 'w_down': out['w_down'], 'final_norm': out['final_norm'], 'loss_target': out['loss_target'], 'm_meta': out['m_meta'], 'm_norm1': out['m_norm1'], 'm_w_in': out['m_w_in'], 'm_conv_dw_w': out['m_conv_dw_w'], 'm_conv_dw_b': out['m_conv_dw_b'], 'm_conv_ln_g': out['m_conv_ln_g'], 'm_conv_ln_b': out['m_conv_ln_b'], 'm_w_conv_out': out['m_w_conv_out'], 'm_b_conv_out': out['m_b_conv_out'], 'm_w_pool_grp': out['m_w_pool_grp'], 'm_pool_scale': out['m_pool_scale'], 'm_w_attn_out': out['m_w_attn_out'], 'm_w_o': out['m_w_o'], 'm_norm2': out['m_norm2'], 'm_w_up': out['m_w_up'], 'm_ffn_dw_w': out['m_ffn_dw_w'], 'm_ffn_dw_b': out['m_ffn_dw_b'], 'm_w_down': out['m_w_down'], 'm_final_norm': out['m_final_norm'], 'v_meta': out['v_meta'], 'v_norm1': out['v_norm1'], 'v_w_in': out['v_w_in'], 'v_conv_dw_w': out['v_conv_dw_w'], 'v_conv_dw_b': out['v_conv_dw_b'], 'v_conv_ln_g': out['v_conv_ln_g'], 'v_conv_ln_b': out['v_conv_ln_b'], 'v_w_conv_out': out['v_w_conv_out'], 'v_b_conv_out': out['v_b_conv_out'], 'v_w_pool_grp': out['v_w_pool_grp'], 'v_pool_scale': out['v_pool_scale'], 'v_w_attn_out': out['v_w_attn_out'], 'v_w_o': out['v_w_o'], 'v_norm2': out['v_norm2'], 'v_w_up': out['v_w_up'], 'v_ffn_dw_w': out['v_ffn_dw_w'], 'v_ffn_dw_b': out['v_ffn_dw_b'], 'v_w_down': out['v_w_down'], 'v_final_norm': out['v_final_norm']}


def _loss(weights, diff, rest, loss_target):
    with _jax.named_scope("forward"):
        args = {**rest, TWIN_DIFF_INPUT: diff, **{k: w.astype(_WEIGHT_DTYPES[k]) for k, w in weights.items()}}
        y = _forward(args)
    with _jax.named_scope("loss_head"):
        err = _jnp.square(y.astype(_jnp.float32) - loss_target)
        return 0.5 * _jnp.sum(_jnp.mean(err, axis=-1)) if err.ndim else 0.5 * err


def _adamw(w, g, m, v):
    m = ADAM_B1 * m + (1.0 - ADAM_B1) * g
    v = ADAM_B2 * v + (1.0 - ADAM_B2) * _jnp.square(g)
    m_hat = m / (1.0 - ADAM_B1 ** ADAM_STEP)
    v_hat = v / (1.0 - ADAM_B2 ** ADAM_STEP)
    delta = -ADAM_LR * (m_hat / (_jnp.sqrt(v_hat) + ADAM_EPS) + ADAM_WD * w)
    return delta, m, v


def reference(x, meta, norm1, w_in, conv_dw_w, conv_dw_b, conv_ln_g, conv_ln_b, w_conv_out, b_conv_out, w_pool_grp, pool_scale, w_attn_out, w_o, norm2, w_up, ffn_dw_w, ffn_dw_b, w_down, final_norm, loss_target, m_meta, m_norm1, m_w_in, m_conv_dw_w, m_conv_dw_b, m_conv_ln_g, m_conv_ln_b, m_w_conv_out, m_b_conv_out, m_w_pool_grp, m_pool_scale, m_w_attn_out, m_w_o, m_norm2, m_w_up, m_ffn_dw_w, m_ffn_dw_b, m_w_down, m_final_norm, v_meta, v_norm1, v_w_in, v_conv_dw_w, v_conv_dw_b, v_conv_ln_g, v_conv_ln_b, v_w_conv_out, v_b_conv_out, v_w_pool_grp, v_pool_scale, v_w_attn_out, v_w_o, v_norm2, v_w_up, v_ffn_dw_w, v_ffn_dw_b, v_w_down, v_final_norm):
    given = dict(x=x, meta=meta, norm1=norm1, w_in=w_in, conv_dw_w=conv_dw_w, conv_dw_b=conv_dw_b, conv_ln_g=conv_ln_g, conv_ln_b=conv_ln_b, w_conv_out=w_conv_out, b_conv_out=b_conv_out, w_pool_grp=w_pool_grp, pool_scale=pool_scale, w_attn_out=w_attn_out, w_o=w_o, norm2=norm2, w_up=w_up, ffn_dw_w=ffn_dw_w, ffn_dw_b=ffn_dw_b, w_down=w_down, final_norm=final_norm, loss_target=loss_target, m_meta=m_meta, m_norm1=m_norm1, m_w_in=m_w_in, m_conv_dw_w=m_conv_dw_w, m_conv_dw_b=m_conv_dw_b, m_conv_ln_g=m_conv_ln_g, m_conv_ln_b=m_conv_ln_b, m_w_conv_out=m_w_conv_out, m_b_conv_out=m_b_conv_out, m_w_pool_grp=m_w_pool_grp, m_pool_scale=m_pool_scale, m_w_attn_out=m_w_attn_out, m_w_o=m_w_o, m_norm2=m_norm2, m_w_up=m_w_up, m_ffn_dw_w=m_ffn_dw_w, m_ffn_dw_b=m_ffn_dw_b, m_w_down=m_w_down, m_final_norm=m_final_norm, v_meta=v_meta, v_norm1=v_norm1, v_w_in=v_w_in, v_conv_dw_w=v_conv_dw_w, v_conv_dw_b=v_conv_dw_b, v_conv_ln_g=v_conv_ln_g, v_conv_ln_b=v_conv_ln_b, v_w_conv_out=v_w_conv_out, v_b_conv_out=v_b_conv_out, v_w_pool_grp=v_w_pool_grp, v_pool_scale=v_pool_scale, v_w_attn_out=v_w_attn_out, v_w_o=v_w_o, v_norm2=v_norm2, v_w_up=v_w_up, v_ffn_dw_w=v_ffn_dw_w, v_ffn_dw_b=v_ffn_dw_b, v_w_down=v_w_down, v_final_norm=v_final_norm)
    weights = {n: given[n] for n in TWIN_WEIGHTS}
    shared = {n: given[n] for n in SHARED_INPUTS}
    per_example = {n: given[n] for n in ['x']}
    grad_fn = _jax.value_and_grad(_loss, argnums=(0, 1))

    def one_microbatch(ex, loss_target):
        ex = dict(ex)
        diff = ex.pop(TWIN_DIFF_INPUT)
        return grad_fn(weights, diff, {**shared, **ex}, loss_target)

    if N_MICROBATCH == 1:
        loss, (grad_w, grad_x) = one_microbatch(per_example, given["loss_target"])
    else:
        def body(carry, xs):
            loss_sum, grad_sum = carry
            l_k, (gw_k, gx_k) = one_microbatch(xs[0], xs[1])
            with _jax.named_scope("update"):
                return (loss_sum + l_k, _jax.tree.map(_jnp.add, grad_sum, gw_k)), gx_k

        init = (_jnp.zeros((), _jnp.float32), _jax.tree.map(_jnp.zeros_like, weights))
        (loss, grad_w), grad_x = _jax.lax.scan(body, init, (per_example, given["loss_target"]))
    with _jax.named_scope("update"):
        delta_w, new_m, new_v = {}, {}, {}
        for n in TWIN_WEIGHTS:
            delta_w[n], new_m[n], new_v[n] = _adamw(weights[n], grad_w[n], given["m_" + n], given["v_" + n])
    return (loss, grad_x, *[grad_w[n] for n in TWIN_WEIGHTS], *[delta_w[n] for n in TWIN_WEIGHTS],
            *[new_m[n] for n in TWIN_WEIGHTS], *[new_v[n] for n in TWIN_WEIGHTS])
```

```python
import functools

import jax
import jax.numpy as jnp
from jax import lax
from jax.experimental import pallas as pl
from jax.experimental.pallas import tpu as pltpu

F32 = jnp.float32
BF16 = jnp.bfloat16
MESH = pl.DeviceIdType.MESH

N_DEV = 8
N_META = 16
BLOCK = 128
PAD = 240
FRONT = PAD + N_META
HEADS = 4
HEAD_DIM = 128
CONV_CH = 256
CONV_K = 31
POOL_CH = 256
POOL_WINDOWS = (2, 4, 8, 16)
FFN_K = 3
EPS = 1e-6
ADAM_LR, ADAM_B1, ADAM_B2, ADAM_EPS, ADAM_WD, ADAM_STEP = 0.001, 0.9, 0.999, 1e-08, 0.01, 10

VMEM_LIMIT = 56 * 1024 * 1024
CONV_HALO = 32
POOL_HALO = 16
FFN_HALO = 8
PACK_ALIGN = 8 * 128


def _tile(n, cap, unit):
    if n <= cap:
        return n
    best = None
    t = unit
    while t <= cap:
        if n % t == 0:
            best = t
        t += unit
    assert best is not None, (n, cap, unit)
    return best


def _params(sem):
    return pltpu.CompilerParams(dimension_semantics=sem, vmem_limit_bytes=VMEM_LIMIT)


def _sigmoid(x):
    return 1.0 / (1.0 + jnp.exp(-x))


def _mm(a, b, *, out_dtype, name, res=None, mask_rows=False, tm_cap=640, tn_cap=768, tk_cap=1024):
    m, k = a.shape
    k2, n = b.shape
    assert k == k2
    tm, tn, tk = _tile(m, tm_cap, 128), _tile(n, tn_cap, 128), _tile(k, tk_cap, 128)
    nk = k // tk

    def body(*refs):
        if res is not None:
            a_ref, b_ref, r_ref, o_ref, acc = refs
        else:
            a_ref, b_ref, o_ref, acc = refs
        kk = pl.program_id(2)
        row0 = pl.program_id(0) * tm

        @pl.when(kk == 0)
        def _():
            acc[...] = jnp.zeros_like(acc)

        acc[...] += jnp.dot(a_ref[...].astype(BF16), b_ref[...].astype(BF16), preferred_element_type=F32)

        @pl.when(kk == nk - 1)
        def _():
            y = acc[...]
            if res is not None:
                y = y + r_ref[...].astype(F32)
            if mask_rows:
                row = row0 + lax.broadcasted_iota(jnp.int32, (tm, 1), 0)
                y = jnp.where(row >= PAD, y, 0.0)
            o_ref[...] = y.astype(out_dtype)

    in_specs = [pl.BlockSpec((tm, tk), lambda i, j, kk: (i, kk)), pl.BlockSpec((tk, tn), lambda i, j, kk: (kk, j))]
    args = [a, b]
    if res is not None:
        in_specs.append(pl.BlockSpec((tm, tn), lambda i, j, kk: (i, j)))
        args.append(res)
    return pl.pallas_call(
        body, name=name, grid=(m // tm, n // tn, nk),
        in_specs=in_specs, out_specs=pl.BlockSpec((tm, tn), lambda i, j, kk: (i, j)),
        out_shape=jax.ShapeDtypeStruct((m, n), out_dtype),
        scratch_shapes=[pltpu.VMEM((tm, tn), F32)],
        compiler_params=_params(("parallel", "parallel", "arbitrary")),
    )(*args)


def _mm_tn(a, b, *, name, t1_cap=512, tn_cap=1024, tl_cap=640):
    l, k1 = a.shape
    l2, n = b.shape
    assert l == l2
    t1, tn, tl = _tile(k1, t1_cap, 128), _tile(n, tn_cap, 128), _tile(l, tl_cap, 128)

    def body(a_ref, b_ref, o_ref):
        @pl.when(pl.program_id(2) == 0)
        def _():
            o_ref[...] = jnp.zeros_like(o_ref)

        o_ref[...] += lax.dot_general(a_ref[...].astype(BF16), b_ref[...].astype(BF16),
                                      (((0,), (0,)), ((), ())), preferred_element_type=F32)

    return pl.pallas_call(
        body, name=name, grid=(k1 // t1, n // tn, l // tl),
        in_specs=[pl.BlockSpec((tl, t1), lambda i, j, ll: (ll, i)), pl.BlockSpec((tl, tn), lambda i, j, ll: (ll, j))],
        out_specs=pl.BlockSpec((t1, tn), lambda i, j, ll: (i, j)),
        out_shape=jax.ShapeDtypeStruct((k1, n), F32),
        compiler_params=_params(("parallel", "parallel", "arbitrary")),
    )(a, b)


def _rms_fwd(x, g, *, name):
    l, d = x.shape
    tm = _tile(l, 640, 128)

    def body(x_ref, g_ref, o_ref):
        xv = x_ref[...]
        r = lax.rsqrt(jnp.mean(xv * xv, axis=-1, keepdims=True) + EPS)
        o_ref[...] = (xv * r * g_ref[...]).astype(BF16)

    return pl.pallas_call(
        body, name=name, grid=(l // tm,),
        in_specs=[pl.BlockSpec((tm, d), lambda i: (i, 0)), pl.BlockSpec((1, d), lambda i: (0, 0))],
        out_specs=pl.BlockSpec((tm, d), lambda i: (i, 0)),
        out_shape=jax.ShapeDtypeStruct((l, d), BF16),
        compiler_params=_params(("parallel",)),
    )(x, g.reshape(1, d))


def _rms_bwd(x, g, dy, dres, *, name):
    l, d = x.shape
    tm = _tile(l, 640, 128)

    def body(x_ref, g_ref, dy_ref, dr_ref, dx_ref, dg_ref):
        i = pl.program_id(0)

        @pl.when(i == 0)
        def _():
            dg_ref[...] = jnp.zeros_like(dg_ref)

        xv = x_ref[...]
        r = lax.rsqrt(jnp.mean(xv * xv, axis=-1, keepdims=True) + EPS)
        xh = xv * r
        dyv = dy_ref[...].astype(F32)
        dxh = dyv * g_ref[...]
        dx = r * (dxh - xh * jnp.mean(dxh * xh, axis=-1, keepdims=True)) + dr_ref[...]
        row = i * tm + lax.broadcasted_iota(jnp.int32, (tm, 1), 0)
        dx_ref[...] = jnp.where(row >= PAD, dx, 0.0)
        dg_ref[0:1, :] += jnp.sum(dyv * xh, axis=0, keepdims=True)

    dx, dg = pl.pallas_call(
        body, name=name, grid=(l // tm,),
        in_specs=[pl.BlockSpec((tm, d), lambda i: (i, 0)), pl.BlockSpec((1, d), lambda i: (0, 0)),
                  pl.BlockSpec((tm, d), lambda i: (i, 0)), pl.BlockSpec((tm, d), lambda i: (i, 0))],
        out_specs=[pl.BlockSpec((tm, d), lambda i: (i, 0)), pl.BlockSpec((8, d), lambda i: (0, 0))],
        out_shape=[jax.ShapeDtypeStruct((l, d), F32), jax.ShapeDtypeStruct((8, d), F32)],
        compiler_params=_params(("arbitrary",)),
    )(x, g.reshape(1, d), dy, dres)
    return dx, dg[0]


def _loss_head(h, g, target, *, name):
    l, d = h.shape
    tm = FRONT
    assert l % tm == 0 and target.shape[0] == l - tm

    def body(h_ref, g_ref, t_ref, dh_ref, loss_ref, dg_ref):
        i = pl.program_id(0)

        @pl.when(i == 0)
        def _():
            loss_ref[...] = jnp.zeros_like(loss_ref)
            dg_ref[...] = jnp.zeros_like(dg_ref)
            dh_ref[...] = jnp.zeros_like(dh_ref)

        @pl.when(i > 0)
        def _():
            xv = h_ref[...]
            r = lax.rsqrt(jnp.mean(xv * xv, axis=-1, keepdims=True) + EPS)
            xh = xv * r
            gv = g_ref[...]
            err = xh * gv - t_ref[...]
            loss_ref[...] += 0.5 * jnp.sum(jnp.mean(err * err, axis=-1, keepdims=True))
            dy = err * (1.0 / d)
            dxh = dy * gv
            dh_ref[...] = r * (dxh - xh * jnp.mean(dxh * xh, axis=-1, keepdims=True))
            dg_ref[0:1, :] += jnp.sum(dy * xh, axis=0, keepdims=True)

    dh, loss, dg = pl.pallas_call(
        body, name=name, grid=(l // tm,),
        in_specs=[pl.BlockSpec((tm, d), lambda i: (i, 0)), pl.BlockSpec((1, d), lambda i: (0, 0)),
                  pl.BlockSpec((tm, d), lambda i: (jnp.maximum(i - 1, 0), 0))],
        out_specs=[pl.BlockSpec((tm, d), lambda i: (i, 0)), pl.BlockSpec((8, 128), lambda i: (0, 0)),
                   pl.BlockSpec((8, d), lambda i: (0, 0))],
        out_shape=[jax.ShapeDtypeStruct((l, d), F32), jax.ShapeDtypeStruct((8, 128), F32),
                   jax.ShapeDtypeStruct((8, d), F32)],
        compiler_params=_params(("arbitrary",)),
    )(h, g.reshape(1, d), target)
    return loss[0, 0], dh, dg[0]


def _conv_tile(l):
    return _tile(l, 640, 128)


def _conv_core(a, gt, buf, dw_w, dw_b, first):
    tm = a.shape[0]

    @pl.when(first)
    def _():
        buf[0:CONV_HALO, :] = jnp.zeros((CONV_HALO, CONV_CH), F32)

    @pl.when(jnp.logical_not(first))
    def _():
        buf[0:CONV_HALO, :] = buf[tm:tm + CONV_HALO, :]

    sg = _sigmoid(gt)
    buf[CONV_HALO:CONV_HALO + tm, :] = a * sg
    c = jnp.zeros((tm, CONV_CH), F32) + dw_b
    for k in range(CONV_K):
        off = CONV_HALO - (CONV_K - 1) + k
        c = c + dw_w[k:k + 1, :] * buf[off:off + tm, :]
    return c, sg


def _layer_norm(c, ln_g, ln_b):
    mu = jnp.mean(c, axis=-1, keepdims=True)
    xc = c - mu
    r = lax.rsqrt(jnp.mean(xc * xc, axis=-1, keepdims=True) + EPS)
    xh = xc * r
    return xh, r, xh * ln_g + ln_b


def _conv_fwd(pa, dw_w, dw_b, ln_g, ln_b, *, name):
    l = pa.shape[0]
    tm = _conv_tile(l)

    def body(a_ref, gt_ref, w_ref, b_ref, g_ref, bb_ref, o_ref, buf):
        c, _ = _conv_core(a_ref[...], gt_ref[...], buf, w_ref[...], b_ref[...], pl.program_id(0) == 0)
        _, _, y = _layer_norm(c, g_ref[...], bb_ref[...])
        o_ref[...] = (y * _sigmoid(y)).astype(BF16)

    vec = pl.BlockSpec((1, CONV_CH), lambda i: (0, 0))
    return pl.pallas_call(
        body, name=name, grid=(l // tm,),
        in_specs=[pl.BlockSpec((tm, CONV_CH), lambda i: (i, 0)), pl.BlockSpec((tm, CONV_CH), lambda i: (i, 1)),
                  pl.BlockSpec((CONV_K, CONV_CH), lambda i: (0, 0)), vec, vec, vec],
        out_specs=pl.BlockSpec((tm, CONV_CH), lambda i: (i, 0)),
        out_shape=jax.ShapeDtypeStruct((l, CONV_CH), BF16),
        scratch_shapes=[pltpu.VMEM((CONV_HALO + tm, CONV_CH), F32)],
        compiler_params=_params(("arbitrary",)),
    )(pa, pa, dw_w, dw_b.reshape(1, -1), ln_g.reshape(1, -1), ln_b.reshape(1, -1))


def _conv_bwd_ln(pa, ds, dw_w, dw_b, ln_g, ln_b, *, name):
    l = pa.shape[0]
    tm = _conv_tile(l)

    def body(a_ref, gt_ref, ds_ref, w_ref, b_ref, g_ref, bb_ref, dc_ref, gp_ref, buf):
        i = pl.program_id(0)

        @pl.when(i == 0)
        def _():
            gp_ref[...] = jnp.zeros_like(gp_ref)

        c, _ = _conv_core(a_ref[...], gt_ref[...], buf, w_ref[...], b_ref[...], i == 0)
        xh, r, y = _layer_norm(c, g_ref[...], bb_ref[...])
        sy = _sigmoid(y)
        dy = ds_ref[...] * (sy * (1.0 + y * (1.0 - sy)))
        dxh = dy * g_ref[...]
        dc = r * (dxh - jnp.mean(dxh, axis=-1, keepdims=True) - xh * jnp.mean(dxh * xh, axis=-1, keepdims=True))
        dc_ref[...] = dc
        for k in range(CONV_K):
            off = CONV_HALO - (CONV_K - 1) + k
            gp_ref[k:k + 1, :] += jnp.sum(dc * buf[off:off + tm, :], axis=0, keepdims=True)
        gp_ref[32:33, :] += jnp.sum(dc, axis=0, keepdims=True)
        gp_ref[33:34, :] += jnp.sum(dy * xh, axis=0, keepdims=True)
        gp_ref[34:35, :] += jnp.sum(dy, axis=0, keepdims=True)

    vec = pl.BlockSpec((1, CONV_CH), lambda i: (0, 0))
    return pl.pallas_call(
        body, name=name, grid=(l // tm,),
        in_specs=[pl.BlockSpec((tm, CONV_CH), lambda i: (i, 0)), pl.BlockSpec((tm, CONV_CH), lambda i: (i, 1)),
                  pl.BlockSpec((tm, CONV_CH), lambda i: (i, 0)),
                  pl.BlockSpec((CONV_K, CONV_CH), lambda i: (0, 0)), vec, vec, vec],
        out_specs=[pl.BlockSpec((tm, CONV_CH), lambda i: (i, 0)), pl.BlockSpec((40, CONV_CH), lambda i: (0, 0))],
        out_shape=[jax.ShapeDtypeStruct((l, CONV_CH), F32), jax.ShapeDtypeStruct((40, CONV_CH), F32)],
        scratch_shapes=[pltpu.VMEM((CONV_HALO + tm, CONV_CH), F32)],
        compiler_params=_params(("arbitrary",)),
    )(pa, pa, ds, dw_w, dw_b.reshape(1, -1), ln_g.reshape(1, -1), ln_b.reshape(1, -1))


def _conv_bwd_in(pa, dc, dw_w, *, name):
    l = pa.shape[0]
    tm = _conv_tile(l)
    nt = l // tm

    def body(a_ref, gt_ref, dc_ref, w_ref, o_ref, buf):
        first = pl.program_id(0) == 0

        @pl.when(first)
        def _():
            buf[tm:tm + CONV_HALO, :] = jnp.zeros((CONV_HALO, CONV_CH), F32)

        @pl.when(jnp.logical_not(first))
        def _():
            buf[tm:tm + CONV_HALO, :] = buf[0:CONV_HALO, :]

        buf[0:tm, :] = dc_ref[...]
        w = w_ref[...]
        dhc = jnp.zeros((tm, CONV_CH), F32)
        for k in range(CONV_K):
            off = CONV_K - 1 - k
            dhc = dhc + w[k:k + 1, :] * buf[off:off + tm, :]
        a = a_ref[...]
        sg = _sigmoid(gt_ref[...])
        o_ref[:, 0:CONV_CH] = (dhc * sg).astype(BF16)
        o_ref[:, CONV_CH:2 * CONV_CH] = (dhc * a * sg * (1.0 - sg)).astype(BF16)

    return pl.pallas_call(
        body, name=name, grid=(nt,),
        in_specs=[pl.BlockSpec((tm, CONV_CH), lambda i: (nt - 1 - i, 0)),
                  pl.BlockSpec((tm, CONV_CH), lambda i: (nt - 1 - i, 1)),
                  pl.BlockSpec((tm, CONV_CH), lambda i: (nt - 1 - i, 0)),
                  pl.BlockSpec((CONV_K, CONV_CH), lambda i: (0, 0))],
        out_specs=pl.BlockSpec((tm, 2 * CONV_CH), lambda i: (nt - 1 - i, 0)),
        out_shape=jax.ShapeDtypeStruct((l, 2 * CONV_CH), BF16),
        scratch_shapes=[pltpu.VMEM((tm + CONV_HALO, CONV_CH), F32)],
        compiler_params=_params(("arbitrary",)),
    )(pa, pa, dc, dw_w)


def _pool_consts(tm, row0):
    lane = lax.broadcasted_iota(jnp.int32, (1, POOL_CH), 1)
    grp = lane // (POOL_CH // len(POOL_WINDOWS))
    win = jnp.where(grp == 0, 2.0, jnp.where(grp == 1, 4.0, jnp.where(grp == 2, 8.0, 16.0))).astype(F32)
    pos = (row0 + lax.broadcasted_iota(jnp.int32, (tm, 1), 0) - PAD).astype(F32)
    cnt = jnp.maximum(jnp.minimum(pos + 1.0, win), 1.0)
    return grp, cnt


def _pool_select(grp, s2, s4, s8, s16):
    return jnp.where(grp == 0, s2, jnp.where(grp == 1, s4, jnp.where(grp == 2, s8, s16)))


def _pool_fwd(pa, *, name):
    l = pa.shape[0]
    tm = _conv_tile(l)
    ext = POOL_HALO + tm

    def body(p_ref, o_ref, buf):
        i = pl.program_id(0)

        @pl.when(i == 0)
        def _():
            buf[0:POOL_HALO, :] = jnp.zeros((POOL_HALO, POOL_CH), F32)

        @pl.when(i > 0)
        def _():
            buf[0:POOL_HALO, :] = buf[tm:tm + POOL_HALO, :]

        p = p_ref[...]
        buf[POOL_HALO:ext, :] = p
        x = buf[...]
        s2 = x + pltpu.roll(x, 1, 0)
        s4 = s2 + pltpu.roll(s2, 2, 0)
        s8 = s4 + pltpu.roll(s4, 4, 0)
        s16 = s8 + pltpu.roll(s8, 8, 0)
        grp, cnt = _pool_consts(tm, i * tm)
        s = _pool_select(grp, s2, s4, s8, s16)[POOL_HALO:ext, :]
        o_ref[...] = (s / cnt - p).astype(BF16)

    return pl.pallas_call(
        body, name=name, grid=(l // tm,),
        in_specs=[pl.BlockSpec((tm, POOL_CH), lambda i: (i, 2))],
        out_specs=pl.BlockSpec((tm, POOL_CH), lambda i: (i, 0)),
        out_shape=jax.ShapeDtypeStruct((l, POOL_CH), BF16),
        scratch_shapes=[pltpu.VMEM((ext, POOL_CH), F32)],
        compiler_params=_params(("arbitrary",)),
    )(pa)


def _pool_bwd(dpooled, *, name):
    l = dpooled.shape[0]
    tm = _conv_tile(l)
    nt = l // tm
    ext = tm + POOL_HALO

    def body(d_ref, o_ref, buf):
        i = pl.program_id(0)

        @pl.when(i == 0)
        def _():
            buf[tm:ext, :] = jnp.zeros((POOL_HALO, POOL_CH), F32)

        @pl.when(i > 0)
        def _():
            buf[tm:ext, :] = buf[0:POOL_HALO, :]

        d = d_ref[...]
        grp, cnt = _pool_consts(tm, (nt - 1 - i) * tm)
        buf[0:tm, :] = d / cnt
        x = buf[...]
        s2 = x + pltpu.roll(x, ext - 1, 0)
        s4 = s2 + pltpu.roll(s2, ext - 2, 0)
        s8 = s4 + pltpu.roll(s4, ext - 4, 0)
        s16 = s8 + pltpu.roll(s8, ext - 8, 0)
        s = _pool_select(grp, s2, s4, s8, s16)[0:tm, :]
        o_ref[...] = (s - d).astype(BF16)

    return pl.pallas_call(
        body, name=name, grid=(nt,),
        in_specs=[pl.BlockSpec((tm, POOL_CH), lambda i: (nt - 1 - i, 0))],
        out_specs=pl.BlockSpec((tm, POOL_CH), lambda i: (nt - 1 - i, 0)),
        out_shape=jax.ShapeDtypeStruct((l, POOL_CH), BF16),
        scratch_shapes=[pltpu.VMEM((ext, POOL_CH), F32)],
        compiler_params=_params(("arbitrary",)),
    )(dpooled)


ATT_TQ = 256


def _tri_ones():
    r = lax.broadcasted_iota(jnp.int32, (BLOCK, 2 * BLOCK), 0)
    c = lax.broadcasted_iota(jnp.int32, (BLOCK, 2 * BLOCK), 1)
    return jnp.where((c >= BLOCK) | (r > c), 1.0, 0.0).astype(BF16)


def _split_dot(x, rhs):
    hi = x.astype(BF16)
    lo = (x - hi.astype(F32)).astype(BF16)
    return jnp.dot(hi, rhs, preferred_element_type=F32) + jnp.dot(lo, rhs, preferred_element_type=F32)


def _scores(q, kb, qpos, off):
    z = lax.dot_general(q, kb, (((1,), (1,)), ((), ())), preferred_element_type=F32) * (HEAD_DIM ** -0.5)
    kpos = off + lax.broadcasted_iota(jnp.int32, z.shape, 1)
    valid = (kpos < qpos) & (kpos >= PAD)
    sp = jnp.log(1.0 + jnp.exp(-jnp.abs(z)))
    lk = jnp.where(valid, -(jnp.maximum(z, 0.0) + sp), 0.0)
    lb = jnp.minimum(z, 0.0) - sp
    return lk, lb, valid


def _attn_fwd(qkv, *, name):
    l = qkv.shape[0]
    tq = ATT_TQ
    bpq = tq // BLOCK

    def body(q_ref, k_ref, v_ref, o_ref, o32_ref, acc_ref, r_ref):
        i = pl.program_id(1)
        acc_ref[...] = jnp.zeros_like(acc_ref)
        r_ref[...] = jnp.zeros_like(r_ref)
        q = q_ref[...]
        qpos = i * tq + lax.broadcasted_iota(jnp.int32, (tq, BLOCK), 0)
        tri = _tri_ones()
        nkb = (i + 1) * bpq

        def step(jj, carry):
            off = pl.multiple_of((nkb - 1 - jj) * BLOCK, BLOCK)
            lk, lb, valid = _scores(q, k_ref[pl.ds(off, BLOCK), :], qpos, off)
            wt = _split_dot(lk, tri)
            a = jnp.where(valid, jnp.exp(lb + wt[:, :BLOCK] + r_ref[...]), 0.0)
            acc_ref[...] += jnp.dot(a.astype(BF16), v_ref[pl.ds(off, BLOCK), :], preferred_element_type=F32)
            r_ref[...] += wt[:, BLOCK:]
            return carry

        lax.fori_loop(0, nkb, step, 0)
        o_ref[...] = acc_ref[...].astype(BF16)
        o32_ref[...] = acc_ref[...]

    tile = pl.BlockSpec((tq, HEAD_DIM), lambda h, i: (i, h))
    return pl.pallas_call(
        body, name=name, grid=(HEADS, l // tq),
        in_specs=[tile,
                  pl.BlockSpec((l, HEAD_DIM), lambda h, i: (0, HEADS + h)),
                  pl.BlockSpec((l, HEAD_DIM), lambda h, i: (0, 2 * HEADS + h))],
        out_specs=[tile, tile],
        out_shape=[jax.ShapeDtypeStruct((l, HEADS * HEAD_DIM), BF16),
                   jax.ShapeDtypeStruct((l, HEADS * HEAD_DIM), F32)],
        scratch_shapes=[pltpu.VMEM((tq, HEAD_DIM), F32), pltpu.VMEM((tq, BLOCK), F32)],
        compiler_params=_params(("parallel", "arbitrary")),
    )(qkv, qkv, qkv)


def _attn_bwd(qkv, att, datt, *, name):
    l = qkv.shape[0]
    tq = ATT_TQ
    bpq = tq // BLOCK
    nq = l // tq

    def body(q_ref, k_ref, v_ref, o_ref, do_ref, dq_ref, dk_hbm, dv_hbm, dk_acc, dv_acc, dq_acc, r_ref, s_ref, sem):
        h = pl.program_id(0)
        i = pl.program_id(1)

        @pl.when(i == 0)
        def _():
            dk_acc[...] = jnp.zeros_like(dk_acc)
            dv_acc[...] = jnp.zeros_like(dv_acc)

        dq_acc[...] = jnp.zeros_like(dq_acc)
        r_ref[...] = jnp.zeros_like(r_ref)
        s_ref[...] = jnp.zeros_like(s_ref)
        q = q_ref[...]
        do = do_ref[...]
        ptot = jnp.sum(do.astype(F32) * o_ref[...], axis=-1, keepdims=True)
        qpos = i * tq + lax.broadcasted_iota(jnp.int32, (tq, BLOCK), 0)
        tri = _tri_ones()
        nkb = (i + 1) * bpq

        def step(jj, carry):
            off = pl.multiple_of((nkb - 1 - jj) * BLOCK, BLOCK)
            kb = k_ref[pl.ds(off, BLOCK), :]
            vb = v_ref[pl.ds(off, BLOCK), :]
            lk, lb, valid = _scores(q, kb, qpos, off)
            wt = _split_dot(lk, tri)
            a = jnp.where(valid, jnp.exp(lb + wt[:, :BLOCK] + r_ref[...]), 0.0)
            da = lax.dot_general(do, vb, (((1,), (1,)), ((), ())), preferred_element_type=F32)
            ab = a.astype(BF16)
            p = ab.astype(F32) * da
            pw = _split_dot(p, tri)
            c = ptot - (p + pw[:, :BLOCK] + s_ref[...])
            beta = jnp.exp(lb)
            dz = jnp.where(valid, p * (1.0 - beta) - beta * c, 0.0) * (HEAD_DIM ** -0.5)
            dzb = dz.astype(BF16)
            dq_acc[...] += jnp.dot(dzb, kb, preferred_element_type=F32)
            dk_acc[pl.ds(off, BLOCK), :] += lax.dot_general(dzb, q, (((0,), (0,)), ((), ())),
                                                            preferred_element_type=F32)
            dv_acc[pl.ds(off, BLOCK), :] += lax.dot_general(ab, do, (((0,), (0,)), ((), ())),
                                                            preferred_element_type=F32)
            r_ref[...] += wt[:, BLOCK:]
            s_ref[...] += pw[:, BLOCK:]
            return carry

        lax.fori_loop(0, nkb, step, 0)
        dq_ref[...] = dq_acc[...].astype(BF16)

        @pl.when(i == nq - 1)
        def _():
            ck = pltpu.make_async_copy(dk_acc, dk_hbm.at[h], sem.at[0])
            cv = pltpu.make_async_copy(dv_acc, dv_hbm.at[h], sem.at[1])
            ck.start()
            cv.start()
            ck.wait()
            cv.wait()

    tile = pl.BlockSpec((tq, HEAD_DIM), lambda h, i: (i, h))
    return pl.pallas_call(
        body, name=name, grid=(HEADS, nq),
        in_specs=[tile,
                  pl.BlockSpec((l, HEAD_DIM), lambda h, i: (0, HEADS + h)),
                  pl.BlockSpec((l, HEAD_DIM), lambda h, i: (0, 2 * HEADS + h)),
                  tile, tile],
        out_specs=[tile, pl.BlockSpec(memory_space=pl.ANY), pl.BlockSpec(memory_space=pl.ANY)],
        out_shape=[jax.ShapeDtypeStruct((l, HEADS * HEAD_DIM), BF16),
                   jax.ShapeDtypeStruct((HEADS, l, HEAD_DIM), F32), jax.ShapeDtypeStruct((HEADS, l, HEAD_DIM), F32)],
        scratch_shapes=[pltpu.VMEM((l, HEAD_DIM), F32), pltpu.VMEM((l, HEAD_DIM), F32),
                        pltpu.VMEM((tq, HEAD_DIM), F32), pltpu.VMEM((tq, BLOCK), F32), pltpu.VMEM((tq, BLOCK), F32),
                        pltpu.SemaphoreType.DMA((2,))],
        compiler_params=_params(("arbitrary", "arbitrary")),
    )(qkv, qkv, qkv, att, datt)


MIX_TM = 256


def _mix_branches(s_ref, p_ref, t_ref, g_ref, wa_ref, wb_ref, wc_ref, ba_ref, sc_ref, d):
    ya = jnp.dot(s_ref[...], wa_ref[...], preferred_element_type=F32) + ba_ref[...]
    yb0 = jnp.dot(p_ref[...], wb_ref[...], preferred_element_type=F32)
    yc = jnp.dot(t_ref[...], wc_ref[...], preferred_element_type=F32)
    g0 = _sigmoid(g_ref[:, 0:d])
    g1 = _sigmoid(g_ref[:, d:2 * d])
    g2 = _sigmoid(g_ref[:, 2 * d:3 * d])
    return ya, yb0, yc, g0, g1, g2


def _mix_specs(tm, d):
    row = lambda w: pl.BlockSpec((tm, w), lambda i: (i, 0))
    full = lambda r: pl.BlockSpec((r, d), lambda i: (0, 0))
    return [row(CONV_CH), row(POOL_CH), row(HEADS * HEAD_DIM), row(3 * d),
            full(CONV_CH), full(POOL_CH), full(HEADS * HEAD_DIM), full(1), full(1)]


def _mix_fwd(s, pooled, att, gates, wa, wb, wc, ba, scale, *, name):
    l, d = s.shape[0], wa.shape[1]
    tm = _tile(l, MIX_TM, 128)

    def body(s_ref, p_ref, t_ref, g_ref, wa_ref, wb_ref, wc_ref, ba_ref, sc_ref, o_ref):
        ya, yb0, yc, g0, g1, g2 = _mix_branches(s_ref, p_ref, t_ref, g_ref, wa_ref, wb_ref, wc_ref, ba_ref, sc_ref, d)
        o_ref[...] = (g0 * ya + g1 * (yb0 * sc_ref[...]) + g2 * yc).astype(BF16)

    return pl.pallas_call(
        body, name=name, grid=(l // tm,), in_specs=_mix_specs(tm, d),
        out_specs=pl.BlockSpec((tm, d), lambda i: (i, 0)),
        out_shape=jax.ShapeDtypeStruct((l, d), BF16),
        compiler_params=_params(("parallel",)),
    )(s, pooled, att, gates, wa, wb, wc, ba.reshape(1, d), scale.reshape(1, d))


def _mix_bwd(s, pooled, att, gates, wa, wb, wc, ba, scale, dmixed, *, name):
    l, d = s.shape[0], wa.shape[1]
    tm = _tile(l, MIX_TM, 128)

    def body(s_ref, p_ref, t_ref, g_ref, wa_ref, wb_ref, wc_ref, ba_ref, sc_ref, dm_ref,
             dg_ref, dya_ref, dyb_ref, dyc_ref, vec_ref):
        @pl.when(pl.program_id(0) == 0)
        def _():
            vec_ref[...] = jnp.zeros_like(vec_ref)

        ya, yb0, yc, g0, g1, g2 = _mix_branches(s_ref, p_ref, t_ref, g_ref, wa_ref, wb_ref, wc_ref, ba_ref, sc_ref, d)
        dm = dm_ref[...].astype(F32)
        sc = sc_ref[...]
        dg_ref[:, 0:d] = (dm * ya * g0 * (1.0 - g0)).astype(BF16)
        dg_ref[:, d:2 * d] = (dm * (yb0 * sc) * g1 * (1.0 - g1)).astype(BF16)
        dg_ref[:, 2 * d:3 * d] = (dm * yc * g2 * (1.0 - g2)).astype(BF16)
        dya = dm * g0
        dyb = dm * g1
        dya_ref[...] = dya.astype(BF16)
        dyb_ref[...] = (dyb * sc).astype(BF16)
        dyc_ref[...] = (dm * g2).astype(BF16)
        vec_ref[0:1, :] += jnp.sum(dya, axis=0, keepdims=True)
        vec_ref[1:2, :] += jnp.sum(dyb * yb0, axis=0, keepdims=True)

    row = lambda w: pl.BlockSpec((tm, w), lambda i: (i, 0))
    outs = pl.pallas_call(
        body, name=name, grid=(l // tm,), in_specs=_mix_specs(tm, d) + [row(d)],
        out_specs=[row(3 * d), row(d), row(d), row(d), pl.BlockSpec((8, d), lambda i: (0, 0))],
        out_shape=[jax.ShapeDtypeStruct((l, 3 * d), BF16), jax.ShapeDtypeStruct((l, d), BF16),
                   jax.ShapeDtypeStruct((l, d), BF16), jax.ShapeDtypeStruct((l, d), BF16),
                   jax.ShapeDtypeStruct((8, d), F32)],
        compiler_params=_params(("arbitrary",)),
    )(s, pooled, att, gates, wa, wb, wc, ba.reshape(1, d), scale.reshape(1, d), dmixed)
    return outs


FFN_TC = 512
_GELU_C = 0.7978845608028654
_GELU_A = 0.044715


def _gelu(x):
    th = jnp.tanh(_GELU_C * (x + _GELU_A * x * x * x))
    return 0.5 * x * (1.0 + th), th


def _gelu_grad(x, th):
    return 0.5 * (1.0 + th) + 0.5 * x * (1.0 - th * th) * _GELU_C * (1.0 + 3.0 * _GELU_A * x * x)


def _ffn_conv(buf, w, b, tm):
    acc = jnp.zeros((tm, w.shape[1]), F32) + b
    for k in range(FFN_K):
        off = FFN_HALO - (FFN_K - 1) + k
        acc = acc + w[k:k + 1, :] * buf[off:off + tm, :]
    return acc


def _ffn_fwd(ug, uv, wg, wv, bg, bv, *, name):
    l, f = ug.shape
    tm = _conv_tile(l)
    tc = _tile(f, FFN_TC, 128)
    ext = FFN_HALO + tm

    def body(ug_ref, uv_ref, wg_ref, wv_ref, bg_ref, bv_ref, o_ref, bufg, bufv):
        i = pl.program_id(1)
        for buf, u_ref in ((bufg, ug_ref), (bufv, uv_ref)):
            @pl.when(i == 0)
            def _():
                buf[0:FFN_HALO, :] = jnp.zeros((FFN_HALO, tc), F32)

            @pl.when(i > 0)
            def _():
                buf[0:FFN_HALO, :] = buf[tm:ext, :]

            buf[FFN_HALO:ext, :] = u_ref[...]
        gc = _ffn_conv(bufg, wg_ref[...], bg_ref[...], tm)
        vc = _ffn_conv(bufv, wv_ref[...], bv_ref[...], tm)
        o_ref[...] = (_gelu(gc)[0] * vc).astype(BF16)

    tile = pl.BlockSpec((tm, tc), lambda j, i: (i, j))
    wspec = pl.BlockSpec((FFN_K, tc), lambda j, i: (0, j))
    bspec = pl.BlockSpec((1, tc), lambda j, i: (0, j))
    return pl.pallas_call(
        body, name=name, grid=(f // tc, l // tm),
        in_specs=[tile, tile, wspec, wspec, bspec, bspec], out_specs=tile,
        out_shape=jax.ShapeDtypeStruct((l, f), BF16),
        scratch_shapes=[pltpu.VMEM((ext, tc), F32), pltpu.VMEM((ext, tc), F32)],
        compiler_params=_params(("parallel", "arbitrary")),
    )(ug, uv, wg, wv, bg.reshape(1, f), bv.reshape(1, f))


def _ffn_bwd(ug, uv, wg, wv, bg, bv, dact, *, name):
    l, f = ug.shape
    tm = _conv_tile(l)
    tc = _tile(f, FFN_TC, 128)
    nt = l // tm
    ext = FFN_HALO + tm
    hb = tm // FFN_HALO

    def body(ug_ref, uv_ref, pg_ref, pv_ref, wg_ref, wv_ref, bg_ref, bv_ref, da_ref,
             dug_ref, duv_ref, gg_ref, gv_ref, bufg, bufv, dbufg, dbufv):
        i = pl.program_id(1)
        last = i == nt - 1

        @pl.when(i == 0)
        def _():
            gg_ref[...] = jnp.zeros_like(gg_ref)
            gv_ref[...] = jnp.zeros_like(gv_ref)

        for buf, u_ref, prev_ref in ((bufg, ug_ref, pg_ref), (bufv, uv_ref, pv_ref)):
            buf[0:FFN_HALO, :] = jnp.where(last, 0.0, prev_ref[...])
            buf[FFN_HALO:ext, :] = u_ref[...]
        gc = _ffn_conv(bufg, wg_ref[...], bg_ref[...], tm)
        vc = _ffn_conv(bufv, wv_ref[...], bv_ref[...], tm)
        ge, th = _gelu(gc)
        da = da_ref[...].astype(F32)
        dgc = da * vc * _gelu_grad(gc, th)
        dvc = da * ge
        for dbuf, buf, dc, w_ref, du_ref, gp_ref in ((dbufg, bufg, dgc, wg_ref, dug_ref, gg_ref),
                                                      (dbufv, bufv, dvc, wv_ref, duv_ref, gv_ref)):
            @pl.when(i == 0)
            def _():
                dbuf[tm:ext, :] = jnp.zeros((FFN_HALO, tc), F32)

            @pl.when(i > 0)
            def _():
                dbuf[tm:ext, :] = dbuf[0:FFN_HALO, :]

            dbuf[0:tm, :] = dc
            w = w_ref[...]
            du = jnp.zeros((tm, tc), F32)
            for k in range(FFN_K):
                off = FFN_K - 1 - k
                du = du + w[k:k + 1, :] * dbuf[off:off + tm, :]
                uoff = FFN_HALO - (FFN_K - 1) + k
                gp_ref[k:k + 1, :] += jnp.sum(dc * buf[uoff:uoff + tm, :], axis=0, keepdims=True)
            gp_ref[3:4, :] += jnp.sum(dc, axis=0, keepdims=True)
            du_ref[...] = du.astype(BF16)

    tile = pl.BlockSpec((tm, tc), lambda j, i: (nt - 1 - i, j))
    prev = pl.BlockSpec((FFN_HALO, tc), lambda j, i: (jnp.maximum((nt - 1 - i) * hb - 1, 0), j))
    wspec = pl.BlockSpec((FFN_K, tc), lambda j, i: (0, j))
    bspec = pl.BlockSpec((1, tc), lambda j, i: (0, j))
    gspec = pl.BlockSpec((8, tc), lambda j, i: (0, j))
    return pl.pallas_call(
        body, name=name, grid=(f // tc, nt),
        in_specs=[tile, tile, prev, prev, wspec, wspec, bspec, bspec, tile],
        out_specs=[tile, tile, gspec, gspec],
        out_shape=[jax.ShapeDtypeStruct((l, f), BF16), jax.ShapeDtypeStruct((l, f), BF16),
                   jax.ShapeDtypeStruct((8, f), F32), jax.ShapeDtypeStruct((8, f), F32)],
        scratch_shapes=[pltpu.VMEM((ext, tc), F32), pltpu.VMEM((ext, tc), F32),
                        pltpu.VMEM((ext, tc), F32), pltpu.VMEM((ext, tc), F32)],
        compiler_params=_params(("parallel", "arbitrary")),
    )(ug, uv, ug, uv, wg, wv, bg.reshape(1, f), bv.reshape(1, f), dact)


def _adamw(parts, w, m, v, *, name):
    r = w.shape[0]
    tr = _tile(r, 1024, 8)
    c1 = 1.0 / (1.0 - ADAM_B1 ** ADAM_STEP)
    c2 = 1.0 / (1.0 - ADAM_B2 ** ADAM_STEP)

    def body(p_ref, w_ref, m_ref, v_ref, g_ref, d_ref, nm_ref, nv_ref):
        g = p_ref[0]
        for k in range(1, N_DEV):
            g = g + p_ref[k]
        nm = ADAM_B1 * m_ref[...] + (1.0 - ADAM_B1) * g
        nv = ADAM_B2 * v_ref[...] + (1.0 - ADAM_B2) * (g * g)
        g_ref[...] = g
        nm_ref[...] = nm
        nv_ref[...] = nv
        d_ref[...] = -ADAM_LR * ((nm * c1) / (jnp.sqrt(nv * c2) + ADAM_EPS) + ADAM_WD * w_ref[...])

    tile = pl.BlockSpec((tr, 128), lambda i: (i, 0))
    return pl.pallas_call(
        body, name=name, grid=(r // tr,),
        in_specs=[pl.BlockSpec((N_DEV, tr, 128), lambda i: (0, i, 0)), tile, tile, tile],
        out_specs=[tile, tile, tile, tile],
        out_shape=[jax.ShapeDtypeStruct((r, 128), F32)] * 4,
        compiler_params=_params(("parallel",)),
    )(parts, w, m, v)


def _place():
    return lax.axis_index("x"), lax.axis_index("y"), lax.axis_index("c")


def _all_gather(x, *, name):
    def body(x_ref, out_ref, send_sems, recv_sems, local_sem):
        xx, yy, cc = _place()
        me, sibling = (xx, yy, cc), (xx, yy, 1 - cc)
        chips = [(1 - xx, yy), (xx, 1 - yy), (1 - xx, 1 - yy)]

        def slot(px, py, pc):
            return out_ref.at[4 * px + 2 * py + pc]

        def copy(k, block, to, src=None):
            return pltpu.make_async_remote_copy(
                src_ref=slot(*block) if src is None else src, dst_ref=slot(*block),
                send_sem=send_sems.at[k], recv_sem=recv_sems.at[k], device_id=to, device_id_type=MESH)

        mine = pltpu.make_async_copy(x_ref, slot(*me), local_sem)
        mine.start()
        first = [copy(0, me, sibling, src=x_ref)]
        first += [copy(1 + j, me, (*chip, cc), src=x_ref) for j, chip in enumerate(chips)]
        for cp in first:
            cp.start()
        passed = [copy(4 + j, (*chip, cc), sibling) for j, chip in enumerate(chips)]
        for j, chip in enumerate(chips):
            copy(1 + j, (*chip, cc), me).wait_recv()
            passed[j].start()
        copy(0, sibling, me).wait_recv()
        for j, chip in enumerate(chips):
            copy(4 + j, (*chip, 1 - cc), me).wait_recv()
        for cp in first + passed:
            cp.wait_send()
        mine.wait()

    return pl.pallas_call(
        body, name=name,
        in_specs=[pl.BlockSpec(memory_space=pl.ANY)], out_specs=pl.BlockSpec(memory_space=pl.ANY),
        out_shape=jax.ShapeDtypeStruct((N_DEV,) + x.shape, x.dtype),
        scratch_shapes=[pltpu.SemaphoreType.DMA((7,)), pltpu.SemaphoreType.DMA((7,)), pltpu.SemaphoreType.DMA],
    )(x)


def _all_to_all(send, *, name):
    def body(s_ref, r_ref, send_sems, recv_sems, local_sem):
        xx, yy, cc = _place()
        me = 4 * xx + 2 * yy + cc
        local = pltpu.make_async_copy(s_ref.at[me], r_ref.at[me], local_sem)
        local.start()
        copies = []
        for m in range(1, N_DEV):
            px = 1 - xx if m & 4 else xx
            py = 1 - yy if m & 2 else yy
            pc = 1 - cc if m & 1 else cc
            copies.append(pltpu.make_async_remote_copy(
                src_ref=s_ref.at[4 * px + 2 * py + pc], dst_ref=r_ref.at[me],
                send_sem=send_sems.at[m - 1], recv_sem=recv_sems.at[m - 1],
                device_id=(px, py, pc), device_id_type=MESH))
        for cp in copies:
            cp.start()
        for cp in copies:
            cp.wait_recv()
        for cp in copies:
            cp.wait_send()
        local.wait()

    return pl.pallas_call(
        body, name=name,
        in_specs=[pl.BlockSpec(memory_space=pl.ANY)], out_specs=pl.BlockSpec(memory_space=pl.ANY),
        out_shape=jax.ShapeDtypeStruct(send.shape, send.dtype),
        scratch_shapes=[pltpu.SemaphoreType.DMA((7,)), pltpu.SemaphoreType.DMA((7,)), pltpu.SemaphoreType.DMA],
    )(send)


def _pack(arrays, dtype):
    parts = []
    for a in arrays:
        flat = a.reshape(-1).astype(dtype)
        parts.append(jnp.pad(flat, (0, (-flat.size) % PACK_ALIGN)))
    return jnp.concatenate(parts).reshape(-1, 128)


def _pack_pieces(arrays, dtype):
    parts = []
    for a in arrays:
        flat = a.reshape(N_DEV, -1).astype(dtype)
        parts.append(jnp.pad(flat, ((0, 0), (0, (-flat.shape[1]) % PACK_ALIGN))))
    return jnp.concatenate(parts, axis=1).reshape(N_DEV, -1, 128)


def _unpack(buf, shapes, lead=()):
    flat = buf.reshape(lead + (-1,))
    out, off = [], 0
    for shp in shapes:
        size = 1
        for s in shp:
            size *= s
        out.append(flat[..., off:off + size].reshape(lead + tuple(shp)))
        off += size + (-size) % PACK_ALIGN
    return out


def _unshard(g, axis):
    g = jnp.moveaxis(g, 0, axis)
    shp = list(g.shape)
    return g.reshape(shp[:axis] + [shp[axis] * shp[axis + 1]] + shp[axis + 2:])


def _pieces(full, axis):
    shp = list(full.shape)
    g = full.reshape(shp[:axis] + [N_DEV, shp[axis] // N_DEV] + shp[axis + 1:])
    return jnp.moveaxis(g, axis, 0)


SHARDED = (("meta", 1), ("w_in", 2), ("conv_dw_w", 2), ("w_conv_out", 2), ("w_pool_grp", 3), ("w_attn_out", 2),
           ("w_o", 1), ("w_up", 2), ("ffn_dw_w", 2), ("w_down", 1))
MATRICES = ("w_in", "w_conv_out", "w_pool_grp", "w_attn_out", "w_o", "w_up", "w_down")
REPLICATED = ("norm1", "conv_dw_b", "conv_ln_g", "conv_ln_b", "b_conv_out", "pool_scale", "norm2", "ffn_dw_b",
              "final_norm")
WEIGHTS = ("meta", "norm1", "w_in", "conv_dw_w", "conv_dw_b", "conv_ln_g", "conv_ln_b", "w_conv_out", "b_conv_out",
           "w_pool_grp", "pool_scale", "w_attn_out", "w_o", "norm2", "w_up", "ffn_dw_w", "ffn_dw_b", "w_down",
           "final_norm")


def _block_diag(w_grp):
    g, gc, od = w_grp.shape
    out = jnp.zeros((g * gc, g * od), w_grp.dtype)
    for i in range(g):
        out = out.at[i * gc:(i + 1) * gc, i * od:(i + 1) * od].set(w_grp[i])
    return out


def _block_diag_grad(gw, g):
    gc, od = gw.shape[0] // g, gw.shape[1] // g
    return jnp.stack([gw[i * gc:(i + 1) * gc, i * od:(i + 1) * od] for i in range(g)])


C_CONV = 2 * CONV_CH
C_POOL = C_CONV + POOL_CH
C_ATT = HEADS * HEAD_DIM
C_QKV = C_POOL + 3 * C_ATT


def _layer_fwd(h, p, tag):
    d = h.shape[1]
    w_in = p["w_in"]
    hn = _rms_fwd(h, p["norm1"], name=f"rms1_{tag}")
    pa = _mm(hn, w_in[:, :C_POOL], out_dtype=F32, name=f"proj_a_{tag}")
    qkv = _mm(hn, w_in[:, C_POOL:C_QKV], out_dtype=BF16, name=f"proj_qkv_{tag}")
    gates = _mm(hn, w_in[:, C_QKV:], out_dtype=F32, name=f"proj_g_{tag}")
    s = _conv_fwd(pa, p["conv_dw_w"], p["conv_dw_b"], p["conv_ln_g"], p["conv_ln_b"], name=f"conv_{tag}")
    pooled = _pool_fwd(pa, name=f"pool_{tag}")
    att, att32 = _attn_fwd(qkv, name=f"attn_{tag}")
    wb = _block_diag(p["w_pool_grp"])
    mixed = _mix_fwd(s, pooled, att, gates, p["w_conv_out"], wb, p["w_attn_out"], p["b_conv_out"], p["pool_scale"],
                     name=f"mix_{tag}")
    h1 = _mm(mixed, p["w_o"], out_dtype=F32, res=h, mask_rows=True, name=f"wo_{tag}")
    hn2 = _rms_fwd(h1, p["norm2"], name=f"rms2_{tag}")
    f = p["w_up"].shape[1] // 2
    ug = _mm(hn2, p["w_up"][:, :f], out_dtype=F32, name=f"up_g_{tag}")
    uv = _mm(hn2, p["w_up"][:, f:], out_dtype=F32, name=f"up_v_{tag}")
    act = _ffn_fwd(ug, uv, p["ffn_dw_w"][:, :f], p["ffn_dw_w"][:, f:], p["ffn_dw_b"][:f], p["ffn_dw_b"][f:],
                   name=f"ffn_{tag}")
    h2 = _mm(act, p["w_down"], out_dtype=F32, res=h1, mask_rows=True, name=f"down_{tag}")
    saved = dict(h=h, hn=hn, pa=pa, qkv=qkv, gates=gates, s=s, pooled=pooled, att=att, att32=att32, wb=wb, mixed=mixed,
                 h1=h1,
                 hn2=hn2, ug=ug, uv=uv, act=act)
    return h2, saved


def _layer_bwd(dh2, p, sv, tag):
    g = {}
    f = p["w_up"].shape[1] // 2
    dact = _mm(dh2, p["w_down"].T, out_dtype=F32, name=f"b_down_{tag}")
    g["w_down"] = _mm_tn(sv["act"], dh2, name=f"g_down_{tag}")
    dug, duv, gpg, gpv = _ffn_bwd(sv["ug"], sv["uv"], p["ffn_dw_w"][:, :f], p["ffn_dw_w"][:, f:], p["ffn_dw_b"][:f],
                                  p["ffn_dw_b"][f:], dact, name=f"b_ffn_{tag}")
    g["ffn_dw_w"] = jnp.concatenate([gpg[0:FFN_K], gpv[0:FFN_K]], axis=1)
    g["ffn_dw_b"] = jnp.concatenate([gpg[FFN_K], gpv[FFN_K]])
    w_up_t = p["w_up"].T
    dhn2 = _mm(dug, w_up_t[:f], out_dtype=F32, name=f"b_up_g_{tag}")
    dhn2 = _mm(duv, w_up_t[f:], out_dtype=F32, res=dhn2, name=f"b_up_v_{tag}")
    g["w_up"] = jnp.concatenate([_mm_tn(sv["hn2"], dug, name=f"g_up_g_{tag}"),
                                 _mm_tn(sv["hn2"], duv, name=f"g_up_v_{tag}")], axis=1)
    dh1, g["norm2"] = _rms_bwd(sv["h1"], p["norm2"], dhn2, dh2, name=f"b_rms2_{tag}")
    dmixed = _mm(dh1, p["w_o"].T, out_dtype=BF16, name=f"b_wo_{tag}")
    g["w_o"] = _mm_tn(sv["mixed"], dh1, name=f"g_wo_{tag}")
    dgates, dya, dyb, dyc, vec = _mix_bwd(sv["s"], sv["pooled"], sv["att"], sv["gates"], p["w_conv_out"], sv["wb"],
                                          p["w_attn_out"], p["b_conv_out"], p["pool_scale"], dmixed,
                                          name=f"b_mix_{tag}")
    g["b_conv_out"], g["pool_scale"] = vec[0], vec[1]
    ds = _mm(dya, p["w_conv_out"].T, out_dtype=F32, name=f"b_conv_out_{tag}")
    dpooled = _mm(dyb, sv["wb"].T, out_dtype=F32, name=f"b_pool_out_{tag}")
    datt = _mm(dyc, p["w_attn_out"].T, out_dtype=BF16, name=f"b_attn_out_{tag}")
    g["w_conv_out"] = _mm_tn(sv["s"], dya, name=f"g_conv_out_{tag}")
    g["w_pool_grp"] = _block_diag_grad(_mm_tn(sv["pooled"], dyb, name=f"g_pool_{tag}"), len(POOL_WINDOWS))
    g["w_attn_out"] = _mm_tn(sv["att"], dyc, name=f"g_attn_out_{tag}")
    dc, gp = _conv_bwd_ln(sv["pa"], ds, p["conv_dw_w"], p["conv_dw_b"], p["conv_ln_g"], p["conv_ln_b"],
                          name=f"b_conv_ln_{tag}")
    g["conv_dw_w"], g["conv_dw_b"], g["conv_ln_g"], g["conv_ln_b"] = gp[0:CONV_K], gp[32], gp[33], gp[34]
    dconv = _conv_bwd_in(sv["pa"], dc, p["conv_dw_w"], name=f"b_conv_in_{tag}")
    dp = _pool_bwd(dpooled, name=f"b_pool_{tag}")
    dq, dk, dv = _attn_bwd(sv["qkv"], sv["att32"], datt, name=f"b_attn_{tag}")
    dk = jnp.moveaxis(dk, 0, 1).reshape(dq.shape).astype(BF16)
    dv = jnp.moveaxis(dv, 0, 1).reshape(dq.shape).astype(BF16)
    w_in_t = p["w_in"].T
    cols = [(dconv, 0, C_CONV), (dp, C_CONV, C_POOL), (dq, C_POOL, C_POOL + C_ATT),
            (dk, C_POOL + C_ATT, C_POOL + 2 * C_ATT), (dv, C_POOL + 2 * C_ATT, C_QKV),
            (dgates, C_QKV, w_in_t.shape[0])]
    dhn, gw = None, []
    for n, (dcol, lo, hi) in enumerate(cols):
        dhn = _mm(dcol, w_in_t[lo:hi], out_dtype=F32, res=dhn, name=f"b_in{n}_{tag}")
        gw.append(_mm_tn(sv["hn"], dcol, name=f"g_in{n}_{tag}"))
    g["w_in"] = jnp.concatenate(gw, axis=1)
    dh, g["norm1"] = _rms_bwd(sv["h"], p["norm1"], dhn, dh1, name=f"b_rms1_{tag}")
    return dh, g


def kernel(x, meta, norm1, w_in, conv_dw_w, conv_dw_b, conv_ln_g, conv_ln_b, w_conv_out, b_conv_out, w_pool_grp, pool_scale, w_attn_out, w_o, norm2, w_up, ffn_dw_w, ffn_dw_b, w_down, final_norm, loss_target, m_meta, m_norm1, m_w_in, m_conv_dw_w, m_conv_dw_b, m_conv_ln_g, m_conv_ln_b, m_w_conv_out, m_b_conv_out, m_w_pool_grp, m_pool_scale, m_w_attn_out, m_w_o, m_norm2, m_w_up, m_ffn_dw_w, m_ffn_dw_b, m_w_down, m_final_norm, v_meta, v_norm1, v_w_in, v_conv_dw_w, v_conv_dw_b, v_conv_ln_g, v_conv_ln_b, v_w_conv_out, v_b_conv_out, v_w_pool_grp, v_pool_scale, v_w_attn_out, v_w_o, v_norm2, v_w_up, v_ffn_dw_w, v_ffn_dw_b, v_w_down, v_final_norm):
    given = dict(meta=meta, norm1=norm1, w_in=w_in, conv_dw_w=conv_dw_w, conv_dw_b=conv_dw_b, conv_ln_g=conv_ln_g, conv_ln_b=conv_ln_b, w_conv_out=w_conv_out, b_conv_out=b_conv_out, w_pool_grp=w_pool_grp, pool_scale=pool_scale, w_attn_out=w_attn_out, w_o=w_o, norm2=norm2, w_up=w_up, ffn_dw_w=ffn_dw_w, ffn_dw_b=ffn_dw_b, w_down=w_down, final_norm=final_norm)
    mom_m = dict(meta=m_meta, norm1=m_norm1, w_in=m_w_in, conv_dw_w=m_conv_dw_w, conv_dw_b=m_conv_dw_b, conv_ln_g=m_conv_ln_g, conv_ln_b=m_conv_ln_b, w_conv_out=m_w_conv_out, b_conv_out=m_b_conv_out, w_pool_grp=m_w_pool_grp, pool_scale=m_pool_scale, w_attn_out=m_w_attn_out, w_o=m_w_o, norm2=m_norm2, w_up=m_w_up, ffn_dw_w=m_ffn_dw_w, ffn_dw_b=m_ffn_dw_b, w_down=m_w_down, final_norm=m_final_norm)
    mom_v = dict(meta=v_meta, norm1=v_norm1, w_in=v_w_in, conv_dw_w=v_conv_dw_w, conv_dw_b=v_conv_dw_b, conv_ln_g=v_conv_ln_g, conv_ln_b=v_conv_ln_b, w_conv_out=v_w_conv_out, b_conv_out=v_b_conv_out, w_pool_grp=v_w_pool_grp, pool_scale=v_pool_scale, w_attn_out=v_w_attn_out, w_o=v_w_o, norm2=v_norm2, w_up=v_w_up, ffn_dw_w=v_ffn_dw_w, ffn_dw_b=v_ffn_dw_b, w_down=v_w_down, final_norm=v_final_norm)
    sharded_axis = dict(SHARDED)
    vectors = [n for n, _ in SHARDED if n not in MATRICES]
    depth = norm1.shape[0]

    got_mat = _all_gather(_pack([given[n] for n in MATRICES], BF16), name="gather_matrices")
    got_vec = _all_gather(_pack([given[n] for n in vectors], F32), name="gather_vectors")
    full = {n: given[n] for n in REPLICATED}
    for n, a in zip(MATRICES, _unpack(got_mat, [given[n].shape for n in MATRICES], (N_DEV,))):
        full[n] = _unshard(a, sharded_axis[n])
    for n, a in zip(vectors, _unpack(got_vec, [given[n].shape for n in vectors], (N_DEV,))):
        full[n] = _unshard(a, sharded_axis[n])

    xs = x[0]
    d = xs.shape[1]
    h = jnp.concatenate([jnp.zeros((PAD, d), F32), full["meta"], xs], axis=0)
    layers, saved = [], []
    for i in range(depth):
        p = {n: full[n][i] for n in full if n not in ("meta", "final_norm")}
        layers.append(p)
        h, sv = _layer_fwd(h, p, f"l{i}")
        saved.append(sv)
    loss_part, dh, g_final = _loss_head(h, full["final_norm"], loss_target[0], name="loss_head")

    grads = [None] * depth
    for i in reversed(range(depth)):
        dh, grads[i] = _layer_bwd(dh, layers[i], saved[i], f"l{i}")
    full_grad = {n: jnp.stack([grads[i][n] for i in range(depth)]) for n in grads[0]}
    full_grad["meta"] = dh[PAD:FRONT]
    full_grad["final_norm"] = g_final
    grad_x = dh[FRONT:][None]

    names = [n for n, _ in SHARDED]
    recv = _all_to_all(_pack_pieces([_pieces(full_grad[n], sharded_axis[n]) for n in names], F32), name="scatter_grads")
    rep_shapes = [given[n].shape for n in REPLICATED] + [(1,)]
    rep_parts = _all_gather(_pack([full_grad[n] for n in REPLICATED] + [loss_part.reshape(1)], F32),
                            name="gather_partials")

    out = {}
    shapes = [given[n].shape for n in names]
    res = _adamw(recv, _pack([given[n] for n in names], F32), _pack([mom_m[n] for n in names], F32),
                 _pack([mom_v[n] for n in names], F32), name="adamw_sharded")
    for kind, buf in zip(("grad", "delta", "new_m", "new_v"), res):
        for n, a in zip(names, _unpack(buf, shapes)):
            out[kind, n] = a
    rep_w = [given[n] for n in REPLICATED] + [jnp.zeros((1,), F32)]
    rep_m = [mom_m[n] for n in REPLICATED] + [jnp.zeros((1,), F32)]
    rep_v = [mom_v[n] for n in REPLICATED] + [jnp.ones((1,), F32)]
    res = _adamw(rep_parts, _pack(rep_w, F32), _pack(rep_m, F32), _pack(rep_v, F32), name="adamw_replicated")
    for kind, buf in zip(("grad", "delta", "new_m", "new_v"), res):
        for n, a in zip(list(REPLICATED) + ["loss"], _unpack(buf, rep_shapes)):
            out[kind, n] = a
    loss = out["grad", "loss"][0]
    return (loss, grad_x, *[out["grad", n] for n in WEIGHTS], *[out["delta", n] for n in WEIGHTS],
            *[out["new_m", n] for n in WEIGHTS], *[out["new_v", n] for n in WEIGHTS])
```

```python
import functools

import jax
import jax.numpy as jnp
from jax import lax
from jax.experimental import pallas as pl
from jax.experimental.pallas import tpu as pltpu

F32 = jnp.float32
BF16 = jnp.bfloat16
MESH = pl.DeviceIdType.MESH

N_DEV = 8
N_META = 16
BLOCK = 128
PAD = 240
FRONT = PAD + N_META
HEADS = 4
HEAD_DIM = 128
CONV_CH = 256
CONV_K = 31
POOL_CH = 256
POOL_WINDOWS = (2, 4, 8, 16)
FFN_K = 3
EPS = 1e-6
ADAM_LR, ADAM_B1, ADAM_B2, ADAM_EPS, ADAM_WD, ADAM_STEP = 0.001, 0.9, 0.999, 1e-08, 0.01, 10

VMEM_LIMIT = 56 * 1024 * 1024
CONV_HALO = 32
POOL_HALO = 16
FFN_HALO = 8
PACK_ALIGN = 8 * 128


def _tile(n, cap, unit):
    if n <= cap:
        return n
    best = None
    t = unit
    while t <= cap:
        if n % t == 0:
            best = t
        t += unit
    assert best is not None, (n, cap, unit)
    return best


def _params(sem):
    return pltpu.CompilerParams(dimension_semantics=sem, vmem_limit_bytes=VMEM_LIMIT)


def _sigmoid(x):
    return 1.0 / (1.0 + jnp.exp(-x))


def _mm(a, b, *, out_dtype, name, res=None, mask_rows=False, tm_cap=640, tn_cap=768, tk_cap=1024):
    m, k = a.shape
    k2, n = b.shape
    assert k == k2
    tm, tn, tk = _tile(m, tm_cap, 128), _tile(n, tn_cap, 128), _tile(k, tk_cap, 128)
    nk = k // tk

    def body(*refs):
        if res is not None:
            a_ref, b_ref, r_ref, o_ref, acc = refs
        else:
            a_ref, b_ref, o_ref, acc = refs
        kk = pl.program_id(2)
        row0 = pl.program_id(0) * tm

        @pl.when(kk == 0)
        def _():
            acc[...] = jnp.zeros_like(acc)

        acc[...] += jnp.dot(a_ref[...].astype(BF16), b_ref[...].astype(BF16), preferred_element_type=F32)

        @pl.when(kk == nk - 1)
        def _():
            y = acc[...]
            if res is not None:
                y = y + r_ref[...].astype(F32)
            if mask_rows:
                row = row0 + lax.broadcasted_iota(jnp.int32, (tm, 1), 0)
                y = jnp.where(row >= PAD, y, 0.0)
            o_ref[...] = y.astype(out_dtype)

    in_specs = [pl.BlockSpec((tm, tk), lambda i, j, kk: (i, kk)), pl.BlockSpec((tk, tn), lambda i, j, kk: (kk, j))]
    args = [a, b]
    if res is not None:
        in_specs.append(pl.BlockSpec((tm, tn), lambda i, j, kk: (i, j)))
        args.append(res)
    return pl.pallas_call(
        body, name=name, grid=(m // tm, n // tn, nk),
        in_specs=in_specs, out_specs=pl.BlockSpec((tm, tn), lambda i, j, kk: (i, j)),
        out_shape=jax.ShapeDtypeStruct((m, n), out_dtype),
        scratch_shapes=[pltpu.VMEM((tm, tn), F32)],
        compiler_params=_params(("parallel", "parallel", "arbitrary")),
    )(*args)


def _mm_tn(a, b, *, name, t1_cap=512, tn_cap=1024, tl_cap=640):
    l, k1 = a.shape
    l2, n = b.shape
    assert l == l2
    t1, tn, tl = _tile(k1, t1_cap, 128), _tile(n, tn_cap, 128), _tile(l, tl_cap, 128)

    def body(a_ref, b_ref, o_ref):
        @pl.when(pl.program_id(2) == 0)
        def _():
            o_ref[...] = jnp.zeros_like(o_ref)

        o_ref[...] += lax.dot_general(a_ref[...].astype(BF16), b_ref[...].astype(BF16),
                                      (((0,), (0,)), ((), ())), preferred_element_type=F32)

    return pl.pallas_call(
        body, name=name, grid=(k1 // t1, n // tn, l // tl),
        in_specs=[pl.BlockSpec((tl, t1), lambda i, j, ll: (ll, i)), pl.BlockSpec((tl, tn), lambda i, j, ll: (ll, j))],
        out_specs=pl.BlockSpec((t1, tn), lambda i, j, ll: (i, j)),
        out_shape=jax.ShapeDtypeStruct((k1, n), F32),
        compiler_params=_params(("parallel", "parallel", "arbitrary")),
    )(a, b)


def _rms_fwd(x, g, *, name):
    l, d = x.shape
    tm = _tile(l, 640, 128)

    def body(x_ref, g_ref, o_ref):
        xv = x_ref[...]
        r = lax.rsqrt(jnp.mean(xv * xv, axis=-1, keepdims=True) + EPS)
        o_ref[...] = (xv * r * g_ref[...]).astype(BF16)

    return pl.pallas_call(
        body, name=name, grid=(l // tm,),
        in_specs=[pl.BlockSpec((tm, d), lambda i: (i, 0)), pl.BlockSpec((1, d), lambda i: (0, 0))],
        out_specs=pl.BlockSpec((tm, d), lambda i: (i, 0)),
        out_shape=jax.ShapeDtypeStruct((l, d), BF16),
        compiler_params=_params(("parallel",)),
    )(x, g.reshape(1, d))


def _rms_bwd(x, g, dy, dres, *, name):
    l, d = x.shape
    tm = _tile(l, 640, 128)

    def body(x_ref, g_ref, dy_ref, dr_ref, dx_ref, dg_ref):
        i = pl.program_id(0)

        @pl.when(i == 0)
        def _():
            dg_ref[...] = jnp.zeros_like(dg_ref)

        xv = x_ref[...]
        r = lax.rsqrt(jnp.mean(xv * xv, axis=-1, keepdims=True) + EPS)
        xh = xv * r
        dyv = dy_ref[...].astype(F32)
        dxh = dyv * g_ref[...]
        dx = r * (dxh - xh * jnp.mean(dxh * xh, axis=-1, keepdims=True)) + dr_ref[...]
        row = i * tm + lax.broadcasted_iota(jnp.int32, (tm, 1), 0)
        dx_ref[...] = jnp.where(row >= PAD, dx, 0.0)
        dg_ref[0:1, :] += jnp.sum(dyv * xh, axis=0, keepdims=True)

    dx, dg = pl.pallas_call(
        body, name=name, grid=(l // tm,),
        in_specs=[pl.BlockSpec((tm, d), lambda i: (i, 0)), pl.BlockSpec((1, d), lambda i: (0, 0)),
                  pl.BlockSpec((tm, d), lambda i: (i, 0)), pl.BlockSpec((tm, d), lambda i: (i, 0))],
        out_specs=[pl.BlockSpec((tm, d), lambda i: (i, 0)), pl.BlockSpec((8, d), lambda i: (0, 0))],
        out_shape=[jax.ShapeDtypeStruct((l, d), F32), jax.ShapeDtypeStruct((8, d), F32)],
        compiler_params=_params(("arbitrary",)),
    )(x, g.reshape(1, d), dy, dres)
    return dx, dg[0]


def _loss_head(h, g, target, *, name):
    l, d = h.shape
    tm = FRONT
    assert l % tm == 0 and target.shape[0] == l - tm

    def body(h_ref, g_ref, t_ref, dh_ref, loss_ref, dg_ref):
        i = pl.program_id(0)

        @pl.when(i == 0)
        def _():
            loss_ref[...] = jnp.zeros_like(loss_ref)
            dg_ref[...] = jnp.zeros_like(dg_ref)
            dh_ref[...] = jnp.zeros_like(dh_ref)

        @pl.when(i > 0)
        def _():
            xv = h_ref[...]
            r = lax.rsqrt(jnp.mean(xv * xv, axis=-1, keepdims=True) + EPS)
            xh = xv * r
            gv = g_ref[...]
            err = xh * gv - t_ref[...]
            loss_ref[...] += 0.5 * jnp.sum(jnp.mean(err * err, axis=-1, keepdims=True))
            dy = err * (1.0 / d)
            dxh = dy * gv
            dh_ref[...] = r * (dxh - xh * jnp.mean(dxh * xh, axis=-1, keepdims=True))
            dg_ref[0:1, :] += jnp.sum(dy * xh, axis=0, keepdims=True)

    dh, loss, dg = pl.pallas_call(
        body, name=name, grid=(l // tm,),
        in_specs=[pl.BlockSpec((tm, d), lambda i: (i, 0)), pl.BlockSpec((1, d), lambda i: (0, 0)),
                  pl.BlockSpec((tm, d), lambda i: (jnp.maximum(i - 1, 0), 0))],
        out_specs=[pl.BlockSpec((tm, d), lambda i: (i, 0)), pl.BlockSpec((8, 128), lambda i: (0, 0)),
                   pl.BlockSpec((8, d), lambda i: (0, 0))],
        out_shape=[jax.ShapeDtypeStruct((l, d), F32), jax.ShapeDtypeStruct((8, 128), F32),
                   jax.ShapeDtypeStruct((8, d), F32)],
        compiler_params=_params(("arbitrary",)),
    )(h, g.reshape(1, d), target)
    return loss[0, 0], dh, dg[0]


def _conv_tile(l):
    return _tile(l, 640, 128)


def _conv_core(a, gt, buf, dw_w, dw_b, first):
    tm = a.shape[0]

    @pl.when(first)
    def _():
        buf[0:CONV_HALO, :] = jnp.zeros((CONV_HALO, CONV_CH), F32)

    @pl.when(jnp.logical_not(first))
    def _():
        buf[0:CONV_HALO, :] = buf[tm:tm + CONV_HALO, :]

    sg = _sigmoid(gt)
    buf[CONV_HALO:CONV_HALO + tm, :] = a * sg
    c = jnp.zeros((tm, CONV_CH), F32) + dw_b
    for k in range(CONV_K):
        off = CONV_HALO - (CONV_K - 1) + k
        c = c + dw_w[k:k + 1, :] * buf[off:off + tm, :]
    return c, sg


def _layer_norm(c, ln_g, ln_b):
    mu = jnp.mean(c, axis=-1, keepdims=True)
    xc = c - mu
    r = lax.rsqrt(jnp.mean(xc * xc, axis=-1, keepdims=True) + EPS)
    xh = xc * r
    return xh, r, xh * ln_g + ln_b


def _conv_fwd(pa, dw_w, dw_b, ln_g, ln_b, *, name):
    l = pa.shape[0]
    tm = _conv_tile(l)

    def body(a_ref, gt_ref, w_ref, b_ref, g_ref, bb_ref, o_ref, buf):
        c, _ = _conv_core(a_ref[...], gt_ref[...], buf, w_ref[...], b_ref[...], pl.program_id(0) == 0)
        _, _, y = _layer_norm(c, g_ref[...], bb_ref[...])
        o_ref[...] = (y * _sigmoid(y)).astype(BF16)

    vec = pl.BlockSpec((1, CONV_CH), lambda i: (0, 0))
    return pl.pallas_call(
        body, name=name, grid=(l // tm,),
        in_specs=[pl.BlockSpec((tm, CONV_CH), lambda i: (i, 0)), pl.BlockSpec((tm, CONV_CH), lambda i: (i, 1)),
                  pl.BlockSpec((CONV_K, CONV_CH), lambda i: (0, 0)), vec, vec, vec],
        out_specs=pl.BlockSpec((tm, CONV_CH), lambda i: (i, 0)),
        out_shape=jax.ShapeDtypeStruct((l, CONV_CH), BF16),
        scratch_shapes=[pltpu.VMEM((CONV_HALO + tm, CONV_CH), F32)],
        compiler_params=_params(("arbitrary",)),
    )(pa, pa, dw_w, dw_b.reshape(1, -1), ln_g.reshape(1, -1), ln_b.reshape(1, -1))


def _conv_bwd_ln(pa, ds, dw_w, dw_b, ln_g, ln_b, *, name):
    l = pa.shape[0]
    tm = _conv_tile(l)

    def body(a_ref, gt_ref, ds_ref, w_ref, b_ref, g_ref, bb_ref, dc_ref, gp_ref, buf):
        i = pl.program_id(0)

        @pl.when(i == 0)
        def _():
            gp_ref[...] = jnp.zeros_like(gp_ref)

        c, _ = _conv_core(a_ref[...], gt_ref[...], buf, w_ref[...], b_ref[...], i == 0)
        xh, r, y = _layer_norm(c, g_ref[...], bb_ref[...])
        sy = _sigmoid(y)
        dy = ds_ref[...] * (sy * (1.0 + y * (1.0 - sy)))
        dxh = dy * g_ref[...]
        dc = r * (dxh - jnp.mean(dxh, axis=-1, keepdims=True) - xh * jnp.mean(dxh * xh, axis=-1, keepdims=True))
        dc_ref[...] = dc
        for k in range(CONV_K):
            off = CONV_HALO - (CONV_K - 1) + k
            gp_ref[k:k + 1, :] += jnp.sum(dc * buf[off:off + tm, :], axis=0, keepdims=True)
        gp_ref[32:33, :] += jnp.sum(dc, axis=0, keepdims=True)
        gp_ref[33:34, :] += jnp.sum(dy * xh, axis=0, keepdims=True)
        gp_ref[34:35, :] += jnp.sum(dy, axis=0, keepdims=True)

    vec = pl.BlockSpec((1, CONV_CH), lambda i: (0, 0))
    return pl.pallas_call(
        body, name=name, grid=(l // tm,),
        in_specs=[pl.BlockSpec((tm, CONV_CH), lambda i: (i, 0)), pl.BlockSpec((tm, CONV_CH), lambda i: (i, 1)),
                  pl.BlockSpec((tm, CONV_CH), lambda i: (i, 0)),
                  pl.BlockSpec((CONV_K, CONV_CH), lambda i: (0, 0)), vec, vec, vec],
        out_specs=[pl.BlockSpec((tm, CONV_CH), lambda i: (i, 0)), pl.BlockSpec((40, CONV_CH), lambda i: (0, 0))],
        out_shape=[jax.ShapeDtypeStruct((l, CONV_CH), F32), jax.ShapeDtypeStruct((40, CONV_CH), F32)],
        scratch_shapes=[pltpu.VMEM((CONV_HALO + tm, CONV_CH), F32)],
        compiler_params=_params(("arbitrary",)),
    )(pa, pa, ds, dw_w, dw_b.reshape(1, -1), ln_g.reshape(1, -1), ln_b.reshape(1, -1))


def _conv_bwd_in(pa, dc, dw_w, *, name):
    l = pa.shape[0]
    tm = _conv_tile(l)
    nt = l // tm

    def body(a_ref, gt_ref, dc_ref, w_ref, o_ref, buf):
        first = pl.program_id(0) == 0

        @pl.when(first)
        def _():
            buf[tm:tm + CONV_HALO, :] = jnp.zeros((CONV_HALO, CONV_CH), F32)

        @pl.when(jnp.logical_not(first))
        def _():
            buf[tm:tm + CONV_HALO, :] = buf[0:CONV_HALO, :]

        buf[0:tm, :] = dc_ref[...]
        w = w_ref[...]
        dhc = jnp.zeros((tm, CONV_CH), F32)
        for k in range(CONV_K):
            off = CONV_K - 1 - k
            dhc = dhc + w[k:k + 1, :] * buf[off:off + tm, :]
        a = a_ref[...]
        sg = _sigmoid(gt_ref[...])
        o_ref[:, 0:CONV_CH] = (dhc * sg).astype(BF16)
        o_ref[:, CONV_CH:2 * CONV_CH] = (dhc * a * sg * (1.0 - sg)).astype(BF16)

    return pl.pallas_call(
        body, name=name, grid=(nt,),
        in_specs=[pl.BlockSpec((tm, CONV_CH), lambda i: (nt - 1 - i, 0)),
                  pl.BlockSpec((tm, CONV_CH), lambda i: (nt - 1 - i, 1)),
                  pl.BlockSpec((tm, CONV_CH), lambda i: (nt - 1 - i, 0)),
                  pl.BlockSpec((CONV_K, CONV_CH), lambda i: (0, 0))],
        out_specs=pl.BlockSpec((tm, 2 * CONV_CH), lambda i: (nt - 1 - i, 0)),
        out_shape=jax.ShapeDtypeStruct((l, 2 * CONV_CH), BF16),
        scratch_shapes=[pltpu.VMEM((tm + CONV_HALO, CONV_CH), F32)],
        compiler_params=_params(("arbitrary",)),
    )(pa, pa, dc, dw_w)


def _pool_consts(tm, row0):
    lane = lax.broadcasted_iota(jnp.int32, (1, POOL_CH), 1)
    grp = lane // (POOL_CH // len(POOL_WINDOWS))
    win = jnp.where(grp == 0, 2.0, jnp.where(grp == 1, 4.0, jnp.where(grp == 2, 8.0, 16.0))).astype(F32)
    pos = (row0 + lax.broadcasted_iota(jnp.int32, (tm, 1), 0) - PAD).astype(F32)
    cnt = jnp.maximum(jnp.minimum(pos + 1.0, win), 1.0)
    return grp, cnt


def _pool_select(grp, s2, s4, s8, s16):
    return jnp.where(grp == 0, s2, jnp.where(grp == 1, s4, jnp.where(grp == 2, s8, s16)))


def _pool_fwd(pa, *, name):
    l = pa.shape[0]
    tm = _conv_tile(l)
    ext = POOL_HALO + tm

    def body(p_ref, o_ref, buf):
        i = pl.program_id(0)

        @pl.when(i == 0)
        def _():
            buf[0:POOL_HALO, :] = jnp.zeros((POOL_HALO, POOL_CH), F32)

        @pl.when(i > 0)
        def _():
            buf[0:POOL_HALO, :] = buf[tm:tm + POOL_HALO, :]

        p = p_ref[...]
        buf[POOL_HALO:ext, :] = p
        x = buf[...]
        s2 = x + pltpu.roll(x, 1, 0)
        s4 = s2 + pltpu.roll(s2, 2, 0)
        s8 = s4 + pltpu.roll(s4, 4, 0)
        s16 = s8 + pltpu.roll(s8, 8, 0)
        grp, cnt = _pool_consts(tm, i * tm)
        s = _pool_select(grp, s2, s4, s8, s16)[POOL_HALO:ext, :]
        o_ref[...] = (s / cnt - p).astype(BF16)

    return pl.pallas_call(
        body, name=name, grid=(l // tm,),
        in_specs=[pl.BlockSpec((tm, POOL_CH), lambda i: (i, 2))],
        out_specs=pl.BlockSpec((tm, POOL_CH), lambda i: (i, 0)),
        out_shape=jax.ShapeDtypeStruct((l, POOL_CH), BF16),
        scratch_shapes=[pltpu.VMEM((ext, POOL_CH), F32)],
        compiler_params=_params(("arbitrary",)),
    )(pa)


def _pool_bwd(dpooled, *, name):
    l = dpooled.shape[0]
    tm = _conv_tile(l)
    nt = l // tm
    ext = tm + POOL_HALO

    def body(d_ref, o_ref, buf):
        i = pl.program_id(0)

        @pl.when(i == 0)
        def _():
            buf[tm:ext, :] = jnp.zeros((POOL_HALO, POOL_CH), F32)

        @pl.when(i > 0)
        def _():
            buf[tm:ext, :] = buf[0:POOL_HALO, :]

        d = d_ref[...]
        grp, cnt = _pool_consts(tm, (nt - 1 - i) * tm)
        buf[0:tm, :] = d / cnt
        x = buf[...]
        s2 = x + pltpu.roll(x, ext - 1, 0)
        s4 = s2 + pltpu.roll(s2, ext - 2, 0)
        s8 = s4 + pltpu.roll(s4, ext - 4, 0)
        s16 = s8 + pltpu.roll(s8, ext - 8, 0)
        s = _pool_select(grp, s2, s4, s8, s16)[0:tm, :]
        o_ref[...] = (s - d).astype(BF16)

    return pl.pallas_call(
        body, name=name, grid=(nt,),
        in_specs=[pl.BlockSpec((tm, POOL_CH), lambda i: (nt - 1 - i, 0))],
        out_specs=pl.BlockSpec((tm, POOL_CH), lambda i: (nt - 1 - i, 0)),
        out_shape=jax.ShapeDtypeStruct((l, POOL_CH), BF16),
        scratch_shapes=[pltpu.VMEM((ext, POOL_CH), F32)],
        compiler_params=_params(("arbitrary",)),
    )(dpooled)


ATT_TQ = 256
ATT_TK = 5 * BLOCK
ATT_SUB = ATT_TK // BLOCK


def _tri_ones():
    r = lax.broadcasted_iota(jnp.int32, (2 * BLOCK, 2 * BLOCK), 0) % BLOCK
    c = lax.broadcasted_iota(jnp.int32, (2 * BLOCK, 2 * BLOCK), 1)
    return jnp.where((c >= BLOCK) | (r > c), 1.0, 0.0).astype(BF16)


def _split_dot(x, rhs):
    hi = x.astype(BF16)
    lo = (x - hi.astype(F32)).astype(BF16)
    return jnp.dot(jnp.concatenate([hi, lo], axis=1), rhs, preferred_element_type=F32)


def _scores(q, kt, qpos, base, masked):
    z = lax.dot_general(q, kt, (((1,), (1,)), ((), ())), preferred_element_type=F32) * (HEAD_DIM ** -0.5)
    sp = jnp.log(1.0 + jnp.exp(-jnp.abs(z)))
    lk = -(jnp.maximum(z, 0.0) + sp)
    lb = z + lk
    valid = None
    if masked:
        kpos = base + lax.broadcasted_iota(jnp.int32, (1, z.shape[1]), 1)
        valid = (kpos < qpos) & (kpos >= PAD)
        lk = jnp.where(valid, lk, 0.0)
    return lk, lb, valid


def _suffix(x, tri, carry):
    wts = [_split_dot(x[:, b * BLOCK:(b + 1) * BLOCK], tri) for b in range(ATT_SUB)]
    offs = [None] * ATT_SUB
    s = carry
    for b in reversed(range(ATT_SUB)):
        offs[b] = wts[b][:, :BLOCK] + s
        s = s + wts[b][:, BLOCK:]
    return jnp.concatenate(offs, axis=1), s


def _walk_tiles(i, tq, step):
    t_top = ((i + 1) * tq - 1) // ATT_TK
    t_diag = (i * tq) // ATT_TK

    def masked(jj, carry):
        step(t_top - jj, True)
        return carry

    def plain(jj, carry):
        step(t_diag - 1 - jj, False)
        return carry

    lax.fori_loop(0, t_top - t_diag + 1, masked, 0)
    lax.fori_loop(0, jnp.maximum(t_diag - 1, 0), plain, 0)

    @pl.when(t_diag > 0)
    def _():
        step(0, True)


def _tile_base(t):
    base = t * ATT_TK
    return base if isinstance(base, int) else pl.multiple_of(base, BLOCK)


def _attn_fwd(qkv, *, name):
    l = qkv.shape[0]
    tq = ATT_TQ
    assert l % tq == 0 and l % ATT_TK == 0

    def body(q_ref, k_ref, v_ref, o_ref, o32_ref, acc_ref, r_ref):
        i = pl.program_id(1)
        acc_ref[...] = jnp.zeros_like(acc_ref)
        r_ref[...] = jnp.zeros_like(r_ref)
        q = q_ref[...]
        qpos = i * tq + lax.broadcasted_iota(jnp.int32, (tq, 1), 0)
        tri = _tri_ones()

        def step(t, masked):
            base = _tile_base(t)
            lk, lb, valid = _scores(q, k_ref[pl.ds(base, ATT_TK), :], qpos, base, masked)
            off, r_new = _suffix(lk, tri, r_ref[...])
            a = jnp.exp(lb + off)
            if masked:
                a = jnp.where(valid, a, 0.0)
            acc_ref[...] += jnp.dot(a.astype(BF16), v_ref[pl.ds(base, ATT_TK), :], preferred_element_type=F32)
            r_ref[...] = r_new

        _walk_tiles(i, tq, step)
        o_ref[...] = acc_ref[...].astype(BF16)
        o32_ref[...] = acc_ref[...]

    tile = pl.BlockSpec((tq, HEAD_DIM), lambda h, i: (i, h))
    return pl.pallas_call(
        body, name=name, grid=(HEADS, l // tq),
        in_specs=[tile,
                  pl.BlockSpec((l, HEAD_DIM), lambda h, i: (0, HEADS + h)),
                  pl.BlockSpec((l, HEAD_DIM), lambda h, i: (0, 2 * HEADS + h))],
        out_specs=[tile, tile],
        out_shape=[jax.ShapeDtypeStruct((l, HEADS * HEAD_DIM), BF16),
                   jax.ShapeDtypeStruct((l, HEADS * HEAD_DIM), F32)],
        scratch_shapes=[pltpu.VMEM((tq, HEAD_DIM), F32), pltpu.VMEM((tq, BLOCK), F32)],
        compiler_params=_params(("parallel", "arbitrary")),
    )(qkv, qkv, qkv)


def _attn_bwd(qkv, att, datt, *, name):
    l = qkv.shape[0]
    tq = ATT_TQ
    nq = l // tq
    assert l % tq == 0 and l % ATT_TK == 0

    def body(q_ref, k_ref, v_ref, o_ref, do_ref, dq_ref, dk_hbm, dv_hbm, dk_acc, dv_acc, dq_acc, r_ref, s_ref, sem):
        h = pl.program_id(0)
        i = pl.program_id(1)

        @pl.when(i == 0)
        def _():
            dk_acc[...] = jnp.zeros_like(dk_acc)
            dv_acc[...] = jnp.zeros_like(dv_acc)

        dq_acc[...] = jnp.zeros_like(dq_acc)
        r_ref[...] = jnp.zeros_like(r_ref)
        s_ref[...] = jnp.zeros_like(s_ref)
        q = q_ref[...]
        do = do_ref[...]
        ptot = jnp.sum(do.astype(F32) * o_ref[...], axis=-1, keepdims=True)
        qpos = i * tq + lax.broadcasted_iota(jnp.int32, (tq, 1), 0)
        tri = _tri_ones()

        def step(t, masked):
            base = _tile_base(t)
            kt = k_ref[pl.ds(base, ATT_TK), :]
            vt = v_ref[pl.ds(base, ATT_TK), :]
            lk, lb, valid = _scores(q, kt, qpos, base, masked)
            off, r_new = _suffix(lk, tri, r_ref[...])
            a = jnp.exp(lb + off)
            if masked:
                a = jnp.where(valid, a, 0.0)
            ab = a.astype(BF16)
            da = lax.dot_general(do, vt, (((1,), (1,)), ((), ())), preferred_element_type=F32)
            p = ab.astype(F32) * da
            poff, s_new = _suffix(p, tri, s_ref[...])
            dz = (p - jnp.exp(lb) * (ptot - poff)) * (HEAD_DIM ** -0.5)
            if masked:
                dz = jnp.where(valid, dz, 0.0)
            dzb = dz.astype(BF16)
            dq_acc[...] += jnp.dot(dzb, kt, preferred_element_type=F32)
            dk_acc[pl.ds(base, ATT_TK), :] += lax.dot_general(dzb, q, (((0,), (0,)), ((), ())),
                                                              preferred_element_type=F32)
            dv_acc[pl.ds(base, ATT_TK), :] += lax.dot_general(ab, do, (((0,), (0,)), ((), ())),
                                                              preferred_element_type=F32)
            r_ref[...] = r_new
            s_ref[...] = s_new

        _walk_tiles(i, tq, step)
        dq_ref[...] = dq_acc[...].astype(BF16)

        @pl.when(i == nq - 1)
        def _():
            ck = pltpu.make_async_copy(dk_acc, dk_hbm.at[h], sem.at[0])
            cv = pltpu.make_async_copy(dv_acc, dv_hbm.at[h], sem.at[1])
            ck.start()
            cv.start()
            ck.wait()
            cv.wait()

    tile = pl.BlockSpec((tq, HEAD_DIM), lambda h, i: (i, h))
    return pl.pallas_call(
        body, name=name, grid=(HEADS, nq),
        in_specs=[tile,
                  pl.BlockSpec((l, HEAD_DIM), lambda h, i: (0, HEADS + h)),
                  pl.BlockSpec((l, HEAD_DIM), lambda h, i: (0, 2 * HEADS + h)),
                  tile, tile],
        out_specs=[tile, pl.BlockSpec(memory_space=pl.ANY), pl.BlockSpec(memory_space=pl.ANY)],
        out_shape=[jax.ShapeDtypeStruct((l, HEADS * HEAD_DIM), BF16),
                   jax.ShapeDtypeStruct((HEADS, l, HEAD_DIM), F32), jax.ShapeDtypeStruct((HEADS, l, HEAD_DIM), F32)],
        scratch_shapes=[pltpu.VMEM((l, HEAD_DIM), F32), pltpu.VMEM((l, HEAD_DIM), F32),
                        pltpu.VMEM((tq, HEAD_DIM), F32), pltpu.VMEM((tq, BLOCK), F32), pltpu.VMEM((tq, BLOCK), F32),
                        pltpu.SemaphoreType.DMA((2,))],
        compiler_params=_params(("arbitrary", "arbitrary")),
    )(qkv, qkv, qkv, att, datt)


MIX_TM = 256


def _mix_branches(s_ref, p_ref, t_ref, g_ref, wa_ref, wb_ref, wc_ref, ba_ref, sc_ref, d):
    ya = jnp.dot(s_ref[...], wa_ref[...], preferred_element_type=F32) + ba_ref[...]
    yb0 = jnp.dot(p_ref[...], wb_ref[...], preferred_element_type=F32)
    yc = jnp.dot(t_ref[...], wc_ref[...], preferred_element_type=F32)
    g0 = _sigmoid(g_ref[:, 0:d])
    g1 = _sigmoid(g_ref[:, d:2 * d])
    g2 = _sigmoid(g_ref[:, 2 * d:3 * d])
    return ya, yb0, yc, g0, g1, g2


def _mix_specs(tm, d):
    row = lambda w: pl.BlockSpec((tm, w), lambda i: (i, 0))
    full = lambda r: pl.BlockSpec((r, d), lambda i: (0, 0))
    return [row(CONV_CH), row(POOL_CH), row(HEADS * HEAD_DIM), row(3 * d),
            full(CONV_CH), full(POOL_CH), full(HEADS * HEAD_DIM), full(1), full(1)]


def _mix_fwd(s, pooled, att, gates, wa, wb, wc, ba, scale, *, name):
    l, d = s.shape[0], wa.shape[1]
    tm = _tile(l, MIX_TM, 128)

    def body(s_ref, p_ref, t_ref, g_ref, wa_ref, wb_ref, wc_ref, ba_ref, sc_ref, o_ref):
        ya, yb0, yc, g0, g1, g2 = _mix_branches(s_ref, p_ref, t_ref, g_ref, wa_ref, wb_ref, wc_ref, ba_ref, sc_ref, d)
        o_ref[...] = (g0 * ya + g1 * (yb0 * sc_ref[...]) + g2 * yc).astype(BF16)

    return pl.pallas_call(
        body, name=name, grid=(l // tm,), in_specs=_mix_specs(tm, d),
        out_specs=pl.BlockSpec((tm, d), lambda i: (i, 0)),
        out_shape=jax.ShapeDtypeStruct((l, d), BF16),
        compiler_params=_params(("parallel",)),
    )(s, pooled, att, gates, wa, wb, wc, ba.reshape(1, d), scale.reshape(1, d))


def _mix_bwd(s, pooled, att, gates, wa, wb, wc, ba, scale, dmixed, *, name):
    l, d = s.shape[0], wa.shape[1]
    tm = _tile(l, MIX_TM, 128)

    def body(s_ref, p_ref, t_ref, g_ref, wa_ref, wb_ref, wc_ref, ba_ref, sc_ref, dm_ref,
             dg_ref, dya_ref, dyb_ref, dyc_ref, vec_ref):
        @pl.when(pl.program_id(0) == 0)
        def _():
            vec_ref[...] = jnp.zeros_like(vec_ref)

        ya, yb0, yc, g0, g1, g2 = _mix_branches(s_ref, p_ref, t_ref, g_ref, wa_ref, wb_ref, wc_ref, ba_ref, sc_ref, d)
        dm = dm_ref[...].astype(F32)
        sc = sc_ref[...]
        dg_ref[:, 0:d] = (dm * ya * g0 * (1.0 - g0)).astype(BF16)
        dg_ref[:, d:2 * d] = (dm * (yb0 * sc) * g1 * (1.0 - g1)).astype(BF16)
        dg_ref[:, 2 * d:3 * d] = (dm * yc * g2 * (1.0 - g2)).astype(BF16)
        dya = dm * g0
        dyb = dm * g1
        dya_ref[...] = dya.astype(BF16)
        dyb_ref[...] = (dyb * sc).astype(BF16)
        dyc_ref[...] = (dm * g2).astype(BF16)
        vec_ref[0:1, :] += jnp.sum(dya, axis=0, keepdims=True)
        vec_ref[1:2, :] += jnp.sum(dyb * yb0, axis=0, keepdims=True)

    row = lambda w: pl.BlockSpec((tm, w), lambda i: (i, 0))
    outs = pl.pallas_call(
        body, name=name, grid=(l // tm,), in_specs=_mix_specs(tm, d) + [row(d)],
        out_specs=[row(3 * d), row(d), row(d), row(d), pl.BlockSpec((8, d), lambda i: (0, 0))],
        out_shape=[jax.ShapeDtypeStruct((l, 3 * d), BF16), jax.ShapeDtypeStruct((l, d), BF16),
                   jax.ShapeDtypeStruct((l, d), BF16), jax.ShapeDtypeStruct((l, d), BF16),
                   jax.ShapeDtypeStruct((8, d), F32)],
        compiler_params=_params(("arbitrary",)),
    )(s, pooled, att, gates, wa, wb, wc, ba.reshape(1, d), scale.reshape(1, d), dmixed)
    return outs


FFN_TC = 512
_GELU_C = 0.7978845608028654
_GELU_A = 0.044715


def _gelu(x):
    th = jnp.tanh(_GELU_C * (x + _GELU_A * x * x * x))
    return 0.5 * x * (1.0 + th), th


def _gelu_grad(x, th):
    return 0.5 * (1.0 + th) + 0.5 * x * (1.0 - th * th) * _GELU_C * (1.0 + 3.0 * _GELU_A * x * x)


def _ffn_conv(buf, w, b, tm):
    acc = jnp.zeros((tm, w.shape[1]), F32) + b
    for k in range(FFN_K):
        off = FFN_HALO - (FFN_K - 1) + k
        acc = acc + w[k:k + 1, :] * buf[off:off + tm, :]
    return acc


def _ffn_fwd(ug, uv, wg, wv, bg, bv, *, name):
    l, f = ug.shape
    tm = _conv_tile(l)
    tc = _tile(f, FFN_TC, 128)
    ext = FFN_HALO + tm

    def body(ug_ref, uv_ref, wg_ref, wv_ref, bg_ref, bv_ref, o_ref, bufg, bufv):
        i = pl.program_id(1)
        for buf, u_ref in ((bufg, ug_ref), (bufv, uv_ref)):
            @pl.when(i == 0)
            def _():
                buf[0:FFN_HALO, :] = jnp.zeros((FFN_HALO, tc), F32)

            @pl.when(i > 0)
            def _():
                buf[0:FFN_HALO, :] = buf[tm:ext, :]

            buf[FFN_HALO:ext, :] = u_ref[...]
        gc = _ffn_conv(bufg, wg_ref[...], bg_ref[...], tm)
        vc = _ffn_conv(bufv, wv_ref[...], bv_ref[...], tm)
        o_ref[...] = (_gelu(gc)[0] * vc).astype(BF16)

    tile = pl.BlockSpec((tm, tc), lambda j, i: (i, j))
    wspec = pl.BlockSpec((FFN_K, tc), lambda j, i: (0, j))
    bspec = pl.BlockSpec((1, tc), lambda j, i: (0, j))
    return pl.pallas_call(
        body, name=name, grid=(f // tc, l // tm),
        in_specs=[tile, tile, wspec, wspec, bspec, bspec], out_specs=tile,
        out_shape=jax.ShapeDtypeStruct((l, f), BF16),
        scratch_shapes=[pltpu.VMEM((ext, tc), F32), pltpu.VMEM((ext, tc), F32)],
        compiler_params=_params(("parallel", "arbitrary")),
    )(ug, uv, wg, wv, bg.reshape(1, f), bv.reshape(1, f))


def _ffn_bwd(ug, uv, wg, wv, bg, bv, dact, *, name):
    l, f = ug.shape
    tm = _conv_tile(l)
    tc = _tile(f, FFN_TC, 128)
    nt = l // tm
    ext = FFN_HALO + tm
    hb = tm // FFN_HALO

    def body(ug_ref, uv_ref, pg_ref, pv_ref, wg_ref, wv_ref, bg_ref, bv_ref, da_ref,
             dug_ref, duv_ref, gg_ref, gv_ref, bufg, bufv, dbufg, dbufv):
        i = pl.program_id(1)
        last = i == nt - 1

        @pl.when(i == 0)
        def _():
            gg_ref[...] = jnp.zeros_like(gg_ref)
            gv_ref[...] = jnp.zeros_like(gv_ref)

        for buf, u_ref, prev_ref in ((bufg, ug_ref, pg_ref), (bufv, uv_ref, pv_ref)):
            buf[0:FFN_HALO, :] = jnp.where(last, 0.0, prev_ref[...])
            buf[FFN_HALO:ext, :] = u_ref[...]
        gc = _ffn_conv(bufg, wg_ref[...], bg_ref[...], tm)
        vc = _ffn_conv(bufv, wv_ref[...], bv_ref[...], tm)
        ge, th = _gelu(gc)
        da = da_ref[...].astype(F32)
        dgc = da * vc * _gelu_grad(gc, th)
        dvc = da * ge
        for dbuf, buf, dc, w_ref, du_ref, gp_ref in ((dbufg, bufg, dgc, wg_ref, dug_ref, gg_ref),
                                                      (dbufv, bufv, dvc, wv_ref, duv_ref, gv_ref)):
            @pl.when(i == 0)
            def _():
                dbuf[tm:ext, :] = jnp.zeros((FFN_HALO, tc), F32)

            @pl.when(i > 0)
            def _():
                dbuf[tm:ext, :] = dbuf[0:FFN_HALO, :]

            dbuf[0:tm, :] = dc
            w = w_ref[...]
            du = jnp.zeros((tm, tc), F32)
            for k in range(FFN_K):
                off = FFN_K - 1 - k
                du = du + w[k:k + 1, :] * dbuf[off:off + tm, :]
                uoff = FFN_HALO - (FFN_K - 1) + k
                gp_ref[k:k + 1, :] += jnp.sum(dc * buf[uoff:uoff + tm, :], axis=0, keepdims=True)
            gp_ref[3:4, :] += jnp.sum(dc, axis=0, keepdims=True)
            du_ref[...] = du.astype(BF16)

    tile = pl.BlockSpec((tm, tc), lambda j, i: (nt - 1 - i, j))
    prev = pl.BlockSpec((FFN_HALO, tc), lambda j, i: (jnp.maximum((nt - 1 - i) * hb - 1, 0), j))
    wspec = pl.BlockSpec((FFN_K, tc), lambda j, i: (0, j))
    bspec = pl.BlockSpec((1, tc), lambda j, i: (0, j))
    gspec = pl.BlockSpec((8, tc), lambda j, i: (0, j))
    return pl.pallas_call(
        body, name=name, grid=(f // tc, nt),
        in_specs=[tile, tile, prev, prev, wspec, wspec, bspec, bspec, tile],
        out_specs=[tile, tile, gspec, gspec],
        out_shape=[jax.ShapeDtypeStruct((l, f), BF16), jax.ShapeDtypeStruct((l, f), BF16),
                   jax.ShapeDtypeStruct((8, f), F32), jax.ShapeDtypeStruct((8, f), F32)],
        scratch_shapes=[pltpu.VMEM((ext, tc), F32), pltpu.VMEM((ext, tc), F32),
                        pltpu.VMEM((ext, tc), F32), pltpu.VMEM((ext, tc), F32)],
        compiler_params=_params(("parallel", "arbitrary")),
    )(ug, uv, ug, uv, wg, wv, bg.reshape(1, f), bv.reshape(1, f), dact)


def _adamw(parts, w, m, v, *, name):
    r = w.shape[0]
    tr = _tile(r, 1024, 8)
    c1 = 1.0 / (1.0 - ADAM_B1 ** ADAM_STEP)
    c2 = 1.0 / (1.0 - ADAM_B2 ** ADAM_STEP)

    def body(p_ref, w_ref, m_ref, v_ref, g_ref, d_ref, nm_ref, nv_ref):
        g = p_ref[0]
        for k in range(1, N_DEV):
            g = g + p_ref[k]
        nm = ADAM_B1 * m_ref[...] + (1.0 - ADAM_B1) * g
        nv = ADAM_B2 * v_ref[...] + (1.0 - ADAM_B2) * (g * g)
        g_ref[...] = g
        nm_ref[...] = nm
        nv_ref[...] = nv
        d_ref[...] = -ADAM_LR * ((nm * c1) / (jnp.sqrt(nv * c2) + ADAM_EPS) + ADAM_WD * w_ref[...])

    tile = pl.BlockSpec((tr, 128), lambda i: (i, 0))
    return pl.pallas_call(
        body, name=name, grid=(r // tr,),
        in_specs=[pl.BlockSpec((N_DEV, tr, 128), lambda i: (0, i, 0)), tile, tile, tile],
        out_specs=[tile, tile, tile, tile],
        out_shape=[jax.ShapeDtypeStruct((r, 128), F32)] * 4,
        compiler_params=_params(("parallel",)),
    )(parts, w, m, v)


def _place():
    return lax.axis_index("x"), lax.axis_index("y"), lax.axis_index("c")


def _all_gather(x, *, name):
    def body(x_ref, out_ref, send_sems, recv_sems, local_sem):
        xx, yy, cc = _place()
        me, sibling = (xx, yy, cc), (xx, yy, 1 - cc)
        chips = [(1 - xx, yy), (xx, 1 - yy), (1 - xx, 1 - yy)]

        def slot(px, py, pc):
            return out_ref.at[4 * px + 2 * py + pc]

        def copy(k, block, to, src=None):
            return pltpu.make_async_remote_copy(
                src_ref=slot(*block) if src is None else src, dst_ref=slot(*block),
                send_sem=send_sems.at[k], recv_sem=recv_sems.at[k], device_id=to, device_id_type=MESH)

        mine = pltpu.make_async_copy(x_ref, slot(*me), local_sem)
        mine.start()
        first = [copy(0, me, sibling, src=x_ref)]
        first += [copy(1 + j, me, (*chip, cc), src=x_ref) for j, chip in enumerate(chips)]
        for cp in first:
            cp.start()
        passed = [copy(4 + j, (*chip, cc), sibling) for j, chip in enumerate(chips)]
        for j, chip in enumerate(chips):
            copy(1 + j, (*chip, cc), me).wait_recv()
            passed[j].start()
        copy(0, sibling, me).wait_recv()
        for j, chip in enumerate(chips):
            copy(4 + j, (*chip, 1 - cc), me).wait_recv()
        for cp in first + passed:
            cp.wait_send()
        mine.wait()

    return pl.pallas_call(
        body, name=name,
        in_specs=[pl.BlockSpec(memory_space=pl.ANY)], out_specs=pl.BlockSpec(memory_space=pl.ANY),
        out_shape=jax.ShapeDtypeStruct((N_DEV,) + x.shape, x.dtype),
        scratch_shapes=[pltpu.SemaphoreType.DMA((7,)), pltpu.SemaphoreType.DMA((7,)), pltpu.SemaphoreType.DMA],
    )(x)


def _all_to_all(send, *, name):
    def body(s_ref, r_ref, send_sems, recv_sems, local_sem):
        xx, yy, cc = _place()
        me = 4 * xx + 2 * yy + cc
        local = pltpu.make_async_copy(s_ref.at[me], r_ref.at[me], local_sem)
        local.start()
        copies = []
        for m in range(1, N_DEV):
            px = 1 - xx if m & 4 else xx
            py = 1 - yy if m & 2 else yy
            pc = 1 - cc if m & 1 else cc
            copies.append(pltpu.make_async_remote_copy(
                src_ref=s_ref.at[4 * px + 2 * py + pc], dst_ref=r_ref.at[me],
                send_sem=send_sems.at[m - 1], recv_sem=recv_sems.at[m - 1],
                device_id=(px, py, pc), device_id_type=MESH))
        for cp in copies:
            cp.start()
        for cp in copies:
            cp.wait_recv()
        for cp in copies:
            cp.wait_send()
        local.wait()

    return pl.pallas_call(
        body, name=name,
        in_specs=[pl.BlockSpec(memory_space=pl.ANY)], out_specs=pl.BlockSpec(memory_space=pl.ANY),
        out_shape=jax.ShapeDtypeStruct(send.shape, send.dtype),
        scratch_shapes=[pltpu.SemaphoreType.DMA((7,)), pltpu.SemaphoreType.DMA((7,)), pltpu.SemaphoreType.DMA],
    )(send)


def _pack(arrays, dtype):
    parts = []
    for a in arrays:
        flat = a.reshape(-1).astype(dtype)
        parts.append(jnp.pad(flat, (0, (-flat.size) % PACK_ALIGN)))
    return jnp.concatenate(parts).reshape(-1, 128)


def _pack_pieces(arrays, dtype):
    parts = []
    for a in arrays:
        flat = a.reshape(N_DEV, -1).astype(dtype)
        parts.append(jnp.pad(flat, ((0, 0), (0, (-flat.shape[1]) % PACK_ALIGN))))
    return jnp.concatenate(parts, axis=1).reshape(N_DEV, -1, 128)


def _unpack(buf, shapes, lead=()):
    flat = buf.reshape(lead + (-1,))
    out, off = [], 0
    for shp in shapes:
        size = 1
        for s in shp:
            size *= s
        out.append(flat[..., off:off + size].reshape(lead + tuple(shp)))
        off += size + (-size) % PACK_ALIGN
    return out


def _unshard(g, axis):
    g = jnp.moveaxis(g, 0, axis)
    shp = list(g.shape)
    return g.reshape(shp[:axis] + [shp[axis] * shp[axis + 1]] + shp[axis + 2:])


def _pieces(full, axis):
    shp = list(full.shape)
    g = full.reshape(shp[:axis] + [N_DEV, shp[axis] // N_DEV] + shp[axis + 1:])
    return jnp.moveaxis(g, axis, 0)


SHARDED = (("meta", 1), ("w_in", 2), ("conv_dw_w", 2), ("w_conv_out", 2), ("w_pool_grp", 3), ("w_attn_out", 2),
           ("w_o", 1), ("w_up", 2), ("ffn_dw_w", 2), ("w_down", 1))
MATRICES = ("w_in", "w_conv_out", "w_pool_grp", "w_attn_out", "w_o", "w_up", "w_down")
REPLICATED = ("norm1", "conv_dw_b", "conv_ln_g", "conv_ln_b", "b_conv_out", "pool_scale", "norm2", "ffn_dw_b",
              "final_norm")
WEIGHTS = ("meta", "norm1", "w_in", "conv_dw_w", "conv_dw_b", "conv_ln_g", "conv_ln_b", "w_conv_out", "b_conv_out",
           "w_pool_grp", "pool_scale", "w_attn_out", "w_o", "norm2", "w_up", "ffn_dw_w", "ffn_dw_b", "w_down",
           "final_norm")


def _block_diag(w_grp):
    g, gc, od = w_grp.shape
    out = jnp.zeros((g * gc, g * od), w_grp.dtype)
    for i in range(g):
        out = out.at[i * gc:(i + 1) * gc, i * od:(i + 1) * od].set(w_grp[i])
    return out


def _block_diag_grad(gw, g):
    gc, od = gw.shape[0] // g, gw.shape[1] // g
    return jnp.stack([gw[i * gc:(i + 1) * gc, i * od:(i + 1) * od] for i in range(g)])


C_CONV = 2 * CONV_CH
C_POOL = C_CONV + POOL_CH
C_ATT = HEADS * HEAD_DIM
C_QKV = C_POOL + 3 * C_ATT


def _layer_fwd(h, p, tag):
    d = h.shape[1]
    w_in = p["w_in"]
    hn = _rms_fwd(h, p["norm1"], name=f"rms1_{tag}")
    pa = _mm(hn, w_in[:, :C_POOL], out_dtype=F32, name=f"proj_a_{tag}")
    qkv = _mm(hn, w_in[:, C_POOL:C_QKV], out_dtype=BF16, name=f"proj_qkv_{tag}")
    gates = _mm(hn, w_in[:, C_QKV:], out_dtype=F32, name=f"proj_g_{tag}")
    s = _conv_fwd(pa, p["conv_dw_w"], p["conv_dw_b"], p["conv_ln_g"], p["conv_ln_b"], name=f"conv_{tag}")
    pooled = _pool_fwd(pa, name=f"pool_{tag}")
    att, att32 = _attn_fwd(qkv, name=f"attn_{tag}")
    wb = _block_diag(p["w_pool_grp"])
    mixed = _mix_fwd(s, pooled, att, gates, p["w_conv_out"], wb, p["w_attn_out"], p["b_conv_out"], p["pool_scale"],
                     name=f"mix_{tag}")
    h1 = _mm(mixed, p["w_o"], out_dtype=F32, res=h, mask_rows=True, name=f"wo_{tag}")
    hn2 = _rms_fwd(h1, p["norm2"], name=f"rms2_{tag}")
    f = p["w_up"].shape[1] // 2
    ug = _mm(hn2, p["w_up"][:, :f], out_dtype=F32, name=f"up_g_{tag}")
    uv = _mm(hn2, p["w_up"][:, f:], out_dtype=F32, name=f"up_v_{tag}")
    act = _ffn_fwd(ug, uv, p["ffn_dw_w"][:, :f], p["ffn_dw_w"][:, f:], p["ffn_dw_b"][:f], p["ffn_dw_b"][f:],
                   name=f"ffn_{tag}")
    h2 = _mm(act, p["w_down"], out_dtype=F32, res=h1, mask_rows=True, name=f"down_{tag}")
    saved = dict(h=h, hn=hn, pa=pa, qkv=qkv, gates=gates, s=s, pooled=pooled, att=att, att32=att32, wb=wb, mixed=mixed,
                 h1=h1,
                 hn2=hn2, ug=ug, uv=uv, act=act)
    return h2, saved


def _layer_bwd(dh2, p, sv, tag):
    g = {}
    f = p["w_up"].shape[1] // 2
    dact = _mm(dh2, p["w_down"].T, out_dtype=F32, name=f"b_down_{tag}")
    g["w_down"] = _mm_tn(sv["act"], dh2, name=f"g_down_{tag}")
    dug, duv, gpg, gpv = _ffn_bwd(sv["ug"], sv["uv"], p["ffn_dw_w"][:, :f], p["ffn_dw_w"][:, f:], p["ffn_dw_b"][:f],
                                  p["ffn_dw_b"][f:], dact, name=f"b_ffn_{tag}")
    g["ffn_dw_w"] = jnp.concatenate([gpg[0:FFN_K], gpv[0:FFN_K]], axis=1)
    g["ffn_dw_b"] = jnp.concatenate([gpg[FFN_K], gpv[FFN_K]])
    w_up_t = p["w_up"].T
    dhn2 = _mm(dug, w_up_t[:f], out_dtype=F32, name=f"b_up_g_{tag}")
    dhn2 = _mm(duv, w_up_t[f:], out_dtype=F32, res=dhn2, name=f"b_up_v_{tag}")
    g["w_up"] = jnp.concatenate([_mm_tn(sv["hn2"], dug, name=f"g_up_g_{tag}"),
                                 _mm_tn(sv["hn2"], duv, name=f"g_up_v_{tag}")], axis=1)
    dh1, g["norm2"] = _rms_bwd(sv["h1"], p["norm2"], dhn2, dh2, name=f"b_rms2_{tag}")
    dmixed = _mm(dh1, p["w_o"].T, out_dtype=BF16, name=f"b_wo_{tag}")
    g["w_o"] = _mm_tn(sv["mixed"], dh1, name=f"g_wo_{tag}")
    dgates, dya, dyb, dyc, vec = _mix_bwd(sv["s"], sv["pooled"], sv["att"], sv["gates"], p["w_conv_out"], sv["wb"],
                                          p["w_attn_out"], p["b_conv_out"], p["pool_scale"], dmixed,
                                          name=f"b_mix_{tag}")
    g["b_conv_out"], g["pool_scale"] = vec[0], vec[1]
    ds = _mm(dya, p["w_conv_out"].T, out_dtype=F32, name=f"b_conv_out_{tag}")
    dpooled = _mm(dyb, sv["wb"].T, out_dtype=F32, name=f"b_pool_out_{tag}")
    datt = _mm(dyc, p["w_attn_out"].T, out_dtype=BF16, name=f"b_attn_out_{tag}")
    g["w_conv_out"] = _mm_tn(sv["s"], dya, name=f"g_conv_out_{tag}")
    g["w_pool_grp"] = _block_diag_grad(_mm_tn(sv["pooled"], dyb, name=f"g_pool_{tag}"), len(POOL_WINDOWS))
    g["w_attn_out"] = _mm_tn(sv["att"], dyc, name=f"g_attn_out_{tag}")
    dc, gp = _conv_bwd_ln(sv["pa"], ds, p["conv_dw_w"], p["conv_dw_b"], p["conv_ln_g"], p["conv_ln_b"],
                          name=f"b_conv_ln_{tag}")
    g["conv_dw_w"], g["conv_dw_b"], g["conv_ln_g"], g["conv_ln_b"] = gp[0:CONV_K], gp[32], gp[33], gp[34]
    dconv = _conv_bwd_in(sv["pa"], dc, p["conv_dw_w"], name=f"b_conv_in_{tag}")
    dp = _pool_bwd(dpooled, name=f"b_pool_{tag}")
    dq, dk, dv = _attn_bwd(sv["qkv"], sv["att32"], datt, name=f"b_attn_{tag}")
    dk = jnp.moveaxis(dk, 0, 1).reshape(dq.shape).astype(BF16)
    dv = jnp.moveaxis(dv, 0, 1).reshape(dq.shape).astype(BF16)
    w_in_t = p["w_in"].T
    cols = [(dconv, 0, C_CONV), (dp, C_CONV, C_POOL), (dq, C_POOL, C_POOL + C_ATT),
            (dk, C_POOL + C_ATT, C_POOL + 2 * C_ATT), (dv, C_POOL + 2 * C_ATT, C_QKV),
            (dgates, C_QKV, w_in_t.shape[0])]
    dhn, gw = None, []
    for n, (dcol, lo, hi) in enumerate(cols):
        dhn = _mm(dcol, w_in_t[lo:hi], out_dtype=F32, res=dhn, name=f"b_in{n}_{tag}")
        gw.append(_mm_tn(sv["hn"], dcol, name=f"g_in{n}_{tag}"))
    g["w_in"] = jnp.concatenate(gw, axis=1)
    dh, g["norm1"] = _rms_bwd(sv["h"], p["norm1"], dhn, dh1, name=f"b_rms1_{tag}")
    return dh, g


def kernel(x, meta, norm1, w_in, conv_dw_w, conv_dw_b, conv_ln_g, conv_ln_b, w_conv_out, b_conv_out, w_pool_grp, pool_scale, w_attn_out, w_o, norm2, w_up, ffn_dw_w, ffn_dw_b, w_down, final_norm, loss_target, m_meta, m_norm1, m_w_in, m_conv_dw_w, m_conv_dw_b, m_conv_ln_g, m_conv_ln_b, m_w_conv_out, m_b_conv_out, m_w_pool_grp, m_pool_scale, m_w_attn_out, m_w_o, m_norm2, m_w_up, m_ffn_dw_w, m_ffn_dw_b, m_w_down, m_final_norm, v_meta, v_norm1, v_w_in, v_conv_dw_w, v_conv_dw_b, v_conv_ln_g, v_conv_ln_b, v_w_conv_out, v_b_conv_out, v_w_pool_grp, v_pool_scale, v_w_attn_out, v_w_o, v_norm2, v_w_up, v_ffn_dw_w, v_ffn_dw_b, v_w_down, v_final_norm):
    given = dict(meta=meta, norm1=norm1, w_in=w_in, conv_dw_w=conv_dw_w, conv_dw_b=conv_dw_b, conv_ln_g=conv_ln_g, conv_ln_b=conv_ln_b, w_conv_out=w_conv_out, b_conv_out=b_conv_out, w_pool_grp=w_pool_grp, pool_scale=pool_scale, w_attn_out=w_attn_out, w_o=w_o, norm2=norm2, w_up=w_up, ffn_dw_w=ffn_dw_w, ffn_dw_b=ffn_dw_b, w_down=w_down, final_norm=final_norm)
    mom_m = dict(meta=m_meta, norm1=m_norm1, w_in=m_w_in, conv_dw_w=m_conv_dw_w, conv_dw_b=m_conv_dw_b, conv_ln_g=m_conv_ln_g, conv_ln_b=m_conv_ln_b, w_conv_out=m_w_conv_out, b_conv_out=m_b_conv_out, w_pool_grp=m_w_pool_grp, pool_scale=m_pool_scale, w_attn_out=m_w_attn_out, w_o=m_w_o, norm2=m_norm2, w_up=m_w_up, ffn_dw_w=m_ffn_dw_w, ffn_dw_b=m_ffn_dw_b, w_down=m_w_down, final_norm=m_final_norm)
    mom_v = dict(meta=v_meta, norm1=v_norm1, w_in=v_w_in, conv_dw_w=v_conv_dw_w, conv_dw_b=v_conv_dw_b, conv_ln_g=v_conv_ln_g, conv_ln_b=v_conv_ln_b, w_conv_out=v_w_conv_out, b_conv_out=v_b_conv_out, w_pool_grp=v_w_pool_grp, pool_scale=v_pool_scale, w_attn_out=v_w_attn_out, w_o=v_w_o, norm2=v_norm2, w_up=v_w_up, ffn_dw_w=v_ffn_dw_w, ffn_dw_b=v_ffn_dw_b, w_down=v_w_down, final_norm=v_final_norm)
    sharded_axis = dict(SHARDED)
    vectors = [n for n, _ in SHARDED if n not in MATRICES]
    depth = norm1.shape[0]

    got_mat = _all_gather(_pack([given[n] for n in MATRICES], BF16), name="gather_matrices")
    got_vec = _all_gather(_pack([given[n] for n in vectors], F32), name="gather_vectors")
    full = {n: given[n] for n in REPLICATED}
    for n, a in zip(MATRICES, _unpack(got_mat, [given[n].shape for n in MATRICES], (N_DEV,))):
        full[n] = _unshard(a, sharded_axis[n])
    for n, a in zip(vectors, _unpack(got_vec, [given[n].shape for n in vectors], (N_DEV,))):
        full[n] = _unshard(a, sharded_axis[n])

    xs = x[0]
    d = xs.shape[1]
    h = jnp.concatenate([jnp.zeros((PAD, d), F32), full["meta"], xs], axis=0)
    layers, saved = [], []
    for i in range(depth):
        p = {n: full[n][i] for n in full if n not in ("meta", "final_norm")}
        layers.append(p)
        h, sv = _layer_fwd(h, p, f"l{i}")
        saved.append(sv)
    loss_part, dh, g_final = _loss_head(h, full["final_norm"], loss_target[0], name="loss_head")

    grads = [None] * depth
    for i in reversed(range(depth)):
        dh, grads[i] = _layer_bwd(dh, layers[i], saved[i], f"l{i}")
    full_grad = {n: jnp.stack([grads[i][n] for i in range(depth)]) for n in grads[0]}
    full_grad["meta"] = dh[PAD:FRONT]
    full_grad["final_norm"] = g_final
    grad_x = dh[FRONT:][None]

    names = [n for n, _ in SHARDED]
    recv = _all_to_all(_pack_pieces([_pieces(full_grad[n], sharded_axis[n]) for n in names], F32), name="scatter_grads")
    rep_shapes = [given[n].shape for n in REPLICATED] + [(1,)]
    rep_parts = _all_gather(_pack([full_grad[n] for n in REPLICATED] + [loss_part.reshape(1)], F32),
                            name="gather_partials")

    out = {}
    shapes = [given[n].shape for n in names]
    res = _adamw(recv, _pack([given[n] for n in names], F32), _pack([mom_m[n] for n in names], F32),
                 _pack([mom_v[n] for n in names], F32), name="adamw_sharded")
    for kind, buf in zip(("grad", "delta", "new_m", "new_v"), res):
        for n, a in zip(names, _unpack(buf, shapes)):
            out[kind, n] = a
    rep_w = [given[n] for n in REPLICATED] + [jnp.zeros((1,), F32)]
    rep_m = [mom_m[n] for n in REPLICATED] + [jnp.zeros((1,), F32)]
    rep_v = [mom_v[n] for n in REPLICATED] + [jnp.ones((1,), F32)]
    res = _adamw(rep_parts, _pack(rep_w, F32), _pack(rep_m, F32), _pack(rep_v, F32), name="adamw_replicated")
    for kind, buf in zip(("grad", "delta", "new_m", "new_v"), res):
        for n, a in zip(list(REPLICATED) + ["loss"], _unpack(buf, rep_shapes)):
            out[kind, n] = a
    loss = out["grad", "loss"][0]
    return (loss, grad_x, *[out["grad", n] for n in WEIGHTS], *[out["delta", n] for n in WEIGHTS],
            *[out["new_m", n] for n in WEIGHTS], *[out["new_v", n] for n in WEIGHTS])
```

```python
import functools

import jax
import jax.numpy as jnp
from jax import lax
from jax.experimental import pallas as pl
from jax.experimental.pallas import tpu as pltpu

F32 = jnp.float32
BF16 = jnp.bfloat16
MESH = pl.DeviceIdType.MESH

N_DEV = 8
N_META = 16
BLOCK = 128
PAD = 240
FRONT = PAD + N_META
HEADS = 4
HEAD_DIM = 128
CONV_CH = 256
CONV_K = 31
POOL_CH = 256
POOL_WINDOWS = (2, 4, 8, 16)
FFN_K = 3
EPS = 1e-6
ADAM_LR, ADAM_B1, ADAM_B2, ADAM_EPS, ADAM_WD, ADAM_STEP = 0.001, 0.9, 0.999, 1e-08, 0.01, 10

VMEM_LIMIT = 56 * 1024 * 1024
CONV_HALO = 32
POOL_HALO = 16
FFN_HALO = 8
PACK_ALIGN = 8 * 128


def _tile(n, cap, unit):
    if n <= cap:
        return n
    best = None
    t = unit
    while t <= cap:
        if n % t == 0:
            best = t
        t += unit
    assert best is not None, (n, cap, unit)
    return best


def _params(sem):
    return pltpu.CompilerParams(dimension_semantics=sem, vmem_limit_bytes=VMEM_LIMIT)


def _sigmoid(x):
    return 1.0 / (1.0 + jnp.exp(-x))


def _mm(a, b, *, out_dtype, name, res=None, col_scale=None, mask_rows=False, tm_cap=640, tn_cap=768, tk_cap=1024):
    m, k = a.shape
    k2, n = b.shape
    assert k == k2
    tm, tn, tk = _tile(m, tm_cap, 128), _tile(n, tn_cap, 128), _tile(k, tk_cap, 128)
    nk = k // tk

    def body(*refs):
        refs = list(refs)
        a_ref, b_ref = refs[:2]
        o_ref, acc = refs[-2:]
        r_ref = refs[2] if res is not None else None
        c_ref = refs[-3] if col_scale is not None else None
        kk = pl.program_id(2)
        row0 = pl.program_id(0) * tm

        @pl.when(kk == 0)
        def _():
            acc[...] = jnp.zeros_like(acc)

        acc[...] += jnp.dot(a_ref[...].astype(BF16), b_ref[...].astype(BF16), preferred_element_type=F32)

        @pl.when(kk == nk - 1)
        def _():
            y = acc[...]
            if col_scale is not None:
                y = y * c_ref[...]
            if res is not None:
                y = y + r_ref[...].astype(F32)
            if mask_rows:
                row = row0 + lax.broadcasted_iota(jnp.int32, (tm, 1), 0)
                y = jnp.where(row >= PAD, y, 0.0)
            o_ref[...] = y.astype(out_dtype)

    in_specs = [pl.BlockSpec((tm, tk), lambda i, j, kk: (i, kk)), pl.BlockSpec((tk, tn), lambda i, j, kk: (kk, j))]
    args = [a, b]
    if res is not None:
        in_specs.append(pl.BlockSpec((tm, tn), lambda i, j, kk: (i, j)))
        args.append(res)
    if col_scale is not None:
        in_specs.append(pl.BlockSpec((1, tn), lambda i, j, kk: (0, j)))
        args.append(col_scale.reshape(1, n))
    return pl.pallas_call(
        body, name=name, grid=(m // tm, n // tn, nk),
        in_specs=in_specs, out_specs=pl.BlockSpec((tm, tn), lambda i, j, kk: (i, j)),
        out_shape=jax.ShapeDtypeStruct((m, n), out_dtype),
        scratch_shapes=[pltpu.VMEM((tm, tn), F32)],
        compiler_params=_params(("parallel", "parallel", "arbitrary")),
    )(*args)


def _mm_tn(a, b, *, name, t1_cap=512, tn_cap=1024, tl_cap=640):
    l, k1 = a.shape
    l2, n = b.shape
    assert l == l2
    t1, tn, tl = _tile(k1, t1_cap, 128), _tile(n, tn_cap, 128), _tile(l, tl_cap, 128)

    def body(a_ref, b_ref, o_ref):
        @pl.when(pl.program_id(2) == 0)
        def _():
            o_ref[...] = jnp.zeros_like(o_ref)

        o_ref[...] += lax.dot_general(a_ref[...].astype(BF16), b_ref[...].astype(BF16),
                                      (((0,), (0,)), ((), ())), preferred_element_type=F32)

    return pl.pallas_call(
        body, name=name, grid=(k1 // t1, n // tn, l // tl),
        in_specs=[pl.BlockSpec((tl, t1), lambda i, j, ll: (ll, i)), pl.BlockSpec((tl, tn), lambda i, j, ll: (ll, j))],
        out_specs=pl.BlockSpec((t1, tn), lambda i, j, ll: (i, j)),
        out_shape=jax.ShapeDtypeStruct((k1, n), F32),
        compiler_params=_params(("parallel", "parallel", "arbitrary")),
    )(a, b)


def _rms_fwd(x, g, *, name):
    l, d = x.shape
    tm = _tile(l, 640, 128)

    def body(x_ref, g_ref, o_ref):
        xv = x_ref[...]
        r = lax.rsqrt(jnp.mean(xv * xv, axis=-1, keepdims=True) + EPS)
        o_ref[...] = (xv * r * g_ref[...]).astype(BF16)

    return pl.pallas_call(
        body, name=name, grid=(l // tm,),
        in_specs=[pl.BlockSpec((tm, d), lambda i: (i, 0)), pl.BlockSpec((1, d), lambda i: (0, 0))],
        out_specs=pl.BlockSpec((tm, d), lambda i: (i, 0)),
        out_shape=jax.ShapeDtypeStruct((l, d), BF16),
        compiler_params=_params(("parallel",)),
    )(x, g.reshape(1, d))


def _rms_bwd(x, g, dy, dres, *, name):
    l, d = x.shape
    tm = _tile(l, 640, 128)

    def body(x_ref, g_ref, dy_ref, dr_ref, dx_ref, dg_ref):
        i = pl.program_id(0)

        @pl.when(i == 0)
        def _():
            dg_ref[...] = jnp.zeros_like(dg_ref)

        xv = x_ref[...]
        r = lax.rsqrt(jnp.mean(xv * xv, axis=-1, keepdims=True) + EPS)
        xh = xv * r
        dyv = dy_ref[...].astype(F32)
        dxh = dyv * g_ref[...]
        dx = r * (dxh - xh * jnp.mean(dxh * xh, axis=-1, keepdims=True)) + dr_ref[...]
        row = i * tm + lax.broadcasted_iota(jnp.int32, (tm, 1), 0)
        dx_ref[...] = jnp.where(row >= PAD, dx, 0.0)
        dg_ref[0:1, :] += jnp.sum(dyv * xh, axis=0, keepdims=True)

    dx, dg = pl.pallas_call(
        body, name=name, grid=(l // tm,),
        in_specs=[pl.BlockSpec((tm, d), lambda i: (i, 0)), pl.BlockSpec((1, d), lambda i: (0, 0)),
                  pl.BlockSpec((tm, d), lambda i: (i, 0)), pl.BlockSpec((tm, d), lambda i: (i, 0))],
        out_specs=[pl.BlockSpec((tm, d), lambda i: (i, 0)), pl.BlockSpec((8, d), lambda i: (0, 0))],
        out_shape=[jax.ShapeDtypeStruct((l, d), F32), jax.ShapeDtypeStruct((8, d), F32)],
        compiler_params=_params(("arbitrary",)),
    )(x, g.reshape(1, d), dy, dres)
    return dx, dg[0]


def _loss_head(h, g, target, *, name):
    l, d = h.shape
    tm = FRONT
    assert l % tm == 0 and target.shape[0] == l - tm

    def body(h_ref, g_ref, t_ref, dh_ref, loss_ref, dg_ref):
        i = pl.program_id(0)

        @pl.when(i == 0)
        def _():
            loss_ref[...] = jnp.zeros_like(loss_ref)
            dg_ref[...] = jnp.zeros_like(dg_ref)
            dh_ref[...] = jnp.zeros_like(dh_ref)

        @pl.when(i > 0)
        def _():
            xv = h_ref[...]
            r = lax.rsqrt(jnp.mean(xv * xv, axis=-1, keepdims=True) + EPS)
            xh = xv * r
            gv = g_ref[...]
            err = xh * gv - t_ref[...]
            loss_ref[...] += 0.5 * jnp.sum(jnp.mean(err * err, axis=-1, keepdims=True))
            dy = err * (1.0 / d)
            dxh = dy * gv
            dh_ref[...] = r * (dxh - xh * jnp.mean(dxh * xh, axis=-1, keepdims=True))
            dg_ref[0:1, :] += jnp.sum(dy * xh, axis=0, keepdims=True)

    dh, loss, dg = pl.pallas_call(
        body, name=name, grid=(l // tm,),
        in_specs=[pl.BlockSpec((tm, d), lambda i: (i, 0)), pl.BlockSpec((1, d), lambda i: (0, 0)),
                  pl.BlockSpec((tm, d), lambda i: (jnp.maximum(i - 1, 0), 0))],
        out_specs=[pl.BlockSpec((tm, d), lambda i: (i, 0)), pl.BlockSpec((8, 128), lambda i: (0, 0)),
                   pl.BlockSpec((8, d), lambda i: (0, 0))],
        out_shape=[jax.ShapeDtypeStruct((l, d), F32), jax.ShapeDtypeStruct((8, 128), F32),
                   jax.ShapeDtypeStruct((8, d), F32)],
        compiler_params=_params(("arbitrary",)),
    )(h, g.reshape(1, d), target)
    return loss[0, 0], dh, dg[0]


def _conv_tile(l):
    return _tile(l, 640, 128)


def _conv_core(a, gt, buf, dw_w, dw_b, first):
    tm = a.shape[0]

    @pl.when(first)
    def _():
        buf[0:CONV_HALO, :] = jnp.zeros((CONV_HALO, CONV_CH), F32)

    @pl.when(jnp.logical_not(first))
    def _():
        buf[0:CONV_HALO, :] = buf[tm:tm + CONV_HALO, :]

    sg = _sigmoid(gt)
    buf[CONV_HALO:CONV_HALO + tm, :] = a * sg
    c = jnp.zeros((tm, CONV_CH), F32) + dw_b
    for k in range(CONV_K):
        off = CONV_HALO - (CONV_K - 1) + k
        c = c + dw_w[k:k + 1, :] * buf[off:off + tm, :]
    return c, sg


def _layer_norm(c, ln_g, ln_b):
    mu = jnp.mean(c, axis=-1, keepdims=True)
    xc = c - mu
    r = lax.rsqrt(jnp.mean(xc * xc, axis=-1, keepdims=True) + EPS)
    xh = xc * r
    return xh, r, xh * ln_g + ln_b


def _conv_fwd(pa, dw_w, dw_b, ln_g, ln_b, *, name):
    l = pa.shape[0]
    tm = _conv_tile(l)

    def body(a_ref, gt_ref, w_ref, b_ref, g_ref, bb_ref, o_ref, buf):
        c, _ = _conv_core(a_ref[...], gt_ref[...], buf, w_ref[...], b_ref[...], pl.program_id(0) == 0)
        _, _, y = _layer_norm(c, g_ref[...], bb_ref[...])
        o_ref[...] = (y * _sigmoid(y)).astype(BF16)

    vec = pl.BlockSpec((1, CONV_CH), lambda i: (0, 0))
    return pl.pallas_call(
        body, name=name, grid=(l // tm,),
        in_specs=[pl.BlockSpec((tm, CONV_CH), lambda i: (i, 0)), pl.BlockSpec((tm, CONV_CH), lambda i: (i, 1)),
                  pl.BlockSpec((CONV_K, CONV_CH), lambda i: (0, 0)), vec, vec, vec],
        out_specs=pl.BlockSpec((tm, CONV_CH), lambda i: (i, 0)),
        out_shape=jax.ShapeDtypeStruct((l, CONV_CH), BF16),
        scratch_shapes=[pltpu.VMEM((CONV_HALO + tm, CONV_CH), F32)],
        compiler_params=_params(("arbitrary",)),
    )(pa, pa, dw_w, dw_b.reshape(1, -1), ln_g.reshape(1, -1), ln_b.reshape(1, -1))


def _conv_bwd_ln(pa, ds, dw_w, dw_b, ln_g, ln_b, *, name):
    l = pa.shape[0]
    tm = _conv_tile(l)

    def body(a_ref, gt_ref, ds_ref, w_ref, b_ref, g_ref, bb_ref, dc_ref, gp_ref, buf):
        i = pl.program_id(0)

        @pl.when(i == 0)
        def _():
            gp_ref[...] = jnp.zeros_like(gp_ref)

        c, _ = _conv_core(a_ref[...], gt_ref[...], buf, w_ref[...], b_ref[...], i == 0)
        xh, r, y = _layer_norm(c, g_ref[...], bb_ref[...])
        sy = _sigmoid(y)
        dy = ds_ref[...] * (sy * (1.0 + y * (1.0 - sy)))
        dxh = dy * g_ref[...]
        dc = r * (dxh - jnp.mean(dxh, axis=-1, keepdims=True) - xh * jnp.mean(dxh * xh, axis=-1, keepdims=True))
        dc_ref[...] = dc
        for k in range(CONV_K):
            off = CONV_HALO - (CONV_K - 1) + k
            gp_ref[k:k + 1, :] += jnp.sum(dc * buf[off:off + tm, :], axis=0, keepdims=True)
        gp_ref[32:33, :] += jnp.sum(dc, axis=0, keepdims=True)
        gp_ref[33:34, :] += jnp.sum(dy * xh, axis=0, keepdims=True)
        gp_ref[34:35, :] += jnp.sum(dy, axis=0, keepdims=True)

    vec = pl.BlockSpec((1, CONV_CH), lambda i: (0, 0))
    return pl.pallas_call(
        body, name=name, grid=(l // tm,),
        in_specs=[pl.BlockSpec((tm, CONV_CH), lambda i: (i, 0)), pl.BlockSpec((tm, CONV_CH), lambda i: (i, 1)),
                  pl.BlockSpec((tm, CONV_CH), lambda i: (i, 0)),
                  pl.BlockSpec((CONV_K, CONV_CH), lambda i: (0, 0)), vec, vec, vec],
        out_specs=[pl.BlockSpec((tm, CONV_CH), lambda i: (i, 0)), pl.BlockSpec((40, CONV_CH), lambda i: (0, 0))],
        out_shape=[jax.ShapeDtypeStruct((l, CONV_CH), F32), jax.ShapeDtypeStruct((40, CONV_CH), F32)],
        scratch_shapes=[pltpu.VMEM((CONV_HALO + tm, CONV_CH), F32)],
        compiler_params=_params(("arbitrary",)),
    )(pa, pa, ds, dw_w, dw_b.reshape(1, -1), ln_g.reshape(1, -1), ln_b.reshape(1, -1))


def _conv_bwd_in(pa, dc, dw_w, *, name):
    l = pa.shape[0]
    tm = _conv_tile(l)
    nt = l // tm

    def body(a_ref, gt_ref, dc_ref, w_ref, o_ref, buf):
        first = pl.program_id(0) == 0

        @pl.when(first)
        def _():
            buf[tm:tm + CONV_HALO, :] = jnp.zeros((CONV_HALO, CONV_CH), F32)

        @pl.when(jnp.logical_not(first))
        def _():
            buf[tm:tm + CONV_HALO, :] = buf[0:CONV_HALO, :]

        buf[0:tm, :] = dc_ref[...]
        w = w_ref[...]
        dhc = jnp.zeros((tm, CONV_CH), F32)
        for k in range(CONV_K):
            off = CONV_K - 1 - k
            dhc = dhc + w[k:k + 1, :] * buf[off:off + tm, :]
        a = a_ref[...]
        sg = _sigmoid(gt_ref[...])
        o_ref[:, 0:CONV_CH] = (dhc * sg).astype(BF16)
        o_ref[:, CONV_CH:2 * CONV_CH] = (dhc * a * sg * (1.0 - sg)).astype(BF16)

    return pl.pallas_call(
        body, name=name, grid=(nt,),
        in_specs=[pl.BlockSpec((tm, CONV_CH), lambda i: (nt - 1 - i, 0)),
                  pl.BlockSpec((tm, CONV_CH), lambda i: (nt - 1 - i, 1)),
                  pl.BlockSpec((tm, CONV_CH), lambda i: (nt - 1 - i, 0)),
                  pl.BlockSpec((CONV_K, CONV_CH), lambda i: (0, 0))],
        out_specs=pl.BlockSpec((tm, 2 * CONV_CH), lambda i: (nt - 1 - i, 0)),
        out_shape=jax.ShapeDtypeStruct((l, 2 * CONV_CH), BF16),
        scratch_shapes=[pltpu.VMEM((tm + CONV_HALO, CONV_CH), F32)],
        compiler_params=_params(("arbitrary",)),
    )(pa, pa, dc, dw_w)


def _pool_consts(tm, row0):
    lane = lax.broadcasted_iota(jnp.int32, (1, POOL_CH), 1)
    grp = lane // (POOL_CH // len(POOL_WINDOWS))
    win = jnp.where(grp == 0, 2.0, jnp.where(grp == 1, 4.0, jnp.where(grp == 2, 8.0, 16.0))).astype(F32)
    pos = (row0 + lax.broadcasted_iota(jnp.int32, (tm, 1), 0) - PAD).astype(F32)
    cnt = jnp.maximum(jnp.minimum(pos + 1.0, win), 1.0)
    return grp, cnt


def _pool_select(grp, s2, s4, s8, s16):
    return jnp.where(grp == 0, s2, jnp.where(grp == 1, s4, jnp.where(grp == 2, s8, s16)))


def _pool_fwd(pa, *, name):
    l = pa.shape[0]
    tm = _conv_tile(l)
    ext = POOL_HALO + tm

    def body(p_ref, o_ref, buf):
        i = pl.program_id(0)

        @pl.when(i == 0)
        def _():
            buf[0:POOL_HALO, :] = jnp.zeros((POOL_HALO, POOL_CH), F32)

        @pl.when(i > 0)
        def _():
            buf[0:POOL_HALO, :] = buf[tm:tm + POOL_HALO, :]

        p = p_ref[...]
        buf[POOL_HALO:ext, :] = p
        x = buf[...]
        s2 = x + pltpu.roll(x, 1, 0)
        s4 = s2 + pltpu.roll(s2, 2, 0)
        s8 = s4 + pltpu.roll(s4, 4, 0)
        s16 = s8 + pltpu.roll(s8, 8, 0)
        grp, cnt = _pool_consts(tm, i * tm)
        s = _pool_select(grp, s2, s4, s8, s16)[POOL_HALO:ext, :]
        o_ref[...] = (s / cnt - p).astype(BF16)

    return pl.pallas_call(
        body, name=name, grid=(l // tm,),
        in_specs=[pl.BlockSpec((tm, POOL_CH), lambda i: (i, 2))],
        out_specs=pl.BlockSpec((tm, POOL_CH), lambda i: (i, 0)),
        out_shape=jax.ShapeDtypeStruct((l, POOL_CH), BF16),
        scratch_shapes=[pltpu.VMEM((ext, POOL_CH), F32)],
        compiler_params=_params(("arbitrary",)),
    )(pa)


def _pool_bwd(dpooled, *, name):
    l = dpooled.shape[0]
    tm = _conv_tile(l)
    nt = l // tm
    ext = tm + POOL_HALO

    def body(d_ref, o_ref, buf):
        i = pl.program_id(0)

        @pl.when(i == 0)
        def _():
            buf[tm:ext, :] = jnp.zeros((POOL_HALO, POOL_CH), F32)

        @pl.when(i > 0)
        def _():
            buf[tm:ext, :] = buf[0:POOL_HALO, :]

        d = d_ref[...]
        grp, cnt = _pool_consts(tm, (nt - 1 - i) * tm)
        buf[0:tm, :] = d / cnt
        x = buf[...]
        s2 = x + pltpu.roll(x, ext - 1, 0)
        s4 = s2 + pltpu.roll(s2, ext - 2, 0)
        s8 = s4 + pltpu.roll(s4, ext - 4, 0)
        s16 = s8 + pltpu.roll(s8, ext - 8, 0)
        s = _pool_select(grp, s2, s4, s8, s16)[0:tm, :]
        o_ref[...] = (s - d).astype(BF16)

    return pl.pallas_call(
        body, name=name, grid=(nt,),
        in_specs=[pl.BlockSpec((tm, POOL_CH), lambda i: (nt - 1 - i, 0))],
        out_specs=pl.BlockSpec((tm, POOL_CH), lambda i: (nt - 1 - i, 0)),
        out_shape=jax.ShapeDtypeStruct((l, POOL_CH), BF16),
        scratch_shapes=[pltpu.VMEM((ext, POOL_CH), F32)],
        compiler_params=_params(("arbitrary",)),
    )(dpooled)


ATT_TQ = 256
ATT_TK = 5 * BLOCK
ATT_SUB = ATT_TK // BLOCK
LOG2E = 1.4426950408889634
LN2 = 0.6931471805599453
Q_SCALE = HEAD_DIM ** -0.5 * LOG2E
ATT_CUT = 160.0


def _tri_ones():
    r = lax.broadcasted_iota(jnp.int32, (2 * BLOCK, 2 * BLOCK), 0) % BLOCK
    c = lax.broadcasted_iota(jnp.int32, (2 * BLOCK, 2 * BLOCK), 1)
    return jnp.where((c >= BLOCK) | (r > c), 1.0, 0.0).astype(BF16)


def _split_dot(x, rhs):
    hi = x.astype(BF16)
    lo = (x - hi.astype(F32)).astype(BF16)
    return jnp.dot(jnp.concatenate([hi, lo], axis=1), rhs, preferred_element_type=F32)


def _scores(q, kt, qpos, base, masked):
    z = lax.dot_general(q, kt, (((1,), (1,)), ((), ())), preferred_element_type=F32)
    sp = jnp.log2(1.0 + jnp.exp2(-jnp.abs(z)))
    lb = jnp.minimum(z, 0.0) - sp
    lk = lb - z
    valid = None
    if masked:
        kpos = base + lax.broadcasted_iota(jnp.int32, (1, z.shape[1]), 1)
        valid = (kpos < qpos) & (kpos >= PAD)
        lk = jnp.where(valid, lk, 0.0)
    return lk, lb, valid


def _suffix(x, tri, carry):
    wts = [_split_dot(x[:, b * BLOCK:(b + 1) * BLOCK], tri) for b in range(ATT_SUB)]
    offs = [None] * ATT_SUB
    s = carry
    for b in reversed(range(ATT_SUB)):
        offs[b] = wts[b][:, :BLOCK] + s
        s = s + wts[b][:, BLOCK:]
    return jnp.concatenate(offs, axis=1), s


def _walk_tiles(i, tq, step):
    t_top = ((i + 1) * tq - 1) // ATT_TK
    t_diag = (i * tq) // ATT_TK
    n_plain = jnp.maximum(t_diag - 1, 0)

    def masked(jj, top):
        return step(t_top - jj, True)

    def live(carry):
        return (carry[0] < n_plain) & (carry[1] > -ATT_CUT)

    def plain(carry):
        return carry[0] + 1, step(t_diag - 1 - carry[0], False)

    top = lax.fori_loop(0, t_top - t_diag + 1, masked, jnp.float32(0.0))
    _, top = lax.while_loop(live, plain, (jnp.int32(0), top))

    @pl.when((t_diag > 0) & (top > -ATT_CUT))
    def _():
        step(0, True)


def _tile_base(t):
    base = t * ATT_TK
    return base if isinstance(base, int) else pl.multiple_of(base, BLOCK)


def _attn_fwd(qkv, *, name):
    l = qkv.shape[0]
    tq = ATT_TQ
    assert l % tq == 0 and l % ATT_TK == 0

    def body(q_ref, k_ref, v_ref, o_ref, o32_ref, acc_ref, r_ref):
        i = pl.program_id(1)
        acc_ref[...] = jnp.zeros_like(acc_ref)
        r_ref[...] = jnp.zeros_like(r_ref)
        q = q_ref[...]
        qpos = i * tq + lax.broadcasted_iota(jnp.int32, (tq, 1), 0)
        tri = _tri_ones()

        def step(t, masked):
            base = _tile_base(t)
            lk, lb, valid = _scores(q, k_ref[pl.ds(base, ATT_TK), :], qpos, base, masked)
            off, r_new = _suffix(lk, tri, r_ref[...])
            a = jnp.exp2(lb + off)
            if masked:
                a = jnp.where(valid, a, 0.0)
            acc_ref[...] += jnp.dot(a.astype(BF16), v_ref[pl.ds(base, ATT_TK), :], preferred_element_type=F32)
            r_ref[...] = r_new
            return jnp.max(r_new)

        _walk_tiles(i, tq, step)
        o_ref[...] = acc_ref[...].astype(BF16)
        o32_ref[...] = acc_ref[...]

    tile = pl.BlockSpec((tq, HEAD_DIM), lambda h, i: (i, h))
    return pl.pallas_call(
        body, name=name, grid=(HEADS, l // tq),
        in_specs=[tile,
                  pl.BlockSpec((l, HEAD_DIM), lambda h, i: (0, HEADS + h)),
                  pl.BlockSpec((l, HEAD_DIM), lambda h, i: (0, 2 * HEADS + h))],
        out_specs=[tile, tile],
        out_shape=[jax.ShapeDtypeStruct((l, HEADS * HEAD_DIM), BF16),
                   jax.ShapeDtypeStruct((l, HEADS * HEAD_DIM), F32)],
        scratch_shapes=[pltpu.VMEM((tq, HEAD_DIM), F32), pltpu.VMEM((tq, BLOCK), F32)],
        compiler_params=_params(("parallel", "arbitrary")),
    )(qkv, qkv, qkv)


def _attn_bwd(qkv, att, datt, *, name):
    l = qkv.shape[0]
    tq = ATT_TQ
    nq = l // tq
    assert l % tq == 0 and l % ATT_TK == 0

    def body(q_ref, k_ref, v_ref, o_ref, do_ref, dq_ref, dk_hbm, dv_hbm, dk_acc, dv_acc, dq_acc, r_ref, s_ref, sem):
        h = pl.program_id(0)
        i = pl.program_id(1)

        @pl.when(i == 0)
        def _():
            dk_acc[...] = jnp.zeros_like(dk_acc)
            dv_acc[...] = jnp.zeros_like(dv_acc)

        dq_acc[...] = jnp.zeros_like(dq_acc)
        r_ref[...] = jnp.zeros_like(r_ref)
        s_ref[...] = jnp.zeros_like(s_ref)
        q = q_ref[...]
        do = do_ref[...]
        ptot = jnp.sum(do.astype(F32) * o_ref[...], axis=-1, keepdims=True)
        qpos = i * tq + lax.broadcasted_iota(jnp.int32, (tq, 1), 0)
        tri = _tri_ones()

        def step(t, masked):
            base = _tile_base(t)
            kt = k_ref[pl.ds(base, ATT_TK), :]
            vt = v_ref[pl.ds(base, ATT_TK), :]
            lk, lb, valid = _scores(q, kt, qpos, base, masked)
            off, r_new = _suffix(lk, tri, r_ref[...])
            a = jnp.exp2(lb + off)
            if masked:
                a = jnp.where(valid, a, 0.0)
            ab = a.astype(BF16)
            da = lax.dot_general(do, vt, (((1,), (1,)), ((), ())), preferred_element_type=F32)
            p = ab.astype(F32) * da
            poff, s_new = _suffix(p, tri, s_ref[...])
            dz = (p - jnp.exp2(lb) * (ptot - poff)) * LN2
            if masked:
                dz = jnp.where(valid, dz, 0.0)
            dzb = dz.astype(BF16)
            dq_acc[...] += jnp.dot(dzb, kt, preferred_element_type=F32)
            dk_acc[pl.ds(base, ATT_TK), :] += lax.dot_general(dzb, q, (((0,), (0,)), ((), ())),
                                                              preferred_element_type=F32)
            dv_acc[pl.ds(base, ATT_TK), :] += lax.dot_general(ab, do, (((0,), (0,)), ((), ())),
                                                              preferred_element_type=F32)
            r_ref[...] = r_new
            s_ref[...] = s_new
            return jnp.max(r_new)

        _walk_tiles(i, tq, step)
        dq_ref[...] = (dq_acc[...] * Q_SCALE).astype(BF16)

        @pl.when(i == nq - 1)
        def _():
            ck = pltpu.make_async_copy(dk_acc, dk_hbm.at[h], sem.at[0])
            cv = pltpu.make_async_copy(dv_acc, dv_hbm.at[h], sem.at[1])
            ck.start()
            cv.start()
            ck.wait()
            cv.wait()

    tile = pl.BlockSpec((tq, HEAD_DIM), lambda h, i: (i, h))
    return pl.pallas_call(
        body, name=name, grid=(HEADS, nq),
        in_specs=[tile,
                  pl.BlockSpec((l, HEAD_DIM), lambda h, i: (0, HEADS + h)),
                  pl.BlockSpec((l, HEAD_DIM), lambda h, i: (0, 2 * HEADS + h)),
                  tile, tile],
        out_specs=[tile, pl.BlockSpec(memory_space=pl.ANY), pl.BlockSpec(memory_space=pl.ANY)],
        out_shape=[jax.ShapeDtypeStruct((l, HEADS * HEAD_DIM), BF16),
                   jax.ShapeDtypeStruct((HEADS, l, HEAD_DIM), F32), jax.ShapeDtypeStruct((HEADS, l, HEAD_DIM), F32)],
        scratch_shapes=[pltpu.VMEM((l, HEAD_DIM), F32), pltpu.VMEM((l, HEAD_DIM), F32),
                        pltpu.VMEM((tq, HEAD_DIM), F32), pltpu.VMEM((tq, BLOCK), F32), pltpu.VMEM((tq, BLOCK), F32),
                        pltpu.SemaphoreType.DMA((2,))],
        compiler_params=_params(("arbitrary", "arbitrary")),
    )(qkv, qkv, qkv, att, datt)


MIX_TM = 256


def _mix_branches(s_ref, p_ref, t_ref, g_ref, wa_ref, wb_ref, wc_ref, ba_ref, sc_ref, d):
    ya = jnp.dot(s_ref[...], wa_ref[...], preferred_element_type=F32) + ba_ref[...]
    yb0 = jnp.dot(p_ref[...], wb_ref[...], preferred_element_type=F32)
    yc = jnp.dot(t_ref[...], wc_ref[...], preferred_element_type=F32)
    g0 = _sigmoid(g_ref[:, 0:d])
    g1 = _sigmoid(g_ref[:, d:2 * d])
    g2 = _sigmoid(g_ref[:, 2 * d:3 * d])
    return ya, yb0, yc, g0, g1, g2


def _mix_specs(tm, d):
    row = lambda w: pl.BlockSpec((tm, w), lambda i: (i, 0))
    full = lambda r: pl.BlockSpec((r, d), lambda i: (0, 0))
    return [row(CONV_CH), row(POOL_CH), row(HEADS * HEAD_DIM), row(3 * d),
            full(CONV_CH), full(POOL_CH), full(HEADS * HEAD_DIM), full(1), full(1)]


def _mix_fwd(s, pooled, att, gates, wa, wb, wc, ba, scale, *, name):
    l, d = s.shape[0], wa.shape[1]
    tm = _tile(l, MIX_TM, 128)

    def body(s_ref, p_ref, t_ref, g_ref, wa_ref, wb_ref, wc_ref, ba_ref, sc_ref, o_ref):
        ya, yb0, yc, g0, g1, g2 = _mix_branches(s_ref, p_ref, t_ref, g_ref, wa_ref, wb_ref, wc_ref, ba_ref, sc_ref, d)
        o_ref[...] = (g0 * ya + g1 * (yb0 * sc_ref[...]) + g2 * yc).astype(BF16)

    return pl.pallas_call(
        body, name=name, grid=(l // tm,), in_specs=_mix_specs(tm, d),
        out_specs=pl.BlockSpec((tm, d), lambda i: (i, 0)),
        out_shape=jax.ShapeDtypeStruct((l, d), BF16),
        compiler_params=_params(("parallel",)),
    )(s, pooled, att, gates, wa, wb, wc, ba.reshape(1, d), scale.reshape(1, d))


def _mix_bwd(s, pooled, att, gates, wa, wb, wc, ba, scale, dmixed, *, name):
    l, d = s.shape[0], wa.shape[1]
    tm = _tile(l, MIX_TM, 128)

    def body(s_ref, p_ref, t_ref, g_ref, wa_ref, wb_ref, wc_ref, ba_ref, sc_ref, dm_ref,
             dg_ref, dya_ref, dyb_ref, dyc_ref, vec_ref):
        @pl.when(pl.program_id(0) == 0)
        def _():
            vec_ref[...] = jnp.zeros_like(vec_ref)

        ya, yb0, yc, g0, g1, g2 = _mix_branches(s_ref, p_ref, t_ref, g_ref, wa_ref, wb_ref, wc_ref, ba_ref, sc_ref, d)
        dm = dm_ref[...].astype(F32)
        sc = sc_ref[...]
        dg_ref[:, 0:d] = (dm * ya * g0 * (1.0 - g0)).astype(BF16)
        dg_ref[:, d:2 * d] = (dm * (yb0 * sc) * g1 * (1.0 - g1)).astype(BF16)
        dg_ref[:, 2 * d:3 * d] = (dm * yc * g2 * (1.0 - g2)).astype(BF16)
        dya = dm * g0
        dyb = dm * g1
        dya_ref[...] = dya.astype(BF16)
        dyb_ref[...] = (dyb * sc).astype(BF16)
        dyc_ref[...] = (dm * g2).astype(BF16)
        vec_ref[0:1, :] += jnp.sum(dya, axis=0, keepdims=True)
        vec_ref[1:2, :] += jnp.sum(dyb * yb0, axis=0, keepdims=True)

    row = lambda w: pl.BlockSpec((tm, w), lambda i: (i, 0))
    outs = pl.pallas_call(
        body, name=name, grid=(l // tm,), in_specs=_mix_specs(tm, d) + [row(d)],
        out_specs=[row(3 * d), row(d), row(d), row(d), pl.BlockSpec((8, d), lambda i: (0, 0))],
        out_shape=[jax.ShapeDtypeStruct((l, 3 * d), BF16), jax.ShapeDtypeStruct((l, d), BF16),
                   jax.ShapeDtypeStruct((l, d), BF16), jax.ShapeDtypeStruct((l, d), BF16),
                   jax.ShapeDtypeStruct((8, d), F32)],
        compiler_params=_params(("arbitrary",)),
    )(s, pooled, att, gates, wa, wb, wc, ba.reshape(1, d), scale.reshape(1, d), dmixed)
    return outs


FFN_TC = 512
_GELU_C = 0.7978845608028654
_GELU_A = 0.044715


def _gelu(x):
    th = jnp.tanh(_GELU_C * (x + _GELU_A * x * x * x))
    return 0.5 * x * (1.0 + th), th


def _gelu_grad(x, th):
    return 0.5 * (1.0 + th) + 0.5 * x * (1.0 - th * th) * _GELU_C * (1.0 + 3.0 * _GELU_A * x * x)


def _ffn_conv(buf, w, b, tm):
    acc = jnp.zeros((tm, w.shape[1]), F32) + b
    for k in range(FFN_K):
        off = FFN_HALO - (FFN_K - 1) + k
        acc = acc + w[k:k + 1, :] * buf[off:off + tm, :]
    return acc


def _ffn_fwd(ug, uv, wg, wv, bg, bv, *, name):
    l, f = ug.shape
    tm = _conv_tile(l)
    tc = _tile(f, FFN_TC, 128)
    ext = FFN_HALO + tm

    def body(ug_ref, uv_ref, wg_ref, wv_ref, bg_ref, bv_ref, o_ref, bufg, bufv):
        i = pl.program_id(1)
        for buf, u_ref in ((bufg, ug_ref), (bufv, uv_ref)):
            @pl.when(i == 0)
            def _():
                buf[0:FFN_HALO, :] = jnp.zeros((FFN_HALO, tc), F32)

            @pl.when(i > 0)
            def _():
                buf[0:FFN_HALO, :] = buf[tm:ext, :]

            buf[FFN_HALO:ext, :] = u_ref[...]
        gc = _ffn_conv(bufg, wg_ref[...], bg_ref[...], tm)
        vc = _ffn_conv(bufv, wv_ref[...], bv_ref[...], tm)
        o_ref[...] = (_gelu(gc)[0] * vc).astype(BF16)

    tile = pl.BlockSpec((tm, tc), lambda j, i: (i, j))
    wspec = pl.BlockSpec((FFN_K, tc), lambda j, i: (0, j))
    bspec = pl.BlockSpec((1, tc), lambda j, i: (0, j))
    return pl.pallas_call(
        body, name=name, grid=(f // tc, l // tm),
        in_specs=[tile, tile, wspec, wspec, bspec, bspec], out_specs=tile,
        out_shape=jax.ShapeDtypeStruct((l, f), BF16),
        scratch_shapes=[pltpu.VMEM((ext, tc), F32), pltpu.VMEM((ext, tc), F32)],
        compiler_params=_params(("parallel", "arbitrary")),
    )(ug, uv, wg, wv, bg.reshape(1, f), bv.reshape(1, f))


def _ffn_bwd(ug, uv, wg, wv, bg, bv, dact, *, name):
    l, f = ug.shape
    tm = _conv_tile(l)
    tc = _tile(f, FFN_TC, 128)
    nt = l // tm
    ext = FFN_HALO + tm
    hb = tm // FFN_HALO

    def body(ug_ref, uv_ref, pg_ref, pv_ref, wg_ref, wv_ref, bg_ref, bv_ref, da_ref,
             dug_ref, duv_ref, gg_ref, gv_ref, bufg, bufv, dbufg, dbufv):
        i = pl.program_id(1)
        last = i == nt - 1

        @pl.when(i == 0)
        def _():
            gg_ref[...] = jnp.zeros_like(gg_ref)
            gv_ref[...] = jnp.zeros_like(gv_ref)

        for buf, u_ref, prev_ref in ((bufg, ug_ref, pg_ref), (bufv, uv_ref, pv_ref)):
            buf[0:FFN_HALO, :] = jnp.where(last, 0.0, prev_ref[...])
            buf[FFN_HALO:ext, :] = u_ref[...]
        gc = _ffn_conv(bufg, wg_ref[...], bg_ref[...], tm)
        vc = _ffn_conv(bufv, wv_ref[...], bv_ref[...], tm)
        ge, th = _gelu(gc)
        da = da_ref[...].astype(F32)
        dgc = da * vc * _gelu_grad(gc, th)
        dvc = da * ge
        for dbuf, buf, dc, w_ref, du_ref, gp_ref in ((dbufg, bufg, dgc, wg_ref, dug_ref, gg_ref),
                                                      (dbufv, bufv, dvc, wv_ref, duv_ref, gv_ref)):
            @pl.when(i == 0)
            def _():
                dbuf[tm:ext, :] = jnp.zeros((FFN_HALO, tc), F32)

            @pl.when(i > 0)
            def _():
                dbuf[tm:ext, :] = dbuf[0:FFN_HALO, :]

            dbuf[0:tm, :] = dc
            w = w_ref[...]
            du = jnp.zeros((tm, tc), F32)
            for k in range(FFN_K):
                off = FFN_K - 1 - k
                du = du + w[k:k + 1, :] * dbuf[off:off + tm, :]
                uoff = FFN_HALO - (FFN_K - 1) + k
                gp_ref[k:k + 1, :] += jnp.sum(dc * buf[uoff:uoff + tm, :], axis=0, keepdims=True)
            gp_ref[3:4, :] += jnp.sum(dc, axis=0, keepdims=True)
            du_ref[...] = du.astype(BF16)

    tile = pl.BlockSpec((tm, tc), lambda j, i: (nt - 1 - i, j))
    prev = pl.BlockSpec((FFN_HALO, tc), lambda j, i: (jnp.maximum((nt - 1 - i) * hb - 1, 0), j))
    wspec = pl.BlockSpec((FFN_K, tc), lambda j, i: (0, j))
    bspec = pl.BlockSpec((1, tc), lambda j, i: (0, j))
    gspec = pl.BlockSpec((8, tc), lambda j, i: (0, j))
    return pl.pallas_call(
        body, name=name, grid=(f // tc, nt),
        in_specs=[tile, tile, prev, prev, wspec, wspec, bspec, bspec, tile],
        out_specs=[tile, tile, gspec, gspec],
        out_shape=[jax.ShapeDtypeStruct((l, f), BF16), jax.ShapeDtypeStruct((l, f), BF16),
                   jax.ShapeDtypeStruct((8, f), F32), jax.ShapeDtypeStruct((8, f), F32)],
        scratch_shapes=[pltpu.VMEM((ext, tc), F32), pltpu.VMEM((ext, tc), F32),
                        pltpu.VMEM((ext, tc), F32), pltpu.VMEM((ext, tc), F32)],
        compiler_params=_params(("parallel", "arbitrary")),
    )(ug, uv, ug, uv, wg, wv, bg.reshape(1, f), bv.reshape(1, f), dact)


def _adamw(parts, w, m, v, *, name):
    r = w.shape[0]
    tr = _tile(r, 1024, 8)
    c1 = 1.0 / (1.0 - ADAM_B1 ** ADAM_STEP)
    c2 = 1.0 / (1.0 - ADAM_B2 ** ADAM_STEP)

    def body(p_ref, w_ref, m_ref, v_ref, g_ref, d_ref, nm_ref, nv_ref):
        g = p_ref[0]
        for k in range(1, N_DEV):
            g = g + p_ref[k]
        nm = ADAM_B1 * m_ref[...] + (1.0 - ADAM_B1) * g
        nv = ADAM_B2 * v_ref[...] + (1.0 - ADAM_B2) * (g * g)
        g_ref[...] = g
        nm_ref[...] = nm
        nv_ref[...] = nv
        d_ref[...] = -ADAM_LR * ((nm * c1) / (jnp.sqrt(nv * c2) + ADAM_EPS) + ADAM_WD * w_ref[...])

    tile = pl.BlockSpec((tr, 128), lambda i: (i, 0))
    return pl.pallas_call(
        body, name=name, grid=(r // tr,),
        in_specs=[pl.BlockSpec((N_DEV, tr, 128), lambda i: (0, i, 0)), tile, tile, tile],
        out_specs=[tile, tile, tile, tile],
        out_shape=[jax.ShapeDtypeStruct((r, 128), F32)] * 4,
        compiler_params=_params(("parallel",)),
    )(parts, w, m, v)


def _place():
    return lax.axis_index("x"), lax.axis_index("y"), lax.axis_index("c")


def _all_gather(x, *, name):
    def body(x_ref, out_ref, send_sems, recv_sems, local_sem):
        xx, yy, cc = _place()
        me, sibling = (xx, yy, cc), (xx, yy, 1 - cc)
        chips = [(1 - xx, yy), (xx, 1 - yy), (1 - xx, 1 - yy)]

        def slot(px, py, pc):
            return out_ref.at[4 * px + 2 * py + pc]

        def copy(k, block, to, src=None):
            return pltpu.make_async_remote_copy(
                src_ref=slot(*block) if src is None else src, dst_ref=slot(*block),
                send_sem=send_sems.at[k], recv_sem=recv_sems.at[k], device_id=to, device_id_type=MESH)

        mine = pltpu.make_async_copy(x_ref, slot(*me), local_sem)
        mine.start()
        first = [copy(0, me, sibling, src=x_ref)]
        first += [copy(1 + j, me, (*chip, cc), src=x_ref) for j, chip in enumerate(chips)]
        for cp in first:
            cp.start()
        passed = [copy(4 + j, (*chip, cc), sibling) for j, chip in enumerate(chips)]
        for j, chip in enumerate(chips):
            copy(1 + j, (*chip, cc), me).wait_recv()
            passed[j].start()
        copy(0, sibling, me).wait_recv()
        for j, chip in enumerate(chips):
            copy(4 + j, (*chip, 1 - cc), me).wait_recv()
        for cp in first + passed:
            cp.wait_send()
        mine.wait()

    return pl.pallas_call(
        body, name=name,
        in_specs=[pl.BlockSpec(memory_space=pl.ANY)], out_specs=pl.BlockSpec(memory_space=pl.ANY),
        out_shape=jax.ShapeDtypeStruct((N_DEV,) + x.shape, x.dtype),
        scratch_shapes=[pltpu.SemaphoreType.DMA((7,)), pltpu.SemaphoreType.DMA((7,)), pltpu.SemaphoreType.DMA],
    )(x)


def _all_to_all(send, *, name):
    def body(s_ref, r_ref, send_sems, recv_sems, local_sem):
        xx, yy, cc = _place()
        me = 4 * xx + 2 * yy + cc
        local = pltpu.make_async_copy(s_ref.at[me], r_ref.at[me], local_sem)
        local.start()
        copies = []
        for m in range(1, N_DEV):
            px = 1 - xx if m & 4 else xx
            py = 1 - yy if m & 2 else yy
            pc = 1 - cc if m & 1 else cc
            copies.append(pltpu.make_async_remote_copy(
                src_ref=s_ref.at[4 * px + 2 * py + pc], dst_ref=r_ref.at[me],
                send_sem=send_sems.at[m - 1], recv_sem=recv_sems.at[m - 1],
                device_id=(px, py, pc), device_id_type=MESH))
        for cp in copies:
            cp.start()
        for cp in copies:
            cp.wait_recv()
        for cp in copies:
            cp.wait_send()
        local.wait()

    return pl.pallas_call(
        body, name=name,
        in_specs=[pl.BlockSpec(memory_space=pl.ANY)], out_specs=pl.BlockSpec(memory_space=pl.ANY),
        out_shape=jax.ShapeDtypeStruct(send.shape, send.dtype),
        scratch_shapes=[pltpu.SemaphoreType.DMA((7,)), pltpu.SemaphoreType.DMA((7,)), pltpu.SemaphoreType.DMA],
    )(send)


def _pack(arrays, dtype):
    parts = []
    for a in arrays:
        flat = a.reshape(-1).astype(dtype)
        parts.append(jnp.pad(flat, (0, (-flat.size) % PACK_ALIGN)))
    return jnp.concatenate(parts).reshape(-1, 128)


def _pack_pieces(arrays, dtype):
    parts = []
    for a in arrays:
        flat = a.reshape(N_DEV, -1).astype(dtype)
        parts.append(jnp.pad(flat, ((0, 0), (0, (-flat.shape[1]) % PACK_ALIGN))))
    return jnp.concatenate(parts, axis=1).reshape(N_DEV, -1, 128)


def _unpack(buf, shapes, lead=()):
    flat = buf.reshape(lead + (-1,))
    out, off = [], 0
    for shp in shapes:
        size = 1
        for s in shp:
            size *= s
        out.append(flat[..., off:off + size].reshape(lead + tuple(shp)))
        off += size + (-size) % PACK_ALIGN
    return out


def _unshard(g, axis):
    g = jnp.moveaxis(g, 0, axis)
    shp = list(g.shape)
    return g.reshape(shp[:axis] + [shp[axis] * shp[axis + 1]] + shp[axis + 2:])


def _pieces(full, axis):
    shp = list(full.shape)
    g = full.reshape(shp[:axis] + [N_DEV, shp[axis] // N_DEV] + shp[axis + 1:])
    return jnp.moveaxis(g, axis, 0)


SHARDED = (("meta", 1), ("w_in", 2), ("conv_dw_w", 2), ("w_conv_out", 2), ("w_pool_grp", 3), ("w_attn_out", 2),
           ("w_o", 1), ("w_up", 2), ("ffn_dw_w", 2), ("w_down", 1))
MATRICES = ("w_in", "w_conv_out", "w_pool_grp", "w_attn_out", "w_o", "w_up", "w_down")
REPLICATED = ("norm1", "conv_dw_b", "conv_ln_g", "conv_ln_b", "b_conv_out", "pool_scale", "norm2", "ffn_dw_b",
              "final_norm")
WEIGHTS = ("meta", "norm1", "w_in", "conv_dw_w", "conv_dw_b", "conv_ln_g", "conv_ln_b", "w_conv_out", "b_conv_out",
           "w_pool_grp", "pool_scale", "w_attn_out", "w_o", "norm2", "w_up", "ffn_dw_w", "ffn_dw_b", "w_down",
           "final_norm")


def _block_diag(w_grp):
    g, gc, od = w_grp.shape
    out = jnp.zeros((g * gc, g * od), w_grp.dtype)
    for i in range(g):
        out = out.at[i * gc:(i + 1) * gc, i * od:(i + 1) * od].set(w_grp[i])
    return out


def _block_diag_grad(gw, g):
    gc, od = gw.shape[0] // g, gw.shape[1] // g
    return jnp.stack([gw[i * gc:(i + 1) * gc, i * od:(i + 1) * od] for i in range(g)])


C_CONV = 2 * CONV_CH
C_POOL = C_CONV + POOL_CH
C_ATT = HEADS * HEAD_DIM
C_QKV = C_POOL + 3 * C_ATT


def _layer_fwd(h, p, tag):
    d = h.shape[1]
    w_in = p["w_in"]
    hn = _rms_fwd(h, p["norm1"], name=f"rms1_{tag}")
    pa = _mm(hn, w_in[:, :C_POOL], out_dtype=F32, name=f"proj_a_{tag}")
    q_scale = jnp.concatenate([jnp.full((C_ATT,), Q_SCALE, F32), jnp.ones((2 * C_ATT,), F32)])
    qkv = _mm(hn, w_in[:, C_POOL:C_QKV], out_dtype=BF16, col_scale=q_scale, name=f"proj_qkv_{tag}")
    gates = _mm(hn, w_in[:, C_QKV:], out_dtype=F32, name=f"proj_g_{tag}")
    s = _conv_fwd(pa, p["conv_dw_w"], p["conv_dw_b"], p["conv_ln_g"], p["conv_ln_b"], name=f"conv_{tag}")
    pooled = _pool_fwd(pa, name=f"pool_{tag}")
    att, att32 = _attn_fwd(qkv, name=f"attn_{tag}")
    wb = _block_diag(p["w_pool_grp"])
    mixed = _mix_fwd(s, pooled, att, gates, p["w_conv_out"], wb, p["w_attn_out"], p["b_conv_out"], p["pool_scale"],
                     name=f"mix_{tag}")
    h1 = _mm(mixed, p["w_o"], out_dtype=F32, res=h, mask_rows=True, name=f"wo_{tag}")
    hn2 = _rms_fwd(h1, p["norm2"], name=f"rms2_{tag}")
    f = p["w_up"].shape[1] // 2
    ug = _mm(hn2, p["w_up"][:, :f], out_dtype=F32, name=f"up_g_{tag}")
    uv = _mm(hn2, p["w_up"][:, f:], out_dtype=F32, name=f"up_v_{tag}")
    act = _ffn_fwd(ug, uv, p["ffn_dw_w"][:, :f], p["ffn_dw_w"][:, f:], p["ffn_dw_b"][:f], p["ffn_dw_b"][f:],
                   name=f"ffn_{tag}")
    h2 = _mm(act, p["w_down"], out_dtype=F32, res=h1, mask_rows=True, name=f"down_{tag}")
    saved = dict(h=h, hn=hn, pa=pa, qkv=qkv, gates=gates, s=s, pooled=pooled, att=att, att32=att32, wb=wb, mixed=mixed,
                 h1=h1,
                 hn2=hn2, ug=ug, uv=uv, act=act)
    return h2, saved


def _layer_bwd(dh2, p, sv, tag):
    g = {}
    f = p["w_up"].shape[1] // 2
    dact = _mm(dh2, p["w_down"].T, out_dtype=F32, name=f"b_down_{tag}")
    g["w_down"] = _mm_tn(sv["act"], dh2, name=f"g_down_{tag}")
    dug, duv, gpg, gpv = _ffn_bwd(sv["ug"], sv["uv"], p["ffn_dw_w"][:, :f], p["ffn_dw_w"][:, f:], p["ffn_dw_b"][:f],
                                  p["ffn_dw_b"][f:], dact, name=f"b_ffn_{tag}")
    g["ffn_dw_w"] = jnp.concatenate([gpg[0:FFN_K], gpv[0:FFN_K]], axis=1)
    g["ffn_dw_b"] = jnp.concatenate([gpg[FFN_K], gpv[FFN_K]])
    w_up_t = p["w_up"].T
    dhn2 = _mm(dug, w_up_t[:f], out_dtype=F32, name=f"b_up_g_{tag}")
    dhn2 = _mm(duv, w_up_t[f:], out_dtype=F32, res=dhn2, name=f"b_up_v_{tag}")
    g["w_up"] = jnp.concatenate([_mm_tn(sv["hn2"], dug, name=f"g_up_g_{tag}"),
                                 _mm_tn(sv["hn2"], duv, name=f"g_up_v_{tag}")], axis=1)
    dh1, g["norm2"] = _rms_bwd(sv["h1"], p["norm2"], dhn2, dh2, name=f"b_rms2_{tag}")
    dmixed = _mm(dh1, p["w_o"].T, out_dtype=BF16, name=f"b_wo_{tag}")
    g["w_o"] = _mm_tn(sv["mixed"], dh1, name=f"g_wo_{tag}")
    dgates, dya, dyb, dyc, vec = _mix_bwd(sv["s"], sv["pooled"], sv["att"], sv["gates"], p["w_conv_out"], sv["wb"],
                                          p["w_attn_out"], p["b_conv_out"], p["pool_scale"], dmixed,
                                          name=f"b_mix_{tag}")
    g["b_conv_out"], g["pool_scale"] = vec[0], vec[1]
    ds = _mm(dya, p["w_conv_out"].T, out_dtype=F32, name=f"b_conv_out_{tag}")
    dpooled = _mm(dyb, sv["wb"].T, out_dtype=F32, name=f"b_pool_out_{tag}")
    datt = _mm(dyc, p["w_attn_out"].T, out_dtype=BF16, name=f"b_attn_out_{tag}")
    g["w_conv_out"] = _mm_tn(sv["s"], dya, name=f"g_conv_out_{tag}")
    g["w_pool_grp"] = _block_diag_grad(_mm_tn(sv["pooled"], dyb, name=f"g_pool_{tag}"), len(POOL_WINDOWS))
    g["w_attn_out"] = _mm_tn(sv["att"], dyc, name=f"g_attn_out_{tag}")
    dc, gp = _conv_bwd_ln(sv["pa"], ds, p["conv_dw_w"], p["conv_dw_b"], p["conv_ln_g"], p["conv_ln_b"],
                          name=f"b_conv_ln_{tag}")
    g["conv_dw_w"], g["conv_dw_b"], g["conv_ln_g"], g["conv_ln_b"] = gp[0:CONV_K], gp[32], gp[33], gp[34]
    dconv = _conv_bwd_in(sv["pa"], dc, p["conv_dw_w"], name=f"b_conv_in_{tag}")
    dp = _pool_bwd(dpooled, name=f"b_pool_{tag}")
    dq, dk, dv = _attn_bwd(sv["qkv"], sv["att32"], datt, name=f"b_attn_{tag}")
    dk = jnp.moveaxis(dk, 0, 1).reshape(dq.shape).astype(BF16)
    dv = jnp.moveaxis(dv, 0, 1).reshape(dq.shape).astype(BF16)
    w_in_t = p["w_in"].T
    cols = [(dconv, 0, C_CONV), (dp, C_CONV, C_POOL), (dq, C_POOL, C_POOL + C_ATT),
            (dk, C_POOL + C_ATT, C_POOL + 2 * C_ATT), (dv, C_POOL + 2 * C_ATT, C_QKV),
            (dgates, C_QKV, w_in_t.shape[0])]
    dhn, gw = None, []
    for n, (dcol, lo, hi) in enumerate(cols):
        dhn = _mm(dcol, w_in_t[lo:hi], out_dtype=F32, res=dhn, name=f"b_in{n}_{tag}")
        gw.append(_mm_tn(sv["hn"], dcol, name=f"g_in{n}_{tag}"))
    g["w_in"] = jnp.concatenate(gw, axis=1)
    dh, g["norm1"] = _rms_bwd(sv["h"], p["norm1"], dhn, dh1, name=f"b_rms1_{tag}")
    return dh, g


def kernel(x, meta, norm1, w_in, conv_dw_w, conv_dw_b, conv_ln_g, conv_ln_b, w_conv_out, b_conv_out, w_pool_grp, pool_scale, w_attn_out, w_o, norm2, w_up, ffn_dw_w, ffn_dw_b, w_down, final_norm, loss_target, m_meta, m_norm1, m_w_in, m_conv_dw_w, m_conv_dw_b, m_conv_ln_g, m_conv_ln_b, m_w_conv_out, m_b_conv_out, m_w_pool_grp, m_pool_scale, m_w_attn_out, m_w_o, m_norm2, m_w_up, m_ffn_dw_w, m_ffn_dw_b, m_w_down, m_final_norm, v_meta, v_norm1, v_w_in, v_conv_dw_w, v_conv_dw_b, v_conv_ln_g, v_conv_ln_b, v_w_conv_out, v_b_conv_out, v_w_pool_grp, v_pool_scale, v_w_attn_out, v_w_o, v_norm2, v_w_up, v_ffn_dw_w, v_ffn_dw_b, v_w_down, v_final_norm):
    given = dict(meta=meta, norm1=norm1, w_in=w_in, conv_dw_w=conv_dw_w, conv_dw_b=conv_dw_b, conv_ln_g=conv_ln_g, conv_ln_b=conv_ln_b, w_conv_out=w_conv_out, b_conv_out=b_conv_out, w_pool_grp=w_pool_grp, pool_scale=pool_scale, w_attn_out=w_attn_out, w_o=w_o, norm2=norm2, w_up=w_up, ffn_dw_w=ffn_dw_w, ffn_dw_b=ffn_dw_b, w_down=w_down, final_norm=final_norm)
    mom_m = dict(meta=m_meta, norm1=m_norm1, w_in=m_w_in, conv_dw_w=m_conv_dw_w, conv_dw_b=m_conv_dw_b, conv_ln_g=m_conv_ln_g, conv_ln_b=m_conv_ln_b, w_conv_out=m_w_conv_out, b_conv_out=m_b_conv_out, w_pool_grp=m_w_pool_grp, pool_scale=m_pool_scale, w_attn_out=m_w_attn_out, w_o=m_w_o, norm2=m_norm2, w_up=m_w_up, ffn_dw_w=m_ffn_dw_w, ffn_dw_b=m_ffn_dw_b, w_down=m_w_down, final_norm=m_final_norm)
    mom_v = dict(meta=v_meta, norm1=v_norm1, w_in=v_w_in, conv_dw_w=v_conv_dw_w, conv_dw_b=v_conv_dw_b, conv_ln_g=v_conv_ln_g, conv_ln_b=v_conv_ln_b, w_conv_out=v_w_conv_out, b_conv_out=v_b_conv_out, w_pool_grp=v_w_pool_grp, pool_scale=v_pool_scale, w_attn_out=v_w_attn_out, w_o=v_w_o, norm2=v_norm2, w_up=v_w_up, ffn_dw_w=v_ffn_dw_w, ffn_dw_b=v_ffn_dw_b, w_down=v_w_down, final_norm=v_final_norm)
    sharded_axis = dict(SHARDED)
    vectors = [n for n, _ in SHARDED if n not in MATRICES]
    depth = norm1.shape[0]

    got_mat = _all_gather(_pack([given[n] for n in MATRICES], BF16), name="gather_matrices")
    got_vec = _all_gather(_pack([given[n] for n in vectors], F32), name="gather_vectors")
    full = {n: given[n] for n in REPLICATED}
    for n, a in zip(MATRICES, _unpack(got_mat, [given[n].shape for n in MATRICES], (N_DEV,))):
        full[n] = _unshard(a, sharded_axis[n])
    for n, a in zip(vectors, _unpack(got_vec, [given[n].shape for n in vectors], (N_DEV,))):
        full[n] = _unshard(a, sharded_axis[n])

    xs = x[0]
    d = xs.shape[1]
    h = jnp.concatenate([jnp.zeros((PAD, d), F32), full["meta"], xs], axis=0)
    layers, saved = [], []
    for i in range(depth):
        p = {n: full[n][i] for n in full if n not in ("meta", "final_norm")}
        layers.append(p)
        h, sv = _layer_fwd(h, p, f"l{i}")
        saved.append(sv)
    loss_part, dh, g_final = _loss_head(h, full["final_norm"], loss_target[0], name="loss_head")

    grads = [None] * depth
    for i in reversed(range(depth)):
        dh, grads[i] = _layer_bwd(dh, layers[i], saved[i], f"l{i}")
    full_grad = {n: jnp.stack([grads[i][n] for i in range(depth)]) for n in grads[0]}
    full_grad["meta"] = dh[PAD:FRONT]
    full_grad["final_norm"] = g_final
    grad_x = dh[FRONT:][None]

    names = [n for n, _ in SHARDED]
    recv = _all_to_all(_pack_pieces([_pieces(full_grad[n], sharded_axis[n]) for n in names], F32), name="scatter_grads")
    rep_shapes = [given[n].shape for n in REPLICATED] + [(1,)]
    rep_parts = _all_gather(_pack([full_grad[n] for n in REPLICATED] + [loss_part.reshape(1)], F32),
                            name="gather_partials")

    out = {}
    shapes = [given[n].shape for n in names]
    res = _adamw(recv, _pack([given[n] for n in names], F32), _pack([mom_m[n] for n in names], F32),
                 _pack([mom_v[n] for n in names], F32), name="adamw_sharded")
    for kind, buf in zip(("grad", "delta", "new_m", "new_v"), res):
        for n, a in zip(names, _unpack(buf, shapes)):
            out[kind, n] = a
    rep_w = [given[n] for n in REPLICATED] + [jnp.zeros((1,), F32)]
    rep_m = [mom_m[n] for n in REPLICATED] + [jnp.zeros((1,), F32)]
    rep_v = [mom_v[n] for n in REPLICATED] + [jnp.ones((1,), F32)]
    res = _adamw(rep_parts, _pack(rep_w, F32), _pack(rep_m, F32), _pack(rep_v, F32), name="adamw_replicated")
    for kind, buf in zip(("grad", "delta", "new_m", "new_v"), res):
        for n, a in zip(list(REPLICATED) + ["loss"], _unpack(buf, rep_shapes)):
            out[kind, n] = a
    loss = out["grad", "loss"][0]
    return (loss, grad_x, *[out["grad", n] for n in WEIGHTS], *[out["delta", n] for n in WEIGHTS],
            *[out["new_m", n] for n in WEIGHTS], *[out["new_v", n] for n in WEIGHTS])
```

```python
import functools

import jax
import jax.numpy as jnp
from jax import lax
from jax.experimental import pallas as pl
from jax.experimental.pallas import tpu as pltpu

F32 = jnp.float32
BF16 = jnp.bfloat16
MESH = pl.DeviceIdType.MESH

N_DEV = 8
N_META = 16
BLOCK = 128
PAD = 240
FRONT = PAD + N_META
HEADS = 4
HEAD_DIM = 128
CONV_CH = 256
CONV_K = 31
POOL_CH = 256
POOL_WINDOWS = (2, 4, 8, 16)
FFN_K = 3
EPS = 1e-6
ADAM_LR, ADAM_B1, ADAM_B2, ADAM_EPS, ADAM_WD, ADAM_STEP = 0.001, 0.9, 0.999, 1e-08, 0.01, 10

VMEM_LIMIT = 56 * 1024 * 1024
CONV_HALO = 32
POOL_HALO = 16
FFN_HALO = 8
PACK_ALIGN = 8 * 128


def _tile(n, cap, unit):
    if n <= cap:
        return n
    best = None
    t = unit
    while t <= cap:
        if n % t == 0:
            best = t
        t += unit
    assert best is not None, (n, cap, unit)
    return best


def _params(sem):
    return pltpu.CompilerParams(dimension_semantics=sem, vmem_limit_bytes=VMEM_LIMIT)


def _sigmoid(x):
    return 1.0 / (1.0 + jnp.exp(-x))


MM_MAX_K = 3072


def _mm(a, b, *, out_dtype, name, res=None, col_scale=None, mask_rows=False, tm_cap=640, tn_cap=768):
    m, k = a.shape
    k2, n = b.shape
    assert k == k2 and k <= MM_MAX_K
    tm, tn = _tile(m, tm_cap, 128), _tile(n, tn_cap, 128)

    def body(*refs):
        refs = list(refs)
        a_ref, b_ref = refs[:2]
        o_ref = refs[-1]
        r_ref = refs[2] if res is not None else None
        c_ref = refs[-2] if col_scale is not None else None
        y = jnp.dot(a_ref[...].astype(BF16), b_ref[...].astype(BF16), preferred_element_type=F32)
        if col_scale is not None:
            y = y * c_ref[...]
        if res is not None:
            y = y + r_ref[...].astype(F32)
        if mask_rows:
            row = pl.program_id(0) * tm + lax.broadcasted_iota(jnp.int32, (tm, 1), 0)
            y = jnp.where(row >= PAD, y, 0.0)
        o_ref[...] = y.astype(out_dtype)

    in_specs = [pl.BlockSpec((tm, k), lambda i, j: (i, 0)), pl.BlockSpec((k, tn), lambda i, j: (0, j))]
    args = [a, b]
    if res is not None:
        in_specs.append(pl.BlockSpec((tm, tn), lambda i, j: (i, j)))
        args.append(res)
    if col_scale is not None:
        in_specs.append(pl.BlockSpec((1, tn), lambda i, j: (0, j)))
        args.append(col_scale.reshape(1, n))
    return pl.pallas_call(
        body, name=name, grid=(m // tm, n // tn),
        in_specs=in_specs, out_specs=pl.BlockSpec((tm, tn), lambda i, j: (i, j)),
        out_shape=jax.ShapeDtypeStruct((m, n), out_dtype),
        compiler_params=_params(("parallel", "parallel")),
    )(*args)


def _mm_tn(a, b, *, name, t1_cap=512, tn_cap=1024, tl_cap=1280):
    l, k1 = a.shape
    l2, n = b.shape
    assert l == l2
    t1, tn, tl = _tile(k1, t1_cap, 128), _tile(n, tn_cap, 128), _tile(l, tl_cap, 128)

    def body(a_ref, b_ref, o_ref):
        @pl.when(pl.program_id(2) == 0)
        def _():
            o_ref[...] = jnp.zeros_like(o_ref)

        o_ref[...] += lax.dot_general(a_ref[...].astype(BF16), b_ref[...].astype(BF16),
                                      (((0,), (0,)), ((), ())), preferred_element_type=F32)

    return pl.pallas_call(
        body, name=name, grid=(k1 // t1, n // tn, l // tl),
        in_specs=[pl.BlockSpec((tl, t1), lambda i, j, ll: (ll, i)), pl.BlockSpec((tl, tn), lambda i, j, ll: (ll, j))],
        out_specs=pl.BlockSpec((t1, tn), lambda i, j, ll: (i, j)),
        out_shape=jax.ShapeDtypeStruct((k1, n), F32),
        compiler_params=_params(("parallel", "parallel", "arbitrary")),
    )(a, b)


def _rms_fwd(x, g, *, name):
    l, d = x.shape
    tm = _tile(l, 640, 128)

    def body(x_ref, g_ref, o_ref):
        xv = x_ref[...]
        r = lax.rsqrt(jnp.mean(xv * xv, axis=-1, keepdims=True) + EPS)
        o_ref[...] = (xv * r * g_ref[...]).astype(BF16)

    return pl.pallas_call(
        body, name=name, grid=(l // tm,),
        in_specs=[pl.BlockSpec((tm, d), lambda i: (i, 0)), pl.BlockSpec((1, d), lambda i: (0, 0))],
        out_specs=pl.BlockSpec((tm, d), lambda i: (i, 0)),
        out_shape=jax.ShapeDtypeStruct((l, d), BF16),
        compiler_params=_params(("parallel",)),
    )(x, g.reshape(1, d))


def _rms_bwd(x, g, dy, dres, *, name):
    l, d = x.shape
    tm = _tile(l, 640, 128)

    def body(x_ref, g_ref, dy_ref, dr_ref, dx_ref, dg_ref):
        i = pl.program_id(0)

        @pl.when(i == 0)
        def _():
            dg_ref[...] = jnp.zeros_like(dg_ref)

        xv = x_ref[...]
        r = lax.rsqrt(jnp.mean(xv * xv, axis=-1, keepdims=True) + EPS)
        xh = xv * r
        dyv = dy_ref[...].astype(F32)
        dxh = dyv * g_ref[...]
        dx = r * (dxh - xh * jnp.mean(dxh * xh, axis=-1, keepdims=True)) + dr_ref[...]
        row = i * tm + lax.broadcasted_iota(jnp.int32, (tm, 1), 0)
        dx_ref[...] = jnp.where(row >= PAD, dx, 0.0)
        dg_ref[0:1, :] += jnp.sum(dyv * xh, axis=0, keepdims=True)

    dx, dg = pl.pallas_call(
        body, name=name, grid=(l // tm,),
        in_specs=[pl.BlockSpec((tm, d), lambda i: (i, 0)), pl.BlockSpec((1, d), lambda i: (0, 0)),
                  pl.BlockSpec((tm, d), lambda i: (i, 0)), pl.BlockSpec((tm, d), lambda i: (i, 0))],
        out_specs=[pl.BlockSpec((tm, d), lambda i: (i, 0)), pl.BlockSpec((8, d), lambda i: (0, 0))],
        out_shape=[jax.ShapeDtypeStruct((l, d), F32), jax.ShapeDtypeStruct((8, d), F32)],
        compiler_params=_params(("arbitrary",)),
    )(x, g.reshape(1, d), dy, dres)
    return dx, dg[0]


def _loss_head(h, g, target, *, name):
    l, d = h.shape
    tm = FRONT
    assert l % tm == 0 and target.shape[0] == l - tm

    def body(h_ref, g_ref, t_ref, dh_ref, loss_ref, dg_ref):
        i = pl.program_id(0)

        @pl.when(i == 0)
        def _():
            loss_ref[...] = jnp.zeros_like(loss_ref)
            dg_ref[...] = jnp.zeros_like(dg_ref)
            dh_ref[...] = jnp.zeros_like(dh_ref)

        @pl.when(i > 0)
        def _():
            xv = h_ref[...]
            r = lax.rsqrt(jnp.mean(xv * xv, axis=-1, keepdims=True) + EPS)
            xh = xv * r
            gv = g_ref[...]
            err = xh * gv - t_ref[...]
            loss_ref[...] += 0.5 * jnp.sum(jnp.mean(err * err, axis=-1, keepdims=True))
            dy = err * (1.0 / d)
            dxh = dy * gv
            dh_ref[...] = r * (dxh - xh * jnp.mean(dxh * xh, axis=-1, keepdims=True))
            dg_ref[0:1, :] += jnp.sum(dy * xh, axis=0, keepdims=True)

    dh, loss, dg = pl.pallas_call(
        body, name=name, grid=(l // tm,),
        in_specs=[pl.BlockSpec((tm, d), lambda i: (i, 0)), pl.BlockSpec((1, d), lambda i: (0, 0)),
                  pl.BlockSpec((tm, d), lambda i: (jnp.maximum(i - 1, 0), 0))],
        out_specs=[pl.BlockSpec((tm, d), lambda i: (i, 0)), pl.BlockSpec((8, 128), lambda i: (0, 0)),
                   pl.BlockSpec((8, d), lambda i: (0, 0))],
        out_shape=[jax.ShapeDtypeStruct((l, d), F32), jax.ShapeDtypeStruct((8, 128), F32),
                   jax.ShapeDtypeStruct((8, d), F32)],
        compiler_params=_params(("arbitrary",)),
    )(h, g.reshape(1, d), target)
    return loss[0, 0], dh, dg[0]


def _conv_tile(l):
    return _tile(l, 640, 128)


def _conv_core(a, gt, buf, dw_w, dw_b, first):
    tm = a.shape[0]

    @pl.when(first)
    def _():
        buf[0:CONV_HALO, :] = jnp.zeros((CONV_HALO, CONV_CH), F32)

    @pl.when(jnp.logical_not(first))
    def _():
        buf[0:CONV_HALO, :] = buf[tm:tm + CONV_HALO, :]

    sg = _sigmoid(gt)
    buf[CONV_HALO:CONV_HALO + tm, :] = a * sg
    c = jnp.zeros((tm, CONV_CH), F32) + dw_b
    for k in range(CONV_K):
        off = CONV_HALO - (CONV_K - 1) + k
        c = c + dw_w[k:k + 1, :] * buf[off:off + tm, :]
    return c, sg


def _layer_norm(c, ln_g, ln_b):
    mu = jnp.mean(c, axis=-1, keepdims=True)
    xc = c - mu
    r = lax.rsqrt(jnp.mean(xc * xc, axis=-1, keepdims=True) + EPS)
    xh = xc * r
    return xh, r, xh * ln_g + ln_b


def _conv_fwd(pa, dw_w, dw_b, ln_g, ln_b, *, name):
    l = pa.shape[0]
    tm = _conv_tile(l)

    def body(a_ref, gt_ref, w_ref, b_ref, g_ref, bb_ref, o_ref, buf):
        c, _ = _conv_core(a_ref[...], gt_ref[...], buf, w_ref[...], b_ref[...], pl.program_id(0) == 0)
        _, _, y = _layer_norm(c, g_ref[...], bb_ref[...])
        o_ref[...] = (y * _sigmoid(y)).astype(BF16)

    vec = pl.BlockSpec((1, CONV_CH), lambda i: (0, 0))
    return pl.pallas_call(
        body, name=name, grid=(l // tm,),
        in_specs=[pl.BlockSpec((tm, CONV_CH), lambda i: (i, 0)), pl.BlockSpec((tm, CONV_CH), lambda i: (i, 1)),
                  pl.BlockSpec((CONV_K, CONV_CH), lambda i: (0, 0)), vec, vec, vec],
        out_specs=pl.BlockSpec((tm, CONV_CH), lambda i: (i, 0)),
        out_shape=jax.ShapeDtypeStruct((l, CONV_CH), BF16),
        scratch_shapes=[pltpu.VMEM((CONV_HALO + tm, CONV_CH), F32)],
        compiler_params=_params(("arbitrary",)),
    )(pa, pa, dw_w, dw_b.reshape(1, -1), ln_g.reshape(1, -1), ln_b.reshape(1, -1))


def _conv_bwd_ln(pa, ds, dw_w, dw_b, ln_g, ln_b, *, name):
    l = pa.shape[0]
    tm = _conv_tile(l)

    def body(a_ref, gt_ref, ds_ref, w_ref, b_ref, g_ref, bb_ref, dc_ref, gp_ref, buf):
        i = pl.program_id(0)

        @pl.when(i == 0)
        def _():
            gp_ref[...] = jnp.zeros_like(gp_ref)

        c, _ = _conv_core(a_ref[...], gt_ref[...], buf, w_ref[...], b_ref[...], i == 0)
        xh, r, y = _layer_norm(c, g_ref[...], bb_ref[...])
        sy = _sigmoid(y)
        dy = ds_ref[...] * (sy * (1.0 + y * (1.0 - sy)))
        dxh = dy * g_ref[...]
        dc = r * (dxh - jnp.mean(dxh, axis=-1, keepdims=True) - xh * jnp.mean(dxh * xh, axis=-1, keepdims=True))
        dc_ref[...] = dc
        for k in range(CONV_K):
            off = CONV_HALO - (CONV_K - 1) + k
            gp_ref[k:k + 1, :] += jnp.sum(dc * buf[off:off + tm, :], axis=0, keepdims=True)
        gp_ref[32:33, :] += jnp.sum(dc, axis=0, keepdims=True)
        gp_ref[33:34, :] += jnp.sum(dy * xh, axis=0, keepdims=True)
        gp_ref[34:35, :] += jnp.sum(dy, axis=0, keepdims=True)

    vec = pl.BlockSpec((1, CONV_CH), lambda i: (0, 0))
    return pl.pallas_call(
        body, name=name, grid=(l // tm,),
        in_specs=[pl.BlockSpec((tm, CONV_CH), lambda i: (i, 0)), pl.BlockSpec((tm, CONV_CH), lambda i: (i, 1)),
                  pl.BlockSpec((tm, CONV_CH), lambda i: (i, 0)),
                  pl.BlockSpec((CONV_K, CONV_CH), lambda i: (0, 0)), vec, vec, vec],
        out_specs=[pl.BlockSpec((tm, CONV_CH), lambda i: (i, 0)), pl.BlockSpec((40, CONV_CH), lambda i: (0, 0))],
        out_shape=[jax.ShapeDtypeStruct((l, CONV_CH), F32), jax.ShapeDtypeStruct((40, CONV_CH), F32)],
        scratch_shapes=[pltpu.VMEM((CONV_HALO + tm, CONV_CH), F32)],
        compiler_params=_params(("arbitrary",)),
    )(pa, pa, ds, dw_w, dw_b.reshape(1, -1), ln_g.reshape(1, -1), ln_b.reshape(1, -1))


def _conv_bwd_in(pa, dc, dw_w, *, name):
    l = pa.shape[0]
    tm = _conv_tile(l)
    nt = l // tm

    def body(a_ref, gt_ref, dc_ref, w_ref, o_ref, buf):
        first = pl.program_id(0) == 0

        @pl.when(first)
        def _():
            buf[tm:tm + CONV_HALO, :] = jnp.zeros((CONV_HALO, CONV_CH), F32)

        @pl.when(jnp.logical_not(first))
        def _():
            buf[tm:tm + CONV_HALO, :] = buf[0:CONV_HALO, :]

        buf[0:tm, :] = dc_ref[...]
        w = w_ref[...]
        dhc = jnp.zeros((tm, CONV_CH), F32)
        for k in range(CONV_K):
            off = CONV_K - 1 - k
            dhc = dhc + w[k:k + 1, :] * buf[off:off + tm, :]
        a = a_ref[...]
        sg = _sigmoid(gt_ref[...])
        o_ref[:, 0:CONV_CH] = (dhc * sg).astype(BF16)
        o_ref[:, CONV_CH:2 * CONV_CH] = (dhc * a * sg * (1.0 - sg)).astype(BF16)

    return pl.pallas_call(
        body, name=name, grid=(nt,),
        in_specs=[pl.BlockSpec((tm, CONV_CH), lambda i: (nt - 1 - i, 0)),
                  pl.BlockSpec((tm, CONV_CH), lambda i: (nt - 1 - i, 1)),
                  pl.BlockSpec((tm, CONV_CH), lambda i: (nt - 1 - i, 0)),
                  pl.BlockSpec((CONV_K, CONV_CH), lambda i: (0, 0))],
        out_specs=pl.BlockSpec((tm, 2 * CONV_CH), lambda i: (nt - 1 - i, 0)),
        out_shape=jax.ShapeDtypeStruct((l, 2 * CONV_CH), BF16),
        scratch_shapes=[pltpu.VMEM((tm + CONV_HALO, CONV_CH), F32)],
        compiler_params=_params(("arbitrary",)),
    )(pa, pa, dc, dw_w)


def _pool_consts(tm, row0):
    lane = lax.broadcasted_iota(jnp.int32, (1, POOL_CH), 1)
    grp = lane // (POOL_CH // len(POOL_WINDOWS))
    win = jnp.where(grp == 0, 2.0, jnp.where(grp == 1, 4.0, jnp.where(grp == 2, 8.0, 16.0))).astype(F32)
    pos = (row0 + lax.broadcasted_iota(jnp.int32, (tm, 1), 0) - PAD).astype(F32)
    cnt = jnp.maximum(jnp.minimum(pos + 1.0, win), 1.0)
    return grp, cnt


def _pool_select(grp, s2, s4, s8, s16):
    return jnp.where(grp == 0, s2, jnp.where(grp == 1, s4, jnp.where(grp == 2, s8, s16)))


def _pool_fwd(pa, *, name):
    l = pa.shape[0]
    tm = _conv_tile(l)
    ext = POOL_HALO + tm

    def body(p_ref, o_ref, buf):
        i = pl.program_id(0)

        @pl.when(i == 0)
        def _():
            buf[0:POOL_HALO, :] = jnp.zeros((POOL_HALO, POOL_CH), F32)

        @pl.when(i > 0)
        def _():
            buf[0:POOL_HALO, :] = buf[tm:tm + POOL_HALO, :]

        p = p_ref[...]
        buf[POOL_HALO:ext, :] = p
        x = buf[...]
        s2 = x + pltpu.roll(x, 1, 0)
        s4 = s2 + pltpu.roll(s2, 2, 0)
        s8 = s4 + pltpu.roll(s4, 4, 0)
        s16 = s8 + pltpu.roll(s8, 8, 0)
        grp, cnt = _pool_consts(tm, i * tm)
        s = _pool_select(grp, s2, s4, s8, s16)[POOL_HALO:ext, :]
        o_ref[...] = (s / cnt - p).astype(BF16)

    return pl.pallas_call(
        body, name=name, grid=(l // tm,),
        in_specs=[pl.BlockSpec((tm, POOL_CH), lambda i: (i, 2))],
        out_specs=pl.BlockSpec((tm, POOL_CH), lambda i: (i, 0)),
        out_shape=jax.ShapeDtypeStruct((l, POOL_CH), BF16),
        scratch_shapes=[pltpu.VMEM((ext, POOL_CH), F32)],
        compiler_params=_params(("arbitrary",)),
    )(pa)


def _pool_bwd(dpooled, *, name):
    l = dpooled.shape[0]
    tm = _conv_tile(l)
    nt = l // tm
    ext = tm + POOL_HALO

    def body(d_ref, o_ref, buf):
        i = pl.program_id(0)

        @pl.when(i == 0)
        def _():
            buf[tm:ext, :] = jnp.zeros((POOL_HALO, POOL_CH), F32)

        @pl.when(i > 0)
        def _():
            buf[tm:ext, :] = buf[0:POOL_HALO, :]

        d = d_ref[...]
        grp, cnt = _pool_consts(tm, (nt - 1 - i) * tm)
        buf[0:tm, :] = d / cnt
        x = buf[...]
        s2 = x + pltpu.roll(x, ext - 1, 0)
        s4 = s2 + pltpu.roll(s2, ext - 2, 0)
        s8 = s4 + pltpu.roll(s4, ext - 4, 0)
        s16 = s8 + pltpu.roll(s8, ext - 8, 0)
        s = _pool_select(grp, s2, s4, s8, s16)[0:tm, :]
        o_ref[...] = (s - d).astype(BF16)

    return pl.pallas_call(
        body, name=name, grid=(nt,),
        in_specs=[pl.BlockSpec((tm, POOL_CH), lambda i: (nt - 1 - i, 0))],
        out_specs=pl.BlockSpec((tm, POOL_CH), lambda i: (nt - 1 - i, 0)),
        out_shape=jax.ShapeDtypeStruct((l, POOL_CH), BF16),
        scratch_shapes=[pltpu.VMEM((ext, POOL_CH), F32)],
        compiler_params=_params(("arbitrary",)),
    )(dpooled)


ATT_TQ = 256
ATT_TK = 5 * BLOCK
ATT_SUB = ATT_TK // BLOCK
LOG2E = 1.4426950408889634
LN2 = 0.6931471805599453
Q_SCALE = HEAD_DIM ** -0.5 * LOG2E
ATT_CUT = 160.0


def _tri_ones():
    r = lax.broadcasted_iota(jnp.int32, (2 * BLOCK, 2 * BLOCK), 0) % BLOCK
    c = lax.broadcasted_iota(jnp.int32, (2 * BLOCK, 2 * BLOCK), 1)
    return jnp.where((c >= BLOCK) | (r > c), 1.0, 0.0).astype(BF16)


def _split_dot(x, rhs):
    hi = x.astype(BF16)
    lo = (x - hi.astype(F32)).astype(BF16)
    return jnp.dot(jnp.concatenate([hi, lo], axis=1), rhs, preferred_element_type=F32)


def _scores(q, kt, qpos, base, masked):
    z = lax.dot_general(q, kt, (((1,), (1,)), ((), ())), preferred_element_type=F32)
    sp = jnp.log2(1.0 + jnp.exp2(-jnp.abs(z)))
    lb = jnp.minimum(z, 0.0) - sp
    lk = lb - z
    valid = None
    if masked:
        kpos = base + lax.broadcasted_iota(jnp.int32, (1, z.shape[1]), 1)
        valid = (kpos < qpos) & (kpos >= PAD)
        lk = jnp.where(valid, lk, 0.0)
    return lk, lb, valid


def _suffix(x, tri, carry):
    wts = [_split_dot(x[:, b * BLOCK:(b + 1) * BLOCK], tri) for b in range(ATT_SUB)]
    offs = [None] * ATT_SUB
    s = carry
    for b in reversed(range(ATT_SUB)):
        offs[b] = wts[b][:, :BLOCK] + s
        s = s + wts[b][:, BLOCK:]
    return jnp.concatenate(offs, axis=1), s


def _walk_tiles(i, tq, step):
    t_top = ((i + 1) * tq - 1) // ATT_TK
    t_diag = (i * tq) // ATT_TK
    n_plain = jnp.maximum(t_diag - 1, 0)

    def masked(jj, top):
        return step(t_top - jj, True)

    def live(carry):
        return (carry[0] < n_plain) & (carry[1] > -ATT_CUT)

    def plain(carry):
        return carry[0] + 1, step(t_diag - 1 - carry[0], False)

    top = lax.fori_loop(0, t_top - t_diag + 1, masked, jnp.float32(0.0))
    _, top = lax.while_loop(live, plain, (jnp.int32(0), top))

    @pl.when((t_diag > 0) & (top > -ATT_CUT))
    def _():
        step(0, True)


def _tile_base(t):
    base = t * ATT_TK
    return base if isinstance(base, int) else pl.multiple_of(base, BLOCK)


def _attn_fwd(qkv, *, name):
    l = qkv.shape[0]
    tq = ATT_TQ
    assert l % tq == 0 and l % ATT_TK == 0

    def body(q_ref, k_ref, v_ref, o_ref, o32_ref, acc_ref, r_ref):
        i = pl.program_id(1)
        acc_ref[...] = jnp.zeros_like(acc_ref)
        r_ref[...] = jnp.zeros_like(r_ref)
        q = q_ref[...]
        qpos = i * tq + lax.broadcasted_iota(jnp.int32, (tq, 1), 0)
        tri = _tri_ones()

        def step(t, masked):
            base = _tile_base(t)
            lk, lb, valid = _scores(q, k_ref[pl.ds(base, ATT_TK), :], qpos, base, masked)
            off, r_new = _suffix(lk, tri, r_ref[...])
            a = jnp.exp2(lb + off)
            if masked:
                a = jnp.where(valid, a, 0.0)
            acc_ref[...] += jnp.dot(a.astype(BF16), v_ref[pl.ds(base, ATT_TK), :], preferred_element_type=F32)
            r_ref[...] = r_new
            return jnp.max(r_new)

        _walk_tiles(i, tq, step)
        o_ref[...] = acc_ref[...].astype(BF16)
        o32_ref[...] = acc_ref[...]

    tile = pl.BlockSpec((tq, HEAD_DIM), lambda h, i: (i, h))
    return pl.pallas_call(
        body, name=name, grid=(HEADS, l // tq),
        in_specs=[tile,
                  pl.BlockSpec((l, HEAD_DIM), lambda h, i: (0, HEADS + h)),
                  pl.BlockSpec((l, HEAD_DIM), lambda h, i: (0, 2 * HEADS + h))],
        out_specs=[tile, tile],
        out_shape=[jax.ShapeDtypeStruct((l, HEADS * HEAD_DIM), BF16),
                   jax.ShapeDtypeStruct((l, HEADS * HEAD_DIM), F32)],
        scratch_shapes=[pltpu.VMEM((tq, HEAD_DIM), F32), pltpu.VMEM((tq, BLOCK), F32)],
        compiler_params=_params(("parallel", "arbitrary")),
    )(qkv, qkv, qkv)


def _attn_bwd(qkv, att, datt, *, name):
    l = qkv.shape[0]
    tq = ATT_TQ
    nq = l // tq
    assert l % tq == 0 and l % ATT_TK == 0

    def body(q_ref, k_ref, v_ref, o_ref, do_ref, dq_ref, dk_hbm, dv_hbm, dk_acc, dv_acc, dq_acc, r_ref, s_ref, sem):
        h = pl.program_id(0)
        i = pl.program_id(1)

        @pl.when(i == 0)
        def _():
            dk_acc[...] = jnp.zeros_like(dk_acc)
            dv_acc[...] = jnp.zeros_like(dv_acc)

        dq_acc[...] = jnp.zeros_like(dq_acc)
        r_ref[...] = jnp.zeros_like(r_ref)
        s_ref[...] = jnp.zeros_like(s_ref)
        q = q_ref[...]
        do = do_ref[...]
        ptot = jnp.sum(do.astype(F32) * o_ref[...], axis=-1, keepdims=True)
        qpos = i * tq + lax.broadcasted_iota(jnp.int32, (tq, 1), 0)
        tri = _tri_ones()

        def step(t, masked):
            base = _tile_base(t)
            kt = k_ref[pl.ds(base, ATT_TK), :]
            vt = v_ref[pl.ds(base, ATT_TK), :]
            lk, lb, valid = _scores(q, kt, qpos, base, masked)
            off, r_new = _suffix(lk, tri, r_ref[...])
            a = jnp.exp2(lb + off)
            if masked:
                a = jnp.where(valid, a, 0.0)
            ab = a.astype(BF16)
            da = lax.dot_general(do, vt, (((1,), (1,)), ((), ())), preferred_element_type=F32)
            p = ab.astype(F32) * da
            poff, s_new = _suffix(p, tri, s_ref[...])
            dz = (p - jnp.exp2(lb) * (ptot - poff)) * LN2
            if masked:
                dz = jnp.where(valid, dz, 0.0)
            dzb = dz.astype(BF16)
            dq_acc[...] += jnp.dot(dzb, kt, preferred_element_type=F32)
            dk_acc[pl.ds(base, ATT_TK), :] += lax.dot_general(dzb, q, (((0,), (0,)), ((), ())),
                                                              preferred_element_type=F32)
            dv_acc[pl.ds(base, ATT_TK), :] += lax.dot_general(ab, do, (((0,), (0,)), ((), ())),
                                                              preferred_element_type=F32)
            r_ref[...] = r_new
            s_ref[...] = s_new
            return jnp.max(r_new)

        _walk_tiles(i, tq, step)
        dq_ref[...] = (dq_acc[...] * Q_SCALE).astype(BF16)

        @pl.when(i == nq - 1)
        def _():
            ck = pltpu.make_async_copy(dk_acc, dk_hbm.at[h], sem.at[0])
            cv = pltpu.make_async_copy(dv_acc, dv_hbm.at[h], sem.at[1])
            ck.start()
            cv.start()
            ck.wait()
            cv.wait()

    tile = pl.BlockSpec((tq, HEAD_DIM), lambda h, i: (i, h))
    return pl.pallas_call(
        body, name=name, grid=(HEADS, nq),
        in_specs=[tile,
                  pl.BlockSpec((l, HEAD_DIM), lambda h, i: (0, HEADS + h)),
                  pl.BlockSpec((l, HEAD_DIM), lambda h, i: (0, 2 * HEADS + h)),
                  tile, tile],
        out_specs=[tile, pl.BlockSpec(memory_space=pl.ANY), pl.BlockSpec(memory_space=pl.ANY)],
        out_shape=[jax.ShapeDtypeStruct((l, HEADS * HEAD_DIM), BF16),
                   jax.ShapeDtypeStruct((HEADS, l, HEAD_DIM), F32), jax.ShapeDtypeStruct((HEADS, l, HEAD_DIM), F32)],
        scratch_shapes=[pltpu.VMEM((l, HEAD_DIM), F32), pltpu.VMEM((l, HEAD_DIM), F32),
                        pltpu.VMEM((tq, HEAD_DIM), F32), pltpu.VMEM((tq, BLOCK), F32), pltpu.VMEM((tq, BLOCK), F32),
                        pltpu.SemaphoreType.DMA((2,))],
        compiler_params=_params(("arbitrary", "arbitrary")),
    )(qkv, qkv, qkv, att, datt)


MIX_TM = 256


def _mix_branches(s_ref, p_ref, t_ref, g_ref, wa_ref, wb_ref, wc_ref, ba_ref, sc_ref, d):
    ya = jnp.dot(s_ref[...], wa_ref[...], preferred_element_type=F32) + ba_ref[...]
    yb0 = jnp.dot(p_ref[...], wb_ref[...], preferred_element_type=F32)
    yc = jnp.dot(t_ref[...], wc_ref[...], preferred_element_type=F32)
    g0 = _sigmoid(g_ref[:, 0:d])
    g1 = _sigmoid(g_ref[:, d:2 * d])
    g2 = _sigmoid(g_ref[:, 2 * d:3 * d])
    return ya, yb0, yc, g0, g1, g2


def _mix_specs(tm, d):
    row = lambda w: pl.BlockSpec((tm, w), lambda i: (i, 0))
    full = lambda r: pl.BlockSpec((r, d), lambda i: (0, 0))
    return [row(CONV_CH), row(POOL_CH), row(HEADS * HEAD_DIM), row(3 * d),
            full(CONV_CH), full(POOL_CH), full(HEADS * HEAD_DIM), full(1), full(1)]


def _mix_fwd(s, pooled, att, gates, wa, wb, wc, ba, scale, *, name):
    l, d = s.shape[0], wa.shape[1]
    tm = _tile(l, MIX_TM, 128)

    def body(s_ref, p_ref, t_ref, g_ref, wa_ref, wb_ref, wc_ref, ba_ref, sc_ref, o_ref):
        ya, yb0, yc, g0, g1, g2 = _mix_branches(s_ref, p_ref, t_ref, g_ref, wa_ref, wb_ref, wc_ref, ba_ref, sc_ref, d)
        o_ref[...] = (g0 * ya + g1 * (yb0 * sc_ref[...]) + g2 * yc).astype(BF16)

    return pl.pallas_call(
        body, name=name, grid=(l // tm,), in_specs=_mix_specs(tm, d),
        out_specs=pl.BlockSpec((tm, d), lambda i: (i, 0)),
        out_shape=jax.ShapeDtypeStruct((l, d), BF16),
        compiler_params=_params(("parallel",)),
    )(s, pooled, att, gates, wa, wb, wc, ba.reshape(1, d), scale.reshape(1, d))


def _mix_bwd(s, pooled, att, gates, wa, wb, wc, ba, scale, dmixed, *, name):
    l, d = s.shape[0], wa.shape[1]
    tm = _tile(l, MIX_TM, 128)

    def body(s_ref, p_ref, t_ref, g_ref, wa_ref, wb_ref, wc_ref, ba_ref, sc_ref, dm_ref,
             dg_ref, dya_ref, dyb_ref, dyc_ref, vec_ref):
        @pl.when(pl.program_id(0) == 0)
        def _():
            vec_ref[...] = jnp.zeros_like(vec_ref)

        ya, yb0, yc, g0, g1, g2 = _mix_branches(s_ref, p_ref, t_ref, g_ref, wa_ref, wb_ref, wc_ref, ba_ref, sc_ref, d)
        dm = dm_ref[...].astype(F32)
        sc = sc_ref[...]
        dg_ref[:, 0:d] = (dm * ya * g0 * (1.0 - g0)).astype(BF16)
        dg_ref[:, d:2 * d] = (dm * (yb0 * sc) * g1 * (1.0 - g1)).astype(BF16)
        dg_ref[:, 2 * d:3 * d] = (dm * yc * g2 * (1.0 - g2)).astype(BF16)
        dya = dm * g0
        dyb = dm * g1
        dya_ref[...] = dya.astype(BF16)
        dyb_ref[...] = (dyb * sc).astype(BF16)
        dyc_ref[...] = (dm * g2).astype(BF16)
        vec_ref[0:1, :] += jnp.sum(dya, axis=0, keepdims=True)
        vec_ref[1:2, :] += jnp.sum(dyb * yb0, axis=0, keepdims=True)

    row = lambda w: pl.BlockSpec((tm, w), lambda i: (i, 0))
    outs = pl.pallas_call(
        body, name=name, grid=(l // tm,), in_specs=_mix_specs(tm, d) + [row(d)],
        out_specs=[row(3 * d), row(d), row(d), row(d), pl.BlockSpec((8, d), lambda i: (0, 0))],
        out_shape=[jax.ShapeDtypeStruct((l, 3 * d), BF16), jax.ShapeDtypeStruct((l, d), BF16),
                   jax.ShapeDtypeStruct((l, d), BF16), jax.ShapeDtypeStruct((l, d), BF16),
                   jax.ShapeDtypeStruct((8, d), F32)],
        compiler_params=_params(("arbitrary",)),
    )(s, pooled, att, gates, wa, wb, wc, ba.reshape(1, d), scale.reshape(1, d), dmixed)
    return outs


FFN_TC = 512
_GELU_C = 0.7978845608028654
_GELU_A = 0.044715


def _gelu(x):
    th = jnp.tanh(_GELU_C * (x + _GELU_A * x * x * x))
    return 0.5 * x * (1.0 + th), th


def _gelu_grad(x, th):
    return 0.5 * (1.0 + th) + 0.5 * x * (1.0 - th * th) * _GELU_C * (1.0 + 3.0 * _GELU_A * x * x)


FFN_CH = 32


def _ffn_taps(win):
    return (pltpu.roll(win, 2, 0)[FFN_HALO:, :], pltpu.roll(win, 1, 0)[FFN_HALO:, :], win[FFN_HALO:, :])


def _ffn_conv(taps, w, b):
    return b + w[0:1, :] * taps[0] + w[1:2, :] * taps[1] + w[2:3, :] * taps[2]


def _fold8(x):
    acc = x[0:8, :]
    for r in range(8, x.shape[0], 8):
        acc = acc + x[r:r + 8, :]
    return acc


def _ffn_fwd(ug, uv, wg, wv, bg, bv, *, name):
    l, f = ug.shape
    tm = _conv_tile(l)
    tc = _tile(f, FFN_TC, 128)
    ext = FFN_HALO + tm

    def body(ug_ref, uv_ref, wg_ref, wv_ref, bg_ref, bv_ref, o_ref, bufg, bufv):
        i = pl.program_id(1)
        for buf, u_ref in ((bufg, ug_ref), (bufv, uv_ref)):
            @pl.when(i == 0)
            def _():
                buf[0:FFN_HALO, :] = jnp.zeros((FFN_HALO, tc), F32)

            @pl.when(i > 0)
            def _():
                buf[0:FFN_HALO, :] = buf[tm:ext, :]

            buf[FFN_HALO:ext, :] = u_ref[...]
        wg, wv, bg_, bv_ = wg_ref[...], wv_ref[...], bg_ref[...], bv_ref[...]

        def chunk(c, carry):
            r0 = pl.multiple_of(c * FFN_CH, FFN_CH)
            gc = _ffn_conv(_ffn_taps(bufg[pl.ds(r0, FFN_HALO + FFN_CH), :]), wg, bg_)
            vc = _ffn_conv(_ffn_taps(bufv[pl.ds(r0, FFN_HALO + FFN_CH), :]), wv, bv_)
            o_ref[pl.ds(r0, FFN_CH), :] = (_gelu(gc)[0] * vc).astype(BF16)
            return carry

        lax.fori_loop(0, tm // FFN_CH, chunk, 0)

    assert tm % FFN_CH == 0
    tile = pl.BlockSpec((tm, tc), lambda j, i: (i, j))
    wspec = pl.BlockSpec((FFN_K, tc), lambda j, i: (0, j))
    bspec = pl.BlockSpec((1, tc), lambda j, i: (0, j))
    return pl.pallas_call(
        body, name=name, grid=(f // tc, l // tm),
        in_specs=[tile, tile, wspec, wspec, bspec, bspec], out_specs=tile,
        out_shape=jax.ShapeDtypeStruct((l, f), BF16),
        scratch_shapes=[pltpu.VMEM((ext, tc), F32), pltpu.VMEM((ext, tc), F32)],
        compiler_params=_params(("parallel", "arbitrary")),
    )(ug, uv, wg, wv, bg.reshape(1, f), bv.reshape(1, f))


def _ffn_bwd(ug, uv, wg, wv, bg, bv, dact, *, name):
    l, f = ug.shape
    tm = _conv_tile(l)
    tc = _tile(f, FFN_TC, 128)
    nt = l // tm
    ext = FFN_HALO + tm
    hb = tm // FFN_HALO

    def body(ug_ref, uv_ref, pg_ref, pv_ref, wg_ref, wv_ref, bg_ref, bv_ref, da_ref,
             dug_ref, duv_ref, gg_ref, gv_ref, bufg, bufv, dbufg, dbufv, gaccg, gaccv):
        i = pl.program_id(1)
        last = i == nt - 1

        @pl.when(i == 0)
        def _():
            gg_ref[...] = jnp.zeros_like(gg_ref)
            gv_ref[...] = jnp.zeros_like(gv_ref)

        for buf, u_ref, prev_ref in ((bufg, ug_ref, pg_ref), (bufv, uv_ref, pv_ref)):
            buf[0:FFN_HALO, :] = jnp.where(last, 0.0, prev_ref[...])
            buf[FFN_HALO:ext, :] = u_ref[...]
        for dbuf, gacc in ((dbufg, gaccg), (dbufv, gaccv)):
            @pl.when(i == 0)
            def _():
                dbuf[tm:ext, :] = jnp.zeros((FFN_HALO, tc), F32)

            @pl.when(i > 0)
            def _():
                dbuf[tm:ext, :] = dbuf[0:FFN_HALO, :]

            gacc[...] = jnp.zeros_like(gacc)
        wg, wv, bg_, bv_ = wg_ref[...], wv_ref[...], bg_ref[...], bv_ref[...]
        win_rows = FFN_CH + FFN_HALO

        def chunk(cc, carry):
            r0 = pl.multiple_of((tm // FFN_CH - 1 - cc) * FFN_CH, FFN_CH)
            taps_g = _ffn_taps(bufg[pl.ds(r0, win_rows), :])
            taps_v = _ffn_taps(bufv[pl.ds(r0, win_rows), :])
            gc = _ffn_conv(taps_g, wg, bg_)
            vc = _ffn_conv(taps_v, wv, bv_)
            ge, th = _gelu(gc)
            da = da_ref[pl.ds(r0, FFN_CH), :].astype(F32)
            for dc, taps, w, dbuf, du_ref, gacc in ((da * vc * _gelu_grad(gc, th), taps_g, wg, dbufg, dug_ref, gaccg),
                                                    (da * ge, taps_v, wv, dbufv, duv_ref, gaccv)):
                dbuf[pl.ds(r0, FFN_CH), :] = dc
                dwin = dbuf[pl.ds(r0, win_rows), :]
                du = (w[2:3, :] * dc + w[1:2, :] * pltpu.roll(dwin, win_rows - 1, 0)[0:FFN_CH, :]
                      + w[0:1, :] * pltpu.roll(dwin, win_rows - 2, 0)[0:FFN_CH, :])
                du_ref[pl.ds(r0, FFN_CH), :] = du.astype(BF16)
                for k in range(FFN_K):
                    gacc[8 * k:8 * k + 8, :] += _fold8(dc * taps[k])
                gacc[24:32, :] += _fold8(dc)
            return carry

        lax.fori_loop(0, tm // FFN_CH, chunk, 0)
        for gacc, gp_ref in ((gaccg, gg_ref), (gaccv, gv_ref)):
            for k in range(FFN_K + 1):
                gp_ref[k:k + 1, :] += jnp.sum(gacc[8 * k:8 * k + 8, :], axis=0, keepdims=True)

    assert tm % FFN_CH == 0
    tile = pl.BlockSpec((tm, tc), lambda j, i: (nt - 1 - i, j))
    prev = pl.BlockSpec((FFN_HALO, tc), lambda j, i: (jnp.maximum((nt - 1 - i) * hb - 1, 0), j))
    wspec = pl.BlockSpec((FFN_K, tc), lambda j, i: (0, j))
    bspec = pl.BlockSpec((1, tc), lambda j, i: (0, j))
    gspec = pl.BlockSpec((8, tc), lambda j, i: (0, j))
    return pl.pallas_call(
        body, name=name, grid=(f // tc, nt),
        in_specs=[tile, tile, prev, prev, wspec, wspec, bspec, bspec, tile],
        out_specs=[tile, tile, gspec, gspec],
        out_shape=[jax.ShapeDtypeStruct((l, f), BF16), jax.ShapeDtypeStruct((l, f), BF16),
                   jax.ShapeDtypeStruct((8, f), F32), jax.ShapeDtypeStruct((8, f), F32)],
        scratch_shapes=[pltpu.VMEM((ext, tc), F32), pltpu.VMEM((ext, tc), F32),
                        pltpu.VMEM((ext, tc), F32), pltpu.VMEM((ext, tc), F32),
                        pltpu.VMEM((32, tc), F32), pltpu.VMEM((32, tc), F32)],
        compiler_params=_params(("parallel", "arbitrary")),
    )(ug, uv, ug, uv, wg, wv, bg.reshape(1, f), bv.reshape(1, f), dact)


def _adamw(parts, w, m, v, *, name):
    r = w.shape[0]
    tr = _tile(r, 1024, 8)
    c1 = 1.0 / (1.0 - ADAM_B1 ** ADAM_STEP)
    c2 = 1.0 / (1.0 - ADAM_B2 ** ADAM_STEP)

    def body(p_ref, w_ref, m_ref, v_ref, g_ref, d_ref, nm_ref, nv_ref):
        g = p_ref[0]
        for k in range(1, N_DEV):
            g = g + p_ref[k]
        nm = ADAM_B1 * m_ref[...] + (1.0 - ADAM_B1) * g
        nv = ADAM_B2 * v_ref[...] + (1.0 - ADAM_B2) * (g * g)
        g_ref[...] = g
        nm_ref[...] = nm
        nv_ref[...] = nv
        d_ref[...] = -ADAM_LR * ((nm * c1) / (jnp.sqrt(nv * c2) + ADAM_EPS) + ADAM_WD * w_ref[...])

    tile = pl.BlockSpec((tr, 128), lambda i: (i, 0))
    return pl.pallas_call(
        body, name=name, grid=(r // tr,),
        in_specs=[pl.BlockSpec((N_DEV, tr, 128), lambda i: (0, i, 0)), tile, tile, tile],
        out_specs=[tile, tile, tile, tile],
        out_shape=[jax.ShapeDtypeStruct((r, 128), F32)] * 4,
        compiler_params=_params(("parallel",)),
    )(parts, w, m, v)


def _place():
    return lax.axis_index("x"), lax.axis_index("y"), lax.axis_index("c")


def _all_gather(x, *, name):
    def body(x_ref, out_ref, send_sems, recv_sems, local_sem):
        xx, yy, cc = _place()
        me, sibling = (xx, yy, cc), (xx, yy, 1 - cc)
        chips = [(1 - xx, yy), (xx, 1 - yy), (1 - xx, 1 - yy)]

        def slot(px, py, pc):
            return out_ref.at[4 * px + 2 * py + pc]

        def copy(k, block, to, src=None):
            return pltpu.make_async_remote_copy(
                src_ref=slot(*block) if src is None else src, dst_ref=slot(*block),
                send_sem=send_sems.at[k], recv_sem=recv_sems.at[k], device_id=to, device_id_type=MESH)

        mine = pltpu.make_async_copy(x_ref, slot(*me), local_sem)
        mine.start()
        first = [copy(0, me, sibling, src=x_ref)]
        first += [copy(1 + j, me, (*chip, cc), src=x_ref) for j, chip in enumerate(chips)]
        for cp in first:
            cp.start()
        passed = [copy(4 + j, (*chip, cc), sibling) for j, chip in enumerate(chips)]
        for j, chip in enumerate(chips):
            copy(1 + j, (*chip, cc), me).wait_recv()
            passed[j].start()
        copy(0, sibling, me).wait_recv()
        for j, chip in enumerate(chips):
            copy(4 + j, (*chip, 1 - cc), me).wait_recv()
        for cp in first + passed:
            cp.wait_send()
        mine.wait()

    return pl.pallas_call(
        body, name=name,
        in_specs=[pl.BlockSpec(memory_space=pl.ANY)], out_specs=pl.BlockSpec(memory_space=pl.ANY),
        out_shape=jax.ShapeDtypeStruct((N_DEV,) + x.shape, x.dtype),
        scratch_shapes=[pltpu.SemaphoreType.DMA((7,)), pltpu.SemaphoreType.DMA((7,)), pltpu.SemaphoreType.DMA],
    )(x)


def _all_to_all(send, *, name):
    def body(s_ref, r_ref, send_sems, recv_sems, local_sem):
        xx, yy, cc = _place()
        me = 4 * xx + 2 * yy + cc
        local = pltpu.make_async_copy(s_ref.at[me], r_ref.at[me], local_sem)
        local.start()
        copies = []
        for m in range(1, N_DEV):
            px = 1 - xx if m & 4 else xx
            py = 1 - yy if m & 2 else yy
            pc = 1 - cc if m & 1 else cc
            copies.append(pltpu.make_async_remote_copy(
                src_ref=s_ref.at[4 * px + 2 * py + pc], dst_ref=r_ref.at[me],
                send_sem=send_sems.at[m - 1], recv_sem=recv_sems.at[m - 1],
                device_id=(px, py, pc), device_id_type=MESH))
        for cp in copies:
            cp.start()
        for cp in copies:
            cp.wait_recv()
        for cp in copies:
            cp.wait_send()
        local.wait()

    return pl.pallas_call(
        body, name=name,
        in_specs=[pl.BlockSpec(memory_space=pl.ANY)], out_specs=pl.BlockSpec(memory_space=pl.ANY),
        out_shape=jax.ShapeDtypeStruct(send.shape, send.dtype),
        scratch_shapes=[pltpu.SemaphoreType.DMA((7,)), pltpu.SemaphoreType.DMA((7,)), pltpu.SemaphoreType.DMA],
    )(send)


def _as_rows(a, lead, dtype):
    a = a.astype(dtype)
    size = 1
    for s in a.shape[len(lead):]:
        size *= s
    if size % PACK_ALIGN:
        a = jnp.pad(a.reshape(lead + (size,)), [(0, 0)] * len(lead) + [(0, (-size) % PACK_ALIGN)])
    return a.reshape(lead + (-1, 128))


def _pack(arrays, dtype):
    return jnp.concatenate([_as_rows(a, (), dtype) for a in arrays], axis=0)


def _pack_pieces(arrays, dtype):
    return jnp.concatenate([_as_rows(a, (N_DEV,), dtype) for a in arrays], axis=1)


def _unpack(buf, shapes, lead=()):
    out, row = [], 0
    for shp in shapes:
        size = 1
        for s in shp:
            size *= s
        rows = (size + (-size) % PACK_ALIGN) // 128
        part = buf[..., row:row + rows, :]
        if size % PACK_ALIGN:
            part = part.reshape(lead + (rows * 128,))[..., :size]
        out.append(part.reshape(lead + tuple(shp)))
        row += rows
    return out


def _unshard(g, axis):
    g = jnp.moveaxis(g, 0, axis)
    shp = list(g.shape)
    return g.reshape(shp[:axis] + [shp[axis] * shp[axis + 1]] + shp[axis + 2:])


def _pieces(full, axis):
    shp = list(full.shape)
    g = full.reshape(shp[:axis] + [N_DEV, shp[axis] // N_DEV] + shp[axis + 1:])
    return jnp.moveaxis(g, axis, 0)


SHARDED = (("meta", 1), ("w_in", 2), ("conv_dw_w", 2), ("w_conv_out", 2), ("w_pool_grp", 3), ("w_attn_out", 2),
           ("w_o", 1), ("w_up", 2), ("ffn_dw_w", 2), ("w_down", 1))
MATRICES = ("w_in", "w_conv_out", "w_pool_grp", "w_attn_out", "w_o", "w_up", "w_down")
REPLICATED = ("norm1", "conv_dw_b", "conv_ln_g", "conv_ln_b", "b_conv_out", "pool_scale", "norm2", "ffn_dw_b",
              "final_norm")
WEIGHTS = ("meta", "norm1", "w_in", "conv_dw_w", "conv_dw_b", "conv_ln_g", "conv_ln_b", "w_conv_out", "b_conv_out",
           "w_pool_grp", "pool_scale", "w_attn_out", "w_o", "norm2", "w_up", "ffn_dw_w", "ffn_dw_b", "w_down",
           "final_norm")


def _block_diag(w_grp):
    g, gc, od = w_grp.shape
    out = jnp.zeros((g * gc, g * od), w_grp.dtype)
    for i in range(g):
        out = out.at[i * gc:(i + 1) * gc, i * od:(i + 1) * od].set(w_grp[i])
    return out


def _block_diag_grad(gw, g):
    gc, od = gw.shape[0] // g, gw.shape[1] // g
    return jnp.stack([gw[i * gc:(i + 1) * gc, i * od:(i + 1) * od] for i in range(g)])


C_CONV = 2 * CONV_CH
C_POOL = C_CONV + POOL_CH
C_ATT = HEADS * HEAD_DIM
C_QKV = C_POOL + 3 * C_ATT


def _layer_fwd(h, p, tag):
    d = h.shape[1]
    w_in = p["w_in"]
    hn = _rms_fwd(h, p["norm1"], name=f"rms1_{tag}")
    pa = _mm(hn, w_in[:, :C_POOL], out_dtype=F32, name=f"proj_a_{tag}")
    q_scale = jnp.concatenate([jnp.full((C_ATT,), Q_SCALE, F32), jnp.ones((2 * C_ATT,), F32)])
    qkv = _mm(hn, w_in[:, C_POOL:C_QKV], out_dtype=BF16, col_scale=q_scale, name=f"proj_qkv_{tag}")
    gates = _mm(hn, w_in[:, C_QKV:], out_dtype=F32, name=f"proj_g_{tag}")
    s = _conv_fwd(pa, p["conv_dw_w"], p["conv_dw_b"], p["conv_ln_g"], p["conv_ln_b"], name=f"conv_{tag}")
    pooled = _pool_fwd(pa, name=f"pool_{tag}")
    att, att32 = _attn_fwd(qkv, name=f"attn_{tag}")
    wb = _block_diag(p["w_pool_grp"])
    mixed = _mix_fwd(s, pooled, att, gates, p["w_conv_out"], wb, p["w_attn_out"], p["b_conv_out"], p["pool_scale"],
                     name=f"mix_{tag}")
    h1 = _mm(mixed, p["w_o"], out_dtype=F32, res=h, mask_rows=True, name=f"wo_{tag}")
    hn2 = _rms_fwd(h1, p["norm2"], name=f"rms2_{tag}")
    f = p["w_up"].shape[1] // 2
    ug = _mm(hn2, p["w_up"][:, :f], out_dtype=F32, name=f"up_g_{tag}")
    uv = _mm(hn2, p["w_up"][:, f:], out_dtype=F32, name=f"up_v_{tag}")
    act = _ffn_fwd(ug, uv, p["ffn_dw_w"][:, :f], p["ffn_dw_w"][:, f:], p["ffn_dw_b"][:f], p["ffn_dw_b"][f:],
                   name=f"ffn_{tag}")
    h2 = _mm(act, p["w_down"], out_dtype=F32, res=h1, mask_rows=True, name=f"down_{tag}")
    saved = dict(h=h, hn=hn, pa=pa, qkv=qkv, gates=gates, s=s, pooled=pooled, att=att, att32=att32, wb=wb, mixed=mixed,
                 h1=h1,
                 hn2=hn2, ug=ug, uv=uv, act=act)
    return h2, saved


def _layer_bwd(dh2, p, sv, tag):
    g = {}
    f = p["w_up"].shape[1] // 2
    dact = _mm(dh2, p["w_down"].T, out_dtype=F32, name=f"b_down_{tag}")
    g["w_down"] = _mm_tn(sv["act"], dh2, name=f"g_down_{tag}")
    dug, duv, gpg, gpv = _ffn_bwd(sv["ug"], sv["uv"], p["ffn_dw_w"][:, :f], p["ffn_dw_w"][:, f:], p["ffn_dw_b"][:f],
                                  p["ffn_dw_b"][f:], dact, name=f"b_ffn_{tag}")
    g["ffn_dw_w"] = jnp.concatenate([gpg[0:FFN_K], gpv[0:FFN_K]], axis=1)
    g["ffn_dw_b"] = jnp.concatenate([gpg[FFN_K], gpv[FFN_K]])
    w_up_t = p["w_up"].T
    dhn2 = _mm(dug, w_up_t[:f], out_dtype=F32, name=f"b_up_g_{tag}")
    dhn2 = _mm(duv, w_up_t[f:], out_dtype=F32, res=dhn2, name=f"b_up_v_{tag}")
    g["w_up"] = jnp.concatenate([_mm_tn(sv["hn2"], dug, name=f"g_up_g_{tag}"),
                                 _mm_tn(sv["hn2"], duv, name=f"g_up_v_{tag}")], axis=1)
    dh1, g["norm2"] = _rms_bwd(sv["h1"], p["norm2"], dhn2, dh2, name=f"b_rms2_{tag}")
    dmixed = _mm(dh1, p["w_o"].T, out_dtype=BF16, name=f"b_wo_{tag}")
    g["w_o"] = _mm_tn(sv["mixed"], dh1, name=f"g_wo_{tag}")
    dgates, dya, dyb, dyc, vec = _mix_bwd(sv["s"], sv["pooled"], sv["att"], sv["gates"], p["w_conv_out"], sv["wb"],
                                          p["w_attn_out"], p["b_conv_out"], p["pool_scale"], dmixed,
                                          name=f"b_mix_{tag}")
    g["b_conv_out"], g["pool_scale"] = vec[0], vec[1]
    ds = _mm(dya, p["w_conv_out"].T, out_dtype=F32, name=f"b_conv_out_{tag}")
    dpooled = _mm(dyb, sv["wb"].T, out_dtype=F32, name=f"b_pool_out_{tag}")
    datt = _mm(dyc, p["w_attn_out"].T, out_dtype=BF16, name=f"b_attn_out_{tag}")
    g["w_conv_out"] = _mm_tn(sv["s"], dya, name=f"g_conv_out_{tag}")
    g["w_pool_grp"] = _block_diag_grad(_mm_tn(sv["pooled"], dyb, name=f"g_pool_{tag}"), len(POOL_WINDOWS))
    g["w_attn_out"] = _mm_tn(sv["att"], dyc, name=f"g_attn_out_{tag}")
    dc, gp = _conv_bwd_ln(sv["pa"], ds, p["conv_dw_w"], p["conv_dw_b"], p["conv_ln_g"], p["conv_ln_b"],
                          name=f"b_conv_ln_{tag}")
    g["conv_dw_w"], g["conv_dw_b"], g["conv_ln_g"], g["conv_ln_b"] = gp[0:CONV_K], gp[32], gp[33], gp[34]
    dconv = _conv_bwd_in(sv["pa"], dc, p["conv_dw_w"], name=f"b_conv_in_{tag}")
    dp = _pool_bwd(dpooled, name=f"b_pool_{tag}")
    dq, dk, dv = _attn_bwd(sv["qkv"], sv["att32"], datt, name=f"b_attn_{tag}")
    dk = jnp.moveaxis(dk, 0, 1).reshape(dq.shape).astype(BF16)
    dv = jnp.moveaxis(dv, 0, 1).reshape(dq.shape).astype(BF16)
    w_in_t = p["w_in"].T
    cols = [(dconv, 0, C_CONV), (dp, C_CONV, C_POOL), (dq, C_POOL, C_POOL + C_ATT),
            (dk, C_POOL + C_ATT, C_POOL + 2 * C_ATT), (dv, C_POOL + 2 * C_ATT, C_QKV),
            (dgates, C_QKV, w_in_t.shape[0])]
    dhn, gw = None, []
    for n, (dcol, lo, hi) in enumerate(cols):
        dhn = _mm(dcol, w_in_t[lo:hi], out_dtype=F32, res=dhn, name=f"b_in{n}_{tag}")
        gw.append(_mm_tn(sv["hn"], dcol, name=f"g_in{n}_{tag}"))
    g["w_in"] = jnp.concatenate(gw, axis=1)
    dh, g["norm1"] = _rms_bwd(sv["h"], p["norm1"], dhn, dh1, name=f"b_rms1_{tag}")
    return dh, g


def kernel(x, meta, norm1, w_in, conv_dw_w, conv_dw_b, conv_ln_g, conv_ln_b, w_conv_out, b_conv_out, w_pool_grp, pool_scale, w_attn_out, w_o, norm2, w_up, ffn_dw_w, ffn_dw_b, w_down, final_norm, loss_target, m_meta, m_norm1, m_w_in, m_conv_dw_w, m_conv_dw_b, m_conv_ln_g, m_conv_ln_b, m_w_conv_out, m_b_conv_out, m_w_pool_grp, m_pool_scale, m_w_attn_out, m_w_o, m_norm2, m_w_up, m_ffn_dw_w, m_ffn_dw_b, m_w_down, m_final_norm, v_meta, v_norm1, v_w_in, v_conv_dw_w, v_conv_dw_b, v_conv_ln_g, v_conv_ln_b, v_w_conv_out, v_b_conv_out, v_w_pool_grp, v_pool_scale, v_w_attn_out, v_w_o, v_norm2, v_w_up, v_ffn_dw_w, v_ffn_dw_b, v_w_down, v_final_norm):
    given = dict(meta=meta, norm1=norm1, w_in=w_in, conv_dw_w=conv_dw_w, conv_dw_b=conv_dw_b, conv_ln_g=conv_ln_g, conv_ln_b=conv_ln_b, w_conv_out=w_conv_out, b_conv_out=b_conv_out, w_pool_grp=w_pool_grp, pool_scale=pool_scale, w_attn_out=w_attn_out, w_o=w_o, norm2=norm2, w_up=w_up, ffn_dw_w=ffn_dw_w, ffn_dw_b=ffn_dw_b, w_down=w_down, final_norm=final_norm)
    mom_m = dict(meta=m_meta, norm1=m_norm1, w_in=m_w_in, conv_dw_w=m_conv_dw_w, conv_dw_b=m_conv_dw_b, conv_ln_g=m_conv_ln_g, conv_ln_b=m_conv_ln_b, w_conv_out=m_w_conv_out, b_conv_out=m_b_conv_out, w_pool_grp=m_w_pool_grp, pool_scale=m_pool_scale, w_attn_out=m_w_attn_out, w_o=m_w_o, norm2=m_norm2, w_up=m_w_up, ffn_dw_w=m_ffn_dw_w, ffn_dw_b=m_ffn_dw_b, w_down=m_w_down, final_norm=m_final_norm)
    mom_v = dict(meta=v_meta, norm1=v_norm1, w_in=v_w_in, conv_dw_w=v_conv_dw_w, conv_dw_b=v_conv_dw_b, conv_ln_g=v_conv_ln_g, conv_ln_b=v_conv_ln_b, w_conv_out=v_w_conv_out, b_conv_out=v_b_conv_out, w_pool_grp=v_w_pool_grp, pool_scale=v_pool_scale, w_attn_out=v_w_attn_out, w_o=v_w_o, norm2=v_norm2, w_up=v_w_up, ffn_dw_w=v_ffn_dw_w, ffn_dw_b=v_ffn_dw_b, w_down=v_w_down, final_norm=v_final_norm)
    sharded_axis = dict(SHARDED)
    vectors = [n for n, _ in SHARDED if n not in MATRICES]
    depth = norm1.shape[0]

    got_mat = _all_gather(_pack([given[n] for n in MATRICES], BF16), name="gather_matrices")
    got_vec = _all_gather(_pack([given[n] for n in vectors], F32), name="gather_vectors")
    full = {n: given[n] for n in REPLICATED}
    for n, a in zip(MATRICES, _unpack(got_mat, [given[n].shape for n in MATRICES], (N_DEV,))):
        full[n] = _unshard(a, sharded_axis[n])
    for n, a in zip(vectors, _unpack(got_vec, [given[n].shape for n in vectors], (N_DEV,))):
        full[n] = _unshard(a, sharded_axis[n])

    xs = x[0]
    d = xs.shape[1]
    h = jnp.concatenate([jnp.zeros((PAD, d), F32), full["meta"], xs], axis=0)
    layers, saved = [], []
    for i in range(depth):
        p = {n: full[n][i] for n in full if n not in ("meta", "final_norm")}
        layers.append(p)
        h, sv = _layer_fwd(h, p, f"l{i}")
        saved.append(sv)
    loss_part, dh, g_final = _loss_head(h, full["final_norm"], loss_target[0], name="loss_head")

    grads = [None] * depth
    for i in reversed(range(depth)):
        dh, grads[i] = _layer_bwd(dh, layers[i], saved[i], f"l{i}")
    full_grad = {n: jnp.stack([grads[i][n] for i in range(depth)]) for n in grads[0]}
    full_grad["meta"] = dh[PAD:FRONT]
    full_grad["final_norm"] = g_final
    grad_x = dh[FRONT:][None]

    names = [n for n, _ in SHARDED]
    recv = _all_to_all(_pack_pieces([_pieces(full_grad[n], sharded_axis[n]) for n in names], F32), name="scatter_grads")
    rep_shapes = [given[n].shape for n in REPLICATED] + [(1,)]
    rep_parts = _all_gather(_pack([full_grad[n] for n in REPLICATED] + [loss_part.reshape(1)], F32),
                            name="gather_partials")

    out = {}
    shapes = [given[n].shape for n in names]
    res = _adamw(recv, _pack([given[n] for n in names], F32), _pack([mom_m[n] for n in names], F32),
                 _pack([mom_v[n] for n in names], F32), name="adamw_sharded")
    for kind, buf in zip(("grad", "delta", "new_m", "new_v"), res):
        for n, a in zip(names, _unpack(buf, shapes)):
            out[kind, n] = a
    rep_w = [given[n] for n in REPLICATED] + [jnp.zeros((1,), F32)]
    rep_m = [mom_m[n] for n in REPLICATED] + [jnp.zeros((1,), F32)]
    rep_v = [mom_v[n] for n in REPLICATED] + [jnp.ones((1,), F32)]
    res = _adamw(rep_parts, _pack(rep_w, F32), _pack(rep_m, F32), _pack(rep_v, F32), name="adamw_replicated")
    for kind, buf in zip(("grad", "delta", "new_m", "new_v"), res):
        for n, a in zip(list(REPLICATED) + ["loss"], _unpack(buf, rep_shapes)):
            out[kind, n] = a
    loss = out["grad", "loss"][0]
    return (loss, grad_x, *[out["grad", n] for n in WEIGHTS], *[out["delta", n] for n in WEIGHTS],
            *[out["new_m", n] for n in WEIGHTS], *[out["new_v", n] for n in WEIGHTS])
```

```python
import functools

import jax
import jax.numpy as jnp
from jax import lax
from jax.experimental import pallas as pl
from jax.experimental.pallas import tpu as pltpu

F32 = jnp.float32
BF16 = jnp.bfloat16
MESH = pl.DeviceIdType.MESH

N_DEV = 8
N_META = 16
BLOCK = 128
PAD = 240
FRONT = PAD + N_META
HEADS = 4
HEAD_DIM = 128
CONV_CH = 256
CONV_K = 31
POOL_CH = 256
POOL_WINDOWS = (2, 4, 8, 16)
FFN_K = 3
EPS = 1e-6
ADAM_LR, ADAM_B1, ADAM_B2, ADAM_EPS, ADAM_WD, ADAM_STEP = 0.001, 0.9, 0.999, 1e-08, 0.01, 10

VMEM_LIMIT = 56 * 1024 * 1024
CONV_HALO = 32
POOL_HALO = 16
FFN_HALO = 8
PACK_ALIGN = 16 * 128
PACK_ROWS = 512


def _tile(n, cap, unit):
    if n <= cap:
        return n
    best = None
    t = unit
    while t <= cap:
        if n % t == 0:
            best = t
        t += unit
    assert best is not None, (n, cap, unit)
    return best


def _params(sem):
    return pltpu.CompilerParams(dimension_semantics=sem, vmem_limit_bytes=VMEM_LIMIT)


def _sigmoid(x):
    return 1.0 / (1.0 + jnp.exp(-x))


MM_MAX_K = 3072


def _mm(a, b, *, out_dtype, name, res=None, col_scale=None, mask_rows=False, tn_cap=768):
    m, k = a.shape
    k2, n = b.shape
    assert k == k2 and k <= MM_MAX_K
    tm, tn = _tile(m, 1280 if k <= 1024 else 640, 128), _tile(n, tn_cap, 128)

    def body(*refs):
        refs = list(refs)
        a_ref, b_ref = refs[:2]
        o_ref = refs[-1]
        r_ref = refs[2] if res is not None else None
        c_ref = refs[-2] if col_scale is not None else None
        y = jnp.dot(a_ref[...].astype(BF16), b_ref[...].astype(BF16), preferred_element_type=F32)
        if col_scale is not None:
            y = y * c_ref[...]
        if res is not None:
            y = y + r_ref[...].astype(F32)
        if mask_rows:
            row = pl.program_id(0) * tm + lax.broadcasted_iota(jnp.int32, (tm, 1), 0)
            y = jnp.where(row >= PAD, y, 0.0)
        o_ref[...] = y.astype(out_dtype)

    in_specs = [pl.BlockSpec((tm, k), lambda i, j: (i, 0)), pl.BlockSpec((k, tn), lambda i, j: (0, j))]
    args = [a, b]
    if res is not None:
        in_specs.append(pl.BlockSpec((tm, tn), lambda i, j: (i, j)))
        args.append(res)
    if col_scale is not None:
        in_specs.append(pl.BlockSpec((1, tn), lambda i, j: (0, j)))
        args.append(col_scale.reshape(1, n))
    return pl.pallas_call(
        body, name=name, grid=(m // tm, n // tn),
        in_specs=in_specs, out_specs=pl.BlockSpec((tm, tn), lambda i, j: (i, j)),
        out_shape=jax.ShapeDtypeStruct((m, n), out_dtype),
        compiler_params=_params(("parallel", "parallel")),
    )(*args)


def _mm_tn(a, b, *, name, t1_cap=512, tn_cap=1024, tl_cap=1280):
    l, k1 = a.shape
    l2, n = b.shape
    assert l == l2
    t1, tn, tl = _tile(k1, t1_cap, 128), _tile(n, tn_cap, 128), _tile(l, tl_cap, 128)

    def body(a_ref, b_ref, o_ref):
        @pl.when(pl.program_id(2) == 0)
        def _():
            o_ref[...] = jnp.zeros_like(o_ref)

        o_ref[...] += lax.dot_general(a_ref[...].astype(BF16), b_ref[...].astype(BF16),
                                      (((0,), (0,)), ((), ())), preferred_element_type=F32)

    return pl.pallas_call(
        body, name=name, grid=(k1 // t1, n // tn, l // tl),
        in_specs=[pl.BlockSpec((tl, t1), lambda i, j, ll: (ll, i)), pl.BlockSpec((tl, tn), lambda i, j, ll: (ll, j))],
        out_specs=pl.BlockSpec((t1, tn), lambda i, j, ll: (i, j)),
        out_shape=jax.ShapeDtypeStruct((k1, n), F32),
        compiler_params=_params(("parallel", "parallel", "arbitrary")),
    )(a, b)


def _rms_fwd(x, g, *, name):
    l, d = x.shape
    tm = _tile(l, 640, 128)

    def body(x_ref, g_ref, o_ref):
        xv = x_ref[...]
        r = lax.rsqrt(jnp.mean(xv * xv, axis=-1, keepdims=True) + EPS)
        o_ref[...] = (xv * r * g_ref[...]).astype(BF16)

    return pl.pallas_call(
        body, name=name, grid=(l // tm,),
        in_specs=[pl.BlockSpec((tm, d), lambda i: (i, 0)), pl.BlockSpec((1, d), lambda i: (0, 0))],
        out_specs=pl.BlockSpec((tm, d), lambda i: (i, 0)),
        out_shape=jax.ShapeDtypeStruct((l, d), BF16),
        compiler_params=_params(("parallel",)),
    )(x, g.reshape(1, d))


def _rms_bwd(x, g, dy, dres, *, name):
    l, d = x.shape
    tm = _tile(l, 640, 128)

    def body(x_ref, g_ref, dy_ref, dr_ref, dx_ref, dg_ref):
        i = pl.program_id(0)

        @pl.when(i == 0)
        def _():
            dg_ref[...] = jnp.zeros_like(dg_ref)

        xv = x_ref[...]
        r = lax.rsqrt(jnp.mean(xv * xv, axis=-1, keepdims=True) + EPS)
        xh = xv * r
        dyv = dy_ref[...].astype(F32)
        dxh = dyv * g_ref[...]
        dx = r * (dxh - xh * jnp.mean(dxh * xh, axis=-1, keepdims=True)) + dr_ref[...]
        row = i * tm + lax.broadcasted_iota(jnp.int32, (tm, 1), 0)
        dx_ref[...] = jnp.where(row >= PAD, dx, 0.0)
        dg_ref[0:1, :] += jnp.sum(dyv * xh, axis=0, keepdims=True)

    dx, dg = pl.pallas_call(
        body, name=name, grid=(l // tm,),
        in_specs=[pl.BlockSpec((tm, d), lambda i: (i, 0)), pl.BlockSpec((1, d), lambda i: (0, 0)),
                  pl.BlockSpec((tm, d), lambda i: (i, 0)), pl.BlockSpec((tm, d), lambda i: (i, 0))],
        out_specs=[pl.BlockSpec((tm, d), lambda i: (i, 0)), pl.BlockSpec((8, d), lambda i: (0, 0))],
        out_shape=[jax.ShapeDtypeStruct((l, d), F32), jax.ShapeDtypeStruct((8, d), F32)],
        compiler_params=_params(("arbitrary",)),
    )(x, g.reshape(1, d), dy, dres)
    return dx, dg[0]


def _loss_head(h, g, target, *, name):
    l, d = h.shape
    tm = FRONT
    assert l % tm == 0 and target.shape[0] == l - tm

    def body(h_ref, g_ref, t_ref, dh_ref, loss_ref, dg_ref):
        i = pl.program_id(0)

        @pl.when(i == 0)
        def _():
            loss_ref[...] = jnp.zeros_like(loss_ref)
            dg_ref[...] = jnp.zeros_like(dg_ref)
            dh_ref[...] = jnp.zeros_like(dh_ref)

        @pl.when(i > 0)
        def _():
            xv = h_ref[...]
            r = lax.rsqrt(jnp.mean(xv * xv, axis=-1, keepdims=True) + EPS)
            xh = xv * r
            gv = g_ref[...]
            err = xh * gv - t_ref[...]
            loss_ref[...] += 0.5 * jnp.sum(jnp.mean(err * err, axis=-1, keepdims=True))
            dy = err * (1.0 / d)
            dxh = dy * gv
            dh_ref[...] = r * (dxh - xh * jnp.mean(dxh * xh, axis=-1, keepdims=True))
            dg_ref[0:1, :] += jnp.sum(dy * xh, axis=0, keepdims=True)

    dh, loss, dg = pl.pallas_call(
        body, name=name, grid=(l // tm,),
        in_specs=[pl.BlockSpec((tm, d), lambda i: (i, 0)), pl.BlockSpec((1, d), lambda i: (0, 0)),
                  pl.BlockSpec((tm, d), lambda i: (jnp.maximum(i - 1, 0), 0))],
        out_specs=[pl.BlockSpec((tm, d), lambda i: (i, 0)), pl.BlockSpec((8, 128), lambda i: (0, 0)),
                   pl.BlockSpec((8, d), lambda i: (0, 0))],
        out_shape=[jax.ShapeDtypeStruct((l, d), F32), jax.ShapeDtypeStruct((8, 128), F32),
                   jax.ShapeDtypeStruct((8, d), F32)],
        compiler_params=_params(("arbitrary",)),
    )(h, g.reshape(1, d), target)
    return loss[0, 0], dh, dg[0]


def _conv_tile(l):
    return _tile(l, 640, 128)


def _conv_core(a, gt, buf, dw_w, dw_b, first):
    tm = a.shape[0]

    @pl.when(first)
    def _():
        buf[0:CONV_HALO, :] = jnp.zeros((CONV_HALO, CONV_CH), F32)

    @pl.when(jnp.logical_not(first))
    def _():
        buf[0:CONV_HALO, :] = buf[tm:tm + CONV_HALO, :]

    sg = _sigmoid(gt)
    buf[CONV_HALO:CONV_HALO + tm, :] = a * sg
    c = jnp.zeros((tm, CONV_CH), F32) + dw_b
    for k in range(CONV_K):
        off = CONV_HALO - (CONV_K - 1) + k
        c = c + dw_w[k:k + 1, :] * buf[off:off + tm, :]
    return c, sg


def _layer_norm(c, ln_g, ln_b):
    mu = jnp.mean(c, axis=-1, keepdims=True)
    xc = c - mu
    r = lax.rsqrt(jnp.mean(xc * xc, axis=-1, keepdims=True) + EPS)
    xh = xc * r
    return xh, r, xh * ln_g + ln_b


def _conv_fwd(pa, dw_w, dw_b, ln_g, ln_b, *, name):
    l = pa.shape[0]
    tm = _conv_tile(l)

    def body(a_ref, gt_ref, w_ref, b_ref, g_ref, bb_ref, o_ref, buf):
        c, _ = _conv_core(a_ref[...], gt_ref[...], buf, w_ref[...], b_ref[...], pl.program_id(0) == 0)
        _, _, y = _layer_norm(c, g_ref[...], bb_ref[...])
        o_ref[...] = (y * _sigmoid(y)).astype(BF16)

    vec = pl.BlockSpec((1, CONV_CH), lambda i: (0, 0))
    return pl.pallas_call(
        body, name=name, grid=(l // tm,),
        in_specs=[pl.BlockSpec((tm, CONV_CH), lambda i: (i, 0)), pl.BlockSpec((tm, CONV_CH), lambda i: (i, 1)),
                  pl.BlockSpec((CONV_K, CONV_CH), lambda i: (0, 0)), vec, vec, vec],
        out_specs=pl.BlockSpec((tm, CONV_CH), lambda i: (i, 0)),
        out_shape=jax.ShapeDtypeStruct((l, CONV_CH), BF16),
        scratch_shapes=[pltpu.VMEM((CONV_HALO + tm, CONV_CH), F32)],
        compiler_params=_params(("arbitrary",)),
    )(pa, pa, dw_w, dw_b.reshape(1, -1), ln_g.reshape(1, -1), ln_b.reshape(1, -1))


def _conv_bwd_ln(pa, ds, dw_w, dw_b, ln_g, ln_b, *, name):
    l = pa.shape[0]
    tm = _conv_tile(l)

    def body(a_ref, gt_ref, ds_ref, w_ref, b_ref, g_ref, bb_ref, dc_ref, gp_ref, buf):
        i = pl.program_id(0)

        @pl.when(i == 0)
        def _():
            gp_ref[...] = jnp.zeros_like(gp_ref)

        c, _ = _conv_core(a_ref[...], gt_ref[...], buf, w_ref[...], b_ref[...], i == 0)
        xh, r, y = _layer_norm(c, g_ref[...], bb_ref[...])
        sy = _sigmoid(y)
        dy = ds_ref[...] * (sy * (1.0 + y * (1.0 - sy)))
        dxh = dy * g_ref[...]
        dc = r * (dxh - jnp.mean(dxh, axis=-1, keepdims=True) - xh * jnp.mean(dxh * xh, axis=-1, keepdims=True))
        dc_ref[...] = dc
        for k in range(CONV_K):
            off = CONV_HALO - (CONV_K - 1) + k
            gp_ref[k:k + 1, :] += jnp.sum(dc * buf[off:off + tm, :], axis=0, keepdims=True)
        gp_ref[32:33, :] += jnp.sum(dc, axis=0, keepdims=True)
        gp_ref[33:34, :] += jnp.sum(dy * xh, axis=0, keepdims=True)
        gp_ref[34:35, :] += jnp.sum(dy, axis=0, keepdims=True)

    vec = pl.BlockSpec((1, CONV_CH), lambda i: (0, 0))
    return pl.pallas_call(
        body, name=name, grid=(l // tm,),
        in_specs=[pl.BlockSpec((tm, CONV_CH), lambda i: (i, 0)), pl.BlockSpec((tm, CONV_CH), lambda i: (i, 1)),
                  pl.BlockSpec((tm, CONV_CH), lambda i: (i, 0)),
                  pl.BlockSpec((CONV_K, CONV_CH), lambda i: (0, 0)), vec, vec, vec],
        out_specs=[pl.BlockSpec((tm, CONV_CH), lambda i: (i, 0)), pl.BlockSpec((40, CONV_CH), lambda i: (0, 0))],
        out_shape=[jax.ShapeDtypeStruct((l, CONV_CH), F32), jax.ShapeDtypeStruct((40, CONV_CH), F32)],
        scratch_shapes=[pltpu.VMEM((CONV_HALO + tm, CONV_CH), F32)],
        compiler_params=_params(("arbitrary",)),
    )(pa, pa, ds, dw_w, dw_b.reshape(1, -1), ln_g.reshape(1, -1), ln_b.reshape(1, -1))


def _conv_bwd_in(pa, dc, dw_w, *, name):
    l = pa.shape[0]
    tm = _conv_tile(l)
    nt = l // tm

    def body(a_ref, gt_ref, dc_ref, w_ref, o_ref, buf):
        first = pl.program_id(0) == 0

        @pl.when(first)
        def _():
            buf[tm:tm + CONV_HALO, :] = jnp.zeros((CONV_HALO, CONV_CH), F32)

        @pl.when(jnp.logical_not(first))
        def _():
            buf[tm:tm + CONV_HALO, :] = buf[0:CONV_HALO, :]

        buf[0:tm, :] = dc_ref[...]
        w = w_ref[...]
        dhc = jnp.zeros((tm, CONV_CH), F32)
        for k in range(CONV_K):
            off = CONV_K - 1 - k
            dhc = dhc + w[k:k + 1, :] * buf[off:off + tm, :]
        a = a_ref[...]
        sg = _sigmoid(gt_ref[...])
        o_ref[:, 0:CONV_CH] = (dhc * sg).astype(BF16)
        o_ref[:, CONV_CH:2 * CONV_CH] = (dhc * a * sg * (1.0 - sg)).astype(BF16)

    return pl.pallas_call(
        body, name=name, grid=(nt,),
        in_specs=[pl.BlockSpec((tm, CONV_CH), lambda i: (nt - 1 - i, 0)),
                  pl.BlockSpec((tm, CONV_CH), lambda i: (nt - 1 - i, 1)),
                  pl.BlockSpec((tm, CONV_CH), lambda i: (nt - 1 - i, 0)),
                  pl.BlockSpec((CONV_K, CONV_CH), lambda i: (0, 0))],
        out_specs=pl.BlockSpec((tm, 2 * CONV_CH), lambda i: (nt - 1 - i, 0)),
        out_shape=jax.ShapeDtypeStruct((l, 2 * CONV_CH), BF16),
        scratch_shapes=[pltpu.VMEM((tm + CONV_HALO, CONV_CH), F32)],
        compiler_params=_params(("arbitrary",)),
    )(pa, pa, dc, dw_w)


def _pool_consts(tm, row0):
    lane = lax.broadcasted_iota(jnp.int32, (1, POOL_CH), 1)
    grp = lane // (POOL_CH // len(POOL_WINDOWS))
    win = jnp.where(grp == 0, 2.0, jnp.where(grp == 1, 4.0, jnp.where(grp == 2, 8.0, 16.0))).astype(F32)
    pos = (row0 + lax.broadcasted_iota(jnp.int32, (tm, 1), 0) - PAD).astype(F32)
    cnt = jnp.maximum(jnp.minimum(pos + 1.0, win), 1.0)
    return grp, cnt


def _pool_select(grp, s2, s4, s8, s16):
    return jnp.where(grp == 0, s2, jnp.where(grp == 1, s4, jnp.where(grp == 2, s8, s16)))


def _pool_fwd(pa, *, name):
    l = pa.shape[0]
    tm = _conv_tile(l)
    ext = POOL_HALO + tm

    def body(p_ref, o_ref, buf):
        i = pl.program_id(0)

        @pl.when(i == 0)
        def _():
            buf[0:POOL_HALO, :] = jnp.zeros((POOL_HALO, POOL_CH), F32)

        @pl.when(i > 0)
        def _():
            buf[0:POOL_HALO, :] = buf[tm:tm + POOL_HALO, :]

        p = p_ref[...]
        buf[POOL_HALO:ext, :] = p
        x = buf[...]
        s2 = x + pltpu.roll(x, 1, 0)
        s4 = s2 + pltpu.roll(s2, 2, 0)
        s8 = s4 + pltpu.roll(s4, 4, 0)
        s16 = s8 + pltpu.roll(s8, 8, 0)
        grp, cnt = _pool_consts(tm, i * tm)
        s = _pool_select(grp, s2, s4, s8, s16)[POOL_HALO:ext, :]
        o_ref[...] = (s / cnt - p).astype(BF16)

    return pl.pallas_call(
        body, name=name, grid=(l // tm,),
        in_specs=[pl.BlockSpec((tm, POOL_CH), lambda i: (i, 2))],
        out_specs=pl.BlockSpec((tm, POOL_CH), lambda i: (i, 0)),
        out_shape=jax.ShapeDtypeStruct((l, POOL_CH), BF16),
        scratch_shapes=[pltpu.VMEM((ext, POOL_CH), F32)],
        compiler_params=_params(("arbitrary",)),
    )(pa)


def _pool_bwd(dpooled, *, name):
    l = dpooled.shape[0]
    tm = _conv_tile(l)
    nt = l // tm
    ext = tm + POOL_HALO

    def body(d_ref, o_ref, buf):
        i = pl.program_id(0)

        @pl.when(i == 0)
        def _():
            buf[tm:ext, :] = jnp.zeros((POOL_HALO, POOL_CH), F32)

        @pl.when(i > 0)
        def _():
            buf[tm:ext, :] = buf[0:POOL_HALO, :]

        d = d_ref[...]
        grp, cnt = _pool_consts(tm, (nt - 1 - i) * tm)
        buf[0:tm, :] = d / cnt
        x = buf[...]
        s2 = x + pltpu.roll(x, ext - 1, 0)
        s4 = s2 + pltpu.roll(s2, ext - 2, 0)
        s8 = s4 + pltpu.roll(s4, ext - 4, 0)
        s16 = s8 + pltpu.roll(s8, ext - 8, 0)
        s = _pool_select(grp, s2, s4, s8, s16)[0:tm, :]
        o_ref[...] = (s - d).astype(BF16)

    return pl.pallas_call(
        body, name=name, grid=(nt,),
        in_specs=[pl.BlockSpec((tm, POOL_CH), lambda i: (nt - 1 - i, 0))],
        out_specs=pl.BlockSpec((tm, POOL_CH), lambda i: (nt - 1 - i, 0)),
        out_shape=jax.ShapeDtypeStruct((l, POOL_CH), BF16),
        scratch_shapes=[pltpu.VMEM((ext, POOL_CH), F32)],
        compiler_params=_params(("arbitrary",)),
    )(dpooled)


ATT_TQ = 256
ATT_TK = 5 * BLOCK
ATT_SUB = ATT_TK // BLOCK
LOG2E = 1.4426950408889634
LN2 = 0.6931471805599453
Q_SCALE = HEAD_DIM ** -0.5 * LOG2E
ATT_CUT = 160.0


def _tri_ones():
    r = lax.broadcasted_iota(jnp.int32, (2 * BLOCK, 2 * BLOCK), 0) % BLOCK
    c = lax.broadcasted_iota(jnp.int32, (2 * BLOCK, 2 * BLOCK), 1)
    return jnp.where((c >= BLOCK) | (r > c), 1.0, 0.0).astype(BF16)


def _split_dot(x, rhs):
    hi = x.astype(BF16)
    lo = (x - hi.astype(F32)).astype(BF16)
    return jnp.dot(jnp.concatenate([hi, lo], axis=1), rhs, preferred_element_type=F32)


def _scores(q, kt, qpos, base, masked):
    z = lax.dot_general(q, kt, (((1,), (1,)), ((), ())), preferred_element_type=F32)
    sp = jnp.log2(1.0 + jnp.exp2(-jnp.abs(z)))
    lb = jnp.minimum(z, 0.0) - sp
    lk = lb - z
    valid = None
    if masked:
        kpos = base + lax.broadcasted_iota(jnp.int32, (1, z.shape[1]), 1)
        valid = (kpos < qpos) & (kpos >= PAD)
        lk = jnp.where(valid, lk, 0.0)
    return lk, lb, valid


def _suffix(x, tri, carry):
    wts = [_split_dot(x[:, b * BLOCK:(b + 1) * BLOCK], tri) for b in range(ATT_SUB)]
    offs = [None] * ATT_SUB
    s = carry
    for b in reversed(range(ATT_SUB)):
        offs[b] = wts[b][:, :BLOCK] + s
        s = s + wts[b][:, BLOCK:]
    return jnp.concatenate(offs, axis=1), s


def _walk_tiles(i, tq, step):
    t_top = ((i + 1) * tq - 1) // ATT_TK
    t_diag = (i * tq) // ATT_TK
    n_plain = jnp.maximum(t_diag - 1, 0)

    def masked(jj, top):
        return step(t_top - jj, True)

    def live(carry):
        return (carry[0] < n_plain) & (carry[1] > -ATT_CUT)

    def plain(carry):
        return carry[0] + 1, step(t_diag - 1 - carry[0], False)

    top = lax.fori_loop(0, t_top - t_diag + 1, masked, jnp.float32(0.0))
    _, top = lax.while_loop(live, plain, (jnp.int32(0), top))

    @pl.when((t_diag > 0) & (top > -ATT_CUT))
    def _():
        step(0, True)


def _tile_base(t):
    base = t * ATT_TK
    return base if isinstance(base, int) else pl.multiple_of(base, BLOCK)


def _attn_fwd(qkv, *, name):
    l = qkv.shape[0]
    tq = ATT_TQ
    assert l % tq == 0 and l % ATT_TK == 0

    def body(q_ref, k_ref, v_ref, o_ref, o32_ref, acc_ref, r_ref):
        i = pl.program_id(1)
        acc_ref[...] = jnp.zeros_like(acc_ref)
        r_ref[...] = jnp.zeros_like(r_ref)
        q = q_ref[...]
        qpos = i * tq + lax.broadcasted_iota(jnp.int32, (tq, 1), 0)
        tri = _tri_ones()

        def step(t, masked):
            base = _tile_base(t)
            lk, lb, valid = _scores(q, k_ref[pl.ds(base, ATT_TK), :], qpos, base, masked)
            off, r_new = _suffix(lk, tri, r_ref[...])
            a = jnp.exp2(lb + off)
            if masked:
                a = jnp.where(valid, a, 0.0)
            acc_ref[...] += jnp.dot(a.astype(BF16), v_ref[pl.ds(base, ATT_TK), :], preferred_element_type=F32)
            r_ref[...] = r_new
            return jnp.max(r_new)

        _walk_tiles(i, tq, step)
        o_ref[...] = acc_ref[...].astype(BF16)
        o32_ref[...] = acc_ref[...]

    tile = pl.BlockSpec((tq, HEAD_DIM), lambda h, i: (i, h))
    return pl.pallas_call(
        body, name=name, grid=(HEADS, l // tq),
        in_specs=[tile,
                  pl.BlockSpec((l, HEAD_DIM), lambda h, i: (0, HEADS + h)),
                  pl.BlockSpec((l, HEAD_DIM), lambda h, i: (0, 2 * HEADS + h))],
        out_specs=[tile, tile],
        out_shape=[jax.ShapeDtypeStruct((l, HEADS * HEAD_DIM), BF16),
                   jax.ShapeDtypeStruct((l, HEADS * HEAD_DIM), F32)],
        scratch_shapes=[pltpu.VMEM((tq, HEAD_DIM), F32), pltpu.VMEM((tq, BLOCK), F32)],
        compiler_params=_params(("parallel", "arbitrary")),
    )(qkv, qkv, qkv)


def _attn_bwd(qkv, att, datt, *, name):
    l = qkv.shape[0]
    tq = ATT_TQ
    nq = l // tq
    assert l % tq == 0 and l % ATT_TK == 0

    def body(q_ref, k_ref, v_ref, o_ref, do_ref, dq_ref, dk_hbm, dv_hbm, dk_acc, dv_acc, dq_acc, r_ref, s_ref, sem):
        h = pl.program_id(0)
        i = pl.program_id(1)

        @pl.when(i == 0)
        def _():
            dk_acc[...] = jnp.zeros_like(dk_acc)
            dv_acc[...] = jnp.zeros_like(dv_acc)

        dq_acc[...] = jnp.zeros_like(dq_acc)
        r_ref[...] = jnp.zeros_like(r_ref)
        s_ref[...] = jnp.zeros_like(s_ref)
        q = q_ref[...]
        do = do_ref[...]
        ptot = jnp.sum(do.astype(F32) * o_ref[...], axis=-1, keepdims=True)
        qpos = i * tq + lax.broadcasted_iota(jnp.int32, (tq, 1), 0)
        tri = _tri_ones()

        def step(t, masked):
            base = _tile_base(t)
            kt = k_ref[pl.ds(base, ATT_TK), :]
            vt = v_ref[pl.ds(base, ATT_TK), :]
            lk, lb, valid = _scores(q, kt, qpos, base, masked)
            off, r_new = _suffix(lk, tri, r_ref[...])
            a = jnp.exp2(lb + off)
            if masked:
                a = jnp.where(valid, a, 0.0)
            ab = a.astype(BF16)
            da = lax.dot_general(do, vt, (((1,), (1,)), ((), ())), preferred_element_type=F32)
            p = ab.astype(F32) * da
            poff, s_new = _suffix(p, tri, s_ref[...])
            dz = (p - jnp.exp2(lb) * (ptot - poff)) * LN2
            if masked:
                dz = jnp.where(valid, dz, 0.0)
            dzb = dz.astype(BF16)
            dq_acc[...] += jnp.dot(dzb, kt, preferred_element_type=F32)
            dk_acc[pl.ds(base, ATT_TK), :] += lax.dot_general(dzb, q, (((0,), (0,)), ((), ())),
                                                              preferred_element_type=F32)
            dv_acc[pl.ds(base, ATT_TK), :] += lax.dot_general(ab, do, (((0,), (0,)), ((), ())),
                                                              preferred_element_type=F32)
            r_ref[...] = r_new
            s_ref[...] = s_new
            return jnp.max(r_new)

        _walk_tiles(i, tq, step)
        dq_ref[...] = (dq_acc[...] * Q_SCALE).astype(BF16)

        @pl.when(i == nq - 1)
        def _():
            ck = pltpu.make_async_copy(dk_acc, dk_hbm.at[h], sem.at[0])
            cv = pltpu.make_async_copy(dv_acc, dv_hbm.at[h], sem.at[1])
            ck.start()
            cv.start()
            ck.wait()
            cv.wait()

    tile = pl.BlockSpec((tq, HEAD_DIM), lambda h, i: (i, h))
    return pl.pallas_call(
        body, name=name, grid=(HEADS, nq),
        in_specs=[tile,
                  pl.BlockSpec((l, HEAD_DIM), lambda h, i: (0, HEADS + h)),
                  pl.BlockSpec((l, HEAD_DIM), lambda h, i: (0, 2 * HEADS + h)),
                  tile, tile],
        out_specs=[tile, pl.BlockSpec(memory_space=pl.ANY), pl.BlockSpec(memory_space=pl.ANY)],
        out_shape=[jax.ShapeDtypeStruct((l, HEADS * HEAD_DIM), BF16),
                   jax.ShapeDtypeStruct((HEADS, l, HEAD_DIM), F32), jax.ShapeDtypeStruct((HEADS, l, HEAD_DIM), F32)],
        scratch_shapes=[pltpu.VMEM((l, HEAD_DIM), F32), pltpu.VMEM((l, HEAD_DIM), F32),
                        pltpu.VMEM((tq, HEAD_DIM), F32), pltpu.VMEM((tq, BLOCK), F32), pltpu.VMEM((tq, BLOCK), F32),
                        pltpu.SemaphoreType.DMA((2,))],
        compiler_params=_params(("arbitrary", "arbitrary")),
    )(qkv, qkv, qkv, att, datt)


MIX_TM = 256


def _mix_branches(s_ref, p_ref, t_ref, g_ref, wa_ref, wb_ref, wc_ref, ba_ref, sc_ref, d):
    ya = jnp.dot(s_ref[...], wa_ref[...], preferred_element_type=F32) + ba_ref[...]
    yb0 = jnp.dot(p_ref[...], wb_ref[...], preferred_element_type=F32)
    yc = jnp.dot(t_ref[...], wc_ref[...], preferred_element_type=F32)
    g0 = _sigmoid(g_ref[:, 0:d].astype(F32))
    g1 = _sigmoid(g_ref[:, d:2 * d].astype(F32))
    g2 = _sigmoid(g_ref[:, 2 * d:3 * d].astype(F32))
    return ya, yb0, yc, g0, g1, g2


def _mix_specs(tm, d):
    row = lambda w: pl.BlockSpec((tm, w), lambda i: (i, 0))
    full = lambda r: pl.BlockSpec((r, d), lambda i: (0, 0))
    return [row(CONV_CH), row(POOL_CH), row(HEADS * HEAD_DIM), row(3 * d),
            full(CONV_CH), full(POOL_CH), full(HEADS * HEAD_DIM), full(1), full(1)]


def _mix_fwd(s, pooled, att, gates, wa, wb, wc, ba, scale, *, name):
    l, d = s.shape[0], wa.shape[1]
    tm = _tile(l, MIX_TM, 128)

    def body(s_ref, p_ref, t_ref, g_ref, wa_ref, wb_ref, wc_ref, ba_ref, sc_ref, o_ref):
        ya, yb0, yc, g0, g1, g2 = _mix_branches(s_ref, p_ref, t_ref, g_ref, wa_ref, wb_ref, wc_ref, ba_ref, sc_ref, d)
        o_ref[...] = (g0 * ya + g1 * (yb0 * sc_ref[...]) + g2 * yc).astype(BF16)

    return pl.pallas_call(
        body, name=name, grid=(l // tm,), in_specs=_mix_specs(tm, d),
        out_specs=pl.BlockSpec((tm, d), lambda i: (i, 0)),
        out_shape=jax.ShapeDtypeStruct((l, d), BF16),
        compiler_params=_params(("parallel",)),
    )(s, pooled, att, gates, wa, wb, wc, ba.reshape(1, d), scale.reshape(1, d))


def _mix_bwd(s, pooled, att, gates, wa, wb, wc, ba, scale, dmixed, *, name):
    l, d = s.shape[0], wa.shape[1]
    tm = _tile(l, MIX_TM, 128)

    def body(s_ref, p_ref, t_ref, g_ref, wa_ref, wb_ref, wc_ref, ba_ref, sc_ref, dm_ref,
             dg_ref, dya_ref, dyb_ref, dyc_ref, vec_ref):
        @pl.when(pl.program_id(0) == 0)
        def _():
            vec_ref[...] = jnp.zeros_like(vec_ref)

        ya, yb0, yc, g0, g1, g2 = _mix_branches(s_ref, p_ref, t_ref, g_ref, wa_ref, wb_ref, wc_ref, ba_ref, sc_ref, d)
        dm = dm_ref[...].astype(F32)
        sc = sc_ref[...]
        dg_ref[:, 0:d] = (dm * ya * g0 * (1.0 - g0)).astype(BF16)
        dg_ref[:, d:2 * d] = (dm * (yb0 * sc) * g1 * (1.0 - g1)).astype(BF16)
        dg_ref[:, 2 * d:3 * d] = (dm * yc * g2 * (1.0 - g2)).astype(BF16)
        dya = dm * g0
        dyb = dm * g1
        dya_ref[...] = dya.astype(BF16)
        dyb_ref[...] = (dyb * sc).astype(BF16)
        dyc_ref[...] = (dm * g2).astype(BF16)
        vec_ref[0:1, :] += jnp.sum(dya, axis=0, keepdims=True)
        vec_ref[1:2, :] += jnp.sum(dyb * yb0, axis=0, keepdims=True)

    row = lambda w: pl.BlockSpec((tm, w), lambda i: (i, 0))
    outs = pl.pallas_call(
        body, name=name, grid=(l // tm,), in_specs=_mix_specs(tm, d) + [row(d)],
        out_specs=[row(3 * d), row(d), row(d), row(d), pl.BlockSpec((8, d), lambda i: (0, 0))],
        out_shape=[jax.ShapeDtypeStruct((l, 3 * d), BF16), jax.ShapeDtypeStruct((l, d), BF16),
                   jax.ShapeDtypeStruct((l, d), BF16), jax.ShapeDtypeStruct((l, d), BF16),
                   jax.ShapeDtypeStruct((8, d), F32)],
        compiler_params=_params(("arbitrary",)),
    )(s, pooled, att, gates, wa, wb, wc, ba.reshape(1, d), scale.reshape(1, d), dmixed)
    return outs


FFN_TC = 512
_GELU_C = 0.7978845608028654
_GELU_A = 0.044715


def _gelu(x):
    th = jnp.tanh(_GELU_C * (x + _GELU_A * x * x * x))
    return 0.5 * x * (1.0 + th), th


def _gelu_grad(x, th):
    return 0.5 * (1.0 + th) + 0.5 * x * (1.0 - th * th) * _GELU_C * (1.0 + 3.0 * _GELU_A * x * x)


FFN_CH = 32


def _ffn_taps(win):
    return (pltpu.roll(win, 2, 0)[FFN_HALO:, :], pltpu.roll(win, 1, 0)[FFN_HALO:, :], win[FFN_HALO:, :])


def _ffn_conv(taps, w, b):
    return b + w[0:1, :] * taps[0] + w[1:2, :] * taps[1] + w[2:3, :] * taps[2]


def _fold8(x):
    acc = x[0:8, :]
    for r in range(8, x.shape[0], 8):
        acc = acc + x[r:r + 8, :]
    return acc


def _ffn_fwd(ug, uv, wg, wv, bg, bv, *, name):
    l, f = ug.shape
    tm = _conv_tile(l)
    tc = _tile(f, FFN_TC, 128)
    ext = FFN_HALO + tm

    def body(ug_ref, uv_ref, wg_ref, wv_ref, bg_ref, bv_ref, o_ref, bufg, bufv):
        i = pl.program_id(1)
        for buf, u_ref in ((bufg, ug_ref), (bufv, uv_ref)):
            @pl.when(i == 0)
            def _():
                buf[0:FFN_HALO, :] = jnp.zeros((FFN_HALO, tc), F32)

            @pl.when(i > 0)
            def _():
                buf[0:FFN_HALO, :] = buf[tm:ext, :]

            buf[FFN_HALO:ext, :] = u_ref[...].astype(F32)
        wg, wv, bg_, bv_ = wg_ref[...], wv_ref[...], bg_ref[...], bv_ref[...]

        def chunk(c, carry):
            r0 = pl.multiple_of(c * FFN_CH, FFN_CH)
            gc = _ffn_conv(_ffn_taps(bufg[pl.ds(r0, FFN_HALO + FFN_CH), :]), wg, bg_)
            vc = _ffn_conv(_ffn_taps(bufv[pl.ds(r0, FFN_HALO + FFN_CH), :]), wv, bv_)
            o_ref[pl.ds(r0, FFN_CH), :] = (_gelu(gc)[0] * vc).astype(BF16)
            return carry

        lax.fori_loop(0, tm // FFN_CH, chunk, 0)

    assert tm % FFN_CH == 0
    tile = pl.BlockSpec((tm, tc), lambda j, i: (i, j))
    wspec = pl.BlockSpec((FFN_K, tc), lambda j, i: (0, j))
    bspec = pl.BlockSpec((1, tc), lambda j, i: (0, j))
    return pl.pallas_call(
        body, name=name, grid=(f // tc, l // tm),
        in_specs=[tile, tile, wspec, wspec, bspec, bspec], out_specs=tile,
        out_shape=jax.ShapeDtypeStruct((l, f), BF16),
        scratch_shapes=[pltpu.VMEM((ext, tc), F32), pltpu.VMEM((ext, tc), F32)],
        compiler_params=_params(("parallel", "arbitrary")),
    )(ug, uv, wg, wv, bg.reshape(1, f), bv.reshape(1, f))


def _ffn_bwd(ug, uv, wg, wv, bg, bv, dact, *, name):
    l, f = ug.shape
    tm = _conv_tile(l)
    tc = _tile(f, FFN_TC, 128)
    nt = l // tm
    ext = FFN_HALO + tm
    hb = tm // (2 * FFN_HALO)

    def body(ug_ref, uv_ref, pg_ref, pv_ref, wg_ref, wv_ref, bg_ref, bv_ref, da_ref,
             dug_ref, duv_ref, gg_ref, gv_ref, bufg, bufv, dbufg, dbufv, gaccg, gaccv):
        i = pl.program_id(1)
        last = i == nt - 1

        @pl.when(i == 0)
        def _():
            gg_ref[...] = jnp.zeros_like(gg_ref)
            gv_ref[...] = jnp.zeros_like(gv_ref)

        for buf, u_ref, prev_ref in ((bufg, ug_ref, pg_ref), (bufv, uv_ref, pv_ref)):
            buf[0:FFN_HALO, :] = jnp.where(last, 0.0, prev_ref[FFN_HALO:2 * FFN_HALO, :].astype(F32))
            buf[FFN_HALO:ext, :] = u_ref[...].astype(F32)
        for dbuf, gacc in ((dbufg, gaccg), (dbufv, gaccv)):
            @pl.when(i == 0)
            def _():
                dbuf[tm:ext, :] = jnp.zeros((FFN_HALO, tc), F32)

            @pl.when(i > 0)
            def _():
                dbuf[tm:ext, :] = dbuf[0:FFN_HALO, :]

            gacc[...] = jnp.zeros_like(gacc)
        wg, wv, bg_, bv_ = wg_ref[...], wv_ref[...], bg_ref[...], bv_ref[...]
        win_rows = FFN_CH + FFN_HALO

        def chunk(cc, carry):
            r0 = pl.multiple_of((tm // FFN_CH - 1 - cc) * FFN_CH, FFN_CH)
            taps_g = _ffn_taps(bufg[pl.ds(r0, win_rows), :])
            taps_v = _ffn_taps(bufv[pl.ds(r0, win_rows), :])
            gc = _ffn_conv(taps_g, wg, bg_)
            vc = _ffn_conv(taps_v, wv, bv_)
            ge, th = _gelu(gc)
            da = da_ref[pl.ds(r0, FFN_CH), :].astype(F32)
            for dc, taps, w, dbuf, du_ref, gacc in ((da * vc * _gelu_grad(gc, th), taps_g, wg, dbufg, dug_ref, gaccg),
                                                    (da * ge, taps_v, wv, dbufv, duv_ref, gaccv)):
                dbuf[pl.ds(r0, FFN_CH), :] = dc
                dwin = dbuf[pl.ds(r0, win_rows), :]
                du = (w[2:3, :] * dc + w[1:2, :] * pltpu.roll(dwin, win_rows - 1, 0)[0:FFN_CH, :]
                      + w[0:1, :] * pltpu.roll(dwin, win_rows - 2, 0)[0:FFN_CH, :])
                du_ref[pl.ds(r0, FFN_CH), :] = du.astype(BF16)
                for k in range(FFN_K):
                    gacc[8 * k:8 * k + 8, :] += _fold8(dc * taps[k])
                gacc[24:32, :] += _fold8(dc)
            return carry

        lax.fori_loop(0, tm // FFN_CH, chunk, 0)
        for gacc, gp_ref in ((gaccg, gg_ref), (gaccv, gv_ref)):
            for k in range(FFN_K + 1):
                gp_ref[k:k + 1, :] += jnp.sum(gacc[8 * k:8 * k + 8, :], axis=0, keepdims=True)

    assert tm % FFN_CH == 0
    tile = pl.BlockSpec((tm, tc), lambda j, i: (nt - 1 - i, j))
    prev = pl.BlockSpec((2 * FFN_HALO, tc), lambda j, i: (jnp.maximum((nt - 1 - i) * hb - 1, 0), j))
    wspec = pl.BlockSpec((FFN_K, tc), lambda j, i: (0, j))
    bspec = pl.BlockSpec((1, tc), lambda j, i: (0, j))
    gspec = pl.BlockSpec((8, tc), lambda j, i: (0, j))
    return pl.pallas_call(
        body, name=name, grid=(f // tc, nt),
        in_specs=[tile, tile, prev, prev, wspec, wspec, bspec, bspec, tile],
        out_specs=[tile, tile, gspec, gspec],
        out_shape=[jax.ShapeDtypeStruct((l, f), BF16), jax.ShapeDtypeStruct((l, f), BF16),
                   jax.ShapeDtypeStruct((8, f), F32), jax.ShapeDtypeStruct((8, f), F32)],
        scratch_shapes=[pltpu.VMEM((ext, tc), F32), pltpu.VMEM((ext, tc), F32),
                        pltpu.VMEM((ext, tc), F32), pltpu.VMEM((ext, tc), F32),
                        pltpu.VMEM((32, tc), F32), pltpu.VMEM((32, tc), F32)],
        compiler_params=_params(("parallel", "arbitrary")),
    )(ug, uv, ug, uv, wg, wv, bg.reshape(1, f), bv.reshape(1, f), dact)


def _adamw(parts, w, m, v, *, name):
    r = w.shape[0]
    tr = _tile(r, PACK_ROWS, 16)
    c1 = 1.0 / (1.0 - ADAM_B1 ** ADAM_STEP)
    c2 = 1.0 / (1.0 - ADAM_B2 ** ADAM_STEP)

    def body(p_ref, w_ref, m_ref, v_ref, g_ref, d_ref, nm_ref, nv_ref):
        g = p_ref[0].astype(F32)
        for k in range(1, N_DEV):
            g = g + p_ref[k].astype(F32)
        nm = ADAM_B1 * m_ref[...] + (1.0 - ADAM_B1) * g
        nv = ADAM_B2 * v_ref[...] + (1.0 - ADAM_B2) * (g * g)
        g_ref[...] = g
        nm_ref[...] = nm
        nv_ref[...] = nv
        d_ref[...] = -ADAM_LR * ((nm * c1) / (jnp.sqrt(nv * c2) + ADAM_EPS) + ADAM_WD * w_ref[...])

    tile = pl.BlockSpec((tr, 128), lambda i: (i, 0))
    return pl.pallas_call(
        body, name=name, grid=(r // tr,),
        in_specs=[pl.BlockSpec((N_DEV, tr, 128), lambda i: (0, i, 0)), tile, tile, tile],
        out_specs=[tile, tile, tile, tile],
        out_shape=[jax.ShapeDtypeStruct((r, 128), F32)] * 4,
        compiler_params=_params(("parallel",)),
    )(parts, w, m, v)


def _place():
    return lax.axis_index("x"), lax.axis_index("y"), lax.axis_index("c")


def _all_gather(x, *, name):
    def body(x_ref, out_ref, send_sems, recv_sems, local_sem):
        xx, yy, cc = _place()
        me, sibling = (xx, yy, cc), (xx, yy, 1 - cc)
        chips = [(1 - xx, yy), (xx, 1 - yy), (1 - xx, 1 - yy)]

        def slot(px, py, pc):
            return out_ref.at[4 * px + 2 * py + pc]

        def copy(k, block, to, src=None):
            return pltpu.make_async_remote_copy(
                src_ref=slot(*block) if src is None else src, dst_ref=slot(*block),
                send_sem=send_sems.at[k], recv_sem=recv_sems.at[k], device_id=to, device_id_type=MESH)

        mine = pltpu.make_async_copy(x_ref, slot(*me), local_sem)
        mine.start()
        first = [copy(0, me, sibling, src=x_ref)]
        first += [copy(1 + j, me, (*chip, cc), src=x_ref) for j, chip in enumerate(chips)]
        for cp in first:
            cp.start()
        passed = [copy(4 + j, (*chip, cc), sibling) for j, chip in enumerate(chips)]
        for j, chip in enumerate(chips):
            copy(1 + j, (*chip, cc), me).wait_recv()
            passed[j].start()
        copy(0, sibling, me).wait_recv()
        for j, chip in enumerate(chips):
            copy(4 + j, (*chip, 1 - cc), me).wait_recv()
        for cp in first + passed:
            cp.wait_send()
        mine.wait()

    return pl.pallas_call(
        body, name=name,
        in_specs=[pl.BlockSpec(memory_space=pl.ANY)], out_specs=pl.BlockSpec(memory_space=pl.ANY),
        out_shape=jax.ShapeDtypeStruct((N_DEV,) + x.shape, x.dtype),
        scratch_shapes=[pltpu.SemaphoreType.DMA((7,)), pltpu.SemaphoreType.DMA((7,)), pltpu.SemaphoreType.DMA],
    )(x)


def _all_to_all(send, *, name):
    def body(s_ref, r_ref, send_sems, recv_sems, local_sem):
        xx, yy, cc = _place()
        me = 4 * xx + 2 * yy + cc
        local = pltpu.make_async_copy(s_ref.at[me], r_ref.at[me], local_sem)
        local.start()
        copies = []
        for m in range(1, N_DEV):
            px = 1 - xx if m & 4 else xx
            py = 1 - yy if m & 2 else yy
            pc = 1 - cc if m & 1 else cc
            copies.append(pltpu.make_async_remote_copy(
                src_ref=s_ref.at[4 * px + 2 * py + pc], dst_ref=r_ref.at[me],
                send_sem=send_sems.at[m - 1], recv_sem=recv_sems.at[m - 1],
                device_id=(px, py, pc), device_id_type=MESH))
        for cp in copies:
            cp.start()
        for cp in copies:
            cp.wait_recv()
        for cp in copies:
            cp.wait_send()
        local.wait()

    return pl.pallas_call(
        body, name=name,
        in_specs=[pl.BlockSpec(memory_space=pl.ANY)], out_specs=pl.BlockSpec(memory_space=pl.ANY),
        out_shape=jax.ShapeDtypeStruct(send.shape, send.dtype),
        scratch_shapes=[pltpu.SemaphoreType.DMA((7,)), pltpu.SemaphoreType.DMA((7,)), pltpu.SemaphoreType.DMA],
    )(send)


def _as_rows(a, lead, dtype):
    a = a.astype(dtype)
    size = 1
    for s in a.shape[len(lead):]:
        size *= s
    if size % PACK_ALIGN:
        a = jnp.pad(a.reshape(lead + (size,)), [(0, 0)] * len(lead) + [(0, (-size) % PACK_ALIGN)])
    return a.reshape(lead + (-1, 128))


def _pack(arrays, dtype):
    buf = jnp.concatenate([_as_rows(a, (), dtype) for a in arrays], axis=0)
    return jnp.pad(buf, ((0, (-buf.shape[0]) % PACK_ROWS), (0, 0)))


def _pack_pieces(arrays, dtype):
    buf = jnp.concatenate([_as_rows(a, (N_DEV,), dtype) for a in arrays], axis=1)
    return jnp.pad(buf, ((0, 0), (0, (-buf.shape[1]) % PACK_ROWS), (0, 0)))


def _unpack(buf, shapes, lead=()):
    out, row = [], 0
    for shp in shapes:
        size = 1
        for s in shp:
            size *= s
        rows = (size + (-size) % PACK_ALIGN) // 128
        part = buf[..., row:row + rows, :]
        if size % PACK_ALIGN:
            part = part.reshape(lead + (rows * 128,))[..., :size]
        out.append(part.reshape(lead + tuple(shp)))
        row += rows
    return out


def _unshard(g, axis):
    g = jnp.moveaxis(g, 0, axis)
    shp = list(g.shape)
    return g.reshape(shp[:axis] + [shp[axis] * shp[axis + 1]] + shp[axis + 2:])


def _pieces(full, axis):
    shp = list(full.shape)
    g = full.reshape(shp[:axis] + [N_DEV, shp[axis] // N_DEV] + shp[axis + 1:])
    return jnp.moveaxis(g, axis, 0)


SHARDED = (("meta", 1), ("w_in", 2), ("conv_dw_w", 2), ("w_conv_out", 2), ("w_pool_grp", 3), ("w_attn_out", 2),
           ("w_o", 1), ("w_up", 2), ("ffn_dw_w", 2), ("w_down", 1))
MATRICES = ("w_in", "w_conv_out", "w_pool_grp", "w_attn_out", "w_o", "w_up", "w_down")
REPLICATED = ("norm1", "conv_dw_b", "conv_ln_g", "conv_ln_b", "b_conv_out", "pool_scale", "norm2", "ffn_dw_b",
              "final_norm")
WEIGHTS = ("meta", "norm1", "w_in", "conv_dw_w", "conv_dw_b", "conv_ln_g", "conv_ln_b", "w_conv_out", "b_conv_out",
           "w_pool_grp", "pool_scale", "w_attn_out", "w_o", "norm2", "w_up", "ffn_dw_w", "ffn_dw_b", "w_down",
           "final_norm")


def _block_diag(w_grp):
    g, gc, od = w_grp.shape
    out = jnp.zeros((g * gc, g * od), w_grp.dtype)
    for i in range(g):
        out = out.at[i * gc:(i + 1) * gc, i * od:(i + 1) * od].set(w_grp[i])
    return out


def _block_diag_grad(gw, g):
    gc, od = gw.shape[0] // g, gw.shape[1] // g
    return jnp.stack([gw[i * gc:(i + 1) * gc, i * od:(i + 1) * od] for i in range(g)])


C_CONV = 2 * CONV_CH
C_POOL = C_CONV + POOL_CH
C_ATT = HEADS * HEAD_DIM
C_QKV = C_POOL + 3 * C_ATT


def _layer_fwd(h, p, tag):
    d = h.shape[1]
    w_in = p["w_in"]
    hn = _rms_fwd(h, p["norm1"], name=f"rms1_{tag}")
    pa = _mm(hn, w_in[:, :C_POOL], out_dtype=F32, name=f"proj_a_{tag}")
    q_scale = jnp.concatenate([jnp.full((C_ATT,), Q_SCALE, F32), jnp.ones((2 * C_ATT,), F32)])
    qkv = _mm(hn, w_in[:, C_POOL:C_QKV], out_dtype=BF16, col_scale=q_scale, name=f"proj_qkv_{tag}")
    gates = _mm(hn, w_in[:, C_QKV:], out_dtype=BF16, name=f"proj_g_{tag}")
    s = _conv_fwd(pa, p["conv_dw_w"], p["conv_dw_b"], p["conv_ln_g"], p["conv_ln_b"], name=f"conv_{tag}")
    pooled = _pool_fwd(pa, name=f"pool_{tag}")
    att, att32 = _attn_fwd(qkv, name=f"attn_{tag}")
    wb = _block_diag(p["w_pool_grp"])
    mixed = _mix_fwd(s, pooled, att, gates, p["w_conv_out"], wb, p["w_attn_out"], p["b_conv_out"], p["pool_scale"],
                     name=f"mix_{tag}")
    h1 = _mm(mixed, p["w_o"], out_dtype=F32, res=h, mask_rows=True, name=f"wo_{tag}")
    hn2 = _rms_fwd(h1, p["norm2"], name=f"rms2_{tag}")
    f = p["w_up"].shape[1] // 2
    ug = _mm(hn2, p["w_up"][:, :f], out_dtype=BF16, name=f"up_g_{tag}")
    uv = _mm(hn2, p["w_up"][:, f:], out_dtype=BF16, name=f"up_v_{tag}")
    act = _ffn_fwd(ug, uv, p["ffn_dw_w"][:, :f], p["ffn_dw_w"][:, f:], p["ffn_dw_b"][:f], p["ffn_dw_b"][f:],
                   name=f"ffn_{tag}")
    h2 = _mm(act, p["w_down"], out_dtype=F32, res=h1, mask_rows=True, name=f"down_{tag}")
    saved = dict(h=h, hn=hn, pa=pa, qkv=qkv, gates=gates, s=s, pooled=pooled, att=att, att32=att32, wb=wb, mixed=mixed,
                 h1=h1,
                 hn2=hn2, ug=ug, uv=uv, act=act)
    return h2, saved


def _layer_bwd(dh2, p, sv, tag):
    g = {}
    f = p["w_up"].shape[1] // 2
    dact = _mm(dh2, p["w_down"].T, out_dtype=BF16, name=f"b_down_{tag}")
    g["w_down"] = _mm_tn(sv["act"], dh2, name=f"g_down_{tag}")
    dug, duv, gpg, gpv = _ffn_bwd(sv["ug"], sv["uv"], p["ffn_dw_w"][:, :f], p["ffn_dw_w"][:, f:], p["ffn_dw_b"][:f],
                                  p["ffn_dw_b"][f:], dact, name=f"b_ffn_{tag}")
    g["ffn_dw_w"] = jnp.concatenate([gpg[0:FFN_K], gpv[0:FFN_K]], axis=1)
    g["ffn_dw_b"] = jnp.concatenate([gpg[FFN_K], gpv[FFN_K]])
    w_up_t = p["w_up"].T
    dhn2 = _mm(dug, w_up_t[:f], out_dtype=F32, name=f"b_up_g_{tag}")
    dhn2 = _mm(duv, w_up_t[f:], out_dtype=F32, res=dhn2, name=f"b_up_v_{tag}")
    g["w_up"] = jnp.concatenate([_mm_tn(sv["hn2"], dug, name=f"g_up_g_{tag}"),
                                 _mm_tn(sv["hn2"], duv, name=f"g_up_v_{tag}")], axis=1)
    dh1, g["norm2"] = _rms_bwd(sv["h1"], p["norm2"], dhn2, dh2, name=f"b_rms2_{tag}")
    dmixed = _mm(dh1, p["w_o"].T, out_dtype=BF16, name=f"b_wo_{tag}")
    g["w_o"] = _mm_tn(sv["mixed"], dh1, name=f"g_wo_{tag}")
    dgates, dya, dyb, dyc, vec = _mix_bwd(sv["s"], sv["pooled"], sv["att"], sv["gates"], p["w_conv_out"], sv["wb"],
                                          p["w_attn_out"], p["b_conv_out"], p["pool_scale"], dmixed,
                                          name=f"b_mix_{tag}")
    g["b_conv_out"], g["pool_scale"] = vec[0], vec[1]
    ds = _mm(dya, p["w_conv_out"].T, out_dtype=F32, name=f"b_conv_out_{tag}")
    dpooled = _mm(dyb, sv["wb"].T, out_dtype=F32, name=f"b_pool_out_{tag}")
    datt = _mm(dyc, p["w_attn_out"].T, out_dtype=BF16, name=f"b_attn_out_{tag}")
    g["w_conv_out"] = _mm_tn(sv["s"], dya, name=f"g_conv_out_{tag}")
    g["w_pool_grp"] = _block_diag_grad(_mm_tn(sv["pooled"], dyb, name=f"g_pool_{tag}"), len(POOL_WINDOWS))
    g["w_attn_out"] = _mm_tn(sv["att"], dyc, name=f"g_attn_out_{tag}")
    dc, gp = _conv_bwd_ln(sv["pa"], ds, p["conv_dw_w"], p["conv_dw_b"], p["conv_ln_g"], p["conv_ln_b"],
                          name=f"b_conv_ln_{tag}")
    g["conv_dw_w"], g["conv_dw_b"], g["conv_ln_g"], g["conv_ln_b"] = gp[0:CONV_K], gp[32], gp[33], gp[34]
    dconv = _conv_bwd_in(sv["pa"], dc, p["conv_dw_w"], name=f"b_conv_in_{tag}")
    dp = _pool_bwd(dpooled, name=f"b_pool_{tag}")
    dq, dk, dv = _attn_bwd(sv["qkv"], sv["att32"], datt, name=f"b_attn_{tag}")
    dk = jnp.moveaxis(dk, 0, 1).reshape(dq.shape).astype(BF16)
    dv = jnp.moveaxis(dv, 0, 1).reshape(dq.shape).astype(BF16)
    w_in_t = p["w_in"].T
    cols = [(jnp.concatenate([dconv, dp, dq, dk, dv], axis=1), 0, C_QKV), (dgates, C_QKV, w_in_t.shape[0])]
    dhn, gw = None, []
    for n, (dcol, lo, hi) in enumerate(cols):
        dhn = _mm(dcol, w_in_t[lo:hi], out_dtype=F32, res=dhn, name=f"b_in{n}_{tag}")
        gw.append(_mm_tn(sv["hn"], dcol, name=f"g_in{n}_{tag}"))
    g["w_in"] = jnp.concatenate(gw, axis=1)
    dh, g["norm1"] = _rms_bwd(sv["h"], p["norm1"], dhn, dh1, name=f"b_rms1_{tag}")
    return dh, g


def kernel(x, meta, norm1, w_in, conv_dw_w, conv_dw_b, conv_ln_g, conv_ln_b, w_conv_out, b_conv_out, w_pool_grp, pool_scale, w_attn_out, w_o, norm2, w_up, ffn_dw_w, ffn_dw_b, w_down, final_norm, loss_target, m_meta, m_norm1, m_w_in, m_conv_dw_w, m_conv_dw_b, m_conv_ln_g, m_conv_ln_b, m_w_conv_out, m_b_conv_out, m_w_pool_grp, m_pool_scale, m_w_attn_out, m_w_o, m_norm2, m_w_up, m_ffn_dw_w, m_ffn_dw_b, m_w_down, m_final_norm, v_meta, v_norm1, v_w_in, v_conv_dw_w, v_conv_dw_b, v_conv_ln_g, v_conv_ln_b, v_w_conv_out, v_b_conv_out, v_w_pool_grp, v_pool_scale, v_w_attn_out, v_w_o, v_norm2, v_w_up, v_ffn_dw_w, v_ffn_dw_b, v_w_down, v_final_norm):
    given = dict(meta=meta, norm1=norm1, w_in=w_in, conv_dw_w=conv_dw_w, conv_dw_b=conv_dw_b, conv_ln_g=conv_ln_g, conv_ln_b=conv_ln_b, w_conv_out=w_conv_out, b_conv_out=b_conv_out, w_pool_grp=w_pool_grp, pool_scale=pool_scale, w_attn_out=w_attn_out, w_o=w_o, norm2=norm2, w_up=w_up, ffn_dw_w=ffn_dw_w, ffn_dw_b=ffn_dw_b, w_down=w_down, final_norm=final_norm)
    mom_m = dict(meta=m_meta, norm1=m_norm1, w_in=m_w_in, conv_dw_w=m_conv_dw_w, conv_dw_b=m_conv_dw_b, conv_ln_g=m_conv_ln_g, conv_ln_b=m_conv_ln_b, w_conv_out=m_w_conv_out, b_conv_out=m_b_conv_out, w_pool_grp=m_w_pool_grp, pool_scale=m_pool_scale, w_attn_out=m_w_attn_out, w_o=m_w_o, norm2=m_norm2, w_up=m_w_up, ffn_dw_w=m_ffn_dw_w, ffn_dw_b=m_ffn_dw_b, w_down=m_w_down, final_norm=m_final_norm)
    mom_v = dict(meta=v_meta, norm1=v_norm1, w_in=v_w_in, conv_dw_w=v_conv_dw_w, conv_dw_b=v_conv_dw_b, conv_ln_g=v_conv_ln_g, conv_ln_b=v_conv_ln_b, w_conv_out=v_w_conv_out, b_conv_out=v_b_conv_out, w_pool_grp=v_w_pool_grp, pool_scale=v_pool_scale, w_attn_out=v_w_attn_out, w_o=v_w_o, norm2=v_norm2, w_up=v_w_up, ffn_dw_w=v_ffn_dw_w, ffn_dw_b=v_ffn_dw_b, w_down=v_w_down, final_norm=v_final_norm)
    sharded_axis = dict(SHARDED)
    vectors = [n for n, _ in SHARDED if n not in MATRICES]
    depth = norm1.shape[0]

    got_mat = _all_gather(_pack([given[n] for n in MATRICES], BF16), name="gather_matrices")
    got_vec = _all_gather(_pack([given[n] for n in vectors], F32), name="gather_vectors")
    full = {n: given[n] for n in REPLICATED}
    for n, a in zip(MATRICES, _unpack(got_mat, [given[n].shape for n in MATRICES], (N_DEV,))):
        full[n] = _unshard(a, sharded_axis[n])
    for n, a in zip(vectors, _unpack(got_vec, [given[n].shape for n in vectors], (N_DEV,))):
        full[n] = _unshard(a, sharded_axis[n])

    xs = x[0]
    d = xs.shape[1]
    h = jnp.concatenate([jnp.zeros((PAD, d), F32), full["meta"], xs], axis=0)
    layers, saved = [], []
    for i in range(depth):
        p = {n: full[n][i] for n in full if n not in ("meta", "final_norm")}
        layers.append(p)
        h, sv = _layer_fwd(h, p, f"l{i}")
        saved.append(sv)
    loss_part, dh, g_final = _loss_head(h, full["final_norm"], loss_target[0], name="loss_head")

    grads = [None] * depth
    for i in reversed(range(depth)):
        dh, grads[i] = _layer_bwd(dh, layers[i], saved[i], f"l{i}")
    full_grad = {n: jnp.stack([grads[i][n] for i in range(depth)]) for n in grads[0]}
    full_grad["meta"] = dh[PAD:FRONT]
    full_grad["final_norm"] = g_final
    grad_x = dh[FRONT:][None]

    names = [n for n, _ in SHARDED]
    recv = _all_to_all(_pack_pieces([_pieces(full_grad[n], sharded_axis[n]) for n in names], BF16),
                       name="scatter_grads")
    rep_shapes = [given[n].shape for n in REPLICATED] + [(1,)]
    rep_parts = _all_gather(_pack([full_grad[n] for n in REPLICATED] + [loss_part.reshape(1)], F32),
                            name="gather_partials")

    out = {}
    shapes = [given[n].shape for n in names]
    res = _adamw(recv, _pack([given[n] for n in names], F32), _pack([mom_m[n] for n in names], F32),
                 _pack([mom_v[n] for n in names], F32), name="adamw_sharded")
    for kind, buf in zip(("grad", "delta", "new_m", "new_v"), res):
        for n, a in zip(names, _unpack(buf, shapes)):
            out[kind, n] = a
    rep_w = [given[n] for n in REPLICATED] + [jnp.zeros((1,), F32)]
    rep_m = [mom_m[n] for n in REPLICATED] + [jnp.zeros((1,), F32)]
    rep_v = [mom_v[n] for n in REPLICATED] + [jnp.ones((1,), F32)]
    res = _adamw(rep_parts, _pack(rep_w, F32), _pack(rep_m, F32), _pack(rep_v, F32), name="adamw_replicated")
    for kind, buf in zip(("grad", "delta", "new_m", "new_v"), res):
        for n, a in zip(list(REPLICATED) + ["loss"], _unpack(buf, rep_shapes)):
            out[kind, n] = a
    loss = out["grad", "loss"][0]
    return (loss, grad_x, *[out["grad", n] for n in WEIGHTS], *[out["delta", n] for n in WEIGHTS],
            *[out["new_m", n] for n in WEIGHTS], *[out["new_v", n] for n in WEIGHTS])
```

```python
import functools

import jax
import jax.numpy as jnp
from jax import lax
from jax.experimental import pallas as pl
from jax.experimental.pallas import tpu as pltpu

F32 = jnp.float32
BF16 = jnp.bfloat16
MESH = pl.DeviceIdType.MESH

N_DEV = 8
N_META = 16
BLOCK = 128
PAD = 240
FRONT = PAD + N_META
HEADS = 4
HEAD_DIM = 128
CONV_CH = 256
CONV_K = 31
POOL_CH = 256
POOL_WINDOWS = (2, 4, 8, 16)
FFN_K = 3
EPS = 1e-6
ADAM_LR, ADAM_B1, ADAM_B2, ADAM_EPS, ADAM_WD, ADAM_STEP = 0.001, 0.9, 0.999, 1e-08, 0.01, 10

VMEM_LIMIT = 56 * 1024 * 1024
CONV_HALO = 32
POOL_HALO = 16
FFN_HALO = 8
PACK_ALIGN = 16 * 128
PACK_ROWS = 512


def _tile(n, cap, unit):
    if n <= cap:
        return n
    best = None
    t = unit
    while t <= cap:
        if n % t == 0:
            best = t
        t += unit
    assert best is not None, (n, cap, unit)
    return best


def _params(sem):
    return pltpu.CompilerParams(dimension_semantics=sem, vmem_limit_bytes=VMEM_LIMIT)


def _sigmoid(x):
    return 1.0 / (1.0 + jnp.exp(-x))


MM_MAX_K = 3072
MM_RESIDENT_B = 3072 * 1024 * 2


def _mm(a, b, *, out_dtype, name, res=None, col_scale=None, mask_rows=False, tn_cap=768):
    m, k = a.shape
    k2, n = b.shape
    assert k == k2 and k <= MM_MAX_K
    tn = n if k * n * 2 <= MM_RESIDENT_B else _tile(n, tn_cap, 128)
    tm = _tile(m, 1280 if (k <= 1024 and tn <= 1024) else 640, 128)

    def body(*refs):
        refs = list(refs)
        a_ref, b_ref = refs[:2]
        o_ref = refs[-1]
        r_ref = refs[2] if res is not None else None
        c_ref = refs[-2] if col_scale is not None else None
        y = jnp.dot(a_ref[...].astype(BF16), b_ref[...].astype(BF16), preferred_element_type=F32)
        if col_scale is not None:
            y = y * c_ref[...]
        if res is not None:
            y = y + r_ref[...].astype(F32)
        if mask_rows:
            row = pl.program_id(0) * tm + lax.broadcasted_iota(jnp.int32, (tm, 1), 0)
            y = jnp.where(row >= PAD, y, 0.0)
        o_ref[...] = y.astype(out_dtype)

    in_specs = [pl.BlockSpec((tm, k), lambda i, j: (i, 0)), pl.BlockSpec((k, tn), lambda i, j: (0, j))]
    args = [a, b]
    if res is not None:
        in_specs.append(pl.BlockSpec((tm, tn), lambda i, j: (i, j)))
        args.append(res)
    if col_scale is not None:
        in_specs.append(pl.BlockSpec((1, tn), lambda i, j: (0, j)))
        args.append(col_scale.reshape(1, n))
    return pl.pallas_call(
        body, name=name, grid=(m // tm, n // tn),
        in_specs=in_specs, out_specs=pl.BlockSpec((tm, tn), lambda i, j: (i, j)),
        out_shape=jax.ShapeDtypeStruct((m, n), out_dtype),
        compiler_params=_params(("parallel", "parallel")),
    )(*args)


def _mm_tn(a, b, *, name, t1_cap=512, tn_cap=1024, tl_cap=3328):
    l, k1 = a.shape
    l2, n = b.shape
    assert l == l2
    t1, tn, tl = _tile(k1, t1_cap, 128), _tile(n, tn_cap, 128), _tile(l, tl_cap, 128)

    def body(a_ref, b_ref, o_ref):
        @pl.when(pl.program_id(2) == 0)
        def _():
            o_ref[...] = jnp.zeros_like(o_ref)

        o_ref[...] += lax.dot_general(a_ref[...].astype(BF16), b_ref[...].astype(BF16),
                                      (((0,), (0,)), ((), ())), preferred_element_type=F32)

    return pl.pallas_call(
        body, name=name, grid=(k1 // t1, n // tn, l // tl),
        in_specs=[pl.BlockSpec((tl, t1), lambda i, j, ll: (ll, i)), pl.BlockSpec((tl, tn), lambda i, j, ll: (ll, j))],
        out_specs=pl.BlockSpec((t1, tn), lambda i, j, ll: (i, j)),
        out_shape=jax.ShapeDtypeStruct((k1, n), F32),
        compiler_params=_params(("parallel", "parallel", "arbitrary")),
    )(a, b)


def _rms_fwd(x, g, *, name):
    l, d = x.shape
    tm = _tile(l, 640, 128)

    def body(x_ref, g_ref, o_ref):
        xv = x_ref[...]
        r = lax.rsqrt(jnp.mean(xv * xv, axis=-1, keepdims=True) + EPS)
        o_ref[...] = (xv * r * g_ref[...]).astype(BF16)

    return pl.pallas_call(
        body, name=name, grid=(l // tm,),
        in_specs=[pl.BlockSpec((tm, d), lambda i: (i, 0)), pl.BlockSpec((1, d), lambda i: (0, 0))],
        out_specs=pl.BlockSpec((tm, d), lambda i: (i, 0)),
        out_shape=jax.ShapeDtypeStruct((l, d), BF16),
        compiler_params=_params(("parallel",)),
    )(x, g.reshape(1, d))


def _rms_bwd(x, g, dy, dres, *, name):
    l, d = x.shape
    tm = _tile(l, 640, 128)

    def body(x_ref, g_ref, dy_ref, dr_ref, dx_ref, dxb_ref, dg_ref):
        i = pl.program_id(0)

        @pl.when(i == 0)
        def _():
            dg_ref[...] = jnp.zeros_like(dg_ref)

        xv = x_ref[...]
        r = lax.rsqrt(jnp.mean(xv * xv, axis=-1, keepdims=True) + EPS)
        xh = xv * r
        dyv = dy_ref[...].astype(F32)
        dxh = dyv * g_ref[...]
        dx = r * (dxh - xh * jnp.mean(dxh * xh, axis=-1, keepdims=True)) + dr_ref[...]
        row = i * tm + lax.broadcasted_iota(jnp.int32, (tm, 1), 0)
        dx = jnp.where(row >= PAD, dx, 0.0)
        dx_ref[...] = dx
        dxb_ref[...] = dx.astype(BF16)
        dg_ref[0:1, :] += jnp.sum(dyv * xh, axis=0, keepdims=True)

    tile = pl.BlockSpec((tm, d), lambda i: (i, 0))
    dx, dxb, dg = pl.pallas_call(
        body, name=name, grid=(l // tm,),
        in_specs=[tile, pl.BlockSpec((1, d), lambda i: (0, 0)), tile, tile],
        out_specs=[tile, tile, pl.BlockSpec((8, d), lambda i: (0, 0))],
        out_shape=[jax.ShapeDtypeStruct((l, d), F32), jax.ShapeDtypeStruct((l, d), BF16),
                   jax.ShapeDtypeStruct((8, d), F32)],
        compiler_params=_params(("arbitrary",)),
    )(x, g.reshape(1, d), dy, dres)
    return dx, dxb, dg[0]


def _loss_head(h, g, target, *, name):
    l, d = h.shape
    tm = FRONT
    assert l % tm == 0 and target.shape[0] == l - tm

    def body(h_ref, g_ref, t_ref, dh_ref, dhb_ref, loss_ref, dg_ref):
        i = pl.program_id(0)

        @pl.when(i == 0)
        def _():
            loss_ref[...] = jnp.zeros_like(loss_ref)
            dg_ref[...] = jnp.zeros_like(dg_ref)
            dh_ref[...] = jnp.zeros_like(dh_ref)
            dhb_ref[...] = jnp.zeros_like(dhb_ref)

        @pl.when(i > 0)
        def _():
            xv = h_ref[...]
            r = lax.rsqrt(jnp.mean(xv * xv, axis=-1, keepdims=True) + EPS)
            xh = xv * r
            gv = g_ref[...]
            err = xh * gv - t_ref[...]
            loss_ref[...] += 0.5 * jnp.sum(jnp.mean(err * err, axis=-1, keepdims=True))
            dy = err * (1.0 / d)
            dxh = dy * gv
            dh = r * (dxh - xh * jnp.mean(dxh * xh, axis=-1, keepdims=True))
            dh_ref[...] = dh
            dhb_ref[...] = dh.astype(BF16)
            dg_ref[0:1, :] += jnp.sum(dy * xh, axis=0, keepdims=True)

    tile = pl.BlockSpec((tm, d), lambda i: (i, 0))
    dh, dhb, loss, dg = pl.pallas_call(
        body, name=name, grid=(l // tm,),
        in_specs=[tile, pl.BlockSpec((1, d), lambda i: (0, 0)),
                  pl.BlockSpec((tm, d), lambda i: (jnp.maximum(i - 1, 0), 0))],
        out_specs=[tile, tile, pl.BlockSpec((8, 128), lambda i: (0, 0)), pl.BlockSpec((8, d), lambda i: (0, 0))],
        out_shape=[jax.ShapeDtypeStruct((l, d), F32), jax.ShapeDtypeStruct((l, d), BF16),
                   jax.ShapeDtypeStruct((8, 128), F32), jax.ShapeDtypeStruct((8, d), F32)],
        compiler_params=_params(("arbitrary",)),
    )(h, g.reshape(1, d), target)
    return loss[0, 0], dh, dhb, dg[0]


def _conv_tile(l):
    return _tile(l, 640, 128)


def _conv_core(a, gt, buf, dw_w, dw_b, first):
    tm = a.shape[0]

    @pl.when(first)
    def _():
        buf[0:CONV_HALO, :] = jnp.zeros((CONV_HALO, CONV_CH), F32)

    @pl.when(jnp.logical_not(first))
    def _():
        buf[0:CONV_HALO, :] = buf[tm:tm + CONV_HALO, :]

    sg = _sigmoid(gt)
    buf[CONV_HALO:CONV_HALO + tm, :] = a * sg
    c = jnp.zeros((tm, CONV_CH), F32) + dw_b
    for k in range(CONV_K):
        off = CONV_HALO - (CONV_K - 1) + k
        c = c + dw_w[k:k + 1, :] * buf[off:off + tm, :]
    return c, sg


def _layer_norm(c, ln_g, ln_b):
    mu = jnp.mean(c, axis=-1, keepdims=True)
    xc = c - mu
    r = lax.rsqrt(jnp.mean(xc * xc, axis=-1, keepdims=True) + EPS)
    xh = xc * r
    return xh, r, xh * ln_g + ln_b


def _conv_fwd(pa, dw_w, dw_b, ln_g, ln_b, *, name):
    l = pa.shape[0]
    tm = _conv_tile(l)

    def body(a_ref, gt_ref, w_ref, b_ref, g_ref, bb_ref, o_ref, buf):
        c, _ = _conv_core(a_ref[...], gt_ref[...], buf, w_ref[...], b_ref[...], pl.program_id(0) == 0)
        _, _, y = _layer_norm(c, g_ref[...], bb_ref[...])
        o_ref[...] = (y * _sigmoid(y)).astype(BF16)

    vec = pl.BlockSpec((1, CONV_CH), lambda i: (0, 0))
    return pl.pallas_call(
        body, name=name, grid=(l // tm,),
        in_specs=[pl.BlockSpec((tm, CONV_CH), lambda i: (i, 0)), pl.BlockSpec((tm, CONV_CH), lambda i: (i, 1)),
                  pl.BlockSpec((CONV_K, CONV_CH), lambda i: (0, 0)), vec, vec, vec],
        out_specs=pl.BlockSpec((tm, CONV_CH), lambda i: (i, 0)),
        out_shape=jax.ShapeDtypeStruct((l, CONV_CH), BF16),
        scratch_shapes=[pltpu.VMEM((CONV_HALO + tm, CONV_CH), F32)],
        compiler_params=_params(("arbitrary",)),
    )(pa, pa, dw_w, dw_b.reshape(1, -1), ln_g.reshape(1, -1), ln_b.reshape(1, -1))


def _conv_bwd_ln(pa, ds, dw_w, dw_b, ln_g, ln_b, *, name):
    l = pa.shape[0]
    tm = _conv_tile(l)

    def body(a_ref, gt_ref, ds_ref, w_ref, b_ref, g_ref, bb_ref, dc_ref, gp_ref, buf):
        i = pl.program_id(0)

        @pl.when(i == 0)
        def _():
            gp_ref[...] = jnp.zeros_like(gp_ref)

        c, _ = _conv_core(a_ref[...], gt_ref[...], buf, w_ref[...], b_ref[...], i == 0)
        xh, r, y = _layer_norm(c, g_ref[...], bb_ref[...])
        sy = _sigmoid(y)
        dy = ds_ref[...] * (sy * (1.0 + y * (1.0 - sy)))
        dxh = dy * g_ref[...]
        dc = r * (dxh - jnp.mean(dxh, axis=-1, keepdims=True) - xh * jnp.mean(dxh * xh, axis=-1, keepdims=True))
        dc_ref[...] = dc
        for k in range(CONV_K):
            off = CONV_HALO - (CONV_K - 1) + k
            gp_ref[k:k + 1, :] += jnp.sum(dc * buf[off:off + tm, :], axis=0, keepdims=True)
        gp_ref[32:33, :] += jnp.sum(dc, axis=0, keepdims=True)
        gp_ref[33:34, :] += jnp.sum(dy * xh, axis=0, keepdims=True)
        gp_ref[34:35, :] += jnp.sum(dy, axis=0, keepdims=True)

    vec = pl.BlockSpec((1, CONV_CH), lambda i: (0, 0))
    return pl.pallas_call(
        body, name=name, grid=(l // tm,),
        in_specs=[pl.BlockSpec((tm, CONV_CH), lambda i: (i, 0)), pl.BlockSpec((tm, CONV_CH), lambda i: (i, 1)),
                  pl.BlockSpec((tm, CONV_CH), lambda i: (i, 0)),
                  pl.BlockSpec((CONV_K, CONV_CH), lambda i: (0, 0)), vec, vec, vec],
        out_specs=[pl.BlockSpec((tm, CONV_CH), lambda i: (i, 0)), pl.BlockSpec((40, CONV_CH), lambda i: (0, 0))],
        out_shape=[jax.ShapeDtypeStruct((l, CONV_CH), F32), jax.ShapeDtypeStruct((40, CONV_CH), F32)],
        scratch_shapes=[pltpu.VMEM((CONV_HALO + tm, CONV_CH), F32)],
        compiler_params=_params(("arbitrary",)),
    )(pa, pa, ds, dw_w, dw_b.reshape(1, -1), ln_g.reshape(1, -1), ln_b.reshape(1, -1))


def _conv_bwd_in(pa, dc, dw_w, *, name):
    l = pa.shape[0]
    tm = _conv_tile(l)
    nt = l // tm

    def body(a_ref, gt_ref, dc_ref, w_ref, o_ref, buf):
        first = pl.program_id(0) == 0

        @pl.when(first)
        def _():
            buf[tm:tm + CONV_HALO, :] = jnp.zeros((CONV_HALO, CONV_CH), F32)

        @pl.when(jnp.logical_not(first))
        def _():
            buf[tm:tm + CONV_HALO, :] = buf[0:CONV_HALO, :]

        buf[0:tm, :] = dc_ref[...]
        w = w_ref[...]
        dhc = jnp.zeros((tm, CONV_CH), F32)
        for k in range(CONV_K):
            off = CONV_K - 1 - k
            dhc = dhc + w[k:k + 1, :] * buf[off:off + tm, :]
        a = a_ref[...]
        sg = _sigmoid(gt_ref[...])
        o_ref[:, 0:CONV_CH] = (dhc * sg).astype(BF16)
        o_ref[:, CONV_CH:2 * CONV_CH] = (dhc * a * sg * (1.0 - sg)).astype(BF16)

    return pl.pallas_call(
        body, name=name, grid=(nt,),
        in_specs=[pl.BlockSpec((tm, CONV_CH), lambda i: (nt - 1 - i, 0)),
                  pl.BlockSpec((tm, CONV_CH), lambda i: (nt - 1 - i, 1)),
                  pl.BlockSpec((tm, CONV_CH), lambda i: (nt - 1 - i, 0)),
                  pl.BlockSpec((CONV_K, CONV_CH), lambda i: (0, 0))],
        out_specs=pl.BlockSpec((tm, 2 * CONV_CH), lambda i: (nt - 1 - i, 0)),
        out_shape=jax.ShapeDtypeStruct((l, 2 * CONV_CH), BF16),
        scratch_shapes=[pltpu.VMEM((tm + CONV_HALO, CONV_CH), F32)],
        compiler_params=_params(("arbitrary",)),
    )(pa, pa, dc, dw_w)


def _pool_consts(tm, row0):
    lane = lax.broadcasted_iota(jnp.int32, (1, POOL_CH), 1)
    grp = lane // (POOL_CH // len(POOL_WINDOWS))
    win = jnp.where(grp == 0, 2.0, jnp.where(grp == 1, 4.0, jnp.where(grp == 2, 8.0, 16.0))).astype(F32)
    pos = (row0 + lax.broadcasted_iota(jnp.int32, (tm, 1), 0) - PAD).astype(F32)
    cnt = jnp.maximum(jnp.minimum(pos + 1.0, win), 1.0)
    return grp, cnt


def _pool_select(grp, s2, s4, s8, s16):
    return jnp.where(grp == 0, s2, jnp.where(grp == 1, s4, jnp.where(grp == 2, s8, s16)))


def _pool_fwd(pa, *, name):
    l = pa.shape[0]
    tm = _conv_tile(l)
    ext = POOL_HALO + tm

    def body(p_ref, o_ref, buf):
        i = pl.program_id(0)

        @pl.when(i == 0)
        def _():
            buf[0:POOL_HALO, :] = jnp.zeros((POOL_HALO, POOL_CH), F32)

        @pl.when(i > 0)
        def _():
            buf[0:POOL_HALO, :] = buf[tm:tm + POOL_HALO, :]

        p = p_ref[...]
        buf[POOL_HALO:ext, :] = p
        x = buf[...]
        s2 = x + pltpu.roll(x, 1, 0)
        s4 = s2 + pltpu.roll(s2, 2, 0)
        s8 = s4 + pltpu.roll(s4, 4, 0)
        s16 = s8 + pltpu.roll(s8, 8, 0)
        grp, cnt = _pool_consts(tm, i * tm)
        s = _pool_select(grp, s2, s4, s8, s16)[POOL_HALO:ext, :]
        o_ref[...] = (s / cnt - p).astype(BF16)

    return pl.pallas_call(
        body, name=name, grid=(l // tm,),
        in_specs=[pl.BlockSpec((tm, POOL_CH), lambda i: (i, 2))],
        out_specs=pl.BlockSpec((tm, POOL_CH), lambda i: (i, 0)),
        out_shape=jax.ShapeDtypeStruct((l, POOL_CH), BF16),
        scratch_shapes=[pltpu.VMEM((ext, POOL_CH), F32)],
        compiler_params=_params(("arbitrary",)),
    )(pa)


def _pool_bwd(dpooled, *, name):
    l = dpooled.shape[0]
    tm = _conv_tile(l)
    nt = l // tm
    ext = tm + POOL_HALO

    def body(d_ref, o_ref, buf):
        i = pl.program_id(0)

        @pl.when(i == 0)
        def _():
            buf[tm:ext, :] = jnp.zeros((POOL_HALO, POOL_CH), F32)

        @pl.when(i > 0)
        def _():
            buf[tm:ext, :] = buf[0:POOL_HALO, :]

        d = d_ref[...]
        grp, cnt = _pool_consts(tm, (nt - 1 - i) * tm)
        buf[0:tm, :] = d / cnt
        x = buf[...]
        s2 = x + pltpu.roll(x, ext - 1, 0)
        s4 = s2 + pltpu.roll(s2, ext - 2, 0)
        s8 = s4 + pltpu.roll(s4, ext - 4, 0)
        s16 = s8 + pltpu.roll(s8, ext - 8, 0)
        s = _pool_select(grp, s2, s4, s8, s16)[0:tm, :]
        o_ref[...] = (s - d).astype(BF16)

    return pl.pallas_call(
        body, name=name, grid=(nt,),
        in_specs=[pl.BlockSpec((tm, POOL_CH), lambda i: (nt - 1 - i, 0))],
        out_specs=pl.BlockSpec((tm, POOL_CH), lambda i: (nt - 1 - i, 0)),
        out_shape=jax.ShapeDtypeStruct((l, POOL_CH), BF16),
        scratch_shapes=[pltpu.VMEM((ext, POOL_CH), F32)],
        compiler_params=_params(("arbitrary",)),
    )(dpooled)


ATT_TQ = 256
ATT_TK = 5 * BLOCK
ATT_SUB = ATT_TK // BLOCK
LOG2E = 1.4426950408889634
LN2 = 0.6931471805599453
Q_SCALE = HEAD_DIM ** -0.5 * LOG2E
ATT_CUT = 160.0


def _tri_ones():
    r = lax.broadcasted_iota(jnp.int32, (2 * BLOCK, 2 * BLOCK), 0) % BLOCK
    c = lax.broadcasted_iota(jnp.int32, (2 * BLOCK, 2 * BLOCK), 1)
    return jnp.where((c >= BLOCK) | (r > c), 1.0, 0.0).astype(BF16)


def _split_dot(x, rhs):
    hi = x.astype(BF16)
    lo = (x - hi.astype(F32)).astype(BF16)
    return jnp.dot(jnp.concatenate([hi, lo], axis=1), rhs, preferred_element_type=F32)


def _scores(q, kt, qpos, base, masked):
    z = lax.dot_general(q, kt, (((1,), (1,)), ((), ())), preferred_element_type=F32)
    sp = jnp.log2(1.0 + jnp.exp2(-jnp.abs(z)))
    lb = jnp.minimum(z, 0.0) - sp
    lk = lb - z
    valid = None
    if masked:
        kpos = base + lax.broadcasted_iota(jnp.int32, (1, z.shape[1]), 1)
        valid = (kpos < qpos) & (kpos >= PAD)
        lk = jnp.where(valid, lk, 0.0)
    return lk, lb, valid


def _suffix(x, tri, carry):
    wts = [_split_dot(x[:, b * BLOCK:(b + 1) * BLOCK], tri) for b in range(ATT_SUB)]
    offs = [None] * ATT_SUB
    s = carry
    for b in reversed(range(ATT_SUB)):
        offs[b] = wts[b][:, :BLOCK] + s
        s = s + wts[b][:, BLOCK:]
    return jnp.concatenate(offs, axis=1), s


def _walk_tiles(i, tq, step):
    t_top = ((i + 1) * tq - 1) // ATT_TK
    t_diag = (i * tq) // ATT_TK
    n_plain = jnp.maximum(t_diag - 1, 0)

    def masked(jj, top):
        return step(t_top - jj, True)

    def live(carry):
        return (carry[0] < n_plain) & (carry[1] > -ATT_CUT)

    def plain(carry):
        return carry[0] + 1, step(t_diag - 1 - carry[0], False)

    top = lax.fori_loop(0, t_top - t_diag + 1, masked, jnp.float32(0.0))
    _, top = lax.while_loop(live, plain, (jnp.int32(0), top))

    @pl.when((t_diag > 0) & (top > -ATT_CUT))
    def _():
        step(0, True)


def _tile_base(t):
    base = t * ATT_TK
    return base if isinstance(base, int) else pl.multiple_of(base, BLOCK)


def _attn_fwd(qkv, *, name):
    l = qkv.shape[0]
    tq = ATT_TQ
    assert l % tq == 0 and l % ATT_TK == 0

    def body(q_ref, k_ref, v_ref, o_ref, o32_ref, acc_ref, r_ref):
        i = pl.program_id(1)
        acc_ref[...] = jnp.zeros_like(acc_ref)
        r_ref[...] = jnp.zeros_like(r_ref)
        q = q_ref[...]
        qpos = i * tq + lax.broadcasted_iota(jnp.int32, (tq, 1), 0)
        tri = _tri_ones()

        def step(t, masked):
            base = _tile_base(t)
            lk, lb, valid = _scores(q, k_ref[pl.ds(base, ATT_TK), :], qpos, base, masked)
            off, r_new = _suffix(lk, tri, r_ref[...])
            a = jnp.exp2(lb + off)
            if masked:
                a = jnp.where(valid, a, 0.0)
            acc_ref[...] += jnp.dot(a.astype(BF16), v_ref[pl.ds(base, ATT_TK), :], preferred_element_type=F32)
            r_ref[...] = r_new
            return jnp.max(r_new)

        _walk_tiles(i, tq, step)
        o_ref[...] = acc_ref[...].astype(BF16)
        o32_ref[...] = acc_ref[...]

    tile = pl.BlockSpec((tq, HEAD_DIM), lambda h, i: (i, h))
    return pl.pallas_call(
        body, name=name, grid=(HEADS, l // tq),
        in_specs=[tile,
                  pl.BlockSpec((l, HEAD_DIM), lambda h, i: (0, HEADS + h)),
                  pl.BlockSpec((l, HEAD_DIM), lambda h, i: (0, 2 * HEADS + h))],
        out_specs=[tile, tile],
        out_shape=[jax.ShapeDtypeStruct((l, HEADS * HEAD_DIM), BF16),
                   jax.ShapeDtypeStruct((l, HEADS * HEAD_DIM), F32)],
        scratch_shapes=[pltpu.VMEM((tq, HEAD_DIM), F32), pltpu.VMEM((tq, BLOCK), F32)],
        compiler_params=_params(("parallel", "arbitrary")),
    )(qkv, qkv, qkv)


def _attn_bwd(qkv, att, datt, *, name):
    l = qkv.shape[0]
    tq = ATT_TQ
    nq = l // tq
    assert l % tq == 0 and l % ATT_TK == 0

    def body(q_ref, k_ref, v_ref, o_ref, do_ref, dq_ref, dk_hbm, dv_hbm, dk_acc, dv_acc, dq_acc, r_ref, s_ref, sem):
        h = pl.program_id(0)
        i = pl.program_id(1)

        @pl.when(i == 0)
        def _():
            dk_acc[...] = jnp.zeros_like(dk_acc)
            dv_acc[...] = jnp.zeros_like(dv_acc)

        dq_acc[...] = jnp.zeros_like(dq_acc)
        r_ref[...] = jnp.zeros_like(r_ref)
        s_ref[...] = jnp.zeros_like(s_ref)
        q = q_ref[...]
        do = do_ref[...]
        ptot = jnp.sum(do.astype(F32) * o_ref[...], axis=-1, keepdims=True)
        qpos = i * tq + lax.broadcasted_iota(jnp.int32, (tq, 1), 0)
        tri = _tri_ones()

        def step(t, masked):
            base = _tile_base(t)
            kt = k_ref[pl.ds(base, ATT_TK), :]
            vt = v_ref[pl.ds(base, ATT_TK), :]
            lk, lb, valid = _scores(q, kt, qpos, base, masked)
            off, r_new = _suffix(lk, tri, r_ref[...])
            a = jnp.exp2(lb + off)
            if masked:
                a = jnp.where(valid, a, 0.0)
            ab = a.astype(BF16)
            da = lax.dot_general(do, vt, (((1,), (1,)), ((), ())), preferred_element_type=F32)
            p = ab.astype(F32) * da
            poff, s_new = _suffix(p, tri, s_ref[...])
            dz = (p - jnp.exp2(lb) * (ptot - poff)) * LN2
            if masked:
                dz = jnp.where(valid, dz, 0.0)
            dzb = dz.astype(BF16)
            dq_acc[...] += jnp.dot(dzb, kt, preferred_element_type=F32)
            dk_acc[pl.ds(base, ATT_TK), :] += lax.dot_general(dzb, q, (((0,), (0,)), ((), ())),
                                                              preferred_element_type=F32)
            dv_acc[pl.ds(base, ATT_TK), :] += lax.dot_general(ab, do, (((0,), (0,)), ((), ())),
                                                              preferred_element_type=F32)
            r_ref[...] = r_new
            s_ref[...] = s_new
            return jnp.max(r_new)

        _walk_tiles(i, tq, step)
        dq_ref[...] = (dq_acc[...] * Q_SCALE).astype(BF16)

        @pl.when(i == nq - 1)
        def _():
            ck = pltpu.make_async_copy(dk_acc, dk_hbm.at[h], sem.at[0])
            cv = pltpu.make_async_copy(dv_acc, dv_hbm.at[h], sem.at[1])
            ck.start()
            cv.start()
            ck.wait()
            cv.wait()

    tile = pl.BlockSpec((tq, HEAD_DIM), lambda h, i: (i, h))
    return pl.pallas_call(
        body, name=name, grid=(HEADS, nq),
        in_specs=[tile,
                  pl.BlockSpec((l, HEAD_DIM), lambda h, i: (0, HEADS + h)),
                  pl.BlockSpec((l, HEAD_DIM), lambda h, i: (0, 2 * HEADS + h)),
                  tile, tile],
        out_specs=[tile, pl.BlockSpec(memory_space=pl.ANY), pl.BlockSpec(memory_space=pl.ANY)],
        out_shape=[jax.ShapeDtypeStruct((l, HEADS * HEAD_DIM), BF16),
                   jax.ShapeDtypeStruct((HEADS, l, HEAD_DIM), F32), jax.ShapeDtypeStruct((HEADS, l, HEAD_DIM), F32)],
        scratch_shapes=[pltpu.VMEM((l, HEAD_DIM), F32), pltpu.VMEM((l, HEAD_DIM), F32),
                        pltpu.VMEM((tq, HEAD_DIM), F32), pltpu.VMEM((tq, BLOCK), F32), pltpu.VMEM((tq, BLOCK), F32),
                        pltpu.SemaphoreType.DMA((2,))],
        compiler_params=_params(("arbitrary", "arbitrary")),
    )(qkv, qkv, qkv, att, datt)


MIX_TM = 256


def _mix_branches(s_ref, p_ref, t_ref, g_ref, wa_ref, wb_ref, wc_ref, ba_ref, sc_ref, d):
    ya = jnp.dot(s_ref[...], wa_ref[...], preferred_element_type=F32) + ba_ref[...]
    yb0 = jnp.dot(p_ref[...], wb_ref[...], preferred_element_type=F32)
    yc = jnp.dot(t_ref[...], wc_ref[...], preferred_element_type=F32)
    g0 = _sigmoid(g_ref[:, 0:d].astype(F32))
    g1 = _sigmoid(g_ref[:, d:2 * d].astype(F32))
    g2 = _sigmoid(g_ref[:, 2 * d:3 * d].astype(F32))
    return ya, yb0, yc, g0, g1, g2


def _mix_specs(tm, d):
    row = lambda w: pl.BlockSpec((tm, w), lambda i: (i, 0))
    full = lambda r: pl.BlockSpec((r, d), lambda i: (0, 0))
    return [row(CONV_CH), row(POOL_CH), row(HEADS * HEAD_DIM), row(3 * d),
            full(CONV_CH), full(POOL_CH), full(HEADS * HEAD_DIM), full(1), full(1)]


def _mix_fwd(s, pooled, att, gates, wa, wb, wc, ba, scale, *, name):
    l, d = s.shape[0], wa.shape[1]
    tm = _tile(l, MIX_TM, 128)

    def body(s_ref, p_ref, t_ref, g_ref, wa_ref, wb_ref, wc_ref, ba_ref, sc_ref, o_ref):
        ya, yb0, yc, g0, g1, g2 = _mix_branches(s_ref, p_ref, t_ref, g_ref, wa_ref, wb_ref, wc_ref, ba_ref, sc_ref, d)
        o_ref[...] = (g0 * ya + g1 * (yb0 * sc_ref[...]) + g2 * yc).astype(BF16)

    return pl.pallas_call(
        body, name=name, grid=(l // tm,), in_specs=_mix_specs(tm, d),
        out_specs=pl.BlockSpec((tm, d), lambda i: (i, 0)),
        out_shape=jax.ShapeDtypeStruct((l, d), BF16),
        compiler_params=_params(("parallel",)),
    )(s, pooled, att, gates, wa, wb, wc, ba.reshape(1, d), scale.reshape(1, d))


def _mix_bwd(s, pooled, att, gates, wa, wb, wc, ba, scale, dmixed, *, name):
    l, d = s.shape[0], wa.shape[1]
    tm = _tile(l, MIX_TM, 128)

    def body(s_ref, p_ref, t_ref, g_ref, wa_ref, wb_ref, wc_ref, ba_ref, sc_ref, dm_ref,
             dg_ref, dya_ref, dyb_ref, dyc_ref, vec_ref):
        @pl.when(pl.program_id(0) == 0)
        def _():
            vec_ref[...] = jnp.zeros_like(vec_ref)

        ya, yb0, yc, g0, g1, g2 = _mix_branches(s_ref, p_ref, t_ref, g_ref, wa_ref, wb_ref, wc_ref, ba_ref, sc_ref, d)
        dm = dm_ref[...].astype(F32)
        sc = sc_ref[...]
        dg_ref[:, 0:d] = (dm * ya * g0 * (1.0 - g0)).astype(BF16)
        dg_ref[:, d:2 * d] = (dm * (yb0 * sc) * g1 * (1.0 - g1)).astype(BF16)
        dg_ref[:, 2 * d:3 * d] = (dm * yc * g2 * (1.0 - g2)).astype(BF16)
        dya = dm * g0
        dyb = dm * g1
        dya_ref[...] = dya.astype(BF16)
        dyb_ref[...] = (dyb * sc).astype(BF16)
        dyc_ref[...] = (dm * g2).astype(BF16)
        vec_ref[0:1, :] += jnp.sum(dya, axis=0, keepdims=True)
        vec_ref[1:2, :] += jnp.sum(dyb * yb0, axis=0, keepdims=True)

    row = lambda w: pl.BlockSpec((tm, w), lambda i: (i, 0))
    outs = pl.pallas_call(
        body, name=name, grid=(l // tm,), in_specs=_mix_specs(tm, d) + [row(d)],
        out_specs=[row(3 * d), row(d), row(d), row(d), pl.BlockSpec((8, d), lambda i: (0, 0))],
        out_shape=[jax.ShapeDtypeStruct((l, 3 * d), BF16), jax.ShapeDtypeStruct((l, d), BF16),
                   jax.ShapeDtypeStruct((l, d), BF16), jax.ShapeDtypeStruct((l, d), BF16),
                   jax.ShapeDtypeStruct((8, d), F32)],
        compiler_params=_params(("arbitrary",)),
    )(s, pooled, att, gates, wa, wb, wc, ba.reshape(1, d), scale.reshape(1, d), dmixed)
    return outs


FFN_TC = 512
_GELU_C = 0.7978845608028654
_GELU_A = 0.044715


def _gelu(x):
    th = jnp.tanh(_GELU_C * (x + _GELU_A * x * x * x))
    return 0.5 * x * (1.0 + th), th


def _gelu_grad(x, th):
    return 0.5 * (1.0 + th) + 0.5 * x * (1.0 - th * th) * _GELU_C * (1.0 + 3.0 * _GELU_A * x * x)


FFN_CH = 32


def _ffn_taps(win):
    return (pltpu.roll(win, 2, 0)[FFN_HALO:, :], pltpu.roll(win, 1, 0)[FFN_HALO:, :], win[FFN_HALO:, :])


def _ffn_conv(taps, w, b):
    return b + w[0:1, :] * taps[0] + w[1:2, :] * taps[1] + w[2:3, :] * taps[2]


def _fold8(x):
    acc = x[0:8, :]
    for r in range(8, x.shape[0], 8):
        acc = acc + x[r:r + 8, :]
    return acc


def _ffn_fwd(ug, uv, wg, wv, bg, bv, *, name):
    l, f = ug.shape
    tm = _conv_tile(l)
    tc = _tile(f, FFN_TC, 128)
    ext = FFN_HALO + tm

    def body(ug_ref, uv_ref, wg_ref, wv_ref, bg_ref, bv_ref, o_ref, bufg, bufv):
        i = pl.program_id(1)
        for buf, u_ref in ((bufg, ug_ref), (bufv, uv_ref)):
            @pl.when(i == 0)
            def _():
                buf[0:FFN_HALO, :] = jnp.zeros((FFN_HALO, tc), F32)

            @pl.when(i > 0)
            def _():
                buf[0:FFN_HALO, :] = buf[tm:ext, :]

            buf[FFN_HALO:ext, :] = u_ref[...].astype(F32)
        wg, wv, bg_, bv_ = wg_ref[...], wv_ref[...], bg_ref[...], bv_ref[...]

        def chunk(c, carry):
            r0 = pl.multiple_of(c * FFN_CH, FFN_CH)
            gc = _ffn_conv(_ffn_taps(bufg[pl.ds(r0, FFN_HALO + FFN_CH), :]), wg, bg_)
            vc = _ffn_conv(_ffn_taps(bufv[pl.ds(r0, FFN_HALO + FFN_CH), :]), wv, bv_)
            o_ref[pl.ds(r0, FFN_CH), :] = (_gelu(gc)[0] * vc).astype(BF16)
            return carry

        lax.fori_loop(0, tm // FFN_CH, chunk, 0)

    assert tm % FFN_CH == 0
    tile = pl.BlockSpec((tm, tc), lambda j, i: (i, j))
    wspec = pl.BlockSpec((FFN_K, tc), lambda j, i: (0, j))
    bspec = pl.BlockSpec((1, tc), lambda j, i: (0, j))
    return pl.pallas_call(
        body, name=name, grid=(f // tc, l // tm),
        in_specs=[tile, tile, wspec, wspec, bspec, bspec], out_specs=tile,
        out_shape=jax.ShapeDtypeStruct((l, f), BF16),
        scratch_shapes=[pltpu.VMEM((ext, tc), F32), pltpu.VMEM((ext, tc), F32)],
        compiler_params=_params(("parallel", "arbitrary")),
    )(ug, uv, wg, wv, bg.reshape(1, f), bv.reshape(1, f))


def _ffn_bwd(ug, uv, wg, wv, bg, bv, dact, *, name):
    l, f = ug.shape
    tm = _conv_tile(l)
    tc = _tile(f, FFN_TC, 128)
    nt = l // tm
    ext = FFN_HALO + tm
    hb = tm // (2 * FFN_HALO)

    def body(ug_ref, uv_ref, pg_ref, pv_ref, wg_ref, wv_ref, bg_ref, bv_ref, da_ref,
             dug_ref, duv_ref, gg_ref, gv_ref, bufg, bufv, dbufg, dbufv, gaccg, gaccv):
        i = pl.program_id(1)
        last = i == nt - 1

        @pl.when(i == 0)
        def _():
            gg_ref[...] = jnp.zeros_like(gg_ref)
            gv_ref[...] = jnp.zeros_like(gv_ref)

        for buf, u_ref, prev_ref in ((bufg, ug_ref, pg_ref), (bufv, uv_ref, pv_ref)):
            buf[0:FFN_HALO, :] = jnp.where(last, 0.0, prev_ref[FFN_HALO:2 * FFN_HALO, :].astype(F32))
            buf[FFN_HALO:ext, :] = u_ref[...].astype(F32)
        for dbuf, gacc in ((dbufg, gaccg), (dbufv, gaccv)):
            @pl.when(i == 0)
            def _():
                dbuf[tm:ext, :] = jnp.zeros((FFN_HALO, tc), F32)

            @pl.when(i > 0)
            def _():
                dbuf[tm:ext, :] = dbuf[0:FFN_HALO, :]

            gacc[...] = jnp.zeros_like(gacc)
        wg, wv, bg_, bv_ = wg_ref[...], wv_ref[...], bg_ref[...], bv_ref[...]
        win_rows = FFN_CH + FFN_HALO

        def chunk(cc, carry):
            r0 = pl.multiple_of((tm // FFN_CH - 1 - cc) * FFN_CH, FFN_CH)
            taps_g = _ffn_taps(bufg[pl.ds(r0, win_rows), :])
            taps_v = _ffn_taps(bufv[pl.ds(r0, win_rows), :])
            gc = _ffn_conv(taps_g, wg, bg_)
            vc = _ffn_conv(taps_v, wv, bv_)
            ge, th = _gelu(gc)
            da = da_ref[pl.ds(r0, FFN_CH), :].astype(F32)
            for dc, taps, w, dbuf, du_ref, gacc in ((da * vc * _gelu_grad(gc, th), taps_g, wg, dbufg, dug_ref, gaccg),
                                                    (da * ge, taps_v, wv, dbufv, duv_ref, gaccv)):
                dbuf[pl.ds(r0, FFN_CH), :] = dc
                dwin = dbuf[pl.ds(r0, win_rows), :]
                du = (w[2:3, :] * dc + w[1:2, :] * pltpu.roll(dwin, win_rows - 1, 0)[0:FFN_CH, :]
                      + w[0:1, :] * pltpu.roll(dwin, win_rows - 2, 0)[0:FFN_CH, :])
                du_ref[pl.ds(r0, FFN_CH), :] = du.astype(BF16)
                for k in range(FFN_K):
                    gacc[8 * k:8 * k + 8, :] += _fold8(dc * taps[k])
                gacc[24:32, :] += _fold8(dc)
            return carry

        lax.fori_loop(0, tm // FFN_CH, chunk, 0)
        for gacc, gp_ref in ((gaccg, gg_ref), (gaccv, gv_ref)):
            for k in range(FFN_K + 1):
                gp_ref[k:k + 1, :] += jnp.sum(gacc[8 * k:8 * k + 8, :], axis=0, keepdims=True)

    assert tm % FFN_CH == 0
    tile = pl.BlockSpec((tm, tc), lambda j, i: (nt - 1 - i, j))
    prev = pl.BlockSpec((2 * FFN_HALO, tc), lambda j, i: (jnp.maximum((nt - 1 - i) * hb - 1, 0), j))
    wspec = pl.BlockSpec((FFN_K, tc), lambda j, i: (0, j))
    bspec = pl.BlockSpec((1, tc), lambda j, i: (0, j))
    gspec = pl.BlockSpec((8, tc), lambda j, i: (0, j))
    return pl.pallas_call(
        body, name=name, grid=(f // tc, nt),
        in_specs=[tile, tile, prev, prev, wspec, wspec, bspec, bspec, tile],
        out_specs=[tile, tile, gspec, gspec],
        out_shape=[jax.ShapeDtypeStruct((l, f), BF16), jax.ShapeDtypeStruct((l, f), BF16),
                   jax.ShapeDtypeStruct((8, f), F32), jax.ShapeDtypeStruct((8, f), F32)],
        scratch_shapes=[pltpu.VMEM((ext, tc), F32), pltpu.VMEM((ext, tc), F32),
                        pltpu.VMEM((ext, tc), F32), pltpu.VMEM((ext, tc), F32),
                        pltpu.VMEM((32, tc), F32), pltpu.VMEM((32, tc), F32)],
        compiler_params=_params(("parallel", "arbitrary")),
    )(ug, uv, ug, uv, wg, wv, bg.reshape(1, f), bv.reshape(1, f), dact)


def _adamw(parts, w, m, v, *, name):
    r = w.shape[0]
    tr = _tile(r, PACK_ROWS, 16)
    c1 = 1.0 / (1.0 - ADAM_B1 ** ADAM_STEP)
    c2 = 1.0 / (1.0 - ADAM_B2 ** ADAM_STEP)

    def body(p_ref, w_ref, m_ref, v_ref, g_ref, d_ref, nm_ref, nv_ref):
        g = p_ref[0].astype(F32)
        for k in range(1, N_DEV):
            g = g + p_ref[k].astype(F32)
        nm = ADAM_B1 * m_ref[...] + (1.0 - ADAM_B1) * g
        nv = ADAM_B2 * v_ref[...] + (1.0 - ADAM_B2) * (g * g)
        g_ref[...] = g
        nm_ref[...] = nm
        nv_ref[...] = nv
        d_ref[...] = -ADAM_LR * ((nm * c1) / (jnp.sqrt(nv * c2) + ADAM_EPS) + ADAM_WD * w_ref[...])

    tile = pl.BlockSpec((tr, 128), lambda i: (i, 0))
    return pl.pallas_call(
        body, name=name, grid=(r // tr,),
        in_specs=[pl.BlockSpec((N_DEV, tr, 128), lambda i: (0, i, 0)), tile, tile, tile],
        out_specs=[tile, tile, tile, tile],
        out_shape=[jax.ShapeDtypeStruct((r, 128), F32)] * 4,
        compiler_params=_params(("parallel",)),
    )(parts, w, m, v)


def _place():
    return lax.axis_index("x"), lax.axis_index("y"), lax.axis_index("c")


def _all_gather(x, *, name):
    def body(x_ref, out_ref, send_sems, recv_sems, local_sem):
        xx, yy, cc = _place()
        me, sibling = (xx, yy, cc), (xx, yy, 1 - cc)
        chips = [(1 - xx, yy), (xx, 1 - yy), (1 - xx, 1 - yy)]

        def slot(px, py, pc):
            return out_ref.at[4 * px + 2 * py + pc]

        def copy(k, block, to, src=None):
            return pltpu.make_async_remote_copy(
                src_ref=slot(*block) if src is None else src, dst_ref=slot(*block),
                send_sem=send_sems.at[k], recv_sem=recv_sems.at[k], device_id=to, device_id_type=MESH)

        mine = pltpu.make_async_copy(x_ref, slot(*me), local_sem)
        mine.start()
        first = [copy(0, me, sibling, src=x_ref)]
        first += [copy(1 + j, me, (*chip, cc), src=x_ref) for j, chip in enumerate(chips)]
        for cp in first:
            cp.start()
        passed = [copy(4 + j, (*chip, cc), sibling) for j, chip in enumerate(chips)]
        for j, chip in enumerate(chips):
            copy(1 + j, (*chip, cc), me).wait_recv()
            passed[j].start()
        copy(0, sibling, me).wait_recv()
        for j, chip in enumerate(chips):
            copy(4 + j, (*chip, 1 - cc), me).wait_recv()
        for cp in first + passed:
            cp.wait_send()
        mine.wait()

    return pl.pallas_call(
        body, name=name,
        in_specs=[pl.BlockSpec(memory_space=pl.ANY)], out_specs=pl.BlockSpec(memory_space=pl.ANY),
        out_shape=jax.ShapeDtypeStruct((N_DEV,) + x.shape, x.dtype),
        scratch_shapes=[pltpu.SemaphoreType.DMA((7,)), pltpu.SemaphoreType.DMA((7,)), pltpu.SemaphoreType.DMA],
    )(x)


def _all_to_all(send, *, name):
    def body(s_ref, r_ref, send_sems, recv_sems, local_sem):
        xx, yy, cc = _place()
        me = 4 * xx + 2 * yy + cc
        local = pltpu.make_async_copy(s_ref.at[me], r_ref.at[me], local_sem)
        local.start()
        copies = []
        for m in range(1, N_DEV):
            px = 1 - xx if m & 4 else xx
            py = 1 - yy if m & 2 else yy
            pc = 1 - cc if m & 1 else cc
            copies.append(pltpu.make_async_remote_copy(
                src_ref=s_ref.at[4 * px + 2 * py + pc], dst_ref=r_ref.at[me],
                send_sem=send_sems.at[m - 1], recv_sem=recv_sems.at[m - 1],
                device_id=(px, py, pc), device_id_type=MESH))
        for cp in copies:
            cp.start()
        for cp in copies:
            cp.wait_recv()
        for cp in copies:
            cp.wait_send()
        local.wait()

    return pl.pallas_call(
        body, name=name,
        in_specs=[pl.BlockSpec(memory_space=pl.ANY)], out_specs=pl.BlockSpec(memory_space=pl.ANY),
        out_shape=jax.ShapeDtypeStruct(send.shape, send.dtype),
        scratch_shapes=[pltpu.SemaphoreType.DMA((7,)), pltpu.SemaphoreType.DMA((7,)), pltpu.SemaphoreType.DMA],
    )(send)


def _as_rows(a, lead, dtype):
    a = a.astype(dtype)
    size = 1
    for s in a.shape[len(lead):]:
        size *= s
    if size % PACK_ALIGN:
        a = jnp.pad(a.reshape(lead + (size,)), [(0, 0)] * len(lead) + [(0, (-size) % PACK_ALIGN)])
    return a.reshape(lead + (-1, 128))


def _pack(arrays, dtype):
    buf = jnp.concatenate([_as_rows(a, (), dtype) for a in arrays], axis=0)
    return jnp.pad(buf, ((0, (-buf.shape[0]) % PACK_ROWS), (0, 0)))


def _pack_pieces(arrays, dtype):
    buf = jnp.concatenate([_as_rows(a, (N_DEV,), dtype) for a in arrays], axis=1)
    return jnp.pad(buf, ((0, 0), (0, (-buf.shape[1]) % PACK_ROWS), (0, 0)))


def _unpack(buf, shapes, lead=()):
    out, row = [], 0
    for shp in shapes:
        size = 1
        for s in shp:
            size *= s
        rows = (size + (-size) % PACK_ALIGN) // 128
        part = buf[..., row:row + rows, :]
        if size % PACK_ALIGN:
            part = part.reshape(lead + (rows * 128,))[..., :size]
        out.append(part.reshape(lead + tuple(shp)))
        row += rows
    return out


def _unshard(g, axis):
    g = jnp.moveaxis(g, 0, axis)
    shp = list(g.shape)
    return g.reshape(shp[:axis] + [shp[axis] * shp[axis + 1]] + shp[axis + 2:])


def _pieces(full, axis):
    shp = list(full.shape)
    g = full.reshape(shp[:axis] + [N_DEV, shp[axis] // N_DEV] + shp[axis + 1:])
    return jnp.moveaxis(g, axis, 0)


SHARDED = (("meta", 1), ("w_in", 2), ("conv_dw_w", 2), ("w_conv_out", 2), ("w_pool_grp", 3), ("w_attn_out", 2),
           ("w_o", 1), ("w_up", 2), ("ffn_dw_w", 2), ("w_down", 1))
MATRICES = ("w_in", "w_conv_out", "w_pool_grp", "w_attn_out", "w_o", "w_up", "w_down")
REPLICATED = ("norm1", "conv_dw_b", "conv_ln_g", "conv_ln_b", "b_conv_out", "pool_scale", "norm2", "ffn_dw_b",
              "final_norm")
WEIGHTS = ("meta", "norm1", "w_in", "conv_dw_w", "conv_dw_b", "conv_ln_g", "conv_ln_b", "w_conv_out", "b_conv_out",
           "w_pool_grp", "pool_scale", "w_attn_out", "w_o", "norm2", "w_up", "ffn_dw_w", "ffn_dw_b", "w_down",
           "final_norm")


def _block_diag(w_grp):
    g, gc, od = w_grp.shape
    out = jnp.zeros((g * gc, g * od), w_grp.dtype)
    for i in range(g):
        out = out.at[i * gc:(i + 1) * gc, i * od:(i + 1) * od].set(w_grp[i])
    return out


def _block_diag_grad(gw, g):
    gc, od = gw.shape[0] // g, gw.shape[1] // g
    return jnp.stack([gw[i * gc:(i + 1) * gc, i * od:(i + 1) * od] for i in range(g)])


C_CONV = 2 * CONV_CH
C_POOL = C_CONV + POOL_CH
C_ATT = HEADS * HEAD_DIM
C_QKV = C_POOL + 3 * C_ATT


def _layer_fwd(h, p, tag):
    d = h.shape[1]
    w_in = p["w_in"]
    hn = _rms_fwd(h, p["norm1"], name=f"rms1_{tag}")
    pa = _mm(hn, w_in[:, :C_POOL], out_dtype=F32, name=f"proj_a_{tag}")
    q_scale = jnp.concatenate([jnp.full((C_ATT,), Q_SCALE, F32), jnp.ones((2 * C_ATT,), F32)])
    qkv = _mm(hn, w_in[:, C_POOL:C_QKV], out_dtype=BF16, col_scale=q_scale, name=f"proj_qkv_{tag}")
    gates = _mm(hn, w_in[:, C_QKV:], out_dtype=BF16, name=f"proj_g_{tag}")
    s = _conv_fwd(pa, p["conv_dw_w"], p["conv_dw_b"], p["conv_ln_g"], p["conv_ln_b"], name=f"conv_{tag}")
    pooled = _pool_fwd(pa, name=f"pool_{tag}")
    att, att32 = _attn_fwd(qkv, name=f"attn_{tag}")
    wb = _block_diag(p["w_pool_grp"])
    mixed = _mix_fwd(s, pooled, att, gates, p["w_conv_out"], wb, p["w_attn_out"], p["b_conv_out"], p["pool_scale"],
                     name=f"mix_{tag}")
    h1 = _mm(mixed, p["w_o"], out_dtype=F32, res=h, mask_rows=True, name=f"wo_{tag}")
    hn2 = _rms_fwd(h1, p["norm2"], name=f"rms2_{tag}")
    f = p["w_up"].shape[1] // 2
    ug = _mm(hn2, p["w_up"][:, :f], out_dtype=BF16, name=f"up_g_{tag}")
    uv = _mm(hn2, p["w_up"][:, f:], out_dtype=BF16, name=f"up_v_{tag}")
    act = _ffn_fwd(ug, uv, p["ffn_dw_w"][:, :f], p["ffn_dw_w"][:, f:], p["ffn_dw_b"][:f], p["ffn_dw_b"][f:],
                   name=f"ffn_{tag}")
    h2 = _mm(act, p["w_down"], out_dtype=F32, res=h1, mask_rows=True, name=f"down_{tag}")
    saved = dict(h=h, hn=hn, pa=pa, qkv=qkv, gates=gates, s=s, pooled=pooled, att=att, att32=att32, wb=wb, mixed=mixed,
                 h1=h1,
                 hn2=hn2, ug=ug, uv=uv, act=act)
    return h2, saved


def _layer_bwd(dh2, dh2b, p, sv, tag):
    g = {}
    f = p["w_up"].shape[1] // 2
    dact = _mm(dh2b, p["w_down"].T, out_dtype=BF16, name=f"b_down_{tag}")
    g["w_down"] = _mm_tn(sv["act"], dh2b, name=f"g_down_{tag}")
    dug, duv, gpg, gpv = _ffn_bwd(sv["ug"], sv["uv"], p["ffn_dw_w"][:, :f], p["ffn_dw_w"][:, f:], p["ffn_dw_b"][:f],
                                  p["ffn_dw_b"][f:], dact, name=f"b_ffn_{tag}")
    g["ffn_dw_w"] = jnp.concatenate([gpg[0:FFN_K], gpv[0:FFN_K]], axis=1)
    g["ffn_dw_b"] = jnp.concatenate([gpg[FFN_K], gpv[FFN_K]])
    w_up_t = p["w_up"].T
    dhn2 = _mm(dug, w_up_t[:f], out_dtype=F32, name=f"b_up_g_{tag}")
    dhn2 = _mm(duv, w_up_t[f:], out_dtype=F32, res=dhn2, name=f"b_up_v_{tag}")
    g["w_up"] = jnp.concatenate([_mm_tn(sv["hn2"], dug, name=f"g_up_g_{tag}"),
                                 _mm_tn(sv["hn2"], duv, name=f"g_up_v_{tag}")], axis=1)
    dh1, dh1b, g["norm2"] = _rms_bwd(sv["h1"], p["norm2"], dhn2, dh2, name=f"b_rms2_{tag}")
    dmixed = _mm(dh1b, p["w_o"].T, out_dtype=BF16, name=f"b_wo_{tag}")
    g["w_o"] = _mm_tn(sv["mixed"], dh1b, name=f"g_wo_{tag}")
    dgates, dya, dyb, dyc, vec = _mix_bwd(sv["s"], sv["pooled"], sv["att"], sv["gates"], p["w_conv_out"], sv["wb"],
                                          p["w_attn_out"], p["b_conv_out"], p["pool_scale"], dmixed,
                                          name=f"b_mix_{tag}")
    g["b_conv_out"], g["pool_scale"] = vec[0], vec[1]
    ds = _mm(dya, p["w_conv_out"].T, out_dtype=F32, name=f"b_conv_out_{tag}")
    dpooled = _mm(dyb, sv["wb"].T, out_dtype=F32, name=f"b_pool_out_{tag}")
    datt = _mm(dyc, p["w_attn_out"].T, out_dtype=BF16, name=f"b_attn_out_{tag}")
    g["w_conv_out"] = _mm_tn(sv["s"], dya, name=f"g_conv_out_{tag}")
    g["w_pool_grp"] = _block_diag_grad(_mm_tn(sv["pooled"], dyb, name=f"g_pool_{tag}"), len(POOL_WINDOWS))
    g["w_attn_out"] = _mm_tn(sv["att"], dyc, name=f"g_attn_out_{tag}")
    dc, gp = _conv_bwd_ln(sv["pa"], ds, p["conv_dw_w"], p["conv_dw_b"], p["conv_ln_g"], p["conv_ln_b"],
                          name=f"b_conv_ln_{tag}")
    g["conv_dw_w"], g["conv_dw_b"], g["conv_ln_g"], g["conv_ln_b"] = gp[0:CONV_K], gp[32], gp[33], gp[34]
    dconv = _conv_bwd_in(sv["pa"], dc, p["conv_dw_w"], name=f"b_conv_in_{tag}")
    dp = _pool_bwd(dpooled, name=f"b_pool_{tag}")
    dq, dk, dv = _attn_bwd(sv["qkv"], sv["att32"], datt, name=f"b_attn_{tag}")
    dk = jnp.moveaxis(dk, 0, 1).reshape(dq.shape).astype(BF16)
    dv = jnp.moveaxis(dv, 0, 1).reshape(dq.shape).astype(BF16)
    w_in_t = p["w_in"].T
    cols = [(jnp.concatenate([dconv, dp, dq, dk, dv], axis=1), 0, C_QKV), (dgates, C_QKV, w_in_t.shape[0])]
    dhn, gw = None, []
    for n, (dcol, lo, hi) in enumerate(cols):
        dhn = _mm(dcol, w_in_t[lo:hi], out_dtype=F32, res=dhn, name=f"b_in{n}_{tag}")
        gw.append(_mm_tn(sv["hn"], dcol, name=f"g_in{n}_{tag}"))
    g["w_in"] = jnp.concatenate(gw, axis=1)
    dh, dhb, g["norm1"] = _rms_bwd(sv["h"], p["norm1"], dhn, dh1, name=f"b_rms1_{tag}")
    return dh, dhb, g


def kernel(x, meta, norm1, w_in, conv_dw_w, conv_dw_b, conv_ln_g, conv_ln_b, w_conv_out, b_conv_out, w_pool_grp, pool_scale, w_attn_out, w_o, norm2, w_up, ffn_dw_w, ffn_dw_b, w_down, final_norm, loss_target, m_meta, m_norm1, m_w_in, m_conv_dw_w, m_conv_dw_b, m_conv_ln_g, m_conv_ln_b, m_w_conv_out, m_b_conv_out, m_w_pool_grp, m_pool_scale, m_w_attn_out, m_w_o, m_norm2, m_w_up, m_ffn_dw_w, m_ffn_dw_b, m_w_down, m_final_norm, v_meta, v_norm1, v_w_in, v_conv_dw_w, v_conv_dw_b, v_conv_ln_g, v_conv_ln_b, v_w_conv_out, v_b_conv_out, v_w_pool_grp, v_pool_scale, v_w_attn_out, v_w_o, v_norm2, v_w_up, v_ffn_dw_w, v_ffn_dw_b, v_w_down, v_final_norm):
    given = dict(meta=meta, norm1=norm1, w_in=w_in, conv_dw_w=conv_dw_w, conv_dw_b=conv_dw_b, conv_ln_g=conv_ln_g, conv_ln_b=conv_ln_b, w_conv_out=w_conv_out, b_conv_out=b_conv_out, w_pool_grp=w_pool_grp, pool_scale=pool_scale, w_attn_out=w_attn_out, w_o=w_o, norm2=norm2, w_up=w_up, ffn_dw_w=ffn_dw_w, ffn_dw_b=ffn_dw_b, w_down=w_down, final_norm=final_norm)
    mom_m = dict(meta=m_meta, norm1=m_norm1, w_in=m_w_in, conv_dw_w=m_conv_dw_w, conv_dw_b=m_conv_dw_b, conv_ln_g=m_conv_ln_g, conv_ln_b=m_conv_ln_b, w_conv_out=m_w_conv_out, b_conv_out=m_b_conv_out, w_pool_grp=m_w_pool_grp, pool_scale=m_pool_scale, w_attn_out=m_w_attn_out, w_o=m_w_o, norm2=m_norm2, w_up=m_w_up, ffn_dw_w=m_ffn_dw_w, ffn_dw_b=m_ffn_dw_b, w_down=m_w_down, final_norm=m_final_norm)
    mom_v = dict(meta=v_meta, norm1=v_norm1, w_in=v_w_in, conv_dw_w=v_conv_dw_w, conv_dw_b=v_conv_dw_b, conv_ln_g=v_conv_ln_g, conv_ln_b=v_conv_ln_b, w_conv_out=v_w_conv_out, b_conv_out=v_b_conv_out, w_pool_grp=v_w_pool_grp, pool_scale=v_pool_scale, w_attn_out=v_w_attn_out, w_o=v_w_o, norm2=v_norm2, w_up=v_w_up, ffn_dw_w=v_ffn_dw_w, ffn_dw_b=v_ffn_dw_b, w_down=v_w_down, final_norm=v_final_norm)
    sharded_axis = dict(SHARDED)
    vectors = [n for n, _ in SHARDED if n not in MATRICES]
    depth = norm1.shape[0]

    got_mat = _all_gather(_pack([given[n] for n in MATRICES], BF16), name="gather_matrices")
    got_vec = _all_gather(_pack([given[n] for n in vectors], F32), name="gather_vectors")
    full = {n: given[n] for n in REPLICATED}
    for n, a in zip(MATRICES, _unpack(got_mat, [given[n].shape for n in MATRICES], (N_DEV,))):
        full[n] = _unshard(a, sharded_axis[n])
    for n, a in zip(vectors, _unpack(got_vec, [given[n].shape for n in vectors], (N_DEV,))):
        full[n] = _unshard(a, sharded_axis[n])

    xs = x[0]
    d = xs.shape[1]
    h = jnp.concatenate([jnp.zeros((PAD, d), F32), full["meta"], xs], axis=0)
    layers, saved = [], []
    for i in range(depth):
        p = {n: full[n][i] for n in full if n not in ("meta", "final_norm")}
        layers.append(p)
        h, sv = _layer_fwd(h, p, f"l{i}")
        saved.append(sv)
    loss_part, dh, dhb, g_final = _loss_head(h, full["final_norm"], loss_target[0], name="loss_head")

    grads = [None] * depth
    for i in reversed(range(depth)):
        dh, dhb, grads[i] = _layer_bwd(dh, dhb, layers[i], saved[i], f"l{i}")
    full_grad = {n: jnp.stack([grads[i][n] for i in range(depth)]) for n in grads[0]}
    full_grad["meta"] = dh[PAD:FRONT]
    full_grad["final_norm"] = g_final
    grad_x = dh[FRONT:][None]

    names = [n for n, _ in SHARDED]
    recv = _all_to_all(_pack_pieces([_pieces(full_grad[n], sharded_axis[n]) for n in names], BF16),
                       name="scatter_grads")
    rep_shapes = [given[n].shape for n in REPLICATED] + [(1,)]
    rep_parts = _all_gather(_pack([full_grad[n] for n in REPLICATED] + [loss_part.reshape(1)], F32),
                            name="gather_partials")

    out = {}
    shapes = [given[n].shape for n in names]
    res = _adamw(recv, _pack([given[n] for n in names], F32), _pack([mom_m[n] for n in names], F32),
                 _pack([mom_v[n] for n in names], F32), name="adamw_sharded")
    for kind, buf in zip(("grad", "delta", "new_m", "new_v"), res):
        for n, a in zip(names, _unpack(buf, shapes)):
            out[kind, n] = a
    rep_w = [given[n] for n in REPLICATED] + [jnp.zeros((1,), F32)]
    rep_m = [mom_m[n] for n in REPLICATED] + [jnp.zeros((1,), F32)]
    rep_v = [mom_v[n] for n in REPLICATED] + [jnp.ones((1,), F32)]
    res = _adamw(rep_parts, _pack(rep_w, F32), _pack(rep_m, F32), _pack(rep_v, F32), name="adamw_replicated")
    for kind, buf in zip(("grad", "delta", "new_m", "new_v"), res):
        for n, a in zip(list(REPLICATED) + ["loss"], _unpack(buf, rep_shapes)):
            out[kind, n] = a
    loss = out["grad", "loss"][0]
    return (loss, grad_x, *[out["grad", n] for n in WEIGHTS], *[out["delta", n] for n in WEIGHTS],
            *[out["new_m", n] for n in WEIGHTS], *[out["new_v", n] for n in WEIGHTS])
```

```python
import functools

import jax
import jax.numpy as jnp
from jax import lax
from jax.experimental import pallas as pl
from jax.experimental.pallas import tpu as pltpu

F32 = jnp.float32
BF16 = jnp.bfloat16
MESH = pl.DeviceIdType.MESH

N_DEV = 8
N_META = 16
BLOCK = 128
PAD = 240
FRONT = PAD + N_META
HEADS = 4
HEAD_DIM = 128
CONV_CH = 256
CONV_K = 31
POOL_CH = 256
POOL_WINDOWS = (2, 4, 8, 16)
FFN_K = 3
EPS = 1e-6
ADAM_LR, ADAM_B1, ADAM_B2, ADAM_EPS, ADAM_WD, ADAM_STEP = 0.001, 0.9, 0.999, 1e-08, 0.01, 10

VMEM_LIMIT = 56 * 1024 * 1024
CONV_HALO = 32
POOL_HALO = 16
FFN_HALO = 8
PACK_ALIGN = 16 * 128
PACK_ROWS = 512


def _tile(n, cap, unit):
    if n <= cap:
        return n
    best = None
    t = unit
    while t <= cap:
        if n % t == 0:
            best = t
        t += unit
    assert best is not None, (n, cap, unit)
    return best


def _params(sem):
    return pltpu.CompilerParams(dimension_semantics=sem, vmem_limit_bytes=VMEM_LIMIT)


def _sigmoid(x):
    return 1.0 / (1.0 + jnp.exp(-x))


MM_MAX_K = 3072
MM_RESIDENT_B = 3072 * 1024 * 2


def _mm(a, b, *, out_dtype, name, res=None, col_scale=None, mask_rows=False, tn_cap=768):
    m, k = a.shape
    k2, n = b.shape
    assert k == k2 and k <= MM_MAX_K
    tn = n if k * n * 2 <= MM_RESIDENT_B else _tile(n, tn_cap, 128)
    tm = _tile(m, 1280 if (k <= 1024 and tn <= 1024) else 640, 128)

    def body(*refs):
        refs = list(refs)
        a_ref, b_ref = refs[:2]
        o_ref = refs[-1]
        r_ref = refs[2] if res is not None else None
        c_ref = refs[-2] if col_scale is not None else None
        y = jnp.dot(a_ref[...].astype(BF16), b_ref[...].astype(BF16), preferred_element_type=F32)
        if col_scale is not None:
            y = y * c_ref[...]
        if res is not None:
            y = y + r_ref[...].astype(F32)
        if mask_rows:
            row = pl.program_id(0) * tm + lax.broadcasted_iota(jnp.int32, (tm, 1), 0)
            y = jnp.where(row >= PAD, y, 0.0)
        o_ref[...] = y.astype(out_dtype)

    in_specs = [pl.BlockSpec((tm, k), lambda i, j: (i, 0)), pl.BlockSpec((k, tn), lambda i, j: (0, j))]
    args = [a, b]
    if res is not None:
        in_specs.append(pl.BlockSpec((tm, tn), lambda i, j: (i, j)))
        args.append(res)
    if col_scale is not None:
        in_specs.append(pl.BlockSpec((1, tn), lambda i, j: (0, j)))
        args.append(col_scale.reshape(1, n))
    return pl.pallas_call(
        body, name=name, grid=(m // tm, n // tn),
        in_specs=in_specs, out_specs=pl.BlockSpec((tm, tn), lambda i, j: (i, j)),
        out_shape=jax.ShapeDtypeStruct((m, n), out_dtype),
        compiler_params=_params(("parallel", "parallel")),
    )(*args)


def _mm_tn(a, b, *, name, t1_cap=512, tn_cap=1024, tl_cap=3328):
    l, k1 = a.shape
    l2, n = b.shape
    assert l == l2
    t1, tn, tl = _tile(k1, t1_cap, 128), _tile(n, tn_cap, 128), _tile(l, tl_cap, 128)

    def body(a_ref, b_ref, o_ref):
        @pl.when(pl.program_id(2) == 0)
        def _():
            o_ref[...] = jnp.zeros_like(o_ref)

        o_ref[...] += lax.dot_general(a_ref[...].astype(BF16), b_ref[...].astype(BF16),
                                      (((0,), (0,)), ((), ())), preferred_element_type=F32)

    return pl.pallas_call(
        body, name=name, grid=(k1 // t1, n // tn, l // tl),
        in_specs=[pl.BlockSpec((tl, t1), lambda i, j, ll: (ll, i)), pl.BlockSpec((tl, tn), lambda i, j, ll: (ll, j))],
        out_specs=pl.BlockSpec((t1, tn), lambda i, j, ll: (i, j)),
        out_shape=jax.ShapeDtypeStruct((k1, n), F32),
        compiler_params=_params(("parallel", "parallel", "arbitrary")),
    )(a, b)


def _rms_fwd(x, g, *, name):
    l, d = x.shape
    tm = _tile(l, 640, 128)

    def body(x_ref, g_ref, o_ref):
        xv = x_ref[...]
        r = lax.rsqrt(jnp.mean(xv * xv, axis=-1, keepdims=True) + EPS)
        o_ref[...] = (xv * r * g_ref[...]).astype(BF16)

    return pl.pallas_call(
        body, name=name, grid=(l // tm,),
        in_specs=[pl.BlockSpec((tm, d), lambda i: (i, 0)), pl.BlockSpec((1, d), lambda i: (0, 0))],
        out_specs=pl.BlockSpec((tm, d), lambda i: (i, 0)),
        out_shape=jax.ShapeDtypeStruct((l, d), BF16),
        compiler_params=_params(("parallel",)),
    )(x, g.reshape(1, d))


def _rms_bwd(x, g, dy, dres, *, name):
    l, d = x.shape
    tm = _tile(l, 640, 128)

    def body(x_ref, g_ref, dy_ref, dr_ref, dx_ref, dxb_ref, dg_ref):
        i = pl.program_id(0)

        @pl.when(i == 0)
        def _():
            dg_ref[...] = jnp.zeros_like(dg_ref)

        xv = x_ref[...]
        r = lax.rsqrt(jnp.mean(xv * xv, axis=-1, keepdims=True) + EPS)
        xh = xv * r
        dyv = dy_ref[...].astype(F32)
        dxh = dyv * g_ref[...]
        dx = r * (dxh - xh * jnp.mean(dxh * xh, axis=-1, keepdims=True)) + dr_ref[...]
        row = i * tm + lax.broadcasted_iota(jnp.int32, (tm, 1), 0)
        dx = jnp.where(row >= PAD, dx, 0.0)
        dx_ref[...] = dx
        dxb_ref[...] = dx.astype(BF16)
        dg_ref[0:1, :] += jnp.sum(dyv * xh, axis=0, keepdims=True)

    tile = pl.BlockSpec((tm, d), lambda i: (i, 0))
    dx, dxb, dg = pl.pallas_call(
        body, name=name, grid=(l // tm,),
        in_specs=[tile, pl.BlockSpec((1, d), lambda i: (0, 0)), tile, tile],
        out_specs=[tile, tile, pl.BlockSpec((8, d), lambda i: (0, 0))],
        out_shape=[jax.ShapeDtypeStruct((l, d), F32), jax.ShapeDtypeStruct((l, d), BF16),
                   jax.ShapeDtypeStruct((8, d), F32)],
        compiler_params=_params(("arbitrary",)),
    )(x, g.reshape(1, d), dy, dres)
    return dx, dxb, dg[0]


def _loss_head(h, g, target, *, name):
    l, d = h.shape
    tm = FRONT
    assert l % tm == 0 and target.shape[0] == l - tm

    def body(h_ref, g_ref, t_ref, dh_ref, dhb_ref, loss_ref, dg_ref):
        i = pl.program_id(0)

        @pl.when(i == 0)
        def _():
            loss_ref[...] = jnp.zeros_like(loss_ref)
            dg_ref[...] = jnp.zeros_like(dg_ref)
            dh_ref[...] = jnp.zeros_like(dh_ref)
            dhb_ref[...] = jnp.zeros_like(dhb_ref)

        @pl.when(i > 0)
        def _():
            xv = h_ref[...]
            r = lax.rsqrt(jnp.mean(xv * xv, axis=-1, keepdims=True) + EPS)
            xh = xv * r
            gv = g_ref[...]
            err = xh * gv - t_ref[...]
            loss_ref[...] += 0.5 * jnp.sum(jnp.mean(err * err, axis=-1, keepdims=True))
            dy = err * (1.0 / d)
            dxh = dy * gv
            dh = r * (dxh - xh * jnp.mean(dxh * xh, axis=-1, keepdims=True))
            dh_ref[...] = dh
            dhb_ref[...] = dh.astype(BF16)
            dg_ref[0:1, :] += jnp.sum(dy * xh, axis=0, keepdims=True)

    tile = pl.BlockSpec((tm, d), lambda i: (i, 0))
    dh, dhb, loss, dg = pl.pallas_call(
        body, name=name, grid=(l // tm,),
        in_specs=[tile, pl.BlockSpec((1, d), lambda i: (0, 0)),
                  pl.BlockSpec((tm, d), lambda i: (jnp.maximum(i - 1, 0), 0))],
        out_specs=[tile, tile, pl.BlockSpec((8, 128), lambda i: (0, 0)), pl.BlockSpec((8, d), lambda i: (0, 0))],
        out_shape=[jax.ShapeDtypeStruct((l, d), F32), jax.ShapeDtypeStruct((l, d), BF16),
                   jax.ShapeDtypeStruct((8, 128), F32), jax.ShapeDtypeStruct((8, d), F32)],
        compiler_params=_params(("arbitrary",)),
    )(h, g.reshape(1, d), target)
    return loss[0, 0], dh, dhb, dg[0]


def _conv_tile(l):
    return _tile(l, 640, 128)


def _conv_core(a, gt, buf, dw_w, dw_b, first):
    tm = a.shape[0]

    @pl.when(first)
    def _():
        buf[0:CONV_HALO, :] = jnp.zeros((CONV_HALO, CONV_CH), F32)

    @pl.when(jnp.logical_not(first))
    def _():
        buf[0:CONV_HALO, :] = buf[tm:tm + CONV_HALO, :]

    sg = _sigmoid(gt)
    buf[CONV_HALO:CONV_HALO + tm, :] = a * sg
    c = jnp.zeros((tm, CONV_CH), F32) + dw_b
    for k in range(CONV_K):
        off = CONV_HALO - (CONV_K - 1) + k
        c = c + dw_w[k:k + 1, :] * buf[off:off + tm, :]
    return c, sg


def _layer_norm(c, ln_g, ln_b):
    mu = jnp.mean(c, axis=-1, keepdims=True)
    xc = c - mu
    r = lax.rsqrt(jnp.mean(xc * xc, axis=-1, keepdims=True) + EPS)
    xh = xc * r
    return xh, r, xh * ln_g + ln_b


def _conv_fwd(pa, dw_w, dw_b, ln_g, ln_b, *, name):
    l = pa.shape[0]
    tm = _conv_tile(l)

    def body(a_ref, gt_ref, w_ref, b_ref, g_ref, bb_ref, o_ref, buf):
        c, _ = _conv_core(a_ref[...], gt_ref[...], buf, w_ref[...], b_ref[...], pl.program_id(0) == 0)
        _, _, y = _layer_norm(c, g_ref[...], bb_ref[...])
        o_ref[...] = (y * _sigmoid(y)).astype(BF16)

    vec = pl.BlockSpec((1, CONV_CH), lambda i: (0, 0))
    return pl.pallas_call(
        body, name=name, grid=(l // tm,),
        in_specs=[pl.BlockSpec((tm, CONV_CH), lambda i: (i, 0)), pl.BlockSpec((tm, CONV_CH), lambda i: (i, 1)),
                  pl.BlockSpec((CONV_K, CONV_CH), lambda i: (0, 0)), vec, vec, vec],
        out_specs=pl.BlockSpec((tm, CONV_CH), lambda i: (i, 0)),
        out_shape=jax.ShapeDtypeStruct((l, CONV_CH), BF16),
        scratch_shapes=[pltpu.VMEM((CONV_HALO + tm, CONV_CH), F32)],
        compiler_params=_params(("arbitrary",)),
    )(pa, pa, dw_w, dw_b.reshape(1, -1), ln_g.reshape(1, -1), ln_b.reshape(1, -1))


def _conv_bwd_ln(pa, ds, dw_w, dw_b, ln_g, ln_b, *, name):
    l = pa.shape[0]
    tm = _conv_tile(l)

    def body(a_ref, gt_ref, ds_ref, w_ref, b_ref, g_ref, bb_ref, dc_ref, gp_ref, buf):
        i = pl.program_id(0)

        @pl.when(i == 0)
        def _():
            gp_ref[...] = jnp.zeros_like(gp_ref)

        c, _ = _conv_core(a_ref[...], gt_ref[...], buf, w_ref[...], b_ref[...], i == 0)
        xh, r, y = _layer_norm(c, g_ref[...], bb_ref[...])
        sy = _sigmoid(y)
        dy = ds_ref[...] * (sy * (1.0 + y * (1.0 - sy)))
        dxh = dy * g_ref[...]
        dc = r * (dxh - jnp.mean(dxh, axis=-1, keepdims=True) - xh * jnp.mean(dxh * xh, axis=-1, keepdims=True))
        dc_ref[...] = dc
        for k in range(CONV_K):
            off = CONV_HALO - (CONV_K - 1) + k
            gp_ref[k:k + 1, :] += jnp.sum(dc * buf[off:off + tm, :], axis=0, keepdims=True)
        gp_ref[32:33, :] += jnp.sum(dc, axis=0, keepdims=True)
        gp_ref[33:34, :] += jnp.sum(dy * xh, axis=0, keepdims=True)
        gp_ref[34:35, :] += jnp.sum(dy, axis=0, keepdims=True)

    vec = pl.BlockSpec((1, CONV_CH), lambda i: (0, 0))
    return pl.pallas_call(
        body, name=name, grid=(l // tm,),
        in_specs=[pl.BlockSpec((tm, CONV_CH), lambda i: (i, 0)), pl.BlockSpec((tm, CONV_CH), lambda i: (i, 1)),
                  pl.BlockSpec((tm, CONV_CH), lambda i: (i, 0)),
                  pl.BlockSpec((CONV_K, CONV_CH), lambda i: (0, 0)), vec, vec, vec],
        out_specs=[pl.BlockSpec((tm, CONV_CH), lambda i: (i, 0)), pl.BlockSpec((40, CONV_CH), lambda i: (0, 0))],
        out_shape=[jax.ShapeDtypeStruct((l, CONV_CH), F32), jax.ShapeDtypeStruct((40, CONV_CH), F32)],
        scratch_shapes=[pltpu.VMEM((CONV_HALO + tm, CONV_CH), F32)],
        compiler_params=_params(("arbitrary",)),
    )(pa, pa, ds, dw_w, dw_b.reshape(1, -1), ln_g.reshape(1, -1), ln_b.reshape(1, -1))


def _conv_bwd_in(pa, dc, dw_w, *, name):
    l = pa.shape[0]
    tm = _conv_tile(l)
    nt = l // tm

    def body(a_ref, gt_ref, dc_ref, w_ref, o_ref, buf):
        first = pl.program_id(0) == 0

        @pl.when(first)
        def _():
            buf[tm:tm + CONV_HALO, :] = jnp.zeros((CONV_HALO, CONV_CH), F32)

        @pl.when(jnp.logical_not(first))
        def _():
            buf[tm:tm + CONV_HALO, :] = buf[0:CONV_HALO, :]

        buf[0:tm, :] = dc_ref[...]
        w = w_ref[...]
        dhc = jnp.zeros((tm, CONV_CH), F32)
        for k in range(CONV_K):
            off = CONV_K - 1 - k
            dhc = dhc + w[k:k + 1, :] * buf[off:off + tm, :]
        a = a_ref[...]
        sg = _sigmoid(gt_ref[...])
        o_ref[:, 0:CONV_CH] = (dhc * sg).astype(BF16)
        o_ref[:, CONV_CH:2 * CONV_CH] = (dhc * a * sg * (1.0 - sg)).astype(BF16)

    return pl.pallas_call(
        body, name=name, grid=(nt,),
        in_specs=[pl.BlockSpec((tm, CONV_CH), lambda i: (nt - 1 - i, 0)),
                  pl.BlockSpec((tm, CONV_CH), lambda i: (nt - 1 - i, 1)),
                  pl.BlockSpec((tm, CONV_CH), lambda i: (nt - 1 - i, 0)),
                  pl.BlockSpec((CONV_K, CONV_CH), lambda i: (0, 0))],
        out_specs=pl.BlockSpec((tm, 2 * CONV_CH), lambda i: (nt - 1 - i, 0)),
        out_shape=jax.ShapeDtypeStruct((l, 2 * CONV_CH), BF16),
        scratch_shapes=[pltpu.VMEM((tm + CONV_HALO, CONV_CH), F32)],
        compiler_params=_params(("arbitrary",)),
    )(pa, pa, dc, dw_w)


def _pool_consts(tm, row0):
    lane = lax.broadcasted_iota(jnp.int32, (1, POOL_CH), 1)
    grp = lane // (POOL_CH // len(POOL_WINDOWS))
    win = jnp.where(grp == 0, 2.0, jnp.where(grp == 1, 4.0, jnp.where(grp == 2, 8.0, 16.0))).astype(F32)
    pos = (row0 + lax.broadcasted_iota(jnp.int32, (tm, 1), 0) - PAD).astype(F32)
    cnt = jnp.maximum(jnp.minimum(pos + 1.0, win), 1.0)
    return grp, cnt


def _pool_select(grp, s2, s4, s8, s16):
    return jnp.where(grp == 0, s2, jnp.where(grp == 1, s4, jnp.where(grp == 2, s8, s16)))


def _pool_fwd(pa, *, name):
    l = pa.shape[0]
    tm = _conv_tile(l)
    ext = POOL_HALO + tm

    def body(p_ref, o_ref, buf):
        i = pl.program_id(0)

        @pl.when(i == 0)
        def _():
            buf[0:POOL_HALO, :] = jnp.zeros((POOL_HALO, POOL_CH), F32)

        @pl.when(i > 0)
        def _():
            buf[0:POOL_HALO, :] = buf[tm:tm + POOL_HALO, :]

        p = p_ref[...]
        buf[POOL_HALO:ext, :] = p
        x = buf[...]
        s2 = x + pltpu.roll(x, 1, 0)
        s4 = s2 + pltpu.roll(s2, 2, 0)
        s8 = s4 + pltpu.roll(s4, 4, 0)
        s16 = s8 + pltpu.roll(s8, 8, 0)
        grp, cnt = _pool_consts(tm, i * tm)
        s = _pool_select(grp, s2, s4, s8, s16)[POOL_HALO:ext, :]
        o_ref[...] = (s / cnt - p).astype(BF16)

    return pl.pallas_call(
        body, name=name, grid=(l // tm,),
        in_specs=[pl.BlockSpec((tm, POOL_CH), lambda i: (i, 2))],
        out_specs=pl.BlockSpec((tm, POOL_CH), lambda i: (i, 0)),
        out_shape=jax.ShapeDtypeStruct((l, POOL_CH), BF16),
        scratch_shapes=[pltpu.VMEM((ext, POOL_CH), F32)],
        compiler_params=_params(("arbitrary",)),
    )(pa)


def _pool_bwd(dpooled, *, name):
    l = dpooled.shape[0]
    tm = _conv_tile(l)
    nt = l // tm
    ext = tm + POOL_HALO

    def body(d_ref, o_ref, buf):
        i = pl.program_id(0)

        @pl.when(i == 0)
        def _():
            buf[tm:ext, :] = jnp.zeros((POOL_HALO, POOL_CH), F32)

        @pl.when(i > 0)
        def _():
            buf[tm:ext, :] = buf[0:POOL_HALO, :]

        d = d_ref[...]
        grp, cnt = _pool_consts(tm, (nt - 1 - i) * tm)
        buf[0:tm, :] = d / cnt
        x = buf[...]
        s2 = x + pltpu.roll(x, ext - 1, 0)
        s4 = s2 + pltpu.roll(s2, ext - 2, 0)
        s8 = s4 + pltpu.roll(s4, ext - 4, 0)
        s16 = s8 + pltpu.roll(s8, ext - 8, 0)
        s = _pool_select(grp, s2, s4, s8, s16)[0:tm, :]
        o_ref[...] = (s - d).astype(BF16)

    return pl.pallas_call(
        body, name=name, grid=(nt,),
        in_specs=[pl.BlockSpec((tm, POOL_CH), lambda i: (nt - 1 - i, 0))],
        out_specs=pl.BlockSpec((tm, POOL_CH), lambda i: (nt - 1 - i, 0)),
        out_shape=jax.ShapeDtypeStruct((l, POOL_CH), BF16),
        scratch_shapes=[pltpu.VMEM((ext, POOL_CH), F32)],
        compiler_params=_params(("arbitrary",)),
    )(dpooled)


ATT_TQ = 256
ATT_TK = 2 * BLOCK
ATT_SUB = ATT_TK // BLOCK
LOG2E = 1.4426950408889634
LN2 = 0.6931471805599453
Q_SCALE = HEAD_DIM ** -0.5 * LOG2E
ATT_CUT = 160.0


def _tri_ones():
    r = lax.broadcasted_iota(jnp.int32, (2 * BLOCK, 2 * BLOCK), 0) % BLOCK
    c = lax.broadcasted_iota(jnp.int32, (2 * BLOCK, 2 * BLOCK), 1)
    return jnp.where((c >= BLOCK) | (r > c), 1.0, 0.0).astype(BF16)


def _split_dot(x, rhs):
    hi = x.astype(BF16)
    lo = (x - hi.astype(F32)).astype(BF16)
    return jnp.dot(jnp.concatenate([hi, lo], axis=1), rhs, preferred_element_type=F32)


def _scores(q, kt, qpos, base, masked):
    z = lax.dot_general(q, kt, (((1,), (1,)), ((), ())), preferred_element_type=F32)
    sp = jnp.log2(1.0 + jnp.exp2(-jnp.abs(z)))
    lb = jnp.minimum(z, 0.0) - sp
    lk = lb - z
    valid = None
    if masked:
        kpos = base + lax.broadcasted_iota(jnp.int32, (1, z.shape[1]), 1)
        valid = (kpos < qpos) & (kpos >= PAD)
        lk = jnp.where(valid, lk, 0.0)
    return lk, lb, valid


def _suffix(x, tri, carry):
    wts = [_split_dot(x[:, b * BLOCK:(b + 1) * BLOCK], tri) for b in range(ATT_SUB)]
    offs = [None] * ATT_SUB
    s = carry
    for b in reversed(range(ATT_SUB)):
        offs[b] = wts[b][:, :BLOCK] + s
        s = s + wts[b][:, BLOCK:]
    return jnp.concatenate(offs, axis=1), s


def _walk_tiles(i, tq, step):
    t_top = ((i + 1) * tq - 1) // ATT_TK
    t_diag = (i * tq) // ATT_TK
    n_plain = jnp.maximum(t_diag - 1, 0)

    def masked(jj, top):
        return step(t_top - jj, True)

    def live(carry):
        return (carry[0] < n_plain) & (carry[1] > -ATT_CUT)

    def plain(carry):
        return carry[0] + 1, step(t_diag - 1 - carry[0], False)

    top = lax.fori_loop(0, t_top - t_diag + 1, masked, jnp.float32(0.0))
    _, top = lax.while_loop(live, plain, (jnp.int32(0), top))

    @pl.when((t_diag > 0) & (top > -ATT_CUT))
    def _():
        step(0, True)


def _tile_base(t):
    base = t * ATT_TK
    return base if isinstance(base, int) else pl.multiple_of(base, BLOCK)


def _attn_fwd(qkv, *, name):
    l = qkv.shape[0]
    tq = ATT_TQ
    assert l % tq == 0 and l % ATT_TK == 0

    def body(q_ref, k_ref, v_ref, o_ref, o32_ref, acc_ref, r_ref):
        i = pl.program_id(1)
        acc_ref[...] = jnp.zeros_like(acc_ref)
        r_ref[...] = jnp.zeros_like(r_ref)
        q = q_ref[...]
        qpos = i * tq + lax.broadcasted_iota(jnp.int32, (tq, 1), 0)
        tri = _tri_ones()

        def step(t, masked):
            base = _tile_base(t)
            lk, lb, valid = _scores(q, k_ref[pl.ds(base, ATT_TK), :], qpos, base, masked)
            off, r_new = _suffix(lk, tri, r_ref[...])
            a = jnp.exp2(lb + off)
            if masked:
                a = jnp.where(valid, a, 0.0)
            acc_ref[...] += jnp.dot(a.astype(BF16), v_ref[pl.ds(base, ATT_TK), :], preferred_element_type=F32)
            r_ref[...] = r_new
            return jnp.max(r_new)

        _walk_tiles(i, tq, step)
        o_ref[...] = acc_ref[...].astype(BF16)
        o32_ref[...] = acc_ref[...]

    tile = pl.BlockSpec((tq, HEAD_DIM), lambda h, i: (i, h))
    return pl.pallas_call(
        body, name=name, grid=(HEADS, l // tq),
        in_specs=[tile,
                  pl.BlockSpec((l, HEAD_DIM), lambda h, i: (0, HEADS + h)),
                  pl.BlockSpec((l, HEAD_DIM), lambda h, i: (0, 2 * HEADS + h))],
        out_specs=[tile, tile],
        out_shape=[jax.ShapeDtypeStruct((l, HEADS * HEAD_DIM), BF16),
                   jax.ShapeDtypeStruct((l, HEADS * HEAD_DIM), F32)],
        scratch_shapes=[pltpu.VMEM((tq, HEAD_DIM), F32), pltpu.VMEM((tq, BLOCK), F32)],
        compiler_params=_params(("parallel", "arbitrary")),
    )(qkv, qkv, qkv)


def _attn_bwd(qkv, att, datt, *, name):
    l = qkv.shape[0]
    tq = ATT_TQ
    nq = l // tq
    assert l % tq == 0 and l % ATT_TK == 0

    def body(q_ref, k_ref, v_ref, o_ref, do_ref, dq_ref, dk_hbm, dv_hbm, dk_acc, dv_acc, dq_acc, r_ref, s_ref, sem):
        h = pl.program_id(0)
        i = pl.program_id(1)

        @pl.when(i == 0)
        def _():
            dk_acc[...] = jnp.zeros_like(dk_acc)
            dv_acc[...] = jnp.zeros_like(dv_acc)

        dq_acc[...] = jnp.zeros_like(dq_acc)
        r_ref[...] = jnp.zeros_like(r_ref)
        s_ref[...] = jnp.zeros_like(s_ref)
        q = q_ref[...]
        do = do_ref[...]
        ptot = jnp.sum(do.astype(F32) * o_ref[...], axis=-1, keepdims=True)
        qpos = i * tq + lax.broadcasted_iota(jnp.int32, (tq, 1), 0)
        tri = _tri_ones()

        def step(t, masked):
            base = _tile_base(t)
            kt = k_ref[pl.ds(base, ATT_TK), :]
            vt = v_ref[pl.ds(base, ATT_TK), :]
            lk, lb, valid = _scores(q, kt, qpos, base, masked)
            off, r_new = _suffix(lk, tri, r_ref[...])
            a = jnp.exp2(lb + off)
            if masked:
                a = jnp.where(valid, a, 0.0)
            ab = a.astype(BF16)
            da = lax.dot_general(do, vt, (((1,), (1,)), ((), ())), preferred_element_type=F32)
            p = ab.astype(F32) * da
            poff, s_new = _suffix(p, tri, s_ref[...])
            dz = (p - jnp.exp2(lb) * (ptot - poff)) * LN2
            if masked:
                dz = jnp.where(valid, dz, 0.0)
            dzb = dz.astype(BF16)
            dq_acc[...] += jnp.dot(dzb, kt, preferred_element_type=F32)
            dk_acc[pl.ds(base, ATT_TK), :] += lax.dot_general(dzb, q, (((0,), (0,)), ((), ())),
                                                              preferred_element_type=F32)
            dv_acc[pl.ds(base, ATT_TK), :] += lax.dot_general(ab, do, (((0,), (0,)), ((), ())),
                                                              preferred_element_type=F32)
            r_ref[...] = r_new
            s_ref[...] = s_new
            return jnp.max(r_new)

        _walk_tiles(i, tq, step)
        dq_ref[...] = (dq_acc[...] * Q_SCALE).astype(BF16)

        @pl.when(i == nq - 1)
        def _():
            ck = pltpu.make_async_copy(dk_acc, dk_hbm.at[h], sem.at[0])
            cv = pltpu.make_async_copy(dv_acc, dv_hbm.at[h], sem.at[1])
            ck.start()
            cv.start()
            ck.wait()
            cv.wait()

    tile = pl.BlockSpec((tq, HEAD_DIM), lambda h, i: (i, h))
    return pl.pallas_call(
        body, name=name, grid=(HEADS, nq),
        in_specs=[tile,
                  pl.BlockSpec((l, HEAD_DIM), lambda h, i: (0, HEADS + h)),
                  pl.BlockSpec((l, HEAD_DIM), lambda h, i: (0, 2 * HEADS + h)),
                  tile, tile],
        out_specs=[tile, pl.BlockSpec(memory_space=pl.ANY), pl.BlockSpec(memory_space=pl.ANY)],
        out_shape=[jax.ShapeDtypeStruct((l, HEADS * HEAD_DIM), BF16),
                   jax.ShapeDtypeStruct((HEADS, l, HEAD_DIM), F32), jax.ShapeDtypeStruct((HEADS, l, HEAD_DIM), F32)],
        scratch_shapes=[pltpu.VMEM((l, HEAD_DIM), F32), pltpu.VMEM((l, HEAD_DIM), F32),
                        pltpu.VMEM((tq, HEAD_DIM), F32), pltpu.VMEM((tq, BLOCK), F32), pltpu.VMEM((tq, BLOCK), F32),
                        pltpu.SemaphoreType.DMA((2,))],
        compiler_params=_params(("arbitrary", "arbitrary")),
    )(qkv, qkv, qkv, att, datt)


MIX_TM = 256


def _mix_branches(s_ref, p_ref, t_ref, g_ref, wa_ref, wb_ref, wc_ref, ba_ref, sc_ref, d):
    ya = jnp.dot(s_ref[...], wa_ref[...], preferred_element_type=F32) + ba_ref[...]
    yb0 = jnp.dot(p_ref[...], wb_ref[...], preferred_element_type=F32)
    yc = jnp.dot(t_ref[...], wc_ref[...], preferred_element_type=F32)
    g0 = _sigmoid(g_ref[:, 0:d].astype(F32))
    g1 = _sigmoid(g_ref[:, d:2 * d].astype(F32))
    g2 = _sigmoid(g_ref[:, 2 * d:3 * d].astype(F32))
    return ya, yb0, yc, g0, g1, g2


def _mix_specs(tm, d):
    row = lambda w: pl.BlockSpec((tm, w), lambda i: (i, 0))
    full = lambda r: pl.BlockSpec((r, d), lambda i: (0, 0))
    return [row(CONV_CH), row(POOL_CH), row(HEADS * HEAD_DIM), row(3 * d),
            full(CONV_CH), full(POOL_CH), full(HEADS * HEAD_DIM), full(1), full(1)]


def _mix_fwd(s, pooled, att, gates, wa, wb, wc, ba, scale, *, name):
    l, d = s.shape[0], wa.shape[1]
    tm = _tile(l, MIX_TM, 128)

    def body(s_ref, p_ref, t_ref, g_ref, wa_ref, wb_ref, wc_ref, ba_ref, sc_ref, o_ref):
        ya, yb0, yc, g0, g1, g2 = _mix_branches(s_ref, p_ref, t_ref, g_ref, wa_ref, wb_ref, wc_ref, ba_ref, sc_ref, d)
        o_ref[...] = (g0 * ya + g1 * (yb0 * sc_ref[...]) + g2 * yc).astype(BF16)

    return pl.pallas_call(
        body, name=name, grid=(l // tm,), in_specs=_mix_specs(tm, d),
        out_specs=pl.BlockSpec((tm, d), lambda i: (i, 0)),
        out_shape=jax.ShapeDtypeStruct((l, d), BF16),
        compiler_params=_params(("parallel",)),
    )(s, pooled, att, gates, wa, wb, wc, ba.reshape(1, d), scale.reshape(1, d))


def _mix_bwd(s, pooled, att, gates, wa, wb, wc, ba, scale, dmixed, *, name):
    l, d = s.shape[0], wa.shape[1]
    tm = _tile(l, MIX_TM, 128)

    def body(s_ref, p_ref, t_ref, g_ref, wa_ref, wb_ref, wc_ref, ba_ref, sc_ref, dm_ref,
             dg_ref, dya_ref, dyb_ref, dyc_ref, vec_ref):
        @pl.when(pl.program_id(0) == 0)
        def _():
            vec_ref[...] = jnp.zeros_like(vec_ref)

        ya, yb0, yc, g0, g1, g2 = _mix_branches(s_ref, p_ref, t_ref, g_ref, wa_ref, wb_ref, wc_ref, ba_ref, sc_ref, d)
        dm = dm_ref[...].astype(F32)
        sc = sc_ref[...]
        dg_ref[:, 0:d] = (dm * ya * g0 * (1.0 - g0)).astype(BF16)
        dg_ref[:, d:2 * d] = (dm * (yb0 * sc) * g1 * (1.0 - g1)).astype(BF16)
        dg_ref[:, 2 * d:3 * d] = (dm * yc * g2 * (1.0 - g2)).astype(BF16)
        dya = dm * g0
        dyb = dm * g1
        dya_ref[...] = dya.astype(BF16)
        dyb_ref[...] = (dyb * sc).astype(BF16)
        dyc_ref[...] = (dm * g2).astype(BF16)
        vec_ref[0:1, :] += jnp.sum(dya, axis=0, keepdims=True)
        vec_ref[1:2, :] += jnp.sum(dyb * yb0, axis=0, keepdims=True)

    row = lambda w: pl.BlockSpec((tm, w), lambda i: (i, 0))
    outs = pl.pallas_call(
        body, name=name, grid=(l // tm,), in_specs=_mix_specs(tm, d) + [row(d)],
        out_specs=[row(3 * d), row(d), row(d), row(d), pl.BlockSpec((8, d), lambda i: (0, 0))],
        out_shape=[jax.ShapeDtypeStruct((l, 3 * d), BF16), jax.ShapeDtypeStruct((l, d), BF16),
                   jax.ShapeDtypeStruct((l, d), BF16), jax.ShapeDtypeStruct((l, d), BF16),
                   jax.ShapeDtypeStruct((8, d), F32)],
        compiler_params=_params(("arbitrary",)),
    )(s, pooled, att, gates, wa, wb, wc, ba.reshape(1, d), scale.reshape(1, d), dmixed)
    return outs


FFN_TC = 512
_GELU_C = 0.7978845608028654
_GELU_A = 0.044715


def _gelu(x):
    th = jnp.tanh(_GELU_C * (x + _GELU_A * x * x * x))
    return 0.5 * x * (1.0 + th), th


def _gelu_grad(x, th):
    return 0.5 * (1.0 + th) + 0.5 * x * (1.0 - th * th) * _GELU_C * (1.0 + 3.0 * _GELU_A * x * x)


FFN_CH = 32


def _ffn_taps(win):
    return (pltpu.roll(win, 2, 0)[FFN_HALO:, :], pltpu.roll(win, 1, 0)[FFN_HALO:, :], win[FFN_HALO:, :])


def _ffn_conv(taps, w, b):
    return b + w[0:1, :] * taps[0] + w[1:2, :] * taps[1] + w[2:3, :] * taps[2]


def _fold8(x):
    acc = x[0:8, :]
    for r in range(8, x.shape[0], 8):
        acc = acc + x[r:r + 8, :]
    return acc


def _ffn_fwd(ug, uv, wg, wv, bg, bv, *, name):
    l, f = ug.shape
    tm = _conv_tile(l)
    tc = _tile(f, FFN_TC, 128)
    ext = FFN_HALO + tm

    def body(ug_ref, uv_ref, wg_ref, wv_ref, bg_ref, bv_ref, o_ref, gc_ref, vc_ref, bufg, bufv):
        i = pl.program_id(1)
        for buf, u_ref in ((bufg, ug_ref), (bufv, uv_ref)):
            @pl.when(i == 0)
            def _():
                buf[0:FFN_HALO, :] = jnp.zeros((FFN_HALO, tc), F32)

            @pl.when(i > 0)
            def _():
                buf[0:FFN_HALO, :] = buf[tm:ext, :]

            buf[FFN_HALO:ext, :] = u_ref[...].astype(F32)
        wg, wv, bg_, bv_ = wg_ref[...], wv_ref[...], bg_ref[...], bv_ref[...]

        def chunk(c, carry):
            r0 = pl.multiple_of(c * FFN_CH, FFN_CH)
            gc = _ffn_conv(_ffn_taps(bufg[pl.ds(r0, FFN_HALO + FFN_CH), :]), wg, bg_)
            vc = _ffn_conv(_ffn_taps(bufv[pl.ds(r0, FFN_HALO + FFN_CH), :]), wv, bv_)
            o_ref[pl.ds(r0, FFN_CH), :] = (_gelu(gc)[0] * vc).astype(BF16)
            gc_ref[pl.ds(r0, FFN_CH), :] = gc.astype(BF16)
            vc_ref[pl.ds(r0, FFN_CH), :] = vc.astype(BF16)
            return carry

        lax.fori_loop(0, tm // FFN_CH, chunk, 0)

    assert tm % FFN_CH == 0
    tile = pl.BlockSpec((tm, tc), lambda j, i: (i, j))
    wspec = pl.BlockSpec((FFN_K, tc), lambda j, i: (0, j))
    bspec = pl.BlockSpec((1, tc), lambda j, i: (0, j))
    return pl.pallas_call(
        body, name=name, grid=(f // tc, l // tm),
        in_specs=[tile, tile, wspec, wspec, bspec, bspec], out_specs=[tile, tile, tile],
        out_shape=[jax.ShapeDtypeStruct((l, f), BF16)] * 3,
        scratch_shapes=[pltpu.VMEM((ext, tc), F32), pltpu.VMEM((ext, tc), F32)],
        compiler_params=_params(("parallel", "arbitrary")),
    )(ug, uv, wg, wv, bg.reshape(1, f), bv.reshape(1, f))


def _ffn_bwd(ug, uv, gc, vc, wg, wv, dact, *, name):
    l, f = ug.shape
    tm = _conv_tile(l)
    tc = _tile(f, FFN_TC, 128)
    nt = l // tm
    ext = FFN_HALO + tm
    win_rows = FFN_CH + FFN_HALO

    def body(ug_ref, uv_ref, gc_ref, vc_ref, wg_ref, wv_ref, da_ref,
             dug_ref, duv_ref, gg_ref, gv_ref, dbufg, dbufv, gaccg, gaccv):
        i = pl.program_id(1)

        @pl.when(i == 0)
        def _():
            gg_ref[...] = jnp.zeros_like(gg_ref)
            gv_ref[...] = jnp.zeros_like(gv_ref)

        for dbuf, gacc in ((dbufg, gaccg), (dbufv, gaccv)):
            @pl.when(i == 0)
            def _():
                dbuf[tm:ext, :] = jnp.zeros((FFN_HALO, tc), F32)

            @pl.when(i > 0)
            def _():
                dbuf[tm:ext, :] = dbuf[0:FFN_HALO, :]

            gacc[...] = jnp.zeros_like(gacc)
        wg, wv = wg_ref[...], wv_ref[...]

        def chunk(cc, carry):
            r0 = pl.multiple_of((tm // FFN_CH - 1 - cc) * FFN_CH, FFN_CH)
            rows = pl.ds(r0, FFN_CH)
            gcv = gc_ref[rows, :].astype(F32)
            ge, th = _gelu(gcv)
            da = da_ref[rows, :].astype(F32)
            for dc, u_ref, w, dbuf, du_ref, gacc in (
                    (da * vc_ref[rows, :].astype(F32) * _gelu_grad(gcv, th), ug_ref, wg, dbufg, dug_ref, gaccg),
                    (da * ge, uv_ref, wv, dbufv, duv_ref, gaccv)):
                dbuf[rows, :] = dc
                dwin = dbuf[pl.ds(r0, win_rows), :]
                d1 = pltpu.roll(dwin, win_rows - 1, 0)[0:FFN_CH, :]
                d2 = pltpu.roll(dwin, win_rows - 2, 0)[0:FFN_CH, :]
                du_ref[rows, :] = (w[2:3, :] * dc + w[1:2, :] * d1 + w[0:1, :] * d2).astype(BF16)
                u = u_ref[rows, :].astype(F32)
                gacc[0:8, :] += _fold8(d2 * u)
                gacc[8:16, :] += _fold8(d1 * u)
                gacc[16:24, :] += _fold8(dc * u)
                gacc[24:32, :] += _fold8(dc)
            return carry

        lax.fori_loop(0, tm // FFN_CH, chunk, 0)
        for gacc, gp_ref in ((gaccg, gg_ref), (gaccv, gv_ref)):
            for k in range(FFN_K + 1):
                gp_ref[k:k + 1, :] += jnp.sum(gacc[8 * k:8 * k + 8, :], axis=0, keepdims=True)

    assert tm % FFN_CH == 0
    tile = pl.BlockSpec((tm, tc), lambda j, i: (nt - 1 - i, j))
    wspec = pl.BlockSpec((FFN_K, tc), lambda j, i: (0, j))
    gspec = pl.BlockSpec((8, tc), lambda j, i: (0, j))
    return pl.pallas_call(
        body, name=name, grid=(f // tc, nt),
        in_specs=[tile, tile, tile, tile, wspec, wspec, tile],
        out_specs=[tile, tile, gspec, gspec],
        out_shape=[jax.ShapeDtypeStruct((l, f), BF16), jax.ShapeDtypeStruct((l, f), BF16),
                   jax.ShapeDtypeStruct((8, f), F32), jax.ShapeDtypeStruct((8, f), F32)],
        scratch_shapes=[pltpu.VMEM((ext, tc), F32), pltpu.VMEM((ext, tc), F32),
                        pltpu.VMEM((32, tc), F32), pltpu.VMEM((32, tc), F32)],
        compiler_params=_params(("parallel", "arbitrary")),
    )(ug, uv, gc, vc, wg, wv, dact)


def _adamw(parts, w, m, v, *, name):
    r = w.shape[0]
    tr = _tile(r, PACK_ROWS, 16)
    c1 = 1.0 / (1.0 - ADAM_B1 ** ADAM_STEP)
    c2 = 1.0 / (1.0 - ADAM_B2 ** ADAM_STEP)

    def body(p_ref, w_ref, m_ref, v_ref, g_ref, d_ref, nm_ref, nv_ref):
        g = p_ref[0].astype(F32)
        for k in range(1, N_DEV):
            g = g + p_ref[k].astype(F32)
        nm = ADAM_B1 * m_ref[...] + (1.0 - ADAM_B1) * g
        nv = ADAM_B2 * v_ref[...] + (1.0 - ADAM_B2) * (g * g)
        g_ref[...] = g
        nm_ref[...] = nm
        nv_ref[...] = nv
        d_ref[...] = -ADAM_LR * ((nm * c1) / (jnp.sqrt(nv * c2) + ADAM_EPS) + ADAM_WD * w_ref[...])

    tile = pl.BlockSpec((tr, 128), lambda i: (i, 0))
    return pl.pallas_call(
        body, name=name, grid=(r // tr,),
        in_specs=[pl.BlockSpec((N_DEV, tr, 128), lambda i: (0, i, 0)), tile, tile, tile],
        out_specs=[tile, tile, tile, tile],
        out_shape=[jax.ShapeDtypeStruct((r, 128), F32)] * 4,
        compiler_params=_params(("parallel",)),
    )(parts, w, m, v)


def _place():
    return lax.axis_index("x"), lax.axis_index("y"), lax.axis_index("c")


def _all_gather(x, *, name):
    def body(x_ref, out_ref, send_sems, recv_sems, local_sem):
        xx, yy, cc = _place()
        me, sibling = (xx, yy, cc), (xx, yy, 1 - cc)
        chips = [(1 - xx, yy), (xx, 1 - yy), (1 - xx, 1 - yy)]

        def slot(px, py, pc):
            return out_ref.at[4 * px + 2 * py + pc]

        def copy(k, block, to, src=None):
            return pltpu.make_async_remote_copy(
                src_ref=slot(*block) if src is None else src, dst_ref=slot(*block),
                send_sem=send_sems.at[k], recv_sem=recv_sems.at[k], device_id=to, device_id_type=MESH)

        mine = pltpu.make_async_copy(x_ref, slot(*me), local_sem)
        mine.start()
        first = [copy(0, me, sibling, src=x_ref)]
        first += [copy(1 + j, me, (*chip, cc), src=x_ref) for j, chip in enumerate(chips)]
        for cp in first:
            cp.start()
        passed = [copy(4 + j, (*chip, cc), sibling) for j, chip in enumerate(chips)]
        for j, chip in enumerate(chips):
            copy(1 + j, (*chip, cc), me).wait_recv()
            passed[j].start()
        copy(0, sibling, me).wait_recv()
        for j, chip in enumerate(chips):
            copy(4 + j, (*chip, 1 - cc), me).wait_recv()
        for cp in first + passed:
            cp.wait_send()
        mine.wait()

    return pl.pallas_call(
        body, name=name,
        in_specs=[pl.BlockSpec(memory_space=pl.ANY)], out_specs=pl.BlockSpec(memory_space=pl.ANY),
        out_shape=jax.ShapeDtypeStruct((N_DEV,) + x.shape, x.dtype),
        scratch_shapes=[pltpu.SemaphoreType.DMA((7,)), pltpu.SemaphoreType.DMA((7,)), pltpu.SemaphoreType.DMA],
    )(x)


def _all_to_all(send, *, name):
    def body(s_ref, r_ref, send_sems, recv_sems, local_sem):
        xx, yy, cc = _place()
        me = 4 * xx + 2 * yy + cc
        local = pltpu.make_async_copy(s_ref.at[me], r_ref.at[me], local_sem)
        local.start()
        copies = []
        for m in range(1, N_DEV):
            px = 1 - xx if m & 4 else xx
            py = 1 - yy if m & 2 else yy
            pc = 1 - cc if m & 1 else cc
            copies.append(pltpu.make_async_remote_copy(
                src_ref=s_ref.at[4 * px + 2 * py + pc], dst_ref=r_ref.at[me],
                send_sem=send_sems.at[m - 1], recv_sem=recv_sems.at[m - 1],
                device_id=(px, py, pc), device_id_type=MESH))
        for cp in copies:
            cp.start()
        for cp in copies:
            cp.wait_recv()
        for cp in copies:
            cp.wait_send()
        local.wait()

    return pl.pallas_call(
        body, name=name,
        in_specs=[pl.BlockSpec(memory_space=pl.ANY)], out_specs=pl.BlockSpec(memory_space=pl.ANY),
        out_shape=jax.ShapeDtypeStruct(send.shape, send.dtype),
        scratch_shapes=[pltpu.SemaphoreType.DMA((7,)), pltpu.SemaphoreType.DMA((7,)), pltpu.SemaphoreType.DMA],
    )(send)


def _as_rows(a, lead, dtype):
    a = a.astype(dtype)
    size = 1
    for s in a.shape[len(lead):]:
        size *= s
    if size % PACK_ALIGN:
        a = jnp.pad(a.reshape(lead + (size,)), [(0, 0)] * len(lead) + [(0, (-size) % PACK_ALIGN)])
    return a.reshape(lead + (-1, 128))


def _pack(arrays, dtype):
    buf = jnp.concatenate([_as_rows(a, (), dtype) for a in arrays], axis=0)
    return jnp.pad(buf, ((0, (-buf.shape[0]) % PACK_ROWS), (0, 0)))


def _pack_pieces(arrays, dtype):
    buf = jnp.concatenate([_as_rows(a, (N_DEV,), dtype) for a in arrays], axis=1)
    return jnp.pad(buf, ((0, 0), (0, (-buf.shape[1]) % PACK_ROWS), (0, 0)))


def _unpack(buf, shapes, lead=()):
    out, row = [], 0
    for shp in shapes:
        size = 1
        for s in shp:
            size *= s
        rows = (size + (-size) % PACK_ALIGN) // 128
        part = buf[..., row:row + rows, :]
        if size % PACK_ALIGN:
            part = part.reshape(lead + (rows * 128,))[..., :size]
        out.append(part.reshape(lead + tuple(shp)))
        row += rows
    return out


def _unshard(g, axis):
    g = jnp.moveaxis(g, 0, axis)
    shp = list(g.shape)
    return g.reshape(shp[:axis] + [shp[axis] * shp[axis + 1]] + shp[axis + 2:])


def _pieces(full, axis):
    shp = list(full.shape)
    g = full.reshape(shp[:axis] + [N_DEV, shp[axis] // N_DEV] + shp[axis + 1:])
    return jnp.moveaxis(g, axis, 0)


SHARDED = (("meta", 1), ("w_in", 2), ("conv_dw_w", 2), ("w_conv_out", 2), ("w_pool_grp", 3), ("w_attn_out", 2),
           ("w_o", 1), ("w_up", 2), ("ffn_dw_w", 2), ("w_down", 1))
MATRICES = ("w_in", "w_conv_out", "w_pool_grp", "w_attn_out", "w_o", "w_up", "w_down")
REPLICATED = ("norm1", "conv_dw_b", "conv_ln_g", "conv_ln_b", "b_conv_out", "pool_scale", "norm2", "ffn_dw_b",
              "final_norm")
WEIGHTS = ("meta", "norm1", "w_in", "conv_dw_w", "conv_dw_b", "conv_ln_g", "conv_ln_b", "w_conv_out", "b_conv_out",
           "w_pool_grp", "pool_scale", "w_attn_out", "w_o", "norm2", "w_up", "ffn_dw_w", "ffn_dw_b", "w_down",
           "final_norm")


def _block_diag(w_grp):
    g, gc, od = w_grp.shape
    out = jnp.zeros((g * gc, g * od), w_grp.dtype)
    for i in range(g):
        out = out.at[i * gc:(i + 1) * gc, i * od:(i + 1) * od].set(w_grp[i])
    return out


def _block_diag_grad(gw, g):
    gc, od = gw.shape[0] // g, gw.shape[1] // g
    return jnp.stack([gw[i * gc:(i + 1) * gc, i * od:(i + 1) * od] for i in range(g)])


C_CONV = 2 * CONV_CH
C_POOL = C_CONV + POOL_CH
C_ATT = HEADS * HEAD_DIM
C_QKV = C_POOL + 3 * C_ATT


def _layer_fwd(h, p, tag):
    d = h.shape[1]
    w_in = p["w_in"]
    hn = _rms_fwd(h, p["norm1"], name=f"rms1_{tag}")
    pa = _mm(hn, w_in[:, :C_POOL], out_dtype=F32, name=f"proj_a_{tag}")
    q_scale = jnp.concatenate([jnp.full((C_ATT,), Q_SCALE, F32), jnp.ones((2 * C_ATT,), F32)])
    qkv = _mm(hn, w_in[:, C_POOL:C_QKV], out_dtype=BF16, col_scale=q_scale, name=f"proj_qkv_{tag}")
    gates = _mm(hn, w_in[:, C_QKV:], out_dtype=BF16, name=f"proj_g_{tag}")
    s = _conv_fwd(pa, p["conv_dw_w"], p["conv_dw_b"], p["conv_ln_g"], p["conv_ln_b"], name=f"conv_{tag}")
    pooled = _pool_fwd(pa, name=f"pool_{tag}")
    att, att32 = _attn_fwd(qkv, name=f"attn_{tag}")
    wb = _block_diag(p["w_pool_grp"])
    mixed = _mix_fwd(s, pooled, att, gates, p["w_conv_out"], wb, p["w_attn_out"], p["b_conv_out"], p["pool_scale"],
                     name=f"mix_{tag}")
    h1 = _mm(mixed, p["w_o"], out_dtype=F32, res=h, mask_rows=True, name=f"wo_{tag}")
    hn2 = _rms_fwd(h1, p["norm2"], name=f"rms2_{tag}")
    f = p["w_up"].shape[1] // 2
    ug = _mm(hn2, p["w_up"][:, :f], out_dtype=BF16, name=f"up_g_{tag}")
    uv = _mm(hn2, p["w_up"][:, f:], out_dtype=BF16, name=f"up_v_{tag}")
    act, gc, vc = _ffn_fwd(ug, uv, p["ffn_dw_w"][:, :f], p["ffn_dw_w"][:, f:], p["ffn_dw_b"][:f], p["ffn_dw_b"][f:],
                           name=f"ffn_{tag}")
    h2 = _mm(act, p["w_down"], out_dtype=F32, res=h1, mask_rows=True, name=f"down_{tag}")
    saved = dict(h=h, hn=hn, pa=pa, qkv=qkv, gates=gates, s=s, pooled=pooled, att=att, att32=att32, wb=wb, mixed=mixed,
                 h1=h1,
                 hn2=hn2, ug=ug, uv=uv, gc=gc, vc=vc, act=act)
    return h2, saved


def _layer_bwd(dh2, dh2b, p, sv, tag):
    g = {}
    f = p["w_up"].shape[1] // 2
    dact = _mm(dh2b, p["w_down"].T, out_dtype=BF16, name=f"b_down_{tag}")
    g["w_down"] = _mm_tn(sv["act"], dh2b, name=f"g_down_{tag}")
    dug, duv, gpg, gpv = _ffn_bwd(sv["ug"], sv["uv"], sv["gc"], sv["vc"], p["ffn_dw_w"][:, :f], p["ffn_dw_w"][:, f:],
                                  dact, name=f"b_ffn_{tag}")
    g["ffn_dw_w"] = jnp.concatenate([gpg[0:FFN_K], gpv[0:FFN_K]], axis=1)
    g["ffn_dw_b"] = jnp.concatenate([gpg[FFN_K], gpv[FFN_K]])
    w_up_t = p["w_up"].T
    dhn2 = _mm(dug, w_up_t[:f], out_dtype=F32, name=f"b_up_g_{tag}")
    dhn2 = _mm(duv, w_up_t[f:], out_dtype=F32, res=dhn2, name=f"b_up_v_{tag}")
    g["w_up"] = jnp.concatenate([_mm_tn(sv["hn2"], dug, name=f"g_up_g_{tag}"),
                                 _mm_tn(sv["hn2"], duv, name=f"g_up_v_{tag}")], axis=1)
    dh1, dh1b, g["norm2"] = _rms_bwd(sv["h1"], p["norm2"], dhn2, dh2, name=f"b_rms2_{tag}")
    dmixed = _mm(dh1b, p["w_o"].T, out_dtype=BF16, name=f"b_wo_{tag}")
    g["w_o"] = _mm_tn(sv["mixed"], dh1b, name=f"g_wo_{tag}")
    dgates, dya, dyb, dyc, vec = _mix_bwd(sv["s"], sv["pooled"], sv["att"], sv["gates"], p["w_conv_out"], sv["wb"],
                                          p["w_attn_out"], p["b_conv_out"], p["pool_scale"], dmixed,
                                          name=f"b_mix_{tag}")
    g["b_conv_out"], g["pool_scale"] = vec[0], vec[1]
    ds = _mm(dya, p["w_conv_out"].T, out_dtype=F32, name=f"b_conv_out_{tag}")
    dpooled = _mm(dyb, sv["wb"].T, out_dtype=F32, name=f"b_pool_out_{tag}")
    datt = _mm(dyc, p["w_attn_out"].T, out_dtype=BF16, name=f"b_attn_out_{tag}")
    g["w_conv_out"] = _mm_tn(sv["s"], dya, name=f"g_conv_out_{tag}")
    g["w_pool_grp"] = _block_diag_grad(_mm_tn(sv["pooled"], dyb, name=f"g_pool_{tag}"), len(POOL_WINDOWS))
    g["w_attn_out"] = _mm_tn(sv["att"], dyc, name=f"g_attn_out_{tag}")
    dc, gp = _conv_bwd_ln(sv["pa"], ds, p["conv_dw_w"], p["conv_dw_b"], p["conv_ln_g"], p["conv_ln_b"],
                          name=f"b_conv_ln_{tag}")
    g["conv_dw_w"], g["conv_dw_b"], g["conv_ln_g"], g["conv_ln_b"] = gp[0:CONV_K], gp[32], gp[33], gp[34]
    dconv = _conv_bwd_in(sv["pa"], dc, p["conv_dw_w"], name=f"b_conv_in_{tag}")
    dp = _pool_bwd(dpooled, name=f"b_pool_{tag}")
    dq, dk, dv = _attn_bwd(sv["qkv"], sv["att32"], datt, name=f"b_attn_{tag}")
    dk = jnp.moveaxis(dk, 0, 1).reshape(dq.shape).astype(BF16)
    dv = jnp.moveaxis(dv, 0, 1).reshape(dq.shape).astype(BF16)
    w_in_t = p["w_in"].T
    cols = [(jnp.concatenate([dconv, dp, dq, dk, dv], axis=1), 0, C_QKV), (dgates, C_QKV, w_in_t.shape[0])]
    dhn, gw = None, []
    for n, (dcol, lo, hi) in enumerate(cols):
        dhn = _mm(dcol, w_in_t[lo:hi], out_dtype=F32, res=dhn, name=f"b_in{n}_{tag}")
        gw.append(_mm_tn(sv["hn"], dcol, name=f"g_in{n}_{tag}"))
    g["w_in"] = jnp.concatenate(gw, axis=1)
    dh, dhb, g["norm1"] = _rms_bwd(sv["h"], p["norm1"], dhn, dh1, name=f"b_rms1_{tag}")
    return dh, dhb, g


def kernel(x, meta, norm1, w_in, conv_dw_w, conv_dw_b, conv_ln_g, conv_ln_b, w_conv_out, b_conv_out, w_pool_grp, pool_scale, w_attn_out, w_o, norm2, w_up, ffn_dw_w, ffn_dw_b, w_down, final_norm, loss_target, m_meta, m_norm1, m_w_in, m_conv_dw_w, m_conv_dw_b, m_conv_ln_g, m_conv_ln_b, m_w_conv_out, m_b_conv_out, m_w_pool_grp, m_pool_scale, m_w_attn_out, m_w_o, m_norm2, m_w_up, m_ffn_dw_w, m_ffn_dw_b, m_w_down, m_final_norm, v_meta, v_norm1, v_w_in, v_conv_dw_w, v_conv_dw_b, v_conv_ln_g, v_conv_ln_b, v_w_conv_out, v_b_conv_out, v_w_pool_grp, v_pool_scale, v_w_attn_out, v_w_o, v_norm2, v_w_up, v_ffn_dw_w, v_ffn_dw_b, v_w_down, v_final_norm):
    given = dict(meta=meta, norm1=norm1, w_in=w_in, conv_dw_w=conv_dw_w, conv_dw_b=conv_dw_b, conv_ln_g=conv_ln_g, conv_ln_b=conv_ln_b, w_conv_out=w_conv_out, b_conv_out=b_conv_out, w_pool_grp=w_pool_grp, pool_scale=pool_scale, w_attn_out=w_attn_out, w_o=w_o, norm2=norm2, w_up=w_up, ffn_dw_w=ffn_dw_w, ffn_dw_b=ffn_dw_b, w_down=w_down, final_norm=final_norm)
    mom_m = dict(meta=m_meta, norm1=m_norm1, w_in=m_w_in, conv_dw_w=m_conv_dw_w, conv_dw_b=m_conv_dw_b, conv_ln_g=m_conv_ln_g, conv_ln_b=m_conv_ln_b, w_conv_out=m_w_conv_out, b_conv_out=m_b_conv_out, w_pool_grp=m_w_pool_grp, pool_scale=m_pool_scale, w_attn_out=m_w_attn_out, w_o=m_w_o, norm2=m_norm2, w_up=m_w_up, ffn_dw_w=m_ffn_dw_w, ffn_dw_b=m_ffn_dw_b, w_down=m_w_down, final_norm=m_final_norm)
    mom_v = dict(meta=v_meta, norm1=v_norm1, w_in=v_w_in, conv_dw_w=v_conv_dw_w, conv_dw_b=v_conv_dw_b, conv_ln_g=v_conv_ln_g, conv_ln_b=v_conv_ln_b, w_conv_out=v_w_conv_out, b_conv_out=v_b_conv_out, w_pool_grp=v_w_pool_grp, pool_scale=v_pool_scale, w_attn_out=v_w_attn_out, w_o=v_w_o, norm2=v_norm2, w_up=v_w_up, ffn_dw_w=v_ffn_dw_w, ffn_dw_b=v_ffn_dw_b, w_down=v_w_down, final_norm=v_final_norm)
    sharded_axis = dict(SHARDED)
    vectors = [n for n, _ in SHARDED if n not in MATRICES]
    depth = norm1.shape[0]

    got_mat = _all_gather(_pack([given[n] for n in MATRICES], BF16), name="gather_matrices")
    got_vec = _all_gather(_pack([given[n] for n in vectors], F32), name="gather_vectors")
    full = {n: given[n] for n in REPLICATED}
    for n, a in zip(MATRICES, _unpack(got_mat, [given[n].shape for n in MATRICES], (N_DEV,))):
        full[n] = _unshard(a, sharded_axis[n])
    for n, a in zip(vectors, _unpack(got_vec, [given[n].shape for n in vectors], (N_DEV,))):
        full[n] = _unshard(a, sharded_axis[n])

    xs = x[0]
    d = xs.shape[1]
    h = jnp.concatenate([jnp.zeros((PAD, d), F32), full["meta"], xs], axis=0)
    layers, saved = [], []
    for i in range(depth):
        p = {n: full[n][i] for n in full if n not in ("meta", "final_norm")}
        layers.append(p)
        h, sv = _layer_fwd(h, p, f"l{i}")
        saved.append(sv)
    loss_part, dh, dhb, g_final = _loss_head(h, full["final_norm"], loss_target[0], name="loss_head")

    grads = [None] * depth
    for i in reversed(range(depth)):
        dh, dhb, grads[i] = _layer_bwd(dh, dhb, layers[i], saved[i], f"l{i}")
    full_grad = {n: jnp.stack([grads[i][n] for i in range(depth)]) for n in grads[0]}
    full_grad["meta"] = dh[PAD:FRONT]
    full_grad["final_norm"] = g_final
    grad_x = dh[FRONT:][None]

    names = [n for n, _ in SHARDED]
    recv = _all_to_all(_pack_pieces([_pieces(full_grad[n], sharded_axis[n]) for n in names], BF16),
                       name="scatter_grads")
    rep_shapes = [given[n].shape for n in REPLICATED] + [(1,)]
    rep_parts = _all_gather(_pack([full_grad[n] for n in REPLICATED] + [loss_part.reshape(1)], F32),
                            name="gather_partials")

    out = {}
    shapes = [given[n].shape for n in names]
    res = _adamw(recv, _pack([given[n] for n in names], F32), _pack([mom_m[n] for n in names], F32),
                 _pack([mom_v[n] for n in names], F32), name="adamw_sharded")
    for kind, buf in zip(("grad", "delta", "new_m", "new_v"), res):
        for n, a in zip(names, _unpack(buf, shapes)):
            out[kind, n] = a
    rep_w = [given[n] for n in REPLICATED] + [jnp.zeros((1,), F32)]
    rep_m = [mom_m[n] for n in REPLICATED] + [jnp.zeros((1,), F32)]
    rep_v = [mom_v[n] for n in REPLICATED] + [jnp.ones((1,), F32)]
    res = _adamw(rep_parts, _pack(rep_w, F32), _pack(rep_m, F32), _pack(rep_v, F32), name="adamw_replicated")
    for kind, buf in zip(("grad", "delta", "new_m", "new_v"), res):
        for n, a in zip(list(REPLICATED) + ["loss"], _unpack(buf, rep_shapes)):
            out[kind, n] = a
    loss = out["grad", "loss"][0]
    return (loss, grad_x, *[out["grad", n] for n in WEIGHTS], *[out["delta", n] for n in WEIGHTS],
            *[out["new_m", n] for n in WEIGHTS], *[out["new_v", n] for n in WEIGHTS])
```

```python
import functools

import jax
import jax.numpy as jnp
from jax import lax
from jax.experimental import pallas as pl
from jax.experimental.pallas import tpu as pltpu

F32 = jnp.float32
BF16 = jnp.bfloat16
MESH = pl.DeviceIdType.MESH

N_DEV = 8
N_META = 16
BLOCK = 128
PAD = 240
FRONT = PAD + N_META
HEADS = 4
HEAD_DIM = 128
CONV_CH = 256
CONV_K = 31
POOL_CH = 256
POOL_WINDOWS = (2, 4, 8, 16)
FFN_K = 3
EPS = 1e-6
ADAM_LR, ADAM_B1, ADAM_B2, ADAM_EPS, ADAM_WD, ADAM_STEP = 0.001, 0.9, 0.999, 1e-08, 0.01, 10

VMEM_LIMIT = 56 * 1024 * 1024
CONV_HALO = 32
POOL_HALO = 16
FFN_HALO = 8
PACK_ALIGN = 16 * 128
PACK_ROWS = 512


def _tile(n, cap, unit):
    if n <= cap:
        return n
    best = None
    t = unit
    while t <= cap:
        if n % t == 0:
            best = t
        t += unit
    assert best is not None, (n, cap, unit)
    return best


def _params(sem):
    return pltpu.CompilerParams(dimension_semantics=sem, vmem_limit_bytes=VMEM_LIMIT)


def _sigmoid(x):
    return 1.0 / (1.0 + jnp.exp(-x))


MM_MAX_K = 3072
MM_RESIDENT_B = 3072 * 1024 * 2


def _mm(a, b, *, out_dtype, name, res=None, col_scale=None, mask_rows=False, tn_cap=768):
    m, k = a.shape
    k2, n = b.shape
    assert k == k2 and k <= MM_MAX_K
    tn = n if k * n * 2 <= MM_RESIDENT_B else _tile(n, tn_cap, 128)
    tm = _tile(m, 1280 if (k <= 1024 and tn <= 1024) else 640, 128)

    def body(*refs):
        refs = list(refs)
        a_ref, b_ref = refs[:2]
        o_ref = refs[-1]
        r_ref = refs[2] if res is not None else None
        c_ref = refs[-2] if col_scale is not None else None
        y = jnp.dot(a_ref[...].astype(BF16), b_ref[...].astype(BF16), preferred_element_type=F32)
        if col_scale is not None:
            y = y * c_ref[...]
        if res is not None:
            y = y + r_ref[...].astype(F32)
        if mask_rows:
            row = pl.program_id(0) * tm + lax.broadcasted_iota(jnp.int32, (tm, 1), 0)
            y = jnp.where(row >= PAD, y, 0.0)
        o_ref[...] = y.astype(out_dtype)

    in_specs = [pl.BlockSpec((tm, k), lambda i, j: (i, 0)), pl.BlockSpec((k, tn), lambda i, j: (0, j))]
    args = [a, b]
    if res is not None:
        in_specs.append(pl.BlockSpec((tm, tn), lambda i, j: (i, j)))
        args.append(res)
    if col_scale is not None:
        in_specs.append(pl.BlockSpec((1, tn), lambda i, j: (0, j)))
        args.append(col_scale.reshape(1, n))
    return pl.pallas_call(
        body, name=name, grid=(m // tm, n // tn),
        in_specs=in_specs, out_specs=pl.BlockSpec((tm, tn), lambda i, j: (i, j)),
        out_shape=jax.ShapeDtypeStruct((m, n), out_dtype),
        compiler_params=_params(("parallel", "parallel")),
    )(*args)


def _mm_tn(a, b, *, name, t1_cap=512, tn_cap=1024, tl_cap=3328):
    l, k1 = a.shape
    l2, n = b.shape
    assert l == l2
    t1, tn, tl = _tile(k1, t1_cap, 128), _tile(n, tn_cap, 128), _tile(l, tl_cap, 128)

    def body(a_ref, b_ref, o_ref):
        @pl.when(pl.program_id(2) == 0)
        def _():
            o_ref[...] = jnp.zeros_like(o_ref)

        o_ref[...] += lax.dot_general(a_ref[...].astype(BF16), b_ref[...].astype(BF16),
                                      (((0,), (0,)), ((), ())), preferred_element_type=F32)

    return pl.pallas_call(
        body, name=name, grid=(k1 // t1, n // tn, l // tl),
        in_specs=[pl.BlockSpec((tl, t1), lambda i, j, ll: (ll, i)), pl.BlockSpec((tl, tn), lambda i, j, ll: (ll, j))],
        out_specs=pl.BlockSpec((t1, tn), lambda i, j, ll: (i, j)),
        out_shape=jax.ShapeDtypeStruct((k1, n), F32),
        compiler_params=_params(("parallel", "parallel", "arbitrary")),
    )(a, b)


def _rms_fwd(x, g, *, name):
    l, d = x.shape
    tm = _tile(l, 640, 128)

    def body(x_ref, g_ref, o_ref):
        xv = x_ref[...]
        r = lax.rsqrt(jnp.mean(xv * xv, axis=-1, keepdims=True) + EPS)
        o_ref[...] = (xv * r * g_ref[...]).astype(BF16)

    return pl.pallas_call(
        body, name=name, grid=(l // tm,),
        in_specs=[pl.BlockSpec((tm, d), lambda i: (i, 0)), pl.BlockSpec((1, d), lambda i: (0, 0))],
        out_specs=pl.BlockSpec((tm, d), lambda i: (i, 0)),
        out_shape=jax.ShapeDtypeStruct((l, d), BF16),
        compiler_params=_params(("parallel",)),
    )(x, g.reshape(1, d))


def _rms_bwd(x, g, dy, dres, *, name):
    l, d = x.shape
    tm = _tile(l, 640, 128)

    def body(x_ref, g_ref, dy_ref, dr_ref, dx_ref, dxb_ref, dg_ref):
        i = pl.program_id(0)

        @pl.when(i == 0)
        def _():
            dg_ref[...] = jnp.zeros_like(dg_ref)

        xv = x_ref[...]
        r = lax.rsqrt(jnp.mean(xv * xv, axis=-1, keepdims=True) + EPS)
        xh = xv * r
        dyv = dy_ref[...].astype(F32)
        dxh = dyv * g_ref[...]
        dx = r * (dxh - xh * jnp.mean(dxh * xh, axis=-1, keepdims=True)) + dr_ref[...]
        row = i * tm + lax.broadcasted_iota(jnp.int32, (tm, 1), 0)
        dx = jnp.where(row >= PAD, dx, 0.0)
        dx_ref[...] = dx
        dxb_ref[...] = dx.astype(BF16)
        dg_ref[0:1, :] += jnp.sum(dyv * xh, axis=0, keepdims=True)

    tile = pl.BlockSpec((tm, d), lambda i: (i, 0))
    dx, dxb, dg = pl.pallas_call(
        body, name=name, grid=(l // tm,),
        in_specs=[tile, pl.BlockSpec((1, d), lambda i: (0, 0)), tile, tile],
        out_specs=[tile, tile, pl.BlockSpec((8, d), lambda i: (0, 0))],
        out_shape=[jax.ShapeDtypeStruct((l, d), F32), jax.ShapeDtypeStruct((l, d), BF16),
                   jax.ShapeDtypeStruct((8, d), F32)],
        compiler_params=_params(("arbitrary",)),
    )(x, g.reshape(1, d), dy, dres)
    return dx, dxb, dg[0]


def _loss_head(h, g, target, *, name):
    l, d = h.shape
    tm = FRONT
    assert l % tm == 0 and target.shape[0] == l - tm

    def body(h_ref, g_ref, t_ref, dh_ref, dhb_ref, loss_ref, dg_ref):
        i = pl.program_id(0)

        @pl.when(i == 0)
        def _():
            loss_ref[...] = jnp.zeros_like(loss_ref)
            dg_ref[...] = jnp.zeros_like(dg_ref)
            dh_ref[...] = jnp.zeros_like(dh_ref)
            dhb_ref[...] = jnp.zeros_like(dhb_ref)

        @pl.when(i > 0)
        def _():
            xv = h_ref[...]
            r = lax.rsqrt(jnp.mean(xv * xv, axis=-1, keepdims=True) + EPS)
            xh = xv * r
            gv = g_ref[...]
            err = xh * gv - t_ref[...]
            loss_ref[...] += 0.5 * jnp.sum(jnp.mean(err * err, axis=-1, keepdims=True))
            dy = err * (1.0 / d)
            dxh = dy * gv
            dh = r * (dxh - xh * jnp.mean(dxh * xh, axis=-1, keepdims=True))
            dh_ref[...] = dh
            dhb_ref[...] = dh.astype(BF16)
            dg_ref[0:1, :] += jnp.sum(dy * xh, axis=0, keepdims=True)

    tile = pl.BlockSpec((tm, d), lambda i: (i, 0))
    dh, dhb, loss, dg = pl.pallas_call(
        body, name=name, grid=(l // tm,),
        in_specs=[tile, pl.BlockSpec((1, d), lambda i: (0, 0)),
                  pl.BlockSpec((tm, d), lambda i: (jnp.maximum(i - 1, 0), 0))],
        out_specs=[tile, tile, pl.BlockSpec((8, 128), lambda i: (0, 0)), pl.BlockSpec((8, d), lambda i: (0, 0))],
        out_shape=[jax.ShapeDtypeStruct((l, d), F32), jax.ShapeDtypeStruct((l, d), BF16),
                   jax.ShapeDtypeStruct((8, 128), F32), jax.ShapeDtypeStruct((8, d), F32)],
        compiler_params=_params(("arbitrary",)),
    )(h, g.reshape(1, d), target)
    return loss[0, 0], dh, dhb, dg[0]


def _conv_tile(l):
    return _tile(l, 640, 128)


def _conv_core(a, gt, buf, dw_w, dw_b, first):
    tm = a.shape[0]

    @pl.when(first)
    def _():
        buf[0:CONV_HALO, :] = jnp.zeros((CONV_HALO, CONV_CH), F32)

    @pl.when(jnp.logical_not(first))
    def _():
        buf[0:CONV_HALO, :] = buf[tm:tm + CONV_HALO, :]

    sg = _sigmoid(gt)
    buf[CONV_HALO:CONV_HALO + tm, :] = a * sg
    c = jnp.zeros((tm, CONV_CH), F32) + dw_b
    for k in range(CONV_K):
        off = CONV_HALO - (CONV_K - 1) + k
        c = c + dw_w[k:k + 1, :] * buf[off:off + tm, :]
    return c, sg


def _layer_norm(c, ln_g, ln_b):
    mu = jnp.mean(c, axis=-1, keepdims=True)
    xc = c - mu
    r = lax.rsqrt(jnp.mean(xc * xc, axis=-1, keepdims=True) + EPS)
    xh = xc * r
    return xh, r, xh * ln_g + ln_b


def _conv_fwd(pa, dw_w, dw_b, ln_g, ln_b, *, name):
    l = pa.shape[0]
    tm = _conv_tile(l)

    def body(a_ref, gt_ref, w_ref, b_ref, g_ref, bb_ref, o_ref, buf):
        c, _ = _conv_core(a_ref[...], gt_ref[...], buf, w_ref[...], b_ref[...], pl.program_id(0) == 0)
        _, _, y = _layer_norm(c, g_ref[...], bb_ref[...])
        o_ref[...] = (y * _sigmoid(y)).astype(BF16)

    vec = pl.BlockSpec((1, CONV_CH), lambda i: (0, 0))
    return pl.pallas_call(
        body, name=name, grid=(l // tm,),
        in_specs=[pl.BlockSpec((tm, CONV_CH), lambda i: (i, 0)), pl.BlockSpec((tm, CONV_CH), lambda i: (i, 1)),
                  pl.BlockSpec((CONV_K, CONV_CH), lambda i: (0, 0)), vec, vec, vec],
        out_specs=pl.BlockSpec((tm, CONV_CH), lambda i: (i, 0)),
        out_shape=jax.ShapeDtypeStruct((l, CONV_CH), BF16),
        scratch_shapes=[pltpu.VMEM((CONV_HALO + tm, CONV_CH), F32)],
        compiler_params=_params(("arbitrary",)),
    )(pa, pa, dw_w, dw_b.reshape(1, -1), ln_g.reshape(1, -1), ln_b.reshape(1, -1))


def _conv_bwd_ln(pa, ds, dw_w, dw_b, ln_g, ln_b, *, name):
    l = pa.shape[0]
    tm = _conv_tile(l)

    def body(a_ref, gt_ref, ds_ref, w_ref, b_ref, g_ref, bb_ref, dc_ref, gp_ref, buf):
        i = pl.program_id(0)

        @pl.when(i == 0)
        def _():
            gp_ref[...] = jnp.zeros_like(gp_ref)

        c, _ = _conv_core(a_ref[...], gt_ref[...], buf, w_ref[...], b_ref[...], i == 0)
        xh, r, y = _layer_norm(c, g_ref[...], bb_ref[...])
        sy = _sigmoid(y)
        dy = ds_ref[...] * (sy * (1.0 + y * (1.0 - sy)))
        dxh = dy * g_ref[...]
        dc = r * (dxh - jnp.mean(dxh, axis=-1, keepdims=True) - xh * jnp.mean(dxh * xh, axis=-1, keepdims=True))
        dc_ref[...] = dc
        for k in range(CONV_K):
            off = CONV_HALO - (CONV_K - 1) + k
            gp_ref[k:k + 1, :] += jnp.sum(dc * buf[off:off + tm, :], axis=0, keepdims=True)
        gp_ref[32:33, :] += jnp.sum(dc, axis=0, keepdims=True)
        gp_ref[33:34, :] += jnp.sum(dy * xh, axis=0, keepdims=True)
        gp_ref[34:35, :] += jnp.sum(dy, axis=0, keepdims=True)

    vec = pl.BlockSpec((1, CONV_CH), lambda i: (0, 0))
    return pl.pallas_call(
        body, name=name, grid=(l // tm,),
        in_specs=[pl.BlockSpec((tm, CONV_CH), lambda i: (i, 0)), pl.BlockSpec((tm, CONV_CH), lambda i: (i, 1)),
                  pl.BlockSpec((tm, CONV_CH), lambda i: (i, 0)),
                  pl.BlockSpec((CONV_K, CONV_CH), lambda i: (0, 0)), vec, vec, vec],
        out_specs=[pl.BlockSpec((tm, CONV_CH), lambda i: (i, 0)), pl.BlockSpec((40, CONV_CH), lambda i: (0, 0))],
        out_shape=[jax.ShapeDtypeStruct((l, CONV_CH), F32), jax.ShapeDtypeStruct((40, CONV_CH), F32)],
        scratch_shapes=[pltpu.VMEM((CONV_HALO + tm, CONV_CH), F32)],
        compiler_params=_params(("arbitrary",)),
    )(pa, pa, ds, dw_w, dw_b.reshape(1, -1), ln_g.reshape(1, -1), ln_b.reshape(1, -1))


def _conv_bwd_in(pa, dc, dw_w, *, name):
    l = pa.shape[0]
    tm = _conv_tile(l)
    nt = l // tm

    def body(a_ref, gt_ref, dc_ref, w_ref, o_ref, buf):
        first = pl.program_id(0) == 0

        @pl.when(first)
        def _():
            buf[tm:tm + CONV_HALO, :] = jnp.zeros((CONV_HALO, CONV_CH), F32)

        @pl.when(jnp.logical_not(first))
        def _():
            buf[tm:tm + CONV_HALO, :] = buf[0:CONV_HALO, :]

        buf[0:tm, :] = dc_ref[...]
        w = w_ref[...]
        dhc = jnp.zeros((tm, CONV_CH), F32)
        for k in range(CONV_K):
            off = CONV_K - 1 - k
            dhc = dhc + w[k:k + 1, :] * buf[off:off + tm, :]
        a = a_ref[...]
        sg = _sigmoid(gt_ref[...])
        o_ref[:, 0:CONV_CH] = (dhc * sg).astype(BF16)
        o_ref[:, CONV_CH:2 * CONV_CH] = (dhc * a * sg * (1.0 - sg)).astype(BF16)

    return pl.pallas_call(
        body, name=name, grid=(nt,),
        in_specs=[pl.BlockSpec((tm, CONV_CH), lambda i: (nt - 1 - i, 0)),
                  pl.BlockSpec((tm, CONV_CH), lambda i: (nt - 1 - i, 1)),
                  pl.BlockSpec((tm, CONV_CH), lambda i: (nt - 1 - i, 0)),
                  pl.BlockSpec((CONV_K, CONV_CH), lambda i: (0, 0))],
        out_specs=pl.BlockSpec((tm, 2 * CONV_CH), lambda i: (nt - 1 - i, 0)),
        out_shape=jax.ShapeDtypeStruct((l, 2 * CONV_CH), BF16),
        scratch_shapes=[pltpu.VMEM((tm + CONV_HALO, CONV_CH), F32)],
        compiler_params=_params(("arbitrary",)),
    )(pa, pa, dc, dw_w)


def _pool_consts(tm, row0):
    lane = lax.broadcasted_iota(jnp.int32, (1, POOL_CH), 1)
    grp = lane // (POOL_CH // len(POOL_WINDOWS))
    win = jnp.where(grp == 0, 2.0, jnp.where(grp == 1, 4.0, jnp.where(grp == 2, 8.0, 16.0))).astype(F32)
    pos = (row0 + lax.broadcasted_iota(jnp.int32, (tm, 1), 0) - PAD).astype(F32)
    cnt = jnp.maximum(jnp.minimum(pos + 1.0, win), 1.0)
    return grp, cnt


def _pool_select(grp, s2, s4, s8, s16):
    return jnp.where(grp == 0, s2, jnp.where(grp == 1, s4, jnp.where(grp == 2, s8, s16)))


def _pool_fwd(pa, *, name):
    l = pa.shape[0]
    tm = _conv_tile(l)
    ext = POOL_HALO + tm

    def body(p_ref, o_ref, buf):
        i = pl.program_id(0)

        @pl.when(i == 0)
        def _():
            buf[0:POOL_HALO, :] = jnp.zeros((POOL_HALO, POOL_CH), F32)

        @pl.when(i > 0)
        def _():
            buf[0:POOL_HALO, :] = buf[tm:tm + POOL_HALO, :]

        p = p_ref[...]
        buf[POOL_HALO:ext, :] = p
        x = buf[...]
        s2 = x + pltpu.roll(x, 1, 0)
        s4 = s2 + pltpu.roll(s2, 2, 0)
        s8 = s4 + pltpu.roll(s4, 4, 0)
        s16 = s8 + pltpu.roll(s8, 8, 0)
        grp, cnt = _pool_consts(tm, i * tm)
        s = _pool_select(grp, s2, s4, s8, s16)[POOL_HALO:ext, :]
        o_ref[...] = (s / cnt - p).astype(BF16)

    return pl.pallas_call(
        body, name=name, grid=(l // tm,),
        in_specs=[pl.BlockSpec((tm, POOL_CH), lambda i: (i, 2))],
        out_specs=pl.BlockSpec((tm, POOL_CH), lambda i: (i, 0)),
        out_shape=jax.ShapeDtypeStruct((l, POOL_CH), BF16),
        scratch_shapes=[pltpu.VMEM((ext, POOL_CH), F32)],
        compiler_params=_params(("arbitrary",)),
    )(pa)


def _pool_bwd(dpooled, *, name):
    l = dpooled.shape[0]
    tm = _conv_tile(l)
    nt = l // tm
    ext = tm + POOL_HALO

    def body(d_ref, o_ref, buf):
        i = pl.program_id(0)

        @pl.when(i == 0)
        def _():
            buf[tm:ext, :] = jnp.zeros((POOL_HALO, POOL_CH), F32)

        @pl.when(i > 0)
        def _():
            buf[tm:ext, :] = buf[0:POOL_HALO, :]

        d = d_ref[...]
        grp, cnt = _pool_consts(tm, (nt - 1 - i) * tm)
        buf[0:tm, :] = d / cnt
        x = buf[...]
        s2 = x + pltpu.roll(x, ext - 1, 0)
        s4 = s2 + pltpu.roll(s2, ext - 2, 0)
        s8 = s4 + pltpu.roll(s4, ext - 4, 0)
        s16 = s8 + pltpu.roll(s8, ext - 8, 0)
        s = _pool_select(grp, s2, s4, s8, s16)[0:tm, :]
        o_ref[...] = (s - d).astype(BF16)

    return pl.pallas_call(
        body, name=name, grid=(nt,),
        in_specs=[pl.BlockSpec((tm, POOL_CH), lambda i: (nt - 1 - i, 0))],
        out_specs=pl.BlockSpec((tm, POOL_CH), lambda i: (nt - 1 - i, 0)),
        out_shape=jax.ShapeDtypeStruct((l, POOL_CH), BF16),
        scratch_shapes=[pltpu.VMEM((ext, POOL_CH), F32)],
        compiler_params=_params(("arbitrary",)),
    )(dpooled)


ATT_TQ = 256
ATT_TK = 2 * BLOCK
ATT_SUB = ATT_TK // BLOCK
LOG2E = 1.4426950408889634
LN2 = 0.6931471805599453
Q_SCALE = HEAD_DIM ** -0.5 * LOG2E
ATT_CUT = 160.0


def _tri_ones():
    r = lax.broadcasted_iota(jnp.int32, (2 * BLOCK, 2 * BLOCK), 0) % BLOCK
    c = lax.broadcasted_iota(jnp.int32, (2 * BLOCK, 2 * BLOCK), 1)
    return jnp.where((c >= BLOCK) | (r > c), 1.0, 0.0).astype(BF16)


def _split_dot(x, rhs):
    hi = x.astype(BF16)
    lo = (x - hi.astype(F32)).astype(BF16)
    return jnp.dot(jnp.concatenate([hi, lo], axis=1), rhs, preferred_element_type=F32)


def _scores(q, kt, qpos, base, masked):
    z = lax.dot_general(q, kt, (((1,), (1,)), ((), ())), preferred_element_type=F32)
    sp = jnp.log2(1.0 + jnp.exp2(-jnp.abs(z)))
    lb = jnp.minimum(z, 0.0) - sp
    lk = lb - z
    valid = None
    if masked:
        kpos = base + lax.broadcasted_iota(jnp.int32, (1, z.shape[1]), 1)
        valid = (kpos < qpos) & (kpos >= PAD)
        lk = jnp.where(valid, lk, 0.0)
    return lk, lb, valid


def _suffix(x, tri, carry):
    wts = [_split_dot(x[:, b * BLOCK:(b + 1) * BLOCK], tri) for b in range(ATT_SUB)]
    offs = [None] * ATT_SUB
    s = carry
    for b in reversed(range(ATT_SUB)):
        offs[b] = wts[b][:, :BLOCK] + s
        s = s + wts[b][:, BLOCK:]
    return jnp.concatenate(offs, axis=1), s


def _walk_tiles(i, tq, step):
    t_top = ((i + 1) * tq - 1) // ATT_TK
    t_diag = (i * tq) // ATT_TK
    n_plain = jnp.maximum(t_diag - 1, 0)

    def masked(jj, top):
        return step(t_top - jj, True)

    def live(carry):
        return (carry[0] < n_plain) & (carry[1] > -ATT_CUT)

    def plain(carry):
        return carry[0] + 1, step(t_diag - 1 - carry[0], False)

    top = lax.fori_loop(0, t_top - t_diag + 1, masked, jnp.float32(0.0))
    _, top = lax.while_loop(live, plain, (jnp.int32(0), top))

    @pl.when((t_diag > 0) & (top > -ATT_CUT))
    def _():
        step(0, True)


def _tile_base(t):
    base = t * ATT_TK
    return base if isinstance(base, int) else pl.multiple_of(base, BLOCK)


def _attn_fwd(qkv, *, name, gather=None):
    l = qkv.shape[0]
    tq = ATT_TQ
    nq = l // tq
    assert l % tq == 0 and l % ATT_TK == 0

    def body(*refs):
        if gather is None:
            q_ref, k_ref, v_ref, o_ref, o32_ref, acc_ref, r_ref = refs
        else:
            q_ref, k_ref, v_ref, x_hbm, o_ref, o32_ref, got_hbm, acc_ref, r_ref = refs[:9]
            exchange = _Gather(x_hbm, got_hbm, *refs[9:])
            first = (pl.program_id(0) == 0) & (pl.program_id(1) == 0)
            last = (pl.program_id(0) == HEADS - 1) & (pl.program_id(1) == nq - 1)
            pl.when(first)(exchange.start)
        i = pl.program_id(1)
        acc_ref[...] = jnp.zeros_like(acc_ref)
        r_ref[...] = jnp.zeros_like(r_ref)
        q = q_ref[...]
        qpos = i * tq + lax.broadcasted_iota(jnp.int32, (tq, 1), 0)
        tri = _tri_ones()

        def step(t, masked):
            base = _tile_base(t)
            lk, lb, valid = _scores(q, k_ref[pl.ds(base, ATT_TK), :], qpos, base, masked)
            off, r_new = _suffix(lk, tri, r_ref[...])
            a = jnp.exp2(lb + off)
            if masked:
                a = jnp.where(valid, a, 0.0)
            acc_ref[...] += jnp.dot(a.astype(BF16), v_ref[pl.ds(base, ATT_TK), :], preferred_element_type=F32)
            r_ref[...] = r_new
            return jnp.max(r_new)

        _walk_tiles(i, tq, step)
        o_ref[...] = acc_ref[...].astype(BF16)
        o32_ref[...] = acc_ref[...]
        if gather is not None:
            pl.when(last)(exchange.finish)

    tile = pl.BlockSpec((tq, HEAD_DIM), lambda h, i: (i, h))
    hbm = pl.BlockSpec(memory_space=pl.ANY)
    hosted = gather is not None
    return pl.pallas_call(
        body, name=name, grid=(HEADS, nq),
        in_specs=[tile,
                  pl.BlockSpec((l, HEAD_DIM), lambda h, i: (0, HEADS + h)),
                  pl.BlockSpec((l, HEAD_DIM), lambda h, i: (0, 2 * HEADS + h))] + [hbm] * hosted,
        out_specs=[tile, tile] + [hbm] * hosted,
        out_shape=[jax.ShapeDtypeStruct((l, HEADS * HEAD_DIM), BF16),
                   jax.ShapeDtypeStruct((l, HEADS * HEAD_DIM), F32)]
        + ([jax.ShapeDtypeStruct((N_DEV,) + gather.shape, gather.dtype)] if hosted else []),
        scratch_shapes=[pltpu.VMEM((tq, HEAD_DIM), F32), pltpu.VMEM((tq, BLOCK), F32)] + EXCHANGE_SEMS * hosted,
        compiler_params=_params(("arbitrary", "arbitrary")),
    )(qkv, qkv, qkv, *([gather] * hosted))


def _attn_bwd(qkv, att, datt, *, name, scatter=None):
    l = qkv.shape[0]
    tq = ATT_TQ
    nq = l // tq
    assert l % tq == 0 and l % ATT_TK == 0

    def body(*refs):
        if scatter is None:
            q_ref, k_ref, v_ref, o_ref, do_ref, dq_ref, dk_hbm, dv_hbm, dk_acc, dv_acc, dq_acc, r_ref, s_ref, sem = refs
        else:
            (q_ref, k_ref, v_ref, o_ref, do_ref, send_hbm, dq_ref, dk_hbm, dv_hbm, recv_hbm,
             dk_acc, dv_acc, dq_acc, r_ref, s_ref, sem) = refs[:16]
            exchange = _Scatter(send_hbm, recv_hbm, *refs[16:])
            pl.when((pl.program_id(0) == 0) & (pl.program_id(1) == 0))(exchange.start)
        h = pl.program_id(0)
        i = pl.program_id(1)

        @pl.when(i == 0)
        def _():
            dk_acc[...] = jnp.zeros_like(dk_acc)
            dv_acc[...] = jnp.zeros_like(dv_acc)

        dq_acc[...] = jnp.zeros_like(dq_acc)
        r_ref[...] = jnp.zeros_like(r_ref)
        s_ref[...] = jnp.zeros_like(s_ref)
        q = q_ref[...]
        do = do_ref[...]
        ptot = jnp.sum(do.astype(F32) * o_ref[...], axis=-1, keepdims=True)
        qpos = i * tq + lax.broadcasted_iota(jnp.int32, (tq, 1), 0)
        tri = _tri_ones()

        def step(t, masked):
            base = _tile_base(t)
            kt = k_ref[pl.ds(base, ATT_TK), :]
            vt = v_ref[pl.ds(base, ATT_TK), :]
            lk, lb, valid = _scores(q, kt, qpos, base, masked)
            off, r_new = _suffix(lk, tri, r_ref[...])
            a = jnp.exp2(lb + off)
            if masked:
                a = jnp.where(valid, a, 0.0)
            ab = a.astype(BF16)
            da = lax.dot_general(do, vt, (((1,), (1,)), ((), ())), preferred_element_type=F32)
            p = ab.astype(F32) * da
            poff, s_new = _suffix(p, tri, s_ref[...])
            dz = (p - jnp.exp2(lb) * (ptot - poff)) * LN2
            if masked:
                dz = jnp.where(valid, dz, 0.0)
            dzb = dz.astype(BF16)
            dq_acc[...] += jnp.dot(dzb, kt, preferred_element_type=F32)
            dk_acc[pl.ds(base, ATT_TK), :] += lax.dot_general(dzb, q, (((0,), (0,)), ((), ())),
                                                              preferred_element_type=F32)
            dv_acc[pl.ds(base, ATT_TK), :] += lax.dot_general(ab, do, (((0,), (0,)), ((), ())),
                                                              preferred_element_type=F32)
            r_ref[...] = r_new
            s_ref[...] = s_new
            return jnp.max(r_new)

        _walk_tiles(i, tq, step)
        dq_ref[...] = (dq_acc[...] * Q_SCALE).astype(BF16)

        @pl.when(i == nq - 1)
        def _():
            ck = pltpu.make_async_copy(dk_acc, dk_hbm.at[h], sem.at[0])
            cv = pltpu.make_async_copy(dv_acc, dv_hbm.at[h], sem.at[1])
            ck.start()
            cv.start()
            ck.wait()
            cv.wait()

        if scatter is not None:
            pl.when((h == HEADS - 1) & (i == nq - 1))(exchange.finish)

    tile = pl.BlockSpec((tq, HEAD_DIM), lambda h, i: (i, h))
    hbm = pl.BlockSpec(memory_space=pl.ANY)
    hosted = scatter is not None
    return pl.pallas_call(
        body, name=name, grid=(HEADS, nq),
        in_specs=[tile,
                  pl.BlockSpec((l, HEAD_DIM), lambda h, i: (0, HEADS + h)),
                  pl.BlockSpec((l, HEAD_DIM), lambda h, i: (0, 2 * HEADS + h)),
                  tile, tile] + [hbm] * hosted,
        out_specs=[tile, hbm, hbm] + [hbm] * hosted,
        out_shape=[jax.ShapeDtypeStruct((l, HEADS * HEAD_DIM), BF16),
                   jax.ShapeDtypeStruct((HEADS, l, HEAD_DIM), F32), jax.ShapeDtypeStruct((HEADS, l, HEAD_DIM), F32)]
        + ([jax.ShapeDtypeStruct(scatter.shape, scatter.dtype)] if hosted else []),
        scratch_shapes=[pltpu.VMEM((l, HEAD_DIM), F32), pltpu.VMEM((l, HEAD_DIM), F32),
                        pltpu.VMEM((tq, HEAD_DIM), F32), pltpu.VMEM((tq, BLOCK), F32), pltpu.VMEM((tq, BLOCK), F32),
                        pltpu.SemaphoreType.DMA((2,))] + EXCHANGE_SEMS * hosted,
        compiler_params=_params(("arbitrary", "arbitrary")),
    )(qkv, qkv, qkv, att, datt, *([scatter] * hosted))


MIX_TM = 256


def _mix_branches(s_ref, p_ref, t_ref, g_ref, wa_ref, wb_ref, wc_ref, ba_ref, sc_ref, d):
    ya = jnp.dot(s_ref[...], wa_ref[...], preferred_element_type=F32) + ba_ref[...]
    yb0 = jnp.dot(p_ref[...], wb_ref[...], preferred_element_type=F32)
    yc = jnp.dot(t_ref[...], wc_ref[...], preferred_element_type=F32)
    g0 = _sigmoid(g_ref[:, 0:d].astype(F32))
    g1 = _sigmoid(g_ref[:, d:2 * d].astype(F32))
    g2 = _sigmoid(g_ref[:, 2 * d:3 * d].astype(F32))
    return ya, yb0, yc, g0, g1, g2


def _mix_specs(tm, d):
    row = lambda w: pl.BlockSpec((tm, w), lambda i: (i, 0))
    full = lambda r: pl.BlockSpec((r, d), lambda i: (0, 0))
    return [row(CONV_CH), row(POOL_CH), row(HEADS * HEAD_DIM), row(3 * d),
            full(CONV_CH), full(POOL_CH), full(HEADS * HEAD_DIM), full(1), full(1)]


def _mix_fwd(s, pooled, att, gates, wa, wb, wc, ba, scale, *, name):
    l, d = s.shape[0], wa.shape[1]
    tm = _tile(l, MIX_TM, 128)

    def body(s_ref, p_ref, t_ref, g_ref, wa_ref, wb_ref, wc_ref, ba_ref, sc_ref, o_ref):
        ya, yb0, yc, g0, g1, g2 = _mix_branches(s_ref, p_ref, t_ref, g_ref, wa_ref, wb_ref, wc_ref, ba_ref, sc_ref, d)
        o_ref[...] = (g0 * ya + g1 * (yb0 * sc_ref[...]) + g2 * yc).astype(BF16)

    return pl.pallas_call(
        body, name=name, grid=(l // tm,), in_specs=_mix_specs(tm, d),
        out_specs=pl.BlockSpec((tm, d), lambda i: (i, 0)),
        out_shape=jax.ShapeDtypeStruct((l, d), BF16),
        compiler_params=_params(("parallel",)),
    )(s, pooled, att, gates, wa, wb, wc, ba.reshape(1, d), scale.reshape(1, d))


def _mix_bwd(s, pooled, att, gates, wa, wb, wc, ba, scale, dmixed, *, name):
    l, d = s.shape[0], wa.shape[1]
    tm = _tile(l, MIX_TM, 128)

    def body(s_ref, p_ref, t_ref, g_ref, wa_ref, wb_ref, wc_ref, ba_ref, sc_ref, dm_ref,
             dg_ref, dya_ref, dyb_ref, dyc_ref, vec_ref):
        @pl.when(pl.program_id(0) == 0)
        def _():
            vec_ref[...] = jnp.zeros_like(vec_ref)

        ya, yb0, yc, g0, g1, g2 = _mix_branches(s_ref, p_ref, t_ref, g_ref, wa_ref, wb_ref, wc_ref, ba_ref, sc_ref, d)
        dm = dm_ref[...].astype(F32)
        sc = sc_ref[...]
        dg_ref[:, 0:d] = (dm * ya * g0 * (1.0 - g0)).astype(BF16)
        dg_ref[:, d:2 * d] = (dm * (yb0 * sc) * g1 * (1.0 - g1)).astype(BF16)
        dg_ref[:, 2 * d:3 * d] = (dm * yc * g2 * (1.0 - g2)).astype(BF16)
        dya = dm * g0
        dyb = dm * g1
        dya_ref[...] = dya.astype(BF16)
        dyb_ref[...] = (dyb * sc).astype(BF16)
        dyc_ref[...] = (dm * g2).astype(BF16)
        vec_ref[0:1, :] += jnp.sum(dya, axis=0, keepdims=True)
        vec_ref[1:2, :] += jnp.sum(dyb * yb0, axis=0, keepdims=True)

    row = lambda w: pl.BlockSpec((tm, w), lambda i: (i, 0))
    outs = pl.pallas_call(
        body, name=name, grid=(l // tm,), in_specs=_mix_specs(tm, d) + [row(d)],
        out_specs=[row(3 * d), row(d), row(d), row(d), pl.BlockSpec((8, d), lambda i: (0, 0))],
        out_shape=[jax.ShapeDtypeStruct((l, 3 * d), BF16), jax.ShapeDtypeStruct((l, d), BF16),
                   jax.ShapeDtypeStruct((l, d), BF16), jax.ShapeDtypeStruct((l, d), BF16),
                   jax.ShapeDtypeStruct((8, d), F32)],
        compiler_params=_params(("arbitrary",)),
    )(s, pooled, att, gates, wa, wb, wc, ba.reshape(1, d), scale.reshape(1, d), dmixed)
    return outs


FFN_TC = 512
_GELU_C = 0.7978845608028654
_GELU_A = 0.044715


def _gelu(x):
    th = jnp.tanh(_GELU_C * (x + _GELU_A * x * x * x))
    return 0.5 * x * (1.0 + th), th


def _gelu_grad(x, th):
    return 0.5 * (1.0 + th) + 0.5 * x * (1.0 - th * th) * _GELU_C * (1.0 + 3.0 * _GELU_A * x * x)


FFN_CH = 32


def _ffn_taps(win):
    return (pltpu.roll(win, 2, 0)[FFN_HALO:, :], pltpu.roll(win, 1, 0)[FFN_HALO:, :], win[FFN_HALO:, :])


def _ffn_conv(taps, w, b):
    return b + w[0:1, :] * taps[0] + w[1:2, :] * taps[1] + w[2:3, :] * taps[2]


def _fold8(x):
    acc = x[0:8, :]
    for r in range(8, x.shape[0], 8):
        acc = acc + x[r:r + 8, :]
    return acc


def _ffn_fwd(ug, uv, wg, wv, bg, bv, *, name):
    l, f = ug.shape
    tm = _conv_tile(l)
    tc = _tile(f, FFN_TC, 128)
    ext = FFN_HALO + tm

    def body(ug_ref, uv_ref, wg_ref, wv_ref, bg_ref, bv_ref, o_ref, gc_ref, vc_ref, bufg, bufv):
        i = pl.program_id(1)
        for buf, u_ref in ((bufg, ug_ref), (bufv, uv_ref)):
            @pl.when(i == 0)
            def _():
                buf[0:FFN_HALO, :] = jnp.zeros((FFN_HALO, tc), F32)

            @pl.when(i > 0)
            def _():
                buf[0:FFN_HALO, :] = buf[tm:ext, :]

            buf[FFN_HALO:ext, :] = u_ref[...].astype(F32)
        wg, wv, bg_, bv_ = wg_ref[...], wv_ref[...], bg_ref[...], bv_ref[...]

        def chunk(c, carry):
            r0 = pl.multiple_of(c * FFN_CH, FFN_CH)
            gc = _ffn_conv(_ffn_taps(bufg[pl.ds(r0, FFN_HALO + FFN_CH), :]), wg, bg_)
            vc = _ffn_conv(_ffn_taps(bufv[pl.ds(r0, FFN_HALO + FFN_CH), :]), wv, bv_)
            o_ref[pl.ds(r0, FFN_CH), :] = (_gelu(gc)[0] * vc).astype(BF16)
            gc_ref[pl.ds(r0, FFN_CH), :] = gc.astype(BF16)
            vc_ref[pl.ds(r0, FFN_CH), :] = vc.astype(BF16)
            return carry

        lax.fori_loop(0, tm // FFN_CH, chunk, 0)

    assert tm % FFN_CH == 0
    tile = pl.BlockSpec((tm, tc), lambda j, i: (i, j))
    wspec = pl.BlockSpec((FFN_K, tc), lambda j, i: (0, j))
    bspec = pl.BlockSpec((1, tc), lambda j, i: (0, j))
    return pl.pallas_call(
        body, name=name, grid=(f // tc, l // tm),
        in_specs=[tile, tile, wspec, wspec, bspec, bspec], out_specs=[tile, tile, tile],
        out_shape=[jax.ShapeDtypeStruct((l, f), BF16)] * 3,
        scratch_shapes=[pltpu.VMEM((ext, tc), F32), pltpu.VMEM((ext, tc), F32)],
        compiler_params=_params(("parallel", "arbitrary")),
    )(ug, uv, wg, wv, bg.reshape(1, f), bv.reshape(1, f))


def _ffn_bwd(ug, uv, gc, vc, wg, wv, dact, *, name):
    l, f = ug.shape
    tm = _conv_tile(l)
    tc = _tile(f, FFN_TC, 128)
    nt = l // tm
    ext = FFN_HALO + tm
    win_rows = FFN_CH + FFN_HALO

    def body(ug_ref, uv_ref, gc_ref, vc_ref, wg_ref, wv_ref, da_ref,
             dug_ref, duv_ref, gg_ref, gv_ref, dbufg, dbufv, gaccg, gaccv):
        i = pl.program_id(1)

        @pl.when(i == 0)
        def _():
            gg_ref[...] = jnp.zeros_like(gg_ref)
            gv_ref[...] = jnp.zeros_like(gv_ref)

        for dbuf, gacc in ((dbufg, gaccg), (dbufv, gaccv)):
            @pl.when(i == 0)
            def _():
                dbuf[tm:ext, :] = jnp.zeros((FFN_HALO, tc), F32)

            @pl.when(i > 0)
            def _():
                dbuf[tm:ext, :] = dbuf[0:FFN_HALO, :]

            gacc[...] = jnp.zeros_like(gacc)
        wg, wv = wg_ref[...], wv_ref[...]

        def chunk(cc, carry):
            r0 = pl.multiple_of((tm // FFN_CH - 1 - cc) * FFN_CH, FFN_CH)
            rows = pl.ds(r0, FFN_CH)
            gcv = gc_ref[rows, :].astype(F32)
            ge, th = _gelu(gcv)
            da = da_ref[rows, :].astype(F32)
            for dc, u_ref, w, dbuf, du_ref, gacc in (
                    (da * vc_ref[rows, :].astype(F32) * _gelu_grad(gcv, th), ug_ref, wg, dbufg, dug_ref, gaccg),
                    (da * ge, uv_ref, wv, dbufv, duv_ref, gaccv)):
                dbuf[rows, :] = dc
                dwin = dbuf[pl.ds(r0, win_rows), :]
                d1 = pltpu.roll(dwin, win_rows - 1, 0)[0:FFN_CH, :]
                d2 = pltpu.roll(dwin, win_rows - 2, 0)[0:FFN_CH, :]
                du_ref[rows, :] = (w[2:3, :] * dc + w[1:2, :] * d1 + w[0:1, :] * d2).astype(BF16)
                u = u_ref[rows, :].astype(F32)
                gacc[0:8, :] += _fold8(d2 * u)
                gacc[8:16, :] += _fold8(d1 * u)
                gacc[16:24, :] += _fold8(dc * u)
                gacc[24:32, :] += _fold8(dc)
            return carry

        lax.fori_loop(0, tm // FFN_CH, chunk, 0)
        for gacc, gp_ref in ((gaccg, gg_ref), (gaccv, gv_ref)):
            for k in range(FFN_K + 1):
                gp_ref[k:k + 1, :] += jnp.sum(gacc[8 * k:8 * k + 8, :], axis=0, keepdims=True)

    assert tm % FFN_CH == 0
    tile = pl.BlockSpec((tm, tc), lambda j, i: (nt - 1 - i, j))
    wspec = pl.BlockSpec((FFN_K, tc), lambda j, i: (0, j))
    gspec = pl.BlockSpec((8, tc), lambda j, i: (0, j))
    return pl.pallas_call(
        body, name=name, grid=(f // tc, nt),
        in_specs=[tile, tile, tile, tile, wspec, wspec, tile],
        out_specs=[tile, tile, gspec, gspec],
        out_shape=[jax.ShapeDtypeStruct((l, f), BF16), jax.ShapeDtypeStruct((l, f), BF16),
                   jax.ShapeDtypeStruct((8, f), F32), jax.ShapeDtypeStruct((8, f), F32)],
        scratch_shapes=[pltpu.VMEM((ext, tc), F32), pltpu.VMEM((ext, tc), F32),
                        pltpu.VMEM((32, tc), F32), pltpu.VMEM((32, tc), F32)],
        compiler_params=_params(("parallel", "arbitrary")),
    )(ug, uv, gc, vc, wg, wv, dact)


def _adamw(parts, w, m, v, *, name):
    r = w.shape[0]
    tr = _tile(r, PACK_ROWS, 16)
    c1 = 1.0 / (1.0 - ADAM_B1 ** ADAM_STEP)
    c2 = 1.0 / (1.0 - ADAM_B2 ** ADAM_STEP)

    def body(p_ref, w_ref, m_ref, v_ref, g_ref, d_ref, nm_ref, nv_ref):
        g = p_ref[0].astype(F32)
        for k in range(1, N_DEV):
            g = g + p_ref[k].astype(F32)
        nm = ADAM_B1 * m_ref[...] + (1.0 - ADAM_B1) * g
        nv = ADAM_B2 * v_ref[...] + (1.0 - ADAM_B2) * (g * g)
        g_ref[...] = g
        nm_ref[...] = nm
        nv_ref[...] = nv
        d_ref[...] = -ADAM_LR * ((nm * c1) / (jnp.sqrt(nv * c2) + ADAM_EPS) + ADAM_WD * w_ref[...])

    tile = pl.BlockSpec((tr, 128), lambda i: (i, 0))
    return pl.pallas_call(
        body, name=name, grid=(r // tr,),
        in_specs=[pl.BlockSpec((N_DEV, tr, 128), lambda i: (0, i, 0)), tile, tile, tile],
        out_specs=[tile, tile, tile, tile],
        out_shape=[jax.ShapeDtypeStruct((r, 128), F32)] * 4,
        compiler_params=_params(("parallel",)),
    )(parts, w, m, v)


def _place():
    return lax.axis_index("x"), lax.axis_index("y"), lax.axis_index("c")


EXCHANGE_SEMS = [pltpu.SemaphoreType.DMA((7,)), pltpu.SemaphoreType.DMA((7,)), pltpu.SemaphoreType.DMA]


class _Gather:
    def __init__(self, x_ref, out_ref, send_sems, recv_sems, local_sem):
        self.x_ref, self.out_ref, self.send_sems, self.recv_sems, self.local_sem = (
            x_ref, out_ref, send_sems, recv_sems, local_sem)

    def _parts(self):
        xx, yy, cc = _place()
        me, sibling = (xx, yy, cc), (xx, yy, 1 - cc)
        chips = [(1 - xx, yy), (xx, 1 - yy), (1 - xx, 1 - yy)]

        def slot(px, py, pc):
            return self.out_ref.at[4 * px + 2 * py + pc]

        def copy(k, block, to, src=None):
            return pltpu.make_async_remote_copy(
                src_ref=slot(*block) if src is None else src, dst_ref=slot(*block),
                send_sem=self.send_sems.at[k], recv_sem=self.recv_sems.at[k], device_id=to, device_id_type=MESH)

        mine = pltpu.make_async_copy(self.x_ref, slot(*me), self.local_sem)
        first = [copy(0, me, sibling, src=self.x_ref)]
        first += [copy(1 + j, me, (*chip, cc), src=self.x_ref) for j, chip in enumerate(chips)]
        return cc, me, sibling, chips, copy, mine, first

    def start(self):
        _, _, _, _, _, mine, first = self._parts()
        mine.start()
        for cp in first:
            cp.start()

    def finish(self):
        cc, me, sibling, chips, copy, mine, first = self._parts()
        passed = [copy(4 + j, (*chip, cc), sibling) for j, chip in enumerate(chips)]
        for j, chip in enumerate(chips):
            copy(1 + j, (*chip, cc), me).wait_recv()
            passed[j].start()
        copy(0, sibling, me).wait_recv()
        for j, chip in enumerate(chips):
            copy(4 + j, (*chip, 1 - cc), me).wait_recv()
        for cp in first + passed:
            cp.wait_send()
        mine.wait()


class _Scatter:
    def __init__(self, s_ref, r_ref, send_sems, recv_sems, local_sem):
        self.s_ref, self.r_ref, self.send_sems, self.recv_sems, self.local_sem = (
            s_ref, r_ref, send_sems, recv_sems, local_sem)

    def _parts(self):
        xx, yy, cc = _place()
        me = 4 * xx + 2 * yy + cc
        local = pltpu.make_async_copy(self.s_ref.at[me], self.r_ref.at[me], self.local_sem)
        copies = []
        for m in range(1, N_DEV):
            px = 1 - xx if m & 4 else xx
            py = 1 - yy if m & 2 else yy
            pc = 1 - cc if m & 1 else cc
            copies.append(pltpu.make_async_remote_copy(
                src_ref=self.s_ref.at[4 * px + 2 * py + pc], dst_ref=self.r_ref.at[me],
                send_sem=self.send_sems.at[m - 1], recv_sem=self.recv_sems.at[m - 1],
                device_id=(px, py, pc), device_id_type=MESH))
        return local, copies

    def start(self):
        local, copies = self._parts()
        local.start()
        for cp in copies:
            cp.start()

    def finish(self):
        local, copies = self._parts()
        for cp in copies:
            cp.wait_recv()
        for cp in copies:
            cp.wait_send()
        local.wait()


def _exchange_call(kind, x, out_shape, *, name):
    def body(x_ref, out_ref, send_sems, recv_sems, local_sem):
        ex = kind(x_ref, out_ref, send_sems, recv_sems, local_sem)
        ex.start()
        ex.finish()

    return pl.pallas_call(
        body, name=name,
        in_specs=[pl.BlockSpec(memory_space=pl.ANY)], out_specs=pl.BlockSpec(memory_space=pl.ANY),
        out_shape=jax.ShapeDtypeStruct(out_shape, x.dtype), scratch_shapes=EXCHANGE_SEMS,
    )(x)


def _all_gather(x, *, name):
    return _exchange_call(_Gather, x, (N_DEV,) + x.shape, name=name)


def _all_to_all(send, *, name):
    return _exchange_call(_Scatter, send, send.shape, name=name)


def _as_rows(a, lead, dtype):
    a = a.astype(dtype)
    size = 1
    for s in a.shape[len(lead):]:
        size *= s
    if size % PACK_ALIGN:
        a = jnp.pad(a.reshape(lead + (size,)), [(0, 0)] * len(lead) + [(0, (-size) % PACK_ALIGN)])
    return a.reshape(lead + (-1, 128))


def _pack(arrays, dtype):
    buf = jnp.concatenate([_as_rows(a, (), dtype) for a in arrays], axis=0)
    return jnp.pad(buf, ((0, (-buf.shape[0]) % PACK_ROWS), (0, 0)))


def _pack_pieces(arrays, dtype):
    buf = jnp.concatenate([_as_rows(a, (N_DEV,), dtype) for a in arrays], axis=1)
    return jnp.pad(buf, ((0, 0), (0, (-buf.shape[1]) % PACK_ROWS), (0, 0)))


def _unpack(buf, shapes, lead=()):
    out, row = [], 0
    for shp in shapes:
        size = 1
        for s in shp:
            size *= s
        rows = (size + (-size) % PACK_ALIGN) // 128
        part = buf[..., row:row + rows, :]
        if size % PACK_ALIGN:
            part = part.reshape(lead + (rows * 128,))[..., :size]
        out.append(part.reshape(lead + tuple(shp)))
        row += rows
    return out


def _unshard(g, axis):
    g = jnp.moveaxis(g, 0, axis)
    shp = list(g.shape)
    return g.reshape(shp[:axis] + [shp[axis] * shp[axis + 1]] + shp[axis + 2:])


def _pieces(full, axis):
    shp = list(full.shape)
    g = full.reshape(shp[:axis] + [N_DEV, shp[axis] // N_DEV] + shp[axis + 1:])
    return jnp.moveaxis(g, axis, 0)


SHARDED = (("meta", 1), ("w_in", 2), ("conv_dw_w", 2), ("w_conv_out", 2), ("w_pool_grp", 3), ("w_attn_out", 2),
           ("w_o", 1), ("w_up", 2), ("ffn_dw_w", 2), ("w_down", 1))
MATRICES = ("w_in", "w_conv_out", "w_pool_grp", "w_attn_out", "w_o", "w_up", "w_down")
REPLICATED = ("norm1", "conv_dw_b", "conv_ln_g", "conv_ln_b", "b_conv_out", "pool_scale", "norm2", "ffn_dw_b",
              "final_norm")
WEIGHTS = ("meta", "norm1", "w_in", "conv_dw_w", "conv_dw_b", "conv_ln_g", "conv_ln_b", "w_conv_out", "b_conv_out",
           "w_pool_grp", "pool_scale", "w_attn_out", "w_o", "norm2", "w_up", "ffn_dw_w", "ffn_dw_b", "w_down",
           "final_norm")


def _block_diag(w_grp):
    g, gc, od = w_grp.shape
    out = jnp.zeros((g * gc, g * od), w_grp.dtype)
    for i in range(g):
        out = out.at[i * gc:(i + 1) * gc, i * od:(i + 1) * od].set(w_grp[i])
    return out


def _block_diag_grad(gw, g):
    gc, od = gw.shape[0] // g, gw.shape[1] // g
    return jnp.stack([gw[i * gc:(i + 1) * gc, i * od:(i + 1) * od] for i in range(g)])


C_CONV = 2 * CONV_CH
C_POOL = C_CONV + POOL_CH
C_ATT = HEADS * HEAD_DIM
C_QKV = C_POOL + 3 * C_ATT


def _layer_fwd(h, p, tag, gather=None):
    d = h.shape[1]
    w_in = p["w_in"]
    hn = _rms_fwd(h, p["norm1"], name=f"rms1_{tag}")
    pa = _mm(hn, w_in[:, :C_POOL], out_dtype=F32, name=f"proj_a_{tag}")
    q_scale = jnp.concatenate([jnp.full((C_ATT,), Q_SCALE, F32), jnp.ones((2 * C_ATT,), F32)])
    qkv = _mm(hn, w_in[:, C_POOL:C_QKV], out_dtype=BF16, col_scale=q_scale, name=f"proj_qkv_{tag}")
    gates = _mm(hn, w_in[:, C_QKV:], out_dtype=BF16, name=f"proj_g_{tag}")
    s = _conv_fwd(pa, p["conv_dw_w"], p["conv_dw_b"], p["conv_ln_g"], p["conv_ln_b"], name=f"conv_{tag}")
    pooled = _pool_fwd(pa, name=f"pool_{tag}")
    att, att32, *got = _attn_fwd(qkv, name=f"attn_{tag}", gather=gather)
    wb = _block_diag(p["w_pool_grp"])
    mixed = _mix_fwd(s, pooled, att, gates, p["w_conv_out"], wb, p["w_attn_out"], p["b_conv_out"], p["pool_scale"],
                     name=f"mix_{tag}")
    h1 = _mm(mixed, p["w_o"], out_dtype=F32, res=h, mask_rows=True, name=f"wo_{tag}")
    hn2 = _rms_fwd(h1, p["norm2"], name=f"rms2_{tag}")
    f = p["w_up"].shape[1] // 2
    ug = _mm(hn2, p["w_up"][:, :f], out_dtype=BF16, name=f"up_g_{tag}")
    uv = _mm(hn2, p["w_up"][:, f:], out_dtype=BF16, name=f"up_v_{tag}")
    act, gc, vc = _ffn_fwd(ug, uv, p["ffn_dw_w"][:, :f], p["ffn_dw_w"][:, f:], p["ffn_dw_b"][:f], p["ffn_dw_b"][f:],
                           name=f"ffn_{tag}")
    h2 = _mm(act, p["w_down"], out_dtype=F32, res=h1, mask_rows=True, name=f"down_{tag}")
    saved = dict(h=h, hn=hn, pa=pa, qkv=qkv, gates=gates, s=s, pooled=pooled, att=att, att32=att32, wb=wb, mixed=mixed,
                 h1=h1,
                 hn2=hn2, ug=ug, uv=uv, gc=gc, vc=vc, act=act)
    return h2, saved, (got[0] if got else None)


def _layer_bwd(dh2, dh2b, p, sv, tag, scatter=None):
    g = {}
    f = p["w_up"].shape[1] // 2
    dact = _mm(dh2b, p["w_down"].T, out_dtype=BF16, name=f"b_down_{tag}")
    g["w_down"] = _mm_tn(sv["act"], dh2b, name=f"g_down_{tag}")
    dug, duv, gpg, gpv = _ffn_bwd(sv["ug"], sv["uv"], sv["gc"], sv["vc"], p["ffn_dw_w"][:, :f], p["ffn_dw_w"][:, f:],
                                  dact, name=f"b_ffn_{tag}")
    g["ffn_dw_w"] = jnp.concatenate([gpg[0:FFN_K], gpv[0:FFN_K]], axis=1)
    g["ffn_dw_b"] = jnp.concatenate([gpg[FFN_K], gpv[FFN_K]])
    w_up_t = p["w_up"].T
    dhn2 = _mm(dug, w_up_t[:f], out_dtype=F32, name=f"b_up_g_{tag}")
    dhn2 = _mm(duv, w_up_t[f:], out_dtype=F32, res=dhn2, name=f"b_up_v_{tag}")
    g["w_up"] = jnp.concatenate([_mm_tn(sv["hn2"], dug, name=f"g_up_g_{tag}"),
                                 _mm_tn(sv["hn2"], duv, name=f"g_up_v_{tag}")], axis=1)
    dh1, dh1b, g["norm2"] = _rms_bwd(sv["h1"], p["norm2"], dhn2, dh2, name=f"b_rms2_{tag}")
    dmixed = _mm(dh1b, p["w_o"].T, out_dtype=BF16, name=f"b_wo_{tag}")
    g["w_o"] = _mm_tn(sv["mixed"], dh1b, name=f"g_wo_{tag}")
    dgates, dya, dyb, dyc, vec = _mix_bwd(sv["s"], sv["pooled"], sv["att"], sv["gates"], p["w_conv_out"], sv["wb"],
                                          p["w_attn_out"], p["b_conv_out"], p["pool_scale"], dmixed,
                                          name=f"b_mix_{tag}")
    g["b_conv_out"], g["pool_scale"] = vec[0], vec[1]
    ds = _mm(dya, p["w_conv_out"].T, out_dtype=F32, name=f"b_conv_out_{tag}")
    dpooled = _mm(dyb, sv["wb"].T, out_dtype=F32, name=f"b_pool_out_{tag}")
    datt = _mm(dyc, p["w_attn_out"].T, out_dtype=BF16, name=f"b_attn_out_{tag}")
    g["w_conv_out"] = _mm_tn(sv["s"], dya, name=f"g_conv_out_{tag}")
    g["w_pool_grp"] = _block_diag_grad(_mm_tn(sv["pooled"], dyb, name=f"g_pool_{tag}"), len(POOL_WINDOWS))
    g["w_attn_out"] = _mm_tn(sv["att"], dyc, name=f"g_attn_out_{tag}")
    dc, gp = _conv_bwd_ln(sv["pa"], ds, p["conv_dw_w"], p["conv_dw_b"], p["conv_ln_g"], p["conv_ln_b"],
                          name=f"b_conv_ln_{tag}")
    g["conv_dw_w"], g["conv_dw_b"], g["conv_ln_g"], g["conv_ln_b"] = gp[0:CONV_K], gp[32], gp[33], gp[34]
    dconv = _conv_bwd_in(sv["pa"], dc, p["conv_dw_w"], name=f"b_conv_in_{tag}")
    dp = _pool_bwd(dpooled, name=f"b_pool_{tag}")
    dq, dk, dv, *recv = _attn_bwd(sv["qkv"], sv["att32"], datt, name=f"b_attn_{tag}", scatter=scatter)
    dk = jnp.moveaxis(dk, 0, 1).reshape(dq.shape).astype(BF16)
    dv = jnp.moveaxis(dv, 0, 1).reshape(dq.shape).astype(BF16)
    w_in_t = p["w_in"].T
    cols = [(jnp.concatenate([dconv, dp, dq, dk, dv], axis=1), 0, C_QKV), (dgates, C_QKV, w_in_t.shape[0])]
    dhn, gw = None, []
    for n, (dcol, lo, hi) in enumerate(cols):
        dhn = _mm(dcol, w_in_t[lo:hi], out_dtype=F32, res=dhn, name=f"b_in{n}_{tag}")
        gw.append(_mm_tn(sv["hn"], dcol, name=f"g_in{n}_{tag}"))
    g["w_in"] = jnp.concatenate(gw, axis=1)
    dh, dhb, g["norm1"] = _rms_bwd(sv["h"], p["norm1"], dhn, dh1, name=f"b_rms1_{tag}")
    return dh, dhb, g, (recv[0] if recv else None)


def kernel(x, meta, norm1, w_in, conv_dw_w, conv_dw_b, conv_ln_g, conv_ln_b, w_conv_out, b_conv_out, w_pool_grp, pool_scale, w_attn_out, w_o, norm2, w_up, ffn_dw_w, ffn_dw_b, w_down, final_norm, loss_target, m_meta, m_norm1, m_w_in, m_conv_dw_w, m_conv_dw_b, m_conv_ln_g, m_conv_ln_b, m_w_conv_out, m_b_conv_out, m_w_pool_grp, m_pool_scale, m_w_attn_out, m_w_o, m_norm2, m_w_up, m_ffn_dw_w, m_ffn_dw_b, m_w_down, m_final_norm, v_meta, v_norm1, v_w_in, v_conv_dw_w, v_conv_dw_b, v_conv_ln_g, v_conv_ln_b, v_w_conv_out, v_b_conv_out, v_w_pool_grp, v_pool_scale, v_w_attn_out, v_w_o, v_norm2, v_w_up, v_ffn_dw_w, v_ffn_dw_b, v_w_down, v_final_norm):
    given = dict(meta=meta, norm1=norm1, w_in=w_in, conv_dw_w=conv_dw_w, conv_dw_b=conv_dw_b, conv_ln_g=conv_ln_g, conv_ln_b=conv_ln_b, w_conv_out=w_conv_out, b_conv_out=b_conv_out, w_pool_grp=w_pool_grp, pool_scale=pool_scale, w_attn_out=w_attn_out, w_o=w_o, norm2=norm2, w_up=w_up, ffn_dw_w=ffn_dw_w, ffn_dw_b=ffn_dw_b, w_down=w_down, final_norm=final_norm)
    mom_m = dict(meta=m_meta, norm1=m_norm1, w_in=m_w_in, conv_dw_w=m_conv_dw_w, conv_dw_b=m_conv_dw_b, conv_ln_g=m_conv_ln_g, conv_ln_b=m_conv_ln_b, w_conv_out=m_w_conv_out, b_conv_out=m_b_conv_out, w_pool_grp=m_w_pool_grp, pool_scale=m_pool_scale, w_attn_out=m_w_attn_out, w_o=m_w_o, norm2=m_norm2, w_up=m_w_up, ffn_dw_w=m_ffn_dw_w, ffn_dw_b=m_ffn_dw_b, w_down=m_w_down, final_norm=m_final_norm)
    mom_v = dict(meta=v_meta, norm1=v_norm1, w_in=v_w_in, conv_dw_w=v_conv_dw_w, conv_dw_b=v_conv_dw_b, conv_ln_g=v_conv_ln_g, conv_ln_b=v_conv_ln_b, w_conv_out=v_w_conv_out, b_conv_out=v_b_conv_out, w_pool_grp=v_w_pool_grp, pool_scale=v_pool_scale, w_attn_out=v_w_attn_out, w_o=v_w_o, norm2=v_norm2, w_up=v_w_up, ffn_dw_w=v_ffn_dw_w, ffn_dw_b=v_ffn_dw_b, w_down=v_w_down, final_norm=v_final_norm)
    sharded_axis = dict(SHARDED)
    vectors = [n for n, _ in SHARDED if n not in MATRICES]
    depth = norm1.shape[0]

    got_vec = _all_gather(_pack([given[n] for n in vectors], F32), name="gather_vectors")
    full = {n: given[n] for n in REPLICATED}
    for n, a in zip(vectors, _unpack(got_vec, [given[n].shape for n in vectors], (N_DEV,))):
        full[n] = _unshard(a, sharded_axis[n])
    mat_shapes = [given[n].shape[1:] for n in MATRICES]

    def layer_matrices(i):
        return _pack([given[n][i] for n in MATRICES], BF16)

    got_mat = _all_gather(layer_matrices(0), name="gather_matrices_l0")

    xs = x[0]
    d = xs.shape[1]
    h = jnp.concatenate([jnp.zeros((PAD, d), F32), full["meta"], xs], axis=0)
    layers, saved = [], []
    for i in range(depth):
        p = {n: full[n][i] for n in full if n not in ("meta", "final_norm")}
        for n, a in zip(MATRICES, _unpack(got_mat, mat_shapes, (N_DEV,))):
            p[n] = _unshard(a, sharded_axis[n] - 1)
        layers.append(p)
        h, sv, got_mat = _layer_fwd(h, p, f"l{i}", gather=layer_matrices(i + 1) if i + 1 < depth else None)
        saved.append(sv)
    loss_part, dh, dhb, g_final = _loss_head(h, full["final_norm"], loss_target[0], name="loss_head")

    names = [n for n, _ in SHARDED if n != "meta"]

    def layer_pieces(g, extra=()):
        return _pack_pieces([_pieces(g[n], sharded_axis[n] - 1) for n in names] + list(extra), BF16)

    grads, recvs = [None] * depth, [None] * depth
    for i in reversed(range(depth)):
        send = layer_pieces(grads[i + 1]) if i + 1 < depth else None
        dh, dhb, grads[i], got = _layer_bwd(dh, dhb, layers[i], saved[i], f"l{i}", scatter=send)
        if send is not None:
            recvs[i + 1] = got
    grad_x = dh[FRONT:][None]
    recvs[0] = _all_to_all(layer_pieces(grads[0], [_pieces(dh[PAD:FRONT], sharded_axis["meta"])]), name="scatter_grads_l0")
    full_grad = {n: jnp.stack([grads[i][n] for i in range(depth)]) for n in REPLICATED if n != "final_norm"}
    full_grad["final_norm"] = g_final
    rep_shapes = [given[n].shape for n in REPLICATED] + [(1,)]
    rep_parts = _all_gather(_pack([full_grad[n] for n in REPLICATED] + [loss_part.reshape(1)], F32),
                            name="gather_partials")

    out, per_layer = {}, []
    for i in range(depth):
        extra = ["meta"] if i == 0 else []
        pick = lambda src: [src[n][i] for n in names] + [src[n] for n in extra]
        res = _adamw(recvs[i], _pack(pick(given), F32), _pack(pick(mom_m), F32), _pack(pick(mom_v), F32),
                     name=f"adamw_l{i}")
        shapes = [given[n].shape[1:] for n in names] + [given[n].shape for n in extra]
        per_layer.append([_unpack(buf, shapes) for buf in res])
    for k, kind in enumerate(("grad", "delta", "new_m", "new_v")):
        for j, n in enumerate(names):
            out[kind, n] = jnp.stack([per_layer[i][k][j] for i in range(depth)])
        out[kind, "meta"] = per_layer[0][k][len(names)]
    rep_w = [given[n] for n in REPLICATED] + [jnp.zeros((1,), F32)]
    rep_m = [mom_m[n] for n in REPLICATED] + [jnp.zeros((1,), F32)]
    rep_v = [mom_v[n] for n in REPLICATED] + [jnp.ones((1,), F32)]
    res = _adamw(rep_parts, _pack(rep_w, F32), _pack(rep_m, F32), _pack(rep_v, F32), name="adamw_replicated")
    for kind, buf in zip(("grad", "delta", "new_m", "new_v"), res):
        for n, a in zip(list(REPLICATED) + ["loss"], _unpack(buf, rep_shapes)):
            out[kind, n] = a
    loss = out["grad", "loss"][0]
    return (loss, grad_x, *[out["grad", n] for n in WEIGHTS], *[out["delta", n] for n in WEIGHTS],
            *[out["new_m", n] for n in WEIGHTS], *[out["new_v", n] for n in WEIGHTS])
```

```python
import functools

import jax
import jax.numpy as jnp
from jax import lax
from jax.experimental import pallas as pl
from jax.experimental.pallas import tpu as pltpu

F32 = jnp.float32
BF16 = jnp.bfloat16
MESH = pl.DeviceIdType.MESH

N_DEV = 8
N_META = 16
BLOCK = 128
PAD = 240
FRONT = PAD + N_META
HEADS = 4
HEAD_DIM = 128
CONV_CH = 256
CONV_K = 31
POOL_CH = 256
POOL_WINDOWS = (2, 4, 8, 16)
FFN_K = 3
EPS = 1e-6
ADAM_LR, ADAM_B1, ADAM_B2, ADAM_EPS, ADAM_WD, ADAM_STEP = 0.001, 0.9, 0.999, 1e-08, 0.01, 10

VMEM_LIMIT = 56 * 1024 * 1024
CONV_HALO = 32
POOL_HALO = 16
FFN_HALO = 8
PACK_ALIGN = 16 * 128
PACK_ROWS = 512


def _tile(n, cap, unit):
    if n <= cap:
        return n
    best = None
    t = unit
    while t <= cap:
        if n % t == 0:
            best = t
        t += unit
    assert best is not None, (n, cap, unit)
    return best


def _params(sem):
    return pltpu.CompilerParams(dimension_semantics=sem, vmem_limit_bytes=VMEM_LIMIT)


def _sigmoid(x):
    return 1.0 / (1.0 + jnp.exp(-x))


MM_MAX_K = 3072
MM_RESIDENT_B = 3072 * 1024 * 2


def _mm(a, b, *, out_dtype, name, res=None, col_scale=None, mask_rows=False, tn_cap=768):
    m, k = a.shape
    k2, n = b.shape
    assert k == k2 and k <= MM_MAX_K
    tn = n if k * n * 2 <= MM_RESIDENT_B else _tile(n, tn_cap, 128)
    tm = _tile(m, 1280 if (k <= 1024 and tn <= 1024) else 640, 128)

    def body(*refs):
        refs = list(refs)
        a_ref, b_ref = refs[:2]
        o_ref = refs[-1]
        r_ref = refs[2] if res is not None else None
        c_ref = refs[-2] if col_scale is not None else None
        y = jnp.dot(a_ref[...].astype(BF16), b_ref[...].astype(BF16), preferred_element_type=F32)
        if col_scale is not None:
            y = y * c_ref[...]
        if res is not None:
            y = y + r_ref[...].astype(F32)
        if mask_rows:
            row = pl.program_id(0) * tm + lax.broadcasted_iota(jnp.int32, (tm, 1), 0)
            y = jnp.where(row >= PAD, y, 0.0)
        o_ref[...] = y.astype(out_dtype)

    in_specs = [pl.BlockSpec((tm, k), lambda i, j: (i, 0)), pl.BlockSpec((k, tn), lambda i, j: (0, j))]
    args = [a, b]
    if res is not None:
        in_specs.append(pl.BlockSpec((tm, tn), lambda i, j: (i, j)))
        args.append(res)
    if col_scale is not None:
        in_specs.append(pl.BlockSpec((1, tn), lambda i, j: (0, j)))
        args.append(col_scale.reshape(1, n))
    return pl.pallas_call(
        body, name=name, grid=(m // tm, n // tn),
        in_specs=in_specs, out_specs=pl.BlockSpec((tm, tn), lambda i, j: (i, j)),
        out_shape=jax.ShapeDtypeStruct((m, n), out_dtype),
        compiler_params=_params(("parallel", "parallel")),
    )(*args)


def _mm_tn(a, b, *, name, t1_cap=512, tn_cap=1024, tl_cap=3328):
    l, k1 = a.shape
    l2, n = b.shape
    assert l == l2
    t1, tn, tl = _tile(k1, t1_cap, 128), _tile(n, tn_cap, 128), _tile(l, tl_cap, 128)

    def body(a_ref, b_ref, o_ref):
        @pl.when(pl.program_id(2) == 0)
        def _():
            o_ref[...] = jnp.zeros_like(o_ref)

        o_ref[...] += lax.dot_general(a_ref[...].astype(BF16), b_ref[...].astype(BF16),
                                      (((0,), (0,)), ((), ())), preferred_element_type=F32)

    return pl.pallas_call(
        body, name=name, grid=(k1 // t1, n // tn, l // tl),
        in_specs=[pl.BlockSpec((tl, t1), lambda i, j, ll: (ll, i)), pl.BlockSpec((tl, tn), lambda i, j, ll: (ll, j))],
        out_specs=pl.BlockSpec((t1, tn), lambda i, j, ll: (i, j)),
        out_shape=jax.ShapeDtypeStruct((k1, n), F32),
        compiler_params=_params(("parallel", "parallel", "arbitrary")),
    )(a, b)


def _rms_fwd(x, g, *, name):
    l, d = x.shape
    tm = _tile(l, 640, 128)

    def body(x_ref, g_ref, o_ref):
        xv = x_ref[...]
        r = lax.rsqrt(jnp.mean(xv * xv, axis=-1, keepdims=True) + EPS)
        o_ref[...] = (xv * r * g_ref[...]).astype(BF16)

    return pl.pallas_call(
        body, name=name, grid=(l // tm,),
        in_specs=[pl.BlockSpec((tm, d), lambda i: (i, 0)), pl.BlockSpec((1, d), lambda i: (0, 0))],
        out_specs=pl.BlockSpec((tm, d), lambda i: (i, 0)),
        out_shape=jax.ShapeDtypeStruct((l, d), BF16),
        compiler_params=_params(("parallel",)),
    )(x, g.reshape(1, d))


def _rms_bwd(x, g, dy, dres, *, name):
    l, d = x.shape
    tm = _tile(l, 640, 128)

    def body(x_ref, g_ref, dy_ref, dr_ref, dx_ref, dxb_ref, dg_ref):
        i = pl.program_id(0)

        @pl.when(i == 0)
        def _():
            dg_ref[...] = jnp.zeros_like(dg_ref)

        xv = x_ref[...]
        r = lax.rsqrt(jnp.mean(xv * xv, axis=-1, keepdims=True) + EPS)
        xh = xv * r
        dyv = dy_ref[...].astype(F32)
        dxh = dyv * g_ref[...]
        dx = r * (dxh - xh * jnp.mean(dxh * xh, axis=-1, keepdims=True)) + dr_ref[...]
        row = i * tm + lax.broadcasted_iota(jnp.int32, (tm, 1), 0)
        dx = jnp.where(row >= PAD, dx, 0.0)
        dx_ref[...] = dx
        dxb_ref[...] = dx.astype(BF16)
        dg_ref[0:1, :] += jnp.sum(dyv * xh, axis=0, keepdims=True)

    tile = pl.BlockSpec((tm, d), lambda i: (i, 0))
    dx, dxb, dg = pl.pallas_call(
        body, name=name, grid=(l // tm,),
        in_specs=[tile, pl.BlockSpec((1, d), lambda i: (0, 0)), tile, tile],
        out_specs=[tile, tile, pl.BlockSpec((8, d), lambda i: (0, 0))],
        out_shape=[jax.ShapeDtypeStruct((l, d), F32), jax.ShapeDtypeStruct((l, d), BF16),
                   jax.ShapeDtypeStruct((8, d), F32)],
        compiler_params=_params(("arbitrary",)),
    )(x, g.reshape(1, d), dy, dres)
    return dx, dxb, dg[0]


def _loss_head(h, g, target, *, name):
    l, d = h.shape
    tm = FRONT
    assert l % tm == 0 and target.shape[0] == l - tm

    def body(h_ref, g_ref, t_ref, dh_ref, dhb_ref, loss_ref, dg_ref):
        i = pl.program_id(0)

        @pl.when(i == 0)
        def _():
            loss_ref[...] = jnp.zeros_like(loss_ref)
            dg_ref[...] = jnp.zeros_like(dg_ref)
            dh_ref[...] = jnp.zeros_like(dh_ref)
            dhb_ref[...] = jnp.zeros_like(dhb_ref)

        @pl.when(i > 0)
        def _():
            xv = h_ref[...]
            r = lax.rsqrt(jnp.mean(xv * xv, axis=-1, keepdims=True) + EPS)
            xh = xv * r
            gv = g_ref[...]
            err = xh * gv - t_ref[...]
            loss_ref[...] += 0.5 * jnp.sum(jnp.mean(err * err, axis=-1, keepdims=True))
            dy = err * (1.0 / d)
            dxh = dy * gv
            dh = r * (dxh - xh * jnp.mean(dxh * xh, axis=-1, keepdims=True))
            dh_ref[...] = dh
            dhb_ref[...] = dh.astype(BF16)
            dg_ref[0:1, :] += jnp.sum(dy * xh, axis=0, keepdims=True)

    tile = pl.BlockSpec((tm, d), lambda i: (i, 0))
    dh, dhb, loss, dg = pl.pallas_call(
        body, name=name, grid=(l // tm,),
        in_specs=[tile, pl.BlockSpec((1, d), lambda i: (0, 0)),
                  pl.BlockSpec((tm, d), lambda i: (jnp.maximum(i - 1, 0), 0))],
        out_specs=[tile, tile, pl.BlockSpec((8, 128), lambda i: (0, 0)), pl.BlockSpec((8, d), lambda i: (0, 0))],
        out_shape=[jax.ShapeDtypeStruct((l, d), F32), jax.ShapeDtypeStruct((l, d), BF16),
                   jax.ShapeDtypeStruct((8, 128), F32), jax.ShapeDtypeStruct((8, d), F32)],
        compiler_params=_params(("arbitrary",)),
    )(h, g.reshape(1, d), target)
    return loss[0, 0], dh, dhb, dg[0]


def _conv_tile(l):
    return _tile(l, 640, 128)


def _phase_views(buf, views, tm, max_off):
    for s in range(8):
        n = tm + 8 * ((max_off - s) // 8)
        views[s, 0:n, :] = buf[s:s + n, :]
    return views


def _tap(views, off, tm):
    a, s = divmod(off, 8)
    return views[s, 8 * a:8 * a + tm, :]


def _conv_core(a, gt, buf, views, dw_w, dw_b, first):
    tm = a.shape[0]

    @pl.when(first)
    def _():
        buf[0:CONV_HALO, :] = jnp.zeros((CONV_HALO, CONV_CH), F32)

    @pl.when(jnp.logical_not(first))
    def _():
        buf[0:CONV_HALO, :] = buf[tm:tm + CONV_HALO, :]

    sg = _sigmoid(gt)
    buf[CONV_HALO:CONV_HALO + tm, :] = a * sg
    _phase_views(buf, views, tm, CONV_HALO)
    c = jnp.zeros((tm, CONV_CH), F32) + dw_b
    for k in range(CONV_K):
        c = c + dw_w[k:k + 1, :] * _tap(views, CONV_HALO - (CONV_K - 1) + k, tm)
    return c, sg


def _layer_norm(c, ln_g, ln_b):
    mu = jnp.mean(c, axis=-1, keepdims=True)
    xc = c - mu
    r = lax.rsqrt(jnp.mean(xc * xc, axis=-1, keepdims=True) + EPS)
    xh = xc * r
    return xh, r, xh * ln_g + ln_b


def _conv_fwd(pa, dw_w, dw_b, ln_g, ln_b, *, name):
    l = pa.shape[0]
    tm = _conv_tile(l)

    def body(a_ref, gt_ref, w_ref, b_ref, g_ref, bb_ref, o_ref, buf, views):
        c, _ = _conv_core(a_ref[...], gt_ref[...], buf, views, w_ref[...], b_ref[...], pl.program_id(0) == 0)
        _, _, y = _layer_norm(c, g_ref[...], bb_ref[...])
        o_ref[...] = (y * _sigmoid(y)).astype(BF16)

    vec = pl.BlockSpec((1, CONV_CH), lambda i: (0, 0))
    return pl.pallas_call(
        body, name=name, grid=(l // tm,),
        in_specs=[pl.BlockSpec((tm, CONV_CH), lambda i: (i, 0)), pl.BlockSpec((tm, CONV_CH), lambda i: (i, 1)),
                  pl.BlockSpec((CONV_K, CONV_CH), lambda i: (0, 0)), vec, vec, vec],
        out_specs=pl.BlockSpec((tm, CONV_CH), lambda i: (i, 0)),
        out_shape=jax.ShapeDtypeStruct((l, CONV_CH), BF16),
        scratch_shapes=[pltpu.VMEM((CONV_HALO + tm, CONV_CH), F32), pltpu.VMEM((8, CONV_HALO + tm, CONV_CH), F32)],
        compiler_params=_params(("arbitrary",)),
    )(pa, pa, dw_w, dw_b.reshape(1, -1), ln_g.reshape(1, -1), ln_b.reshape(1, -1))


def _conv_bwd_ln(pa, ds, dw_w, dw_b, ln_g, ln_b, *, name):
    l = pa.shape[0]
    tm = _conv_tile(l)

    def body(a_ref, gt_ref, ds_ref, w_ref, b_ref, g_ref, bb_ref, dc_ref, gp_ref, buf, views):
        i = pl.program_id(0)

        @pl.when(i == 0)
        def _():
            gp_ref[...] = jnp.zeros_like(gp_ref)

        c, _ = _conv_core(a_ref[...], gt_ref[...], buf, views, w_ref[...], b_ref[...], i == 0)
        xh, r, y = _layer_norm(c, g_ref[...], bb_ref[...])
        sy = _sigmoid(y)
        dy = ds_ref[...] * (sy * (1.0 + y * (1.0 - sy)))
        dxh = dy * g_ref[...]
        dc = r * (dxh - jnp.mean(dxh, axis=-1, keepdims=True) - xh * jnp.mean(dxh * xh, axis=-1, keepdims=True))
        dc_ref[...] = dc
        for k in range(CONV_K):
            gp_ref[k:k + 1, :] += jnp.sum(dc * _tap(views, CONV_HALO - (CONV_K - 1) + k, tm), axis=0, keepdims=True)
        gp_ref[32:33, :] += jnp.sum(dc, axis=0, keepdims=True)
        gp_ref[33:34, :] += jnp.sum(dy * xh, axis=0, keepdims=True)
        gp_ref[34:35, :] += jnp.sum(dy, axis=0, keepdims=True)

    vec = pl.BlockSpec((1, CONV_CH), lambda i: (0, 0))
    return pl.pallas_call(
        body, name=name, grid=(l // tm,),
        in_specs=[pl.BlockSpec((tm, CONV_CH), lambda i: (i, 0)), pl.BlockSpec((tm, CONV_CH), lambda i: (i, 1)),
                  pl.BlockSpec((tm, CONV_CH), lambda i: (i, 0)),
                  pl.BlockSpec((CONV_K, CONV_CH), lambda i: (0, 0)), vec, vec, vec],
        out_specs=[pl.BlockSpec((tm, CONV_CH), lambda i: (i, 0)), pl.BlockSpec((40, CONV_CH), lambda i: (0, 0))],
        out_shape=[jax.ShapeDtypeStruct((l, CONV_CH), F32), jax.ShapeDtypeStruct((40, CONV_CH), F32)],
        scratch_shapes=[pltpu.VMEM((CONV_HALO + tm, CONV_CH), F32), pltpu.VMEM((8, CONV_HALO + tm, CONV_CH), F32)],
        compiler_params=_params(("arbitrary",)),
    )(pa, pa, ds, dw_w, dw_b.reshape(1, -1), ln_g.reshape(1, -1), ln_b.reshape(1, -1))


def _conv_bwd_in(pa, dc, dw_w, *, name):
    l = pa.shape[0]
    tm = _conv_tile(l)
    nt = l // tm

    def body(a_ref, gt_ref, dc_ref, w_ref, o_ref, buf, views):
        first = pl.program_id(0) == 0

        @pl.when(first)
        def _():
            buf[tm:tm + CONV_HALO, :] = jnp.zeros((CONV_HALO, CONV_CH), F32)

        @pl.when(jnp.logical_not(first))
        def _():
            buf[tm:tm + CONV_HALO, :] = buf[0:CONV_HALO, :]

        buf[0:tm, :] = dc_ref[...]
        w = w_ref[...]
        _phase_views(buf, views, tm, CONV_K - 1)
        dhc = jnp.zeros((tm, CONV_CH), F32)
        for k in range(CONV_K):
            dhc = dhc + w[k:k + 1, :] * _tap(views, CONV_K - 1 - k, tm)
        a = a_ref[...]
        sg = _sigmoid(gt_ref[...])
        o_ref[:, 0:CONV_CH] = (dhc * sg).astype(BF16)
        o_ref[:, CONV_CH:2 * CONV_CH] = (dhc * a * sg * (1.0 - sg)).astype(BF16)

    return pl.pallas_call(
        body, name=name, grid=(nt,),
        in_specs=[pl.BlockSpec((tm, CONV_CH), lambda i: (nt - 1 - i, 0)),
                  pl.BlockSpec((tm, CONV_CH), lambda i: (nt - 1 - i, 1)),
                  pl.BlockSpec((tm, CONV_CH), lambda i: (nt - 1 - i, 0)),
                  pl.BlockSpec((CONV_K, CONV_CH), lambda i: (0, 0))],
        out_specs=pl.BlockSpec((tm, 2 * CONV_CH), lambda i: (nt - 1 - i, 0)),
        out_shape=jax.ShapeDtypeStruct((l, 2 * CONV_CH), BF16),
        scratch_shapes=[pltpu.VMEM((tm + CONV_HALO, CONV_CH), F32), pltpu.VMEM((8, tm + CONV_HALO, CONV_CH), F32)],
        compiler_params=_params(("arbitrary",)),
    )(pa, pa, dc, dw_w)


def _pool_consts(tm, row0):
    lane = lax.broadcasted_iota(jnp.int32, (1, POOL_CH), 1)
    grp = lane // (POOL_CH // len(POOL_WINDOWS))
    win = jnp.where(grp == 0, 2.0, jnp.where(grp == 1, 4.0, jnp.where(grp == 2, 8.0, 16.0))).astype(F32)
    pos = (row0 + lax.broadcasted_iota(jnp.int32, (tm, 1), 0) - PAD).astype(F32)
    cnt = jnp.maximum(jnp.minimum(pos + 1.0, win), 1.0)
    return grp, cnt


def _pool_select(grp, s2, s4, s8, s16):
    return jnp.where(grp == 0, s2, jnp.where(grp == 1, s4, jnp.where(grp == 2, s8, s16)))


def _pool_fwd(pa, *, name):
    l = pa.shape[0]
    tm = _conv_tile(l)
    ext = POOL_HALO + tm

    def body(p_ref, o_ref, buf):
        i = pl.program_id(0)

        @pl.when(i == 0)
        def _():
            buf[0:POOL_HALO, :] = jnp.zeros((POOL_HALO, POOL_CH), F32)

        @pl.when(i > 0)
        def _():
            buf[0:POOL_HALO, :] = buf[tm:tm + POOL_HALO, :]

        p = p_ref[...]
        buf[POOL_HALO:ext, :] = p
        x = buf[...]
        s2 = x + pltpu.roll(x, 1, 0)
        s4 = s2 + pltpu.roll(s2, 2, 0)
        s8 = s4 + pltpu.roll(s4, 4, 0)
        s16 = s8 + pltpu.roll(s8, 8, 0)
        grp, cnt = _pool_consts(tm, i * tm)
        s = _pool_select(grp, s2, s4, s8, s16)[POOL_HALO:ext, :]
        o_ref[...] = (s / cnt - p).astype(BF16)

    return pl.pallas_call(
        body, name=name, grid=(l // tm,),
        in_specs=[pl.BlockSpec((tm, POOL_CH), lambda i: (i, 2))],
        out_specs=pl.BlockSpec((tm, POOL_CH), lambda i: (i, 0)),
        out_shape=jax.ShapeDtypeStruct((l, POOL_CH), BF16),
        scratch_shapes=[pltpu.VMEM((ext, POOL_CH), F32)],
        compiler_params=_params(("arbitrary",)),
    )(pa)


def _pool_bwd(dpooled, *, name):
    l = dpooled.shape[0]
    tm = _conv_tile(l)
    nt = l // tm
    ext = tm + POOL_HALO

    def body(d_ref, o_ref, buf):
        i = pl.program_id(0)

        @pl.when(i == 0)
        def _():
            buf[tm:ext, :] = jnp.zeros((POOL_HALO, POOL_CH), F32)

        @pl.when(i > 0)
        def _():
            buf[tm:ext, :] = buf[0:POOL_HALO, :]

        d = d_ref[...]
        grp, cnt = _pool_consts(tm, (nt - 1 - i) * tm)
        buf[0:tm, :] = d / cnt
        x = buf[...]
        s2 = x + pltpu.roll(x, ext - 1, 0)
        s4 = s2 + pltpu.roll(s2, ext - 2, 0)
        s8 = s4 + pltpu.roll(s4, ext - 4, 0)
        s16 = s8 + pltpu.roll(s8, ext - 8, 0)
        s = _pool_select(grp, s2, s4, s8, s16)[0:tm, :]
        o_ref[...] = (s - d).astype(BF16)

    return pl.pallas_call(
        body, name=name, grid=(nt,),
        in_specs=[pl.BlockSpec((tm, POOL_CH), lambda i: (nt - 1 - i, 0))],
        out_specs=pl.BlockSpec((tm, POOL_CH), lambda i: (nt - 1 - i, 0)),
        out_shape=jax.ShapeDtypeStruct((l, POOL_CH), BF16),
        scratch_shapes=[pltpu.VMEM((ext, POOL_CH), F32)],
        compiler_params=_params(("arbitrary",)),
    )(dpooled)


ATT_TQ = 256
ATT_TK = 2 * BLOCK
ATT_SUB = ATT_TK // BLOCK
LOG2E = 1.4426950408889634
LN2 = 0.6931471805599453
Q_SCALE = HEAD_DIM ** -0.5 * LOG2E
ATT_CUT = 160.0


def _tri_ones():
    r = lax.broadcasted_iota(jnp.int32, (2 * BLOCK, 2 * BLOCK), 0) % BLOCK
    c = lax.broadcasted_iota(jnp.int32, (2 * BLOCK, 2 * BLOCK), 1)
    return jnp.where((c >= BLOCK) | (r > c), 1.0, 0.0).astype(BF16)


def _split_dot(x, rhs):
    hi = x.astype(BF16)
    lo = (x - hi.astype(F32)).astype(BF16)
    return jnp.dot(jnp.concatenate([hi, lo], axis=1), rhs, preferred_element_type=F32)


def _scores(q, kt, qpos, base, masked):
    z = lax.dot_general(q, kt, (((1,), (1,)), ((), ())), preferred_element_type=F32)
    sp = jnp.log2(1.0 + jnp.exp2(-jnp.abs(z)))
    lb = jnp.minimum(z, 0.0) - sp
    lk = lb - z
    valid = None
    if masked:
        kpos = base + lax.broadcasted_iota(jnp.int32, (1, z.shape[1]), 1)
        valid = (kpos < qpos) & (kpos >= PAD)
        lk = jnp.where(valid, lk, 0.0)
    return lk, lb, valid


def _suffix(x, tri, carry):
    wts = [_split_dot(x[:, b * BLOCK:(b + 1) * BLOCK], tri) for b in range(ATT_SUB)]
    offs = [None] * ATT_SUB
    s = carry
    for b in reversed(range(ATT_SUB)):
        offs[b] = wts[b][:, :BLOCK] + s
        s = s + wts[b][:, BLOCK:]
    return jnp.concatenate(offs, axis=1), s


def _walk_tiles(i, tq, step):
    t_top = ((i + 1) * tq - 1) // ATT_TK
    t_diag = (i * tq) // ATT_TK
    n_plain = jnp.maximum(t_diag - 1, 0)

    def masked(jj, top):
        return step(t_top - jj, True)

    def live(carry):
        return (carry[0] < n_plain) & (carry[1] > -ATT_CUT)

    def plain(carry):
        return carry[0] + 1, step(t_diag - 1 - carry[0], False)

    top = lax.fori_loop(0, t_top - t_diag + 1, masked, jnp.float32(0.0))
    _, top = lax.while_loop(live, plain, (jnp.int32(0), top))

    @pl.when((t_diag > 0) & (top > -ATT_CUT))
    def _():
        step(0, True)


def _tile_base(t):
    base = t * ATT_TK
    return base if isinstance(base, int) else pl.multiple_of(base, BLOCK)


def _attn_fwd(qkv, *, name, gather=None):
    l = qkv.shape[0]
    tq = ATT_TQ
    nq = l // tq
    assert l % tq == 0 and l % ATT_TK == 0

    def body(*refs):
        if gather is None:
            q_ref, k_ref, v_ref, o_ref, o32_ref, acc_ref, r_ref = refs
        else:
            q_ref, k_ref, v_ref, x_hbm, o_ref, o32_ref, got_hbm, acc_ref, r_ref = refs[:9]
            exchange = _Gather(x_hbm, got_hbm, *refs[9:])
            first = (pl.program_id(0) == 0) & (pl.program_id(1) == 0)
            last = (pl.program_id(0) == HEADS - 1) & (pl.program_id(1) == nq - 1)
            pl.when(first)(exchange.start)
        i = pl.program_id(1)
        acc_ref[...] = jnp.zeros_like(acc_ref)
        r_ref[...] = jnp.zeros_like(r_ref)
        q = q_ref[...]
        qpos = i * tq + lax.broadcasted_iota(jnp.int32, (tq, 1), 0)
        tri = _tri_ones()

        def step(t, masked):
            base = _tile_base(t)
            lk, lb, valid = _scores(q, k_ref[pl.ds(base, ATT_TK), :], qpos, base, masked)
            off, r_new = _suffix(lk, tri, r_ref[...])
            a = jnp.exp2(lb + off)
            if masked:
                a = jnp.where(valid, a, 0.0)
            acc_ref[...] += jnp.dot(a.astype(BF16), v_ref[pl.ds(base, ATT_TK), :], preferred_element_type=F32)
            r_ref[...] = r_new
            return jnp.max(r_new)

        _walk_tiles(i, tq, step)
        o_ref[...] = acc_ref[...].astype(BF16)
        o32_ref[...] = acc_ref[...]
        if gather is not None:
            pl.when(last)(exchange.finish)

    tile = pl.BlockSpec((tq, HEAD_DIM), lambda h, i: (i, h))
    hbm = pl.BlockSpec(memory_space=pl.ANY)
    hosted = gather is not None
    return pl.pallas_call(
        body, name=name, grid=(HEADS, nq),
        in_specs=[tile,
                  pl.BlockSpec((l, HEAD_DIM), lambda h, i: (0, HEADS + h)),
                  pl.BlockSpec((l, HEAD_DIM), lambda h, i: (0, 2 * HEADS + h))] + [hbm] * hosted,
        out_specs=[tile, tile] + [hbm] * hosted,
        out_shape=[jax.ShapeDtypeStruct((l, HEADS * HEAD_DIM), BF16),
                   jax.ShapeDtypeStruct((l, HEADS * HEAD_DIM), F32)]
        + ([jax.ShapeDtypeStruct((N_DEV,) + gather.shape, gather.dtype)] if hosted else []),
        scratch_shapes=[pltpu.VMEM((tq, HEAD_DIM), F32), pltpu.VMEM((tq, BLOCK), F32)] + EXCHANGE_SEMS * hosted,
        compiler_params=_params(("arbitrary", "arbitrary")),
    )(qkv, qkv, qkv, *([gather] * hosted))


def _attn_bwd(qkv, att, datt, *, name, scatter=None):
    l = qkv.shape[0]
    tq = ATT_TQ
    nq = l // tq
    assert l % tq == 0 and l % ATT_TK == 0

    def body(*refs):
        if scatter is None:
            q_ref, k_ref, v_ref, o_ref, do_ref, dq_ref, dk_hbm, dv_hbm, dk_acc, dv_acc, dq_acc, r_ref, s_ref, sem = refs
        else:
            (q_ref, k_ref, v_ref, o_ref, do_ref, send_hbm, dq_ref, dk_hbm, dv_hbm, recv_hbm,
             dk_acc, dv_acc, dq_acc, r_ref, s_ref, sem) = refs[:16]
            exchange = _Scatter(send_hbm, recv_hbm, *refs[16:])
            pl.when((pl.program_id(0) == 0) & (pl.program_id(1) == 0))(exchange.start)
        h = pl.program_id(0)
        i = pl.program_id(1)

        @pl.when(i == 0)
        def _():
            dk_acc[...] = jnp.zeros_like(dk_acc)
            dv_acc[...] = jnp.zeros_like(dv_acc)

        dq_acc[...] = jnp.zeros_like(dq_acc)
        r_ref[...] = jnp.zeros_like(r_ref)
        s_ref[...] = jnp.zeros_like(s_ref)
        q = q_ref[...]
        do = do_ref[...]
        ptot = jnp.sum(do.astype(F32) * o_ref[...], axis=-1, keepdims=True)
        qpos = i * tq + lax.broadcasted_iota(jnp.int32, (tq, 1), 0)
        tri = _tri_ones()

        def step(t, masked):
            base = _tile_base(t)
            kt = k_ref[pl.ds(base, ATT_TK), :]
            vt = v_ref[pl.ds(base, ATT_TK), :]
            lk, lb, valid = _scores(q, kt, qpos, base, masked)
            off, r_new = _suffix(lk, tri, r_ref[...])
            a = jnp.exp2(lb + off)
            if masked:
                a = jnp.where(valid, a, 0.0)
            ab = a.astype(BF16)
            da = lax.dot_general(do, vt, (((1,), (1,)), ((), ())), preferred_element_type=F32)
            p = ab.astype(F32) * da
            poff, s_new = _suffix(p, tri, s_ref[...])
            dz = (p - jnp.exp2(lb) * (ptot - poff)) * LN2
            if masked:
                dz = jnp.where(valid, dz, 0.0)
            dzb = dz.astype(BF16)
            dq_acc[...] += jnp.dot(dzb, kt, preferred_element_type=F32)
            dk_acc[pl.ds(base, ATT_TK), :] += lax.dot_general(dzb, q, (((0,), (0,)), ((), ())),
                                                              preferred_element_type=F32)
            dv_acc[pl.ds(base, ATT_TK), :] += lax.dot_general(ab, do, (((0,), (0,)), ((), ())),
                                                              preferred_element_type=F32)
            r_ref[...] = r_new
            s_ref[...] = s_new
            return jnp.max(r_new)

        _walk_tiles(i, tq, step)
        dq_ref[...] = (dq_acc[...] * Q_SCALE).astype(BF16)

        @pl.when(i == nq - 1)
        def _():
            ck = pltpu.make_async_copy(dk_acc, dk_hbm.at[h], sem.at[0])
            cv = pltpu.make_async_copy(dv_acc, dv_hbm.at[h], sem.at[1])
            ck.start()
            cv.start()
            ck.wait()
            cv.wait()

        if scatter is not None:
            pl.when((h == HEADS - 1) & (i == nq - 1))(exchange.finish)

    tile = pl.BlockSpec((tq, HEAD_DIM), lambda h, i: (i, h))
    hbm = pl.BlockSpec(memory_space=pl.ANY)
    hosted = scatter is not None
    return pl.pallas_call(
        body, name=name, grid=(HEADS, nq),
        in_specs=[tile,
                  pl.BlockSpec((l, HEAD_DIM), lambda h, i: (0, HEADS + h)),
                  pl.BlockSpec((l, HEAD_DIM), lambda h, i: (0, 2 * HEADS + h)),
                  tile, tile] + [hbm] * hosted,
        out_specs=[tile, hbm, hbm] + [hbm] * hosted,
        out_shape=[jax.ShapeDtypeStruct((l, HEADS * HEAD_DIM), BF16),
                   jax.ShapeDtypeStruct((HEADS, l, HEAD_DIM), F32), jax.ShapeDtypeStruct((HEADS, l, HEAD_DIM), F32)]
        + ([jax.ShapeDtypeStruct(scatter.shape, scatter.dtype)] if hosted else []),
        scratch_shapes=[pltpu.VMEM((l, HEAD_DIM), F32), pltpu.VMEM((l, HEAD_DIM), F32),
                        pltpu.VMEM((tq, HEAD_DIM), F32), pltpu.VMEM((tq, BLOCK), F32), pltpu.VMEM((tq, BLOCK), F32),
                        pltpu.SemaphoreType.DMA((2,))] + EXCHANGE_SEMS * hosted,
        compiler_params=_params(("arbitrary", "arbitrary")),
    )(qkv, qkv, qkv, att, datt, *([scatter] * hosted))


MIX_TM = 256


def _mix_branches(s_ref, p_ref, t_ref, g_ref, wa_ref, wb_ref, wc_ref, ba_ref, sc_ref, d):
    ya = jnp.dot(s_ref[...], wa_ref[...], preferred_element_type=F32) + ba_ref[...]
    yb0 = jnp.dot(p_ref[...], wb_ref[...], preferred_element_type=F32)
    yc = jnp.dot(t_ref[...], wc_ref[...], preferred_element_type=F32)
    g0 = _sigmoid(g_ref[:, 0:d].astype(F32))
    g1 = _sigmoid(g_ref[:, d:2 * d].astype(F32))
    g2 = _sigmoid(g_ref[:, 2 * d:3 * d].astype(F32))
    return ya, yb0, yc, g0, g1, g2


def _mix_specs(tm, d):
    row = lambda w: pl.BlockSpec((tm, w), lambda i: (i, 0))
    full = lambda r: pl.BlockSpec((r, d), lambda i: (0, 0))
    return [row(CONV_CH), row(POOL_CH), row(HEADS * HEAD_DIM), row(3 * d),
            full(CONV_CH), full(POOL_CH), full(HEADS * HEAD_DIM), full(1), full(1)]


def _mix_fwd(s, pooled, att, gates, wa, wb, wc, ba, scale, *, name):
    l, d = s.shape[0], wa.shape[1]
    tm = _tile(l, MIX_TM, 128)

    def body(s_ref, p_ref, t_ref, g_ref, wa_ref, wb_ref, wc_ref, ba_ref, sc_ref, o_ref):
        ya, yb0, yc, g0, g1, g2 = _mix_branches(s_ref, p_ref, t_ref, g_ref, wa_ref, wb_ref, wc_ref, ba_ref, sc_ref, d)
        o_ref[...] = (g0 * ya + g1 * (yb0 * sc_ref[...]) + g2 * yc).astype(BF16)

    return pl.pallas_call(
        body, name=name, grid=(l // tm,), in_specs=_mix_specs(tm, d),
        out_specs=pl.BlockSpec((tm, d), lambda i: (i, 0)),
        out_shape=jax.ShapeDtypeStruct((l, d), BF16),
        compiler_params=_params(("parallel",)),
    )(s, pooled, att, gates, wa, wb, wc, ba.reshape(1, d), scale.reshape(1, d))


def _mix_bwd(s, pooled, att, gates, wa, wb, wc, ba, scale, dmixed, *, name):
    l, d = s.shape[0], wa.shape[1]
    tm = _tile(l, MIX_TM, 128)

    def body(s_ref, p_ref, t_ref, g_ref, wa_ref, wb_ref, wc_ref, ba_ref, sc_ref, dm_ref,
             dg_ref, dya_ref, dyb_ref, dyc_ref, vec_ref):
        @pl.when(pl.program_id(0) == 0)
        def _():
            vec_ref[...] = jnp.zeros_like(vec_ref)

        ya, yb0, yc, g0, g1, g2 = _mix_branches(s_ref, p_ref, t_ref, g_ref, wa_ref, wb_ref, wc_ref, ba_ref, sc_ref, d)
        dm = dm_ref[...].astype(F32)
        sc = sc_ref[...]
        dg_ref[:, 0:d] = (dm * ya * g0 * (1.0 - g0)).astype(BF16)
        dg_ref[:, d:2 * d] = (dm * (yb0 * sc) * g1 * (1.0 - g1)).astype(BF16)
        dg_ref[:, 2 * d:3 * d] = (dm * yc * g2 * (1.0 - g2)).astype(BF16)
        dya = dm * g0
        dyb = dm * g1
        dya_ref[...] = dya.astype(BF16)
        dyb_ref[...] = (dyb * sc).astype(BF16)
        dyc_ref[...] = (dm * g2).astype(BF16)
        vec_ref[0:1, :] += jnp.sum(dya, axis=0, keepdims=True)
        vec_ref[1:2, :] += jnp.sum(dyb * yb0, axis=0, keepdims=True)

    row = lambda w: pl.BlockSpec((tm, w), lambda i: (i, 0))
    outs = pl.pallas_call(
        body, name=name, grid=(l // tm,), in_specs=_mix_specs(tm, d) + [row(d)],
        out_specs=[row(3 * d), row(d), row(d), row(d), pl.BlockSpec((8, d), lambda i: (0, 0))],
        out_shape=[jax.ShapeDtypeStruct((l, 3 * d), BF16), jax.ShapeDtypeStruct((l, d), BF16),
                   jax.ShapeDtypeStruct((l, d), BF16), jax.ShapeDtypeStruct((l, d), BF16),
                   jax.ShapeDtypeStruct((8, d), F32)],
        compiler_params=_params(("arbitrary",)),
    )(s, pooled, att, gates, wa, wb, wc, ba.reshape(1, d), scale.reshape(1, d), dmixed)
    return outs


FFN_TC = 512
_GELU_C = 0.7978845608028654
_GELU_A = 0.044715


def _gelu(x):
    th = jnp.tanh(_GELU_C * (x + _GELU_A * x * x * x))
    return 0.5 * x * (1.0 + th), th


def _gelu_grad(x, th):
    return 0.5 * (1.0 + th) + 0.5 * x * (1.0 - th * th) * _GELU_C * (1.0 + 3.0 * _GELU_A * x * x)


FFN_CH = 32


def _ffn_taps(win):
    return (pltpu.roll(win, 2, 0)[FFN_HALO:, :], pltpu.roll(win, 1, 0)[FFN_HALO:, :], win[FFN_HALO:, :])


def _ffn_conv(taps, w, b):
    return b + w[0:1, :] * taps[0] + w[1:2, :] * taps[1] + w[2:3, :] * taps[2]


def _fold8(x):
    acc = x[0:8, :]
    for r in range(8, x.shape[0], 8):
        acc = acc + x[r:r + 8, :]
    return acc


def _ffn_fwd(ug, uv, wg, wv, bg, bv, *, name):
    l, f = ug.shape
    tm = _conv_tile(l)
    tc = _tile(f, FFN_TC, 128)
    ext = FFN_HALO + tm

    def body(ug_ref, uv_ref, wg_ref, wv_ref, bg_ref, bv_ref, o_ref, gc_ref, vc_ref, bufg, bufv):
        i = pl.program_id(1)
        for buf, u_ref in ((bufg, ug_ref), (bufv, uv_ref)):
            @pl.when(i == 0)
            def _():
                buf[0:FFN_HALO, :] = jnp.zeros((FFN_HALO, tc), F32)

            @pl.when(i > 0)
            def _():
                buf[0:FFN_HALO, :] = buf[tm:ext, :]

            buf[FFN_HALO:ext, :] = u_ref[...].astype(F32)
        wg, wv, bg_, bv_ = wg_ref[...], wv_ref[...], bg_ref[...], bv_ref[...]

        def chunk(c, carry):
            r0 = pl.multiple_of(c * FFN_CH, FFN_CH)
            gc = _ffn_conv(_ffn_taps(bufg[pl.ds(r0, FFN_HALO + FFN_CH), :]), wg, bg_)
            vc = _ffn_conv(_ffn_taps(bufv[pl.ds(r0, FFN_HALO + FFN_CH), :]), wv, bv_)
            o_ref[pl.ds(r0, FFN_CH), :] = (_gelu(gc)[0] * vc).astype(BF16)
            gc_ref[pl.ds(r0, FFN_CH), :] = gc.astype(BF16)
            vc_ref[pl.ds(r0, FFN_CH), :] = vc.astype(BF16)
            return carry

        lax.fori_loop(0, tm // FFN_CH, chunk, 0)

    assert tm % FFN_CH == 0
    tile = pl.BlockSpec((tm, tc), lambda j, i: (i, j))
    wspec = pl.BlockSpec((FFN_K, tc), lambda j, i: (0, j))
    bspec = pl.BlockSpec((1, tc), lambda j, i: (0, j))
    return pl.pallas_call(
        body, name=name, grid=(f // tc, l // tm),
        in_specs=[tile, tile, wspec, wspec, bspec, bspec], out_specs=[tile, tile, tile],
        out_shape=[jax.ShapeDtypeStruct((l, f), BF16)] * 3,
        scratch_shapes=[pltpu.VMEM((ext, tc), F32), pltpu.VMEM((ext, tc), F32)],
        compiler_params=_params(("parallel", "arbitrary")),
    )(ug, uv, wg, wv, bg.reshape(1, f), bv.reshape(1, f))


def _ffn_bwd(ug, uv, gc, vc, wg, wv, dact, *, name):
    l, f = ug.shape
    tm = _conv_tile(l)
    tc = _tile(f, FFN_TC, 128)
    nt = l // tm
    ext = FFN_HALO + tm
    win_rows = FFN_CH + FFN_HALO

    def body(ug_ref, uv_ref, gc_ref, vc_ref, wg_ref, wv_ref, da_ref,
             dug_ref, duv_ref, gg_ref, gv_ref, dbufg, dbufv, gaccg, gaccv):
        i = pl.program_id(1)

        @pl.when(i == 0)
        def _():
            gg_ref[...] = jnp.zeros_like(gg_ref)
            gv_ref[...] = jnp.zeros_like(gv_ref)

        for dbuf, gacc in ((dbufg, gaccg), (dbufv, gaccv)):
            @pl.when(i == 0)
            def _():
                dbuf[tm:ext, :] = jnp.zeros((FFN_HALO, tc), F32)

            @pl.when(i > 0)
            def _():
                dbuf[tm:ext, :] = dbuf[0:FFN_HALO, :]

            gacc[...] = jnp.zeros_like(gacc)
        wg, wv = wg_ref[...], wv_ref[...]

        def chunk(cc, carry):
            r0 = pl.multiple_of((tm // FFN_CH - 1 - cc) * FFN_CH, FFN_CH)
            rows = pl.ds(r0, FFN_CH)
            gcv = gc_ref[rows, :].astype(F32)
            ge, th = _gelu(gcv)
            da = da_ref[rows, :].astype(F32)
            for dc, u_ref, w, dbuf, du_ref, gacc in (
                    (da * vc_ref[rows, :].astype(F32) * _gelu_grad(gcv, th), ug_ref, wg, dbufg, dug_ref, gaccg),
                    (da * ge, uv_ref, wv, dbufv, duv_ref, gaccv)):
                dbuf[rows, :] = dc
                dwin = dbuf[pl.ds(r0, win_rows), :]
                d1 = pltpu.roll(dwin, win_rows - 1, 0)[0:FFN_CH, :]
                d2 = pltpu.roll(dwin, win_rows - 2, 0)[0:FFN_CH, :]
                du_ref[rows, :] = (w[2:3, :] * dc + w[1:2, :] * d1 + w[0:1, :] * d2).astype(BF16)
                u = u_ref[rows, :].astype(F32)
                gacc[0:8, :] += _fold8(d2 * u)
                gacc[8:16, :] += _fold8(d1 * u)
                gacc[16:24, :] += _fold8(dc * u)
                gacc[24:32, :] += _fold8(dc)
            return carry

        lax.fori_loop(0, tm // FFN_CH, chunk, 0)
        for gacc, gp_ref in ((gaccg, gg_ref), (gaccv, gv_ref)):
            for k in range(FFN_K + 1):
                gp_ref[k:k + 1, :] += jnp.sum(gacc[8 * k:8 * k + 8, :], axis=0, keepdims=True)

    assert tm % FFN_CH == 0
    tile = pl.BlockSpec((tm, tc), lambda j, i: (nt - 1 - i, j))
    wspec = pl.BlockSpec((FFN_K, tc), lambda j, i: (0, j))
    gspec = pl.BlockSpec((8, tc), lambda j, i: (0, j))
    return pl.pallas_call(
        body, name=name, grid=(f // tc, nt),
        in_specs=[tile, tile, tile, tile, wspec, wspec, tile],
        out_specs=[tile, tile, gspec, gspec],
        out_shape=[jax.ShapeDtypeStruct((l, f), BF16), jax.ShapeDtypeStruct((l, f), BF16),
                   jax.ShapeDtypeStruct((8, f), F32), jax.ShapeDtypeStruct((8, f), F32)],
        scratch_shapes=[pltpu.VMEM((ext, tc), F32), pltpu.VMEM((ext, tc), F32),
                        pltpu.VMEM((32, tc), F32), pltpu.VMEM((32, tc), F32)],
        compiler_params=_params(("parallel", "arbitrary")),
    )(ug, uv, gc, vc, wg, wv, dact)


def _adamw(parts, w, m, v, *, name):
    r = w.shape[0]
    tr = _tile(r, PACK_ROWS, 16)
    c1 = 1.0 / (1.0 - ADAM_B1 ** ADAM_STEP)
    c2 = 1.0 / (1.0 - ADAM_B2 ** ADAM_STEP)

    def body(p_ref, w_ref, m_ref, v_ref, g_ref, d_ref, nm_ref, nv_ref):
        g = p_ref[0].astype(F32)
        for k in range(1, N_DEV):
            g = g + p_ref[k].astype(F32)
        nm = ADAM_B1 * m_ref[...] + (1.0 - ADAM_B1) * g
        nv = ADAM_B2 * v_ref[...] + (1.0 - ADAM_B2) * (g * g)
        g_ref[...] = g
        nm_ref[...] = nm
        nv_ref[...] = nv
        d_ref[...] = -ADAM_LR * ((nm * c1) / (jnp.sqrt(nv * c2) + ADAM_EPS) + ADAM_WD * w_ref[...])

    tile = pl.BlockSpec((tr, 128), lambda i: (i, 0))
    return pl.pallas_call(
        body, name=name, grid=(r // tr,),
        in_specs=[pl.BlockSpec((N_DEV, tr, 128), lambda i: (0, i, 0)), tile, tile, tile],
        out_specs=[tile, tile, tile, tile],
        out_shape=[jax.ShapeDtypeStruct((r, 128), F32)] * 4,
        compiler_params=_params(("parallel",)),
    )(parts, w, m, v)


def _place():
    return lax.axis_index("x"), lax.axis_index("y"), lax.axis_index("c")


EXCHANGE_SEMS = [pltpu.SemaphoreType.DMA((7,)), pltpu.SemaphoreType.DMA((7,)), pltpu.SemaphoreType.DMA]


class _Gather:
    def __init__(self, x_ref, out_ref, send_sems, recv_sems, local_sem):
        self.x_ref, self.out_ref, self.send_sems, self.recv_sems, self.local_sem = (
            x_ref, out_ref, send_sems, recv_sems, local_sem)

    def _parts(self):
        xx, yy, cc = _place()
        me, sibling = (xx, yy, cc), (xx, yy, 1 - cc)
        chips = [(1 - xx, yy), (xx, 1 - yy), (1 - xx, 1 - yy)]

        def slot(px, py, pc):
            return self.out_ref.at[4 * px + 2 * py + pc]

        def copy(k, block, to, src=None):
            return pltpu.make_async_remote_copy(
                src_ref=slot(*block) if src is None else src, dst_ref=slot(*block),
                send_sem=self.send_sems.at[k], recv_sem=self.recv_sems.at[k], device_id=to, device_id_type=MESH)

        mine = pltpu.make_async_copy(self.x_ref, slot(*me), self.local_sem)
        first = [copy(0, me, sibling, src=self.x_ref)]
        first += [copy(1 + j, me, (*chip, cc), src=self.x_ref) for j, chip in enumerate(chips)]
        return cc, me, sibling, chips, copy, mine, first

    def start(self):
        _, _, _, _, _, mine, first = self._parts()
        mine.start()
        for cp in first:
            cp.start()

    def finish(self):
        cc, me, sibling, chips, copy, mine, first = self._parts()
        passed = [copy(4 + j, (*chip, cc), sibling) for j, chip in enumerate(chips)]
        for j, chip in enumerate(chips):
            copy(1 + j, (*chip, cc), me).wait_recv()
            passed[j].start()
        copy(0, sibling, me).wait_recv()
        for j, chip in enumerate(chips):
            copy(4 + j, (*chip, 1 - cc), me).wait_recv()
        for cp in first + passed:
            cp.wait_send()
        mine.wait()


class _Scatter:
    def __init__(self, s_ref, r_ref, send_sems, recv_sems, local_sem):
        self.s_ref, self.r_ref, self.send_sems, self.recv_sems, self.local_sem = (
            s_ref, r_ref, send_sems, recv_sems, local_sem)

    def _parts(self):
        xx, yy, cc = _place()
        me = 4 * xx + 2 * yy + cc
        local = pltpu.make_async_copy(self.s_ref.at[me], self.r_ref.at[me], self.local_sem)
        copies = []
        for m in range(1, N_DEV):
            px = 1 - xx if m & 4 else xx
            py = 1 - yy if m & 2 else yy
            pc = 1 - cc if m & 1 else cc
            copies.append(pltpu.make_async_remote_copy(
                src_ref=self.s_ref.at[4 * px + 2 * py + pc], dst_ref=self.r_ref.at[me],
                send_sem=self.send_sems.at[m - 1], recv_sem=self.recv_sems.at[m - 1],
                device_id=(px, py, pc), device_id_type=MESH))
        return local, copies

    def start(self):
        local, copies = self._parts()
        local.start()
        for cp in copies:
            cp.start()

    def finish(self):
        local, copies = self._parts()
        for cp in copies:
            cp.wait_recv()
        for cp in copies:
            cp.wait_send()
        local.wait()


def _exchange_call(kind, x, out_shape, *, name):
    def body(x_ref, out_ref, send_sems, recv_sems, local_sem):
        ex = kind(x_ref, out_ref, send_sems, recv_sems, local_sem)
        ex.start()
        ex.finish()

    return pl.pallas_call(
        body, name=name,
        in_specs=[pl.BlockSpec(memory_space=pl.ANY)], out_specs=pl.BlockSpec(memory_space=pl.ANY),
        out_shape=jax.ShapeDtypeStruct(out_shape, x.dtype), scratch_shapes=EXCHANGE_SEMS,
    )(x)


def _all_gather(x, *, name):
    return _exchange_call(_Gather, x, (N_DEV,) + x.shape, name=name)


def _all_to_all(send, *, name):
    return _exchange_call(_Scatter, send, send.shape, name=name)


def _as_rows(a, lead, dtype):
    a = a.astype(dtype)
    size = 1
    for s in a.shape[len(lead):]:
        size *= s
    if size % PACK_ALIGN:
        a = jnp.pad(a.reshape(lead + (size,)), [(0, 0)] * len(lead) + [(0, (-size) % PACK_ALIGN)])
    return a.reshape(lead + (-1, 128))


def _pack(arrays, dtype):
    buf = jnp.concatenate([_as_rows(a, (), dtype) for a in arrays], axis=0)
    return jnp.pad(buf, ((0, (-buf.shape[0]) % PACK_ROWS), (0, 0)))


def _pack_pieces(arrays, dtype):
    buf = jnp.concatenate([_as_rows(a, (N_DEV,), dtype) for a in arrays], axis=1)
    return jnp.pad(buf, ((0, 0), (0, (-buf.shape[1]) % PACK_ROWS), (0, 0)))


def _unpack(buf, shapes, lead=()):
    out, row = [], 0
    for shp in shapes:
        size = 1
        for s in shp:
            size *= s
        rows = (size + (-size) % PACK_ALIGN) // 128
        part = buf[..., row:row + rows, :]
        if size % PACK_ALIGN:
            part = part.reshape(lead + (rows * 128,))[..., :size]
        out.append(part.reshape(lead + tuple(shp)))
        row += rows
    return out


def _unshard(g, axis):
    g = jnp.moveaxis(g, 0, axis)
    shp = list(g.shape)
    return g.reshape(shp[:axis] + [shp[axis] * shp[axis + 1]] + shp[axis + 2:])


def _pieces(full, axis):
    shp = list(full.shape)
    g = full.reshape(shp[:axis] + [N_DEV, shp[axis] // N_DEV] + shp[axis + 1:])
    return jnp.moveaxis(g, axis, 0)


SHARDED = (("meta", 1), ("w_in", 2), ("conv_dw_w", 2), ("w_conv_out", 2), ("w_pool_grp", 3), ("w_attn_out", 2),
           ("w_o", 1), ("w_up", 2), ("ffn_dw_w", 2), ("w_down", 1))
MATRICES = ("w_in", "w_conv_out", "w_pool_grp", "w_attn_out", "w_o", "w_up", "w_down")
REPLICATED = ("norm1", "conv_dw_b", "conv_ln_g", "conv_ln_b", "b_conv_out", "pool_scale", "norm2", "ffn_dw_b",
              "final_norm")
WEIGHTS = ("meta", "norm1", "w_in", "conv_dw_w", "conv_dw_b", "conv_ln_g", "conv_ln_b", "w_conv_out", "b_conv_out",
           "w_pool_grp", "pool_scale", "w_attn_out", "w_o", "norm2", "w_up", "ffn_dw_w", "ffn_dw_b", "w_down",
           "final_norm")


def _block_diag(w_grp):
    g, gc, od = w_grp.shape
    out = jnp.zeros((g * gc, g * od), w_grp.dtype)
    for i in range(g):
        out = out.at[i * gc:(i + 1) * gc, i * od:(i + 1) * od].set(w_grp[i])
    return out


def _block_diag_grad(gw, g):
    gc, od = gw.shape[0] // g, gw.shape[1] // g
    return jnp.stack([gw[i * gc:(i + 1) * gc, i * od:(i + 1) * od] for i in range(g)])


C_CONV = 2 * CONV_CH
C_POOL = C_CONV + POOL_CH
C_ATT = HEADS * HEAD_DIM
C_QKV = C_POOL + 3 * C_ATT


def _layer_fwd(h, p, tag, gather=None):
    d = h.shape[1]
    w_in = p["w_in"]
    hn = _rms_fwd(h, p["norm1"], name=f"rms1_{tag}")
    pa = _mm(hn, w_in[:, :C_POOL], out_dtype=F32, name=f"proj_a_{tag}")
    q_scale = jnp.concatenate([jnp.full((C_ATT,), Q_SCALE, F32), jnp.ones((2 * C_ATT,), F32)])
    qkv = _mm(hn, w_in[:, C_POOL:C_QKV], out_dtype=BF16, col_scale=q_scale, name=f"proj_qkv_{tag}")
    gates = _mm(hn, w_in[:, C_QKV:], out_dtype=BF16, name=f"proj_g_{tag}")
    s = _conv_fwd(pa, p["conv_dw_w"], p["conv_dw_b"], p["conv_ln_g"], p["conv_ln_b"], name=f"conv_{tag}")
    pooled = _pool_fwd(pa, name=f"pool_{tag}")
    att, att32, *got = _attn_fwd(qkv, name=f"attn_{tag}", gather=gather)
    wb = _block_diag(p["w_pool_grp"])
    mixed = _mix_fwd(s, pooled, att, gates, p["w_conv_out"], wb, p["w_attn_out"], p["b_conv_out"], p["pool_scale"],
                     name=f"mix_{tag}")
    h1 = _mm(mixed, p["w_o"], out_dtype=F32, res=h, mask_rows=True, name=f"wo_{tag}")
    hn2 = _rms_fwd(h1, p["norm2"], name=f"rms2_{tag}")
    f = p["w_up"].shape[1] // 2
    ug = _mm(hn2, p["w_up"][:, :f], out_dtype=BF16, name=f"up_g_{tag}")
    uv = _mm(hn2, p["w_up"][:, f:], out_dtype=BF16, name=f"up_v_{tag}")
    act, gc, vc = _ffn_fwd(ug, uv, p["ffn_dw_w"][:, :f], p["ffn_dw_w"][:, f:], p["ffn_dw_b"][:f], p["ffn_dw_b"][f:],
                           name=f"ffn_{tag}")
    h2 = _mm(act, p["w_down"], out_dtype=F32, res=h1, mask_rows=True, name=f"down_{tag}")
    saved = dict(h=h, hn=hn, pa=pa, qkv=qkv, gates=gates, s=s, pooled=pooled, att=att, att32=att32, wb=wb, mixed=mixed,
                 h1=h1,
                 hn2=hn2, ug=ug, uv=uv, gc=gc, vc=vc, act=act)
    return h2, saved, (got[0] if got else None)


def _layer_bwd(dh2, dh2b, p, sv, tag, scatter=None):
    g = {}
    f = p["w_up"].shape[1] // 2
    dact = _mm(dh2b, p["w_down"].T, out_dtype=BF16, name=f"b_down_{tag}")
    g["w_down"] = _mm_tn(sv["act"], dh2b, name=f"g_down_{tag}")
    dug, duv, gpg, gpv = _ffn_bwd(sv["ug"], sv["uv"], sv["gc"], sv["vc"], p["ffn_dw_w"][:, :f], p["ffn_dw_w"][:, f:],
                                  dact, name=f"b_ffn_{tag}")
    g["ffn_dw_w"] = jnp.concatenate([gpg[0:FFN_K], gpv[0:FFN_K]], axis=1)
    g["ffn_dw_b"] = jnp.concatenate([gpg[FFN_K], gpv[FFN_K]])
    w_up_t = p["w_up"].T
    dhn2 = _mm(dug, w_up_t[:f], out_dtype=F32, name=f"b_up_g_{tag}")
    dhn2 = _mm(duv, w_up_t[f:], out_dtype=F32, res=dhn2, name=f"b_up_v_{tag}")
    g["w_up"] = jnp.concatenate([_mm_tn(sv["hn2"], dug, name=f"g_up_g_{tag}"),
                                 _mm_tn(sv["hn2"], duv, name=f"g_up_v_{tag}")], axis=1)
    dh1, dh1b, g["norm2"] = _rms_bwd(sv["h1"], p["norm2"], dhn2, dh2, name=f"b_rms2_{tag}")
    dmixed = _mm(dh1b, p["w_o"].T, out_dtype=BF16, name=f"b_wo_{tag}")
    g["w_o"] = _mm_tn(sv["mixed"], dh1b, name=f"g_wo_{tag}")
    dgates, dya, dyb, dyc, vec = _mix_bwd(sv["s"], sv["pooled"], sv["att"], sv["gates"], p["w_conv_out"], sv["wb"],
                                          p["w_attn_out"], p["b_conv_out"], p["pool_scale"], dmixed,
                                          name=f"b_mix_{tag}")
    g["b_conv_out"], g["pool_scale"] = vec[0], vec[1]
    ds = _mm(dya, p["w_conv_out"].T, out_dtype=F32, name=f"b_conv_out_{tag}")
    dpooled = _mm(dyb, sv["wb"].T, out_dtype=F32, name=f"b_pool_out_{tag}")
    datt = _mm(dyc, p["w_attn_out"].T, out_dtype=BF16, name=f"b_attn_out_{tag}")
    g["w_conv_out"] = _mm_tn(sv["s"], dya, name=f"g_conv_out_{tag}")
    g["w_pool_grp"] = _block_diag_grad(_mm_tn(sv["pooled"], dyb, name=f"g_pool_{tag}"), len(POOL_WINDOWS))
    g["w_attn_out"] = _mm_tn(sv["att"], dyc, name=f"g_attn_out_{tag}")
    dc, gp = _conv_bwd_ln(sv["pa"], ds, p["conv_dw_w"], p["conv_dw_b"], p["conv_ln_g"], p["conv_ln_b"],
                          name=f"b_conv_ln_{tag}")
    g["conv_dw_w"], g["conv_dw_b"], g["conv_ln_g"], g["conv_ln_b"] = gp[0:CONV_K], gp[32], gp[33], gp[34]
    dconv = _conv_bwd_in(sv["pa"], dc, p["conv_dw_w"], name=f"b_conv_in_{tag}")
    dp = _pool_bwd(dpooled, name=f"b_pool_{tag}")
    dq, dk, dv, *recv = _attn_bwd(sv["qkv"], sv["att32"], datt, name=f"b_attn_{tag}", scatter=scatter)
    dk = jnp.moveaxis(dk, 0, 1).reshape(dq.shape).astype(BF16)
    dv = jnp.moveaxis(dv, 0, 1).reshape(dq.shape).astype(BF16)
    w_in_t = p["w_in"].T
    cols = [(jnp.concatenate([dconv, dp, dq, dk, dv], axis=1), 0, C_QKV), (dgates, C_QKV, w_in_t.shape[0])]
    dhn, gw = None, []
    for n, (dcol, lo, hi) in enumerate(cols):
        dhn = _mm(dcol, w_in_t[lo:hi], out_dtype=F32, res=dhn, name=f"b_in{n}_{tag}")
        gw.append(_mm_tn(sv["hn"], dcol, name=f"g_in{n}_{tag}"))
    g["w_in"] = jnp.concatenate(gw, axis=1)
    dh, dhb, g["norm1"] = _rms_bwd(sv["h"], p["norm1"], dhn, dh1, name=f"b_rms1_{tag}")
    return dh, dhb, g, (recv[0] if recv else None)


def kernel(x, meta, norm1, w_in, conv_dw_w, conv_dw_b, conv_ln_g, conv_ln_b, w_conv_out, b_conv_out, w_pool_grp, pool_scale, w_attn_out, w_o, norm2, w_up, ffn_dw_w, ffn_dw_b, w_down, final_norm, loss_target, m_meta, m_norm1, m_w_in, m_conv_dw_w, m_conv_dw_b, m_conv_ln_g, m_conv_ln_b, m_w_conv_out, m_b_conv_out, m_w_pool_grp, m_pool_scale, m_w_attn_out, m_w_o, m_norm2, m_w_up, m_ffn_dw_w, m_ffn_dw_b, m_w_down, m_final_norm, v_meta, v_norm1, v_w_in, v_conv_dw_w, v_conv_dw_b, v_conv_ln_g, v_conv_ln_b, v_w_conv_out, v_b_conv_out, v_w_pool_grp, v_pool_scale, v_w_attn_out, v_w_o, v_norm2, v_w_up, v_ffn_dw_w, v_ffn_dw_b, v_w_down, v_final_norm):
    given = dict(meta=meta, norm1=norm1, w_in=w_in, conv_dw_w=conv_dw_w, conv_dw_b=conv_dw_b, conv_ln_g=conv_ln_g, conv_ln_b=conv_ln_b, w_conv_out=w_conv_out, b_conv_out=b_conv_out, w_pool_grp=w_pool_grp, pool_scale=pool_scale, w_attn_out=w_attn_out, w_o=w_o, norm2=norm2, w_up=w_up, ffn_dw_w=ffn_dw_w, ffn_dw_b=ffn_dw_b, w_down=w_down, final_norm=final_norm)
    mom_m = dict(meta=m_meta, norm1=m_norm1, w_in=m_w_in, conv_dw_w=m_conv_dw_w, conv_dw_b=m_conv_dw_b, conv_ln_g=m_conv_ln_g, conv_ln_b=m_conv_ln_b, w_conv_out=m_w_conv_out, b_conv_out=m_b_conv_out, w_pool_grp=m_w_pool_grp, pool_scale=m_pool_scale, w_attn_out=m_w_attn_out, w_o=m_w_o, norm2=m_norm2, w_up=m_w_up, ffn_dw_w=m_ffn_dw_w, ffn_dw_b=m_ffn_dw_b, w_down=m_w_down, final_norm=m_final_norm)
    mom_v = dict(meta=v_meta, norm1=v_norm1, w_in=v_w_in, conv_dw_w=v_conv_dw_w, conv_dw_b=v_conv_dw_b, conv_ln_g=v_conv_ln_g, conv_ln_b=v_conv_ln_b, w_conv_out=v_w_conv_out, b_conv_out=v_b_conv_out, w_pool_grp=v_w_pool_grp, pool_scale=v_pool_scale, w_attn_out=v_w_attn_out, w_o=v_w_o, norm2=v_norm2, w_up=v_w_up, ffn_dw_w=v_ffn_dw_w, ffn_dw_b=v_ffn_dw_b, w_down=v_w_down, final_norm=v_final_norm)
    sharded_axis = dict(SHARDED)
    vectors = [n for n, _ in SHARDED if n not in MATRICES]
    depth = norm1.shape[0]

    got_vec = _all_gather(_pack([given[n] for n in vectors], F32), name="gather_vectors")
    full = {n: given[n] for n in REPLICATED}
    for n, a in zip(vectors, _unpack(got_vec, [given[n].shape for n in vectors], (N_DEV,))):
        full[n] = _unshard(a, sharded_axis[n])
    mat_shapes = [given[n].shape[1:] for n in MATRICES]

    def layer_matrices(i):
        return _pack([given[n][i] for n in MATRICES], BF16)

    got_mat = _all_gather(layer_matrices(0), name="gather_matrices_l0")

    xs = x[0]
    d = xs.shape[1]
    h = jnp.concatenate([jnp.zeros((PAD, d), F32), full["meta"], xs], axis=0)
    layers, saved = [], []
    for i in range(depth):
        p = {n: full[n][i] for n in full if n not in ("meta", "final_norm")}
        for n, a in zip(MATRICES, _unpack(got_mat, mat_shapes, (N_DEV,))):
            p[n] = _unshard(a, sharded_axis[n] - 1)
        layers.append(p)
        h, sv, got_mat = _layer_fwd(h, p, f"l{i}", gather=layer_matrices(i + 1) if i + 1 < depth else None)
        saved.append(sv)
    loss_part, dh, dhb, g_final = _loss_head(h, full["final_norm"], loss_target[0], name="loss_head")

    names = [n for n, _ in SHARDED if n != "meta"]

    def layer_pieces(g, extra=()):
        return _pack_pieces([_pieces(g[n], sharded_axis[n] - 1) for n in names] + list(extra), BF16)

    grads, recvs = [None] * depth, [None] * depth
    for i in reversed(range(depth)):
        send = layer_pieces(grads[i + 1]) if i + 1 < depth else None
        dh, dhb, grads[i], got = _layer_bwd(dh, dhb, layers[i], saved[i], f"l{i}", scatter=send)
        if send is not None:
            recvs[i + 1] = got
    grad_x = dh[FRONT:][None]
    recvs[0] = _all_to_all(layer_pieces(grads[0], [_pieces(dh[PAD:FRONT], sharded_axis["meta"])]), name="scatter_grads_l0")
    full_grad = {n: jnp.stack([grads[i][n] for i in range(depth)]) for n in REPLICATED if n != "final_norm"}
    full_grad["final_norm"] = g_final
    rep_shapes = [given[n].shape for n in REPLICATED] + [(1,)]
    rep_parts = _all_gather(_pack([full_grad[n] for n in REPLICATED] + [loss_part.reshape(1)], F32),
                            name="gather_partials")

    out, per_layer = {}, []
    for i in range(depth):
        extra = ["meta"] if i == 0 else []
        pick = lambda src: [src[n][i] for n in names] + [src[n] for n in extra]
        res = _adamw(recvs[i], _pack(pick(given), F32), _pack(pick(mom_m), F32), _pack(pick(mom_v), F32),
                     name=f"adamw_l{i}")
        shapes = [given[n].shape[1:] for n in names] + [given[n].shape for n in extra]
        per_layer.append([_unpack(buf, shapes) for buf in res])
    for k, kind in enumerate(("grad", "delta", "new_m", "new_v")):
        for j, n in enumerate(names):
            out[kind, n] = jnp.stack([per_layer[i][k][j] for i in range(depth)])
        out[kind, "meta"] = per_layer[0][k][len(names)]
    rep_w = [given[n] for n in REPLICATED] + [jnp.zeros((1,), F32)]
    rep_m = [mom_m[n] for n in REPLICATED] + [jnp.zeros((1,), F32)]
    rep_v = [mom_v[n] for n in REPLICATED] + [jnp.ones((1,), F32)]
    res = _adamw(rep_parts, _pack(rep_w, F32), _pack(rep_m, F32), _pack(rep_v, F32), name="adamw_replicated")
    for kind, buf in zip(("grad", "delta", "new_m", "new_v"), res):
        for n, a in zip(list(REPLICATED) + ["loss"], _unpack(buf, rep_shapes)):
            out[kind, n] = a
    loss = out["grad", "loss"][0]
    return (loss, grad_x, *[out["grad", n] for n in WEIGHTS], *[out["delta", n] for n in WEIGHTS],
            *[out["new_m", n] for n in WEIGHTS], *[out["new_v", n] for n in WEIGHTS])
```

```python
import functools

import jax
import jax.numpy as jnp
from jax import lax
from jax.experimental import pallas as pl
from jax.experimental.pallas import tpu as pltpu

F32 = jnp.float32
BF16 = jnp.bfloat16
MESH = pl.DeviceIdType.MESH

N_DEV = 8
N_META = 16
BLOCK = 128
PAD = 240
FRONT = PAD + N_META
HEADS = 4
HEAD_DIM = 128
CONV_CH = 256
CONV_K = 31
POOL_CH = 256
POOL_WINDOWS = (2, 4, 8, 16)
FFN_K = 3
EPS = 1e-6
ADAM_LR, ADAM_B1, ADAM_B2, ADAM_EPS, ADAM_WD, ADAM_STEP = 0.001, 0.9, 0.999, 1e-08, 0.01, 10

VMEM_LIMIT = 56 * 1024 * 1024
CONV_HALO = 32
POOL_HALO = 16
FFN_HALO = 8
PACK_ALIGN = 16 * 128
PACK_ROWS = 512


def _tile(n, cap, unit):
    if n <= cap:
        return n
    best = None
    t = unit
    while t <= cap:
        if n % t == 0:
            best = t
        t += unit
    assert best is not None, (n, cap, unit)
    return best


def _params(sem):
    return pltpu.CompilerParams(dimension_semantics=sem, vmem_limit_bytes=VMEM_LIMIT)


def _sigmoid(x):
    return 0.5 * jnp.tanh(0.5 * x) + 0.5


MM_MAX_K = 3072
MM_RESIDENT_B = 3072 * 1024 * 2


def _mm(a, b, *, out_dtype, name, res=None, col_scale=None, mask_rows=False, tn_cap=768):
    m, k = a.shape
    k2, n = b.shape
    assert k == k2 and k <= MM_MAX_K
    tn = n if k * n * 2 <= MM_RESIDENT_B else _tile(n, tn_cap, 128)
    tm = _tile(m, 1280 if (k <= 1024 and tn <= 1024) else 640, 128)

    def body(*refs):
        refs = list(refs)
        a_ref, b_ref = refs[:2]
        o_ref = refs[-1]
        r_ref = refs[2] if res is not None else None
        c_ref = refs[-2] if col_scale is not None else None
        y = jnp.dot(a_ref[...].astype(BF16), b_ref[...].astype(BF16), preferred_element_type=F32)
        if col_scale is not None:
            y = y * c_ref[...]
        if res is not None:
            y = y + r_ref[...].astype(F32)
        if mask_rows:
            row = pl.program_id(0) * tm + lax.broadcasted_iota(jnp.int32, (tm, 1), 0)
            y = jnp.where(row >= PAD, y, 0.0)
        o_ref[...] = y.astype(out_dtype)

    in_specs = [pl.BlockSpec((tm, k), lambda i, j: (i, 0)), pl.BlockSpec((k, tn), lambda i, j: (0, j))]
    args = [a, b]
    if res is not None:
        in_specs.append(pl.BlockSpec((tm, tn), lambda i, j: (i, j)))
        args.append(res)
    if col_scale is not None:
        in_specs.append(pl.BlockSpec((1, tn), lambda i, j: (0, j)))
        args.append(col_scale.reshape(1, n))
    return pl.pallas_call(
        body, name=name, grid=(m // tm, n // tn),
        in_specs=in_specs, out_specs=pl.BlockSpec((tm, tn), lambda i, j: (i, j)),
        out_shape=jax.ShapeDtypeStruct((m, n), out_dtype),
        compiler_params=_params(("parallel", "parallel")),
    )(*args)


def _mm_tn(a, b, *, name, t1_cap=512, tn_cap=1024, tl_cap=3328):
    l, k1 = a.shape
    l2, n = b.shape
    assert l == l2
    t1, tn, tl = _tile(k1, t1_cap, 128), _tile(n, tn_cap, 128), _tile(l, tl_cap, 128)

    def body(a_ref, b_ref, o_ref):
        @pl.when(pl.program_id(2) == 0)
        def _():
            o_ref[...] = jnp.zeros_like(o_ref)

        o_ref[...] += lax.dot_general(a_ref[...].astype(BF16), b_ref[...].astype(BF16),
                                      (((0,), (0,)), ((), ())), preferred_element_type=F32)

    return pl.pallas_call(
        body, name=name, grid=(k1 // t1, n // tn, l // tl),
        in_specs=[pl.BlockSpec((tl, t1), lambda i, j, ll: (ll, i)), pl.BlockSpec((tl, tn), lambda i, j, ll: (ll, j))],
        out_specs=pl.BlockSpec((t1, tn), lambda i, j, ll: (i, j)),
        out_shape=jax.ShapeDtypeStruct((k1, n), F32),
        compiler_params=_params(("parallel", "parallel", "arbitrary")),
    )(a, b)


def _rms_fwd(x, g, *, name):
    l, d = x.shape
    tm = _tile(l, 640, 128)

    def body(x_ref, g_ref, o_ref):
        xv = x_ref[...]
        r = lax.rsqrt(jnp.mean(xv * xv, axis=-1, keepdims=True) + EPS)
        o_ref[...] = (xv * r * g_ref[...]).astype(BF16)

    return pl.pallas_call(
        body, name=name, grid=(l // tm,),
        in_specs=[pl.BlockSpec((tm, d), lambda i: (i, 0)), pl.BlockSpec((1, d), lambda i: (0, 0))],
        out_specs=pl.BlockSpec((tm, d), lambda i: (i, 0)),
        out_shape=jax.ShapeDtypeStruct((l, d), BF16),
        compiler_params=_params(("parallel",)),
    )(x, g.reshape(1, d))


def _rms_bwd(x, g, dy, dres, *, name):
    l, d = x.shape
    tm = _tile(l, 640, 128)

    def body(x_ref, g_ref, dy_ref, dr_ref, dx_ref, dxb_ref, dg_ref):
        i = pl.program_id(0)

        @pl.when(i == 0)
        def _():
            dg_ref[...] = jnp.zeros_like(dg_ref)

        xv = x_ref[...]
        r = lax.rsqrt(jnp.mean(xv * xv, axis=-1, keepdims=True) + EPS)
        xh = xv * r
        dyv = dy_ref[...].astype(F32)
        dxh = dyv * g_ref[...]
        dx = r * (dxh - xh * jnp.mean(dxh * xh, axis=-1, keepdims=True)) + dr_ref[...]
        row = i * tm + lax.broadcasted_iota(jnp.int32, (tm, 1), 0)
        dx = jnp.where(row >= PAD, dx, 0.0)
        dx_ref[...] = dx
        dxb_ref[...] = dx.astype(BF16)
        dg_ref[0:1, :] += jnp.sum(dyv * xh, axis=0, keepdims=True)

    tile = pl.BlockSpec((tm, d), lambda i: (i, 0))
    dx, dxb, dg = pl.pallas_call(
        body, name=name, grid=(l // tm,),
        in_specs=[tile, pl.BlockSpec((1, d), lambda i: (0, 0)), tile, tile],
        out_specs=[tile, tile, pl.BlockSpec((8, d), lambda i: (0, 0))],
        out_shape=[jax.ShapeDtypeStruct((l, d), F32), jax.ShapeDtypeStruct((l, d), BF16),
                   jax.ShapeDtypeStruct((8, d), F32)],
        compiler_params=_params(("arbitrary",)),
    )(x, g.reshape(1, d), dy, dres)
    return dx, dxb, dg[0]


def _loss_head(h, g, target, *, name):
    l, d = h.shape
    tm = FRONT
    assert l % tm == 0 and target.shape[0] == l - tm

    def body(h_ref, g_ref, t_ref, dh_ref, dhb_ref, loss_ref, dg_ref):
        i = pl.program_id(0)

        @pl.when(i == 0)
        def _():
            loss_ref[...] = jnp.zeros_like(loss_ref)
            dg_ref[...] = jnp.zeros_like(dg_ref)
            dh_ref[...] = jnp.zeros_like(dh_ref)
            dhb_ref[...] = jnp.zeros_like(dhb_ref)

        @pl.when(i > 0)
        def _():
            xv = h_ref[...]
            r = lax.rsqrt(jnp.mean(xv * xv, axis=-1, keepdims=True) + EPS)
            xh = xv * r
            gv = g_ref[...]
            err = xh * gv - t_ref[...]
            loss_ref[...] += 0.5 * jnp.sum(jnp.mean(err * err, axis=-1, keepdims=True))
            dy = err * (1.0 / d)
            dxh = dy * gv
            dh = r * (dxh - xh * jnp.mean(dxh * xh, axis=-1, keepdims=True))
            dh_ref[...] = dh
            dhb_ref[...] = dh.astype(BF16)
            dg_ref[0:1, :] += jnp.sum(dy * xh, axis=0, keepdims=True)

    tile = pl.BlockSpec((tm, d), lambda i: (i, 0))
    dh, dhb, loss, dg = pl.pallas_call(
        body, name=name, grid=(l // tm,),
        in_specs=[tile, pl.BlockSpec((1, d), lambda i: (0, 0)),
                  pl.BlockSpec((tm, d), lambda i: (jnp.maximum(i - 1, 0), 0))],
        out_specs=[tile, tile, pl.BlockSpec((8, 128), lambda i: (0, 0)), pl.BlockSpec((8, d), lambda i: (0, 0))],
        out_shape=[jax.ShapeDtypeStruct((l, d), F32), jax.ShapeDtypeStruct((l, d), BF16),
                   jax.ShapeDtypeStruct((8, 128), F32), jax.ShapeDtypeStruct((8, d), F32)],
        compiler_params=_params(("arbitrary",)),
    )(h, g.reshape(1, d), target)
    return loss[0, 0], dh, dhb, dg[0]


def _conv_tile(l):
    return _tile(l, 640, 128)


def _phase_views(buf, views, tm, max_off):
    for s in range(8):
        n = tm + 8 * ((max_off - s) // 8)
        views[s, 0:n, :] = buf[s:s + n, :]
    return views


def _tap(views, off, tm):
    a, s = divmod(off, 8)
    return views[s, 8 * a:8 * a + tm, :]


def _conv_core(a, gt, buf, views, dw_w, dw_b, first):
    tm = a.shape[0]

    @pl.when(first)
    def _():
        buf[0:CONV_HALO, :] = jnp.zeros((CONV_HALO, CONV_CH), F32)

    @pl.when(jnp.logical_not(first))
    def _():
        buf[0:CONV_HALO, :] = buf[tm:tm + CONV_HALO, :]

    sg = _sigmoid(gt)
    buf[CONV_HALO:CONV_HALO + tm, :] = a * sg
    _phase_views(buf, views, tm, CONV_HALO)
    c = jnp.zeros((tm, CONV_CH), F32) + dw_b
    for k in range(CONV_K):
        c = c + dw_w[k:k + 1, :] * _tap(views, CONV_HALO - (CONV_K - 1) + k, tm)
    return c, sg


def _layer_norm(c, ln_g, ln_b):
    mu = jnp.mean(c, axis=-1, keepdims=True)
    xc = c - mu
    r = lax.rsqrt(jnp.mean(xc * xc, axis=-1, keepdims=True) + EPS)
    xh = xc * r
    return xh, r, xh * ln_g + ln_b


def _conv_fwd(pa, dw_w, dw_b, ln_g, ln_b, *, name):
    l = pa.shape[0]
    tm = _conv_tile(l)

    def body(a_ref, gt_ref, w_ref, b_ref, g_ref, bb_ref, o_ref, buf, views):
        c, _ = _conv_core(a_ref[...], gt_ref[...], buf, views, w_ref[...], b_ref[...], pl.program_id(0) == 0)
        _, _, y = _layer_norm(c, g_ref[...], bb_ref[...])
        o_ref[...] = (y * _sigmoid(y)).astype(BF16)

    vec = pl.BlockSpec((1, CONV_CH), lambda i: (0, 0))
    return pl.pallas_call(
        body, name=name, grid=(l // tm,),
        in_specs=[pl.BlockSpec((tm, CONV_CH), lambda i: (i, 0)), pl.BlockSpec((tm, CONV_CH), lambda i: (i, 1)),
                  pl.BlockSpec((CONV_K, CONV_CH), lambda i: (0, 0)), vec, vec, vec],
        out_specs=pl.BlockSpec((tm, CONV_CH), lambda i: (i, 0)),
        out_shape=jax.ShapeDtypeStruct((l, CONV_CH), BF16),
        scratch_shapes=[pltpu.VMEM((CONV_HALO + tm, CONV_CH), F32), pltpu.VMEM((8, CONV_HALO + tm, CONV_CH), F32)],
        compiler_params=_params(("arbitrary",)),
    )(pa, pa, dw_w, dw_b.reshape(1, -1), ln_g.reshape(1, -1), ln_b.reshape(1, -1))


def _conv_bwd_ln(pa, ds, dw_w, dw_b, ln_g, ln_b, *, name):
    l = pa.shape[0]
    tm = _conv_tile(l)

    def body(a_ref, gt_ref, ds_ref, w_ref, b_ref, g_ref, bb_ref, dc_ref, gp_ref, buf, views):
        i = pl.program_id(0)

        @pl.when(i == 0)
        def _():
            gp_ref[...] = jnp.zeros_like(gp_ref)

        c, _ = _conv_core(a_ref[...], gt_ref[...], buf, views, w_ref[...], b_ref[...], i == 0)
        xh, r, y = _layer_norm(c, g_ref[...], bb_ref[...])
        sy = _sigmoid(y)
        dy = ds_ref[...] * (sy * (1.0 + y * (1.0 - sy)))
        dxh = dy * g_ref[...]
        dc = r * (dxh - jnp.mean(dxh, axis=-1, keepdims=True) - xh * jnp.mean(dxh * xh, axis=-1, keepdims=True))
        dc_ref[...] = dc
        for k in range(CONV_K):
            gp_ref[k:k + 1, :] += jnp.sum(dc * _tap(views, CONV_HALO - (CONV_K - 1) + k, tm), axis=0, keepdims=True)
        gp_ref[32:33, :] += jnp.sum(dc, axis=0, keepdims=True)
        gp_ref[33:34, :] += jnp.sum(dy * xh, axis=0, keepdims=True)
        gp_ref[34:35, :] += jnp.sum(dy, axis=0, keepdims=True)

    vec = pl.BlockSpec((1, CONV_CH), lambda i: (0, 0))
    return pl.pallas_call(
        body, name=name, grid=(l // tm,),
        in_specs=[pl.BlockSpec((tm, CONV_CH), lambda i: (i, 0)), pl.BlockSpec((tm, CONV_CH), lambda i: (i, 1)),
                  pl.BlockSpec((tm, CONV_CH), lambda i: (i, 0)),
                  pl.BlockSpec((CONV_K, CONV_CH), lambda i: (0, 0)), vec, vec, vec],
        out_specs=[pl.BlockSpec((tm, CONV_CH), lambda i: (i, 0)), pl.BlockSpec((40, CONV_CH), lambda i: (0, 0))],
        out_shape=[jax.ShapeDtypeStruct((l, CONV_CH), F32), jax.ShapeDtypeStruct((40, CONV_CH), F32)],
        scratch_shapes=[pltpu.VMEM((CONV_HALO + tm, CONV_CH), F32), pltpu.VMEM((8, CONV_HALO + tm, CONV_CH), F32)],
        compiler_params=_params(("arbitrary",)),
    )(pa, pa, ds, dw_w, dw_b.reshape(1, -1), ln_g.reshape(1, -1), ln_b.reshape(1, -1))


def _conv_bwd_in(pa, dc, dw_w, *, name):
    l = pa.shape[0]
    tm = _conv_tile(l)
    nt = l // tm

    def body(a_ref, gt_ref, dc_ref, w_ref, o_ref, buf, views):
        first = pl.program_id(0) == 0

        @pl.when(first)
        def _():
            buf[tm:tm + CONV_HALO, :] = jnp.zeros((CONV_HALO, CONV_CH), F32)

        @pl.when(jnp.logical_not(first))
        def _():
            buf[tm:tm + CONV_HALO, :] = buf[0:CONV_HALO, :]

        buf[0:tm, :] = dc_ref[...]
        w = w_ref[...]
        _phase_views(buf, views, tm, CONV_K - 1)
        dhc = jnp.zeros((tm, CONV_CH), F32)
        for k in range(CONV_K):
            dhc = dhc + w[k:k + 1, :] * _tap(views, CONV_K - 1 - k, tm)
        a = a_ref[...]
        sg = _sigmoid(gt_ref[...])
        o_ref[:, 0:CONV_CH] = (dhc * sg).astype(BF16)
        o_ref[:, CONV_CH:2 * CONV_CH] = (dhc * a * sg * (1.0 - sg)).astype(BF16)

    return pl.pallas_call(
        body, name=name, grid=(nt,),
        in_specs=[pl.BlockSpec((tm, CONV_CH), lambda i: (nt - 1 - i, 0)),
                  pl.BlockSpec((tm, CONV_CH), lambda i: (nt - 1 - i, 1)),
                  pl.BlockSpec((tm, CONV_CH), lambda i: (nt - 1 - i, 0)),
                  pl.BlockSpec((CONV_K, CONV_CH), lambda i: (0, 0))],
        out_specs=pl.BlockSpec((tm, 2 * CONV_CH), lambda i: (nt - 1 - i, 0)),
        out_shape=jax.ShapeDtypeStruct((l, 2 * CONV_CH), BF16),
        scratch_shapes=[pltpu.VMEM((tm + CONV_HALO, CONV_CH), F32), pltpu.VMEM((8, tm + CONV_HALO, CONV_CH), F32)],
        compiler_params=_params(("arbitrary",)),
    )(pa, pa, dc, dw_w)


def _pool_consts(tm, row0):
    lane = lax.broadcasted_iota(jnp.int32, (1, POOL_CH), 1)
    grp = lane // (POOL_CH // len(POOL_WINDOWS))
    win = jnp.where(grp == 0, 2.0, jnp.where(grp == 1, 4.0, jnp.where(grp == 2, 8.0, 16.0))).astype(F32)
    pos = (row0 + lax.broadcasted_iota(jnp.int32, (tm, 1), 0) - PAD).astype(F32)
    cnt = jnp.maximum(jnp.minimum(pos + 1.0, win), 1.0)
    return grp, cnt


def _pool_select(grp, s2, s4, s8, s16):
    return jnp.where(grp == 0, s2, jnp.where(grp == 1, s4, jnp.where(grp == 2, s8, s16)))


def _pool_fwd(pa, *, name):
    l = pa.shape[0]
    tm = _conv_tile(l)
    ext = POOL_HALO + tm

    def body(p_ref, o_ref, buf):
        i = pl.program_id(0)

        @pl.when(i == 0)
        def _():
            buf[0:POOL_HALO, :] = jnp.zeros((POOL_HALO, POOL_CH), F32)

        @pl.when(i > 0)
        def _():
            buf[0:POOL_HALO, :] = buf[tm:tm + POOL_HALO, :]

        p = p_ref[...]
        buf[POOL_HALO:ext, :] = p
        x = buf[...]
        s2 = x + pltpu.roll(x, 1, 0)
        s4 = s2 + pltpu.roll(s2, 2, 0)
        s8 = s4 + pltpu.roll(s4, 4, 0)
        s16 = s8 + pltpu.roll(s8, 8, 0)
        grp, cnt = _pool_consts(tm, i * tm)
        s = _pool_select(grp, s2, s4, s8, s16)[POOL_HALO:ext, :]
        o_ref[...] = (s / cnt - p).astype(BF16)

    return pl.pallas_call(
        body, name=name, grid=(l // tm,),
        in_specs=[pl.BlockSpec((tm, POOL_CH), lambda i: (i, 2))],
        out_specs=pl.BlockSpec((tm, POOL_CH), lambda i: (i, 0)),
        out_shape=jax.ShapeDtypeStruct((l, POOL_CH), BF16),
        scratch_shapes=[pltpu.VMEM((ext, POOL_CH), F32)],
        compiler_params=_params(("arbitrary",)),
    )(pa)


def _pool_bwd(dpooled, *, name):
    l = dpooled.shape[0]
    tm = _conv_tile(l)
    nt = l // tm
    ext = tm + POOL_HALO

    def body(d_ref, o_ref, buf):
        i = pl.program_id(0)

        @pl.when(i == 0)
        def _():
            buf[tm:ext, :] = jnp.zeros((POOL_HALO, POOL_CH), F32)

        @pl.when(i > 0)
        def _():
            buf[tm:ext, :] = buf[0:POOL_HALO, :]

        d = d_ref[...]
        grp, cnt = _pool_consts(tm, (nt - 1 - i) * tm)
        buf[0:tm, :] = d / cnt
        x = buf[...]
        s2 = x + pltpu.roll(x, ext - 1, 0)
        s4 = s2 + pltpu.roll(s2, ext - 2, 0)
        s8 = s4 + pltpu.roll(s4, ext - 4, 0)
        s16 = s8 + pltpu.roll(s8, ext - 8, 0)
        s = _pool_select(grp, s2, s4, s8, s16)[0:tm, :]
        o_ref[...] = (s - d).astype(BF16)

    return pl.pallas_call(
        body, name=name, grid=(nt,),
        in_specs=[pl.BlockSpec((tm, POOL_CH), lambda i: (nt - 1 - i, 0))],
        out_specs=pl.BlockSpec((tm, POOL_CH), lambda i: (nt - 1 - i, 0)),
        out_shape=jax.ShapeDtypeStruct((l, POOL_CH), BF16),
        scratch_shapes=[pltpu.VMEM((ext, POOL_CH), F32)],
        compiler_params=_params(("arbitrary",)),
    )(dpooled)


ATT_TQ = 256
ATT_TK = 2 * BLOCK
ATT_SUB = ATT_TK // BLOCK
LOG2E = 1.4426950408889634
LN2 = 0.6931471805599453
Q_SCALE = HEAD_DIM ** -0.5 * LOG2E
ATT_CUT = 160.0


def _tri_ones():
    r = lax.broadcasted_iota(jnp.int32, (2 * BLOCK, 2 * BLOCK), 0) % BLOCK
    c = lax.broadcasted_iota(jnp.int32, (2 * BLOCK, 2 * BLOCK), 1)
    return jnp.where((c >= BLOCK) | (r > c), 1.0, 0.0).astype(BF16)


def _split_dot(x, rhs):
    hi = x.astype(BF16)
    lo = (x - hi.astype(F32)).astype(BF16)
    return jnp.dot(jnp.concatenate([hi, lo], axis=1), rhs, preferred_element_type=F32)


def _scores(q, kt, qpos, base, masked):
    z = lax.dot_general(q, kt, (((1,), (1,)), ((), ())), preferred_element_type=F32)
    sp = jnp.log2(1.0 + jnp.exp2(-jnp.abs(z)))
    lb = jnp.minimum(z, 0.0) - sp
    lk = lb - z
    valid = None
    if masked:
        kpos = base + lax.broadcasted_iota(jnp.int32, (1, z.shape[1]), 1)
        valid = (kpos < qpos) & (kpos >= PAD)
        lk = jnp.where(valid, lk, 0.0)
    return lk, lb, valid


def _suffix(x, tri, carry):
    wts = [_split_dot(x[:, b * BLOCK:(b + 1) * BLOCK], tri) for b in range(ATT_SUB)]
    offs = [None] * ATT_SUB
    s = carry
    for b in reversed(range(ATT_SUB)):
        offs[b] = wts[b][:, :BLOCK] + s
        s = s + wts[b][:, BLOCK:]
    return jnp.concatenate(offs, axis=1), s


def _walk_tiles(i, tq, step):
    t_top = ((i + 1) * tq - 1) // ATT_TK
    t_diag = (i * tq) // ATT_TK
    n_plain = jnp.maximum(t_diag - 1, 0)

    def masked(jj, top):
        return step(t_top - jj, True)

    def live(carry):
        return (carry[0] < n_plain) & (carry[1] > -ATT_CUT)

    def plain(carry):
        return carry[0] + 1, step(t_diag - 1 - carry[0], False)

    top = lax.fori_loop(0, t_top - t_diag + 1, masked, jnp.float32(0.0))
    _, top = lax.while_loop(live, plain, (jnp.int32(0), top))

    @pl.when((t_diag > 0) & (top > -ATT_CUT))
    def _():
        step(0, True)


def _tile_base(t):
    base = t * ATT_TK
    return base if isinstance(base, int) else pl.multiple_of(base, BLOCK)


def _attn_fwd(qkv, *, name, gather=None):
    l = qkv.shape[0]
    tq = ATT_TQ
    nq = l // tq
    assert l % tq == 0 and l % ATT_TK == 0

    def body(*refs):
        if gather is None:
            q_ref, k_ref, v_ref, o_ref, o32_ref, acc_ref, r_ref = refs
        else:
            q_ref, k_ref, v_ref, x_hbm, o_ref, o32_ref, got_hbm, acc_ref, r_ref = refs[:9]
            exchange = _Gather(x_hbm, got_hbm, *refs[9:])
            first = (pl.program_id(0) == 0) & (pl.program_id(1) == 0)
            last = (pl.program_id(0) == HEADS - 1) & (pl.program_id(1) == nq - 1)
            pl.when(first)(exchange.start)
        i = pl.program_id(1)
        acc_ref[...] = jnp.zeros_like(acc_ref)
        r_ref[...] = jnp.zeros_like(r_ref)
        q = q_ref[...]
        qpos = i * tq + lax.broadcasted_iota(jnp.int32, (tq, 1), 0)
        tri = _tri_ones()

        def step(t, masked):
            base = _tile_base(t)
            lk, lb, valid = _scores(q, k_ref[pl.ds(base, ATT_TK), :], qpos, base, masked)
            off, r_new = _suffix(lk, tri, r_ref[...])
            a = jnp.exp2(lb + off)
            if masked:
                a = jnp.where(valid, a, 0.0)
            acc_ref[...] += jnp.dot(a.astype(BF16), v_ref[pl.ds(base, ATT_TK), :], preferred_element_type=F32)
            r_ref[...] = r_new
            return jnp.max(r_new)

        _walk_tiles(i, tq, step)
        o_ref[...] = acc_ref[...].astype(BF16)
        o32_ref[...] = acc_ref[...]
        if gather is not None:
            pl.when(last)(exchange.finish)

    tile = pl.BlockSpec((tq, HEAD_DIM), lambda h, i: (i, h))
    hbm = pl.BlockSpec(memory_space=pl.ANY)
    hosted = gather is not None
    return pl.pallas_call(
        body, name=name, grid=(HEADS, nq),
        in_specs=[tile,
                  pl.BlockSpec((l, HEAD_DIM), lambda h, i: (0, HEADS + h)),
                  pl.BlockSpec((l, HEAD_DIM), lambda h, i: (0, 2 * HEADS + h))] + [hbm] * hosted,
        out_specs=[tile, tile] + [hbm] * hosted,
        out_shape=[jax.ShapeDtypeStruct((l, HEADS * HEAD_DIM), BF16),
                   jax.ShapeDtypeStruct((l, HEADS * HEAD_DIM), F32)]
        + ([jax.ShapeDtypeStruct((N_DEV,) + gather.shape, gather.dtype)] if hosted else []),
        scratch_shapes=[pltpu.VMEM((tq, HEAD_DIM), F32), pltpu.VMEM((tq, BLOCK), F32)] + EXCHANGE_SEMS * hosted,
        compiler_params=_params(("arbitrary", "arbitrary")),
    )(qkv, qkv, qkv, *([gather] * hosted))


def _attn_bwd(qkv, att, datt, *, name, scatter=None):
    l = qkv.shape[0]
    tq = ATT_TQ
    nq = l // tq
    assert l % tq == 0 and l % ATT_TK == 0

    def body(*refs):
        if scatter is None:
            q_ref, k_ref, v_ref, o_ref, do_ref, dq_ref, dk_hbm, dv_hbm, dk_acc, dv_acc, dq_acc, r_ref, s_ref, sem = refs
        else:
            (q_ref, k_ref, v_ref, o_ref, do_ref, send_hbm, dq_ref, dk_hbm, dv_hbm, recv_hbm,
             dk_acc, dv_acc, dq_acc, r_ref, s_ref, sem) = refs[:16]
            exchange = _Scatter(send_hbm, recv_hbm, *refs[16:])
            pl.when((pl.program_id(0) == 0) & (pl.program_id(1) == 0))(exchange.start)
        h = pl.program_id(0)
        i = pl.program_id(1)

        @pl.when(i == 0)
        def _():
            dk_acc[...] = jnp.zeros_like(dk_acc)
            dv_acc[...] = jnp.zeros_like(dv_acc)

        dq_acc[...] = jnp.zeros_like(dq_acc)
        r_ref[...] = jnp.zeros_like(r_ref)
        s_ref[...] = jnp.zeros_like(s_ref)
        q = q_ref[...]
        do = do_ref[...]
        ptot = jnp.sum(do.astype(F32) * o_ref[...], axis=-1, keepdims=True)
        qpos = i * tq + lax.broadcasted_iota(jnp.int32, (tq, 1), 0)
        tri = _tri_ones()

        def step(t, masked):
            base = _tile_base(t)
            kt = k_ref[pl.ds(base, ATT_TK), :]
            vt = v_ref[pl.ds(base, ATT_TK), :]
            lk, lb, valid = _scores(q, kt, qpos, base, masked)
            off, r_new = _suffix(lk, tri, r_ref[...])
            a = jnp.exp2(lb + off)
            if masked:
                a = jnp.where(valid, a, 0.0)
            ab = a.astype(BF16)
            da = lax.dot_general(do, vt, (((1,), (1,)), ((), ())), preferred_element_type=F32)
            p = ab.astype(F32) * da
            poff, s_new = _suffix(p, tri, s_ref[...])
            dz = (p - jnp.exp2(lb) * (ptot - poff)) * LN2
            if masked:
                dz = jnp.where(valid, dz, 0.0)
            dzb = dz.astype(BF16)
            dq_acc[...] += jnp.dot(dzb, kt, preferred_element_type=F32)
            dk_acc[pl.ds(base, ATT_TK), :] += lax.dot_general(dzb, q, (((0,), (0,)), ((), ())),
                                                              preferred_element_type=F32)
            dv_acc[pl.ds(base, ATT_TK), :] += lax.dot_general(ab, do, (((0,), (0,)), ((), ())),
                                                              preferred_element_type=F32)
            r_ref[...] = r_new
            s_ref[...] = s_new
            return jnp.max(r_new)

        _walk_tiles(i, tq, step)
        dq_ref[...] = (dq_acc[...] * Q_SCALE).astype(BF16)

        @pl.when(i == nq - 1)
        def _():
            ck = pltpu.make_async_copy(dk_acc, dk_hbm.at[h], sem.at[0])
            cv = pltpu.make_async_copy(dv_acc, dv_hbm.at[h], sem.at[1])
            ck.start()
            cv.start()
            ck.wait()
            cv.wait()

        if scatter is not None:
            pl.when((h == HEADS - 1) & (i == nq - 1))(exchange.finish)

    tile = pl.BlockSpec((tq, HEAD_DIM), lambda h, i: (i, h))
    hbm = pl.BlockSpec(memory_space=pl.ANY)
    hosted = scatter is not None
    return pl.pallas_call(
        body, name=name, grid=(HEADS, nq),
        in_specs=[tile,
                  pl.BlockSpec((l, HEAD_DIM), lambda h, i: (0, HEADS + h)),
                  pl.BlockSpec((l, HEAD_DIM), lambda h, i: (0, 2 * HEADS + h)),
                  tile, tile] + [hbm] * hosted,
        out_specs=[tile, hbm, hbm] + [hbm] * hosted,
        out_shape=[jax.ShapeDtypeStruct((l, HEADS * HEAD_DIM), BF16),
                   jax.ShapeDtypeStruct((HEADS, l, HEAD_DIM), F32), jax.ShapeDtypeStruct((HEADS, l, HEAD_DIM), F32)]
        + ([jax.ShapeDtypeStruct(scatter.shape, scatter.dtype)] if hosted else []),
        scratch_shapes=[pltpu.VMEM((l, HEAD_DIM), F32), pltpu.VMEM((l, HEAD_DIM), F32),
                        pltpu.VMEM((tq, HEAD_DIM), F32), pltpu.VMEM((tq, BLOCK), F32), pltpu.VMEM((tq, BLOCK), F32),
                        pltpu.SemaphoreType.DMA((2,))] + EXCHANGE_SEMS * hosted,
        compiler_params=_params(("arbitrary", "arbitrary")),
    )(qkv, qkv, qkv, att, datt, *([scatter] * hosted))


MIX_TM = 256


def _mix_branches(s_ref, p_ref, t_ref, g_ref, wa_ref, wb_ref, wc_ref, ba_ref, sc_ref, d):
    ya = jnp.dot(s_ref[...], wa_ref[...], preferred_element_type=F32) + ba_ref[...]
    yb0 = jnp.dot(p_ref[...], wb_ref[...], preferred_element_type=F32)
    yc = jnp.dot(t_ref[...], wc_ref[...], preferred_element_type=F32)
    g0 = _sigmoid(g_ref[:, 0:d].astype(F32))
    g1 = _sigmoid(g_ref[:, d:2 * d].astype(F32))
    g2 = _sigmoid(g_ref[:, 2 * d:3 * d].astype(F32))
    return ya, yb0, yc, g0, g1, g2


def _mix_specs(tm, d):
    row = lambda w: pl.BlockSpec((tm, w), lambda i: (i, 0))
    full = lambda r: pl.BlockSpec((r, d), lambda i: (0, 0))
    return [row(CONV_CH), row(POOL_CH), row(HEADS * HEAD_DIM), row(3 * d),
            full(CONV_CH), full(POOL_CH), full(HEADS * HEAD_DIM), full(1), full(1)]


def _mix_fwd(s, pooled, att, gates, wa, wb, wc, ba, scale, *, name):
    l, d = s.shape[0], wa.shape[1]
    tm = _tile(l, MIX_TM, 128)

    def body(s_ref, p_ref, t_ref, g_ref, wa_ref, wb_ref, wc_ref, ba_ref, sc_ref, o_ref):
        ya, yb0, yc, g0, g1, g2 = _mix_branches(s_ref, p_ref, t_ref, g_ref, wa_ref, wb_ref, wc_ref, ba_ref, sc_ref, d)
        o_ref[...] = (g0 * ya + g1 * (yb0 * sc_ref[...]) + g2 * yc).astype(BF16)

    return pl.pallas_call(
        body, name=name, grid=(l // tm,), in_specs=_mix_specs(tm, d),
        out_specs=pl.BlockSpec((tm, d), lambda i: (i, 0)),
        out_shape=jax.ShapeDtypeStruct((l, d), BF16),
        compiler_params=_params(("parallel",)),
    )(s, pooled, att, gates, wa, wb, wc, ba.reshape(1, d), scale.reshape(1, d))


def _mix_bwd(s, pooled, att, gates, wa, wb, wc, ba, scale, dmixed, *, name):
    l, d = s.shape[0], wa.shape[1]
    tm = _tile(l, MIX_TM, 128)

    def body(s_ref, p_ref, t_ref, g_ref, wa_ref, wb_ref, wc_ref, ba_ref, sc_ref, dm_ref,
             dg_ref, dya_ref, dyb_ref, dyc_ref, vec_ref):
        @pl.when(pl.program_id(0) == 0)
        def _():
            vec_ref[...] = jnp.zeros_like(vec_ref)

        ya, yb0, yc, g0, g1, g2 = _mix_branches(s_ref, p_ref, t_ref, g_ref, wa_ref, wb_ref, wc_ref, ba_ref, sc_ref, d)
        dm = dm_ref[...].astype(F32)
        sc = sc_ref[...]
        dg_ref[:, 0:d] = (dm * ya * g0 * (1.0 - g0)).astype(BF16)
        dg_ref[:, d:2 * d] = (dm * (yb0 * sc) * g1 * (1.0 - g1)).astype(BF16)
        dg_ref[:, 2 * d:3 * d] = (dm * yc * g2 * (1.0 - g2)).astype(BF16)
        dya = dm * g0
        dyb = dm * g1
        dya_ref[...] = dya.astype(BF16)
        dyb_ref[...] = (dyb * sc).astype(BF16)
        dyc_ref[...] = (dm * g2).astype(BF16)
        vec_ref[0:1, :] += jnp.sum(dya, axis=0, keepdims=True)
        vec_ref[1:2, :] += jnp.sum(dyb * yb0, axis=0, keepdims=True)

    row = lambda w: pl.BlockSpec((tm, w), lambda i: (i, 0))
    outs = pl.pallas_call(
        body, name=name, grid=(l // tm,), in_specs=_mix_specs(tm, d) + [row(d)],
        out_specs=[row(3 * d), row(d), row(d), row(d), pl.BlockSpec((8, d), lambda i: (0, 0))],
        out_shape=[jax.ShapeDtypeStruct((l, 3 * d), BF16), jax.ShapeDtypeStruct((l, d), BF16),
                   jax.ShapeDtypeStruct((l, d), BF16), jax.ShapeDtypeStruct((l, d), BF16),
                   jax.ShapeDtypeStruct((8, d), F32)],
        compiler_params=_params(("arbitrary",)),
    )(s, pooled, att, gates, wa, wb, wc, ba.reshape(1, d), scale.reshape(1, d), dmixed)
    return outs


FFN_TC = 512
_GELU_C = 0.7978845608028654
_GELU_A = 0.044715


def _gelu(x):
    th = jnp.tanh(_GELU_C * (x + _GELU_A * x * x * x))
    return 0.5 * x * (1.0 + th), th


def _gelu_grad(x, th):
    return 0.5 * (1.0 + th) + 0.5 * x * (1.0 - th * th) * _GELU_C * (1.0 + 3.0 * _GELU_A * x * x)


FFN_CH = 32


def _ffn_taps(win):
    return (pltpu.roll(win, 2, 0)[FFN_HALO:, :], pltpu.roll(win, 1, 0)[FFN_HALO:, :], win[FFN_HALO:, :])


def _ffn_conv(taps, w, b):
    return b + w[0:1, :] * taps[0] + w[1:2, :] * taps[1] + w[2:3, :] * taps[2]


def _fold8(x):
    acc = x[0:8, :]
    for r in range(8, x.shape[0], 8):
        acc = acc + x[r:r + 8, :]
    return acc


def _ffn_fwd(ug, uv, wg, wv, bg, bv, *, name):
    l, f = ug.shape
    tm = _conv_tile(l)
    tc = _tile(f, FFN_TC, 128)
    ext = FFN_HALO + tm

    def body(ug_ref, uv_ref, wg_ref, wv_ref, bg_ref, bv_ref, o_ref, gc_ref, vc_ref, bufg, bufv):
        i = pl.program_id(1)
        for buf, u_ref in ((bufg, ug_ref), (bufv, uv_ref)):
            @pl.when(i == 0)
            def _():
                buf[0:FFN_HALO, :] = jnp.zeros((FFN_HALO, tc), F32)

            @pl.when(i > 0)
            def _():
                buf[0:FFN_HALO, :] = buf[tm:ext, :]

            buf[FFN_HALO:ext, :] = u_ref[...].astype(F32)
        wg, wv, bg_, bv_ = wg_ref[...], wv_ref[...], bg_ref[...], bv_ref[...]

        def chunk(c, carry):
            r0 = pl.multiple_of(c * FFN_CH, FFN_CH)
            gc = _ffn_conv(_ffn_taps(bufg[pl.ds(r0, FFN_HALO + FFN_CH), :]), wg, bg_)
            vc = _ffn_conv(_ffn_taps(bufv[pl.ds(r0, FFN_HALO + FFN_CH), :]), wv, bv_)
            o_ref[pl.ds(r0, FFN_CH), :] = (_gelu(gc)[0] * vc).astype(BF16)
            gc_ref[pl.ds(r0, FFN_CH), :] = gc.astype(BF16)
            vc_ref[pl.ds(r0, FFN_CH), :] = vc.astype(BF16)
            return carry

        lax.fori_loop(0, tm // FFN_CH, chunk, 0)

    assert tm % FFN_CH == 0
    tile = pl.BlockSpec((tm, tc), lambda j, i: (i, j))
    wspec = pl.BlockSpec((FFN_K, tc), lambda j, i: (0, j))
    bspec = pl.BlockSpec((1, tc), lambda j, i: (0, j))
    return pl.pallas_call(
        body, name=name, grid=(f // tc, l // tm),
        in_specs=[tile, tile, wspec, wspec, bspec, bspec], out_specs=[tile, tile, tile],
        out_shape=[jax.ShapeDtypeStruct((l, f), BF16)] * 3,
        scratch_shapes=[pltpu.VMEM((ext, tc), F32), pltpu.VMEM((ext, tc), F32)],
        compiler_params=_params(("parallel", "arbitrary")),
    )(ug, uv, wg, wv, bg.reshape(1, f), bv.reshape(1, f))


def _ffn_bwd(ug, uv, gc, vc, wg, wv, dact, *, name):
    l, f = ug.shape
    tm = _conv_tile(l)
    tc = _tile(f, FFN_TC, 128)
    nt = l // tm
    ext = FFN_HALO + tm
    win_rows = FFN_CH + FFN_HALO

    def body(ug_ref, uv_ref, gc_ref, vc_ref, wg_ref, wv_ref, da_ref,
             dug_ref, duv_ref, gg_ref, gv_ref, dbufg, dbufv, gaccg, gaccv):
        i = pl.program_id(1)

        @pl.when(i == 0)
        def _():
            gg_ref[...] = jnp.zeros_like(gg_ref)
            gv_ref[...] = jnp.zeros_like(gv_ref)

        for dbuf, gacc in ((dbufg, gaccg), (dbufv, gaccv)):
            @pl.when(i == 0)
            def _():
                dbuf[tm:ext, :] = jnp.zeros((FFN_HALO, tc), F32)

            @pl.when(i > 0)
            def _():
                dbuf[tm:ext, :] = dbuf[0:FFN_HALO, :]

            gacc[...] = jnp.zeros_like(gacc)
        wg, wv = wg_ref[...], wv_ref[...]

        def chunk(cc, carry):
            r0 = pl.multiple_of((tm // FFN_CH - 1 - cc) * FFN_CH, FFN_CH)
            rows = pl.ds(r0, FFN_CH)
            gcv = gc_ref[rows, :].astype(F32)
            ge, th = _gelu(gcv)
            da = da_ref[rows, :].astype(F32)
            for dc, u_ref, w, dbuf, du_ref, gacc in (
                    (da * vc_ref[rows, :].astype(F32) * _gelu_grad(gcv, th), ug_ref, wg, dbufg, dug_ref, gaccg),
                    (da * ge, uv_ref, wv, dbufv, duv_ref, gaccv)):
                dbuf[rows, :] = dc
                dwin = dbuf[pl.ds(r0, win_rows), :]
                d1 = pltpu.roll(dwin, win_rows - 1, 0)[0:FFN_CH, :]
                d2 = pltpu.roll(dwin, win_rows - 2, 0)[0:FFN_CH, :]
                du_ref[rows, :] = (w[2:3, :] * dc + w[1:2, :] * d1 + w[0:1, :] * d2).astype(BF16)
                u = u_ref[rows, :].astype(F32)
                gacc[0:8, :] += _fold8(d2 * u)
                gacc[8:16, :] += _fold8(d1 * u)
                gacc[16:24, :] += _fold8(dc * u)
                gacc[24:32, :] += _fold8(dc)
            return carry

        lax.fori_loop(0, tm // FFN_CH, chunk, 0)
        for gacc, gp_ref in ((gaccg, gg_ref), (gaccv, gv_ref)):
            for k in range(FFN_K + 1):
                gp_ref[k:k + 1, :] += jnp.sum(gacc[8 * k:8 * k + 8, :], axis=0, keepdims=True)

    assert tm % FFN_CH == 0
    tile = pl.BlockSpec((tm, tc), lambda j, i: (nt - 1 - i, j))
    wspec = pl.BlockSpec((FFN_K, tc), lambda j, i: (0, j))
    gspec = pl.BlockSpec((8, tc), lambda j, i: (0, j))
    return pl.pallas_call(
        body, name=name, grid=(f // tc, nt),
        in_specs=[tile, tile, tile, tile, wspec, wspec, tile],
        out_specs=[tile, tile, gspec, gspec],
        out_shape=[jax.ShapeDtypeStruct((l, f), BF16), jax.ShapeDtypeStruct((l, f), BF16),
                   jax.ShapeDtypeStruct((8, f), F32), jax.ShapeDtypeStruct((8, f), F32)],
        scratch_shapes=[pltpu.VMEM((ext, tc), F32), pltpu.VMEM((ext, tc), F32),
                        pltpu.VMEM((32, tc), F32), pltpu.VMEM((32, tc), F32)],
        compiler_params=_params(("parallel", "arbitrary")),
    )(ug, uv, gc, vc, wg, wv, dact)


def _adamw(parts, w, m, v, *, name):
    r = w.shape[0]
    tr = _tile(r, PACK_ROWS, 16)
    c1 = 1.0 / (1.0 - ADAM_B1 ** ADAM_STEP)
    c2 = 1.0 / (1.0 - ADAM_B2 ** ADAM_STEP)

    def body(p_ref, w_ref, m_ref, v_ref, g_ref, d_ref, nm_ref, nv_ref):
        g = p_ref[0].astype(F32)
        for k in range(1, N_DEV):
            g = g + p_ref[k].astype(F32)
        nm = ADAM_B1 * m_ref[...] + (1.0 - ADAM_B1) * g
        nv = ADAM_B2 * v_ref[...] + (1.0 - ADAM_B2) * (g * g)
        g_ref[...] = g
        nm_ref[...] = nm
        nv_ref[...] = nv
        d_ref[...] = -ADAM_LR * ((nm * c1) / (jnp.sqrt(nv * c2) + ADAM_EPS) + ADAM_WD * w_ref[...])

    tile = pl.BlockSpec((tr, 128), lambda i: (i, 0))
    return pl.pallas_call(
        body, name=name, grid=(r // tr,),
        in_specs=[pl.BlockSpec((N_DEV, tr, 128), lambda i: (0, i, 0)), tile, tile, tile],
        out_specs=[tile, tile, tile, tile],
        out_shape=[jax.ShapeDtypeStruct((r, 128), F32)] * 4,
        compiler_params=_params(("parallel",)),
    )(parts, w, m, v)


def _place():
    return lax.axis_index("x"), lax.axis_index("y"), lax.axis_index("c")


EXCHANGE_SEMS = [pltpu.SemaphoreType.DMA((7,)), pltpu.SemaphoreType.DMA((7,)), pltpu.SemaphoreType.DMA]


class _Gather:
    def __init__(self, x_ref, out_ref, send_sems, recv_sems, local_sem):
        self.x_ref, self.out_ref, self.send_sems, self.recv_sems, self.local_sem = (
            x_ref, out_ref, send_sems, recv_sems, local_sem)

    def _parts(self):
        xx, yy, cc = _place()
        me, sibling = (xx, yy, cc), (xx, yy, 1 - cc)
        chips = [(1 - xx, yy), (xx, 1 - yy), (1 - xx, 1 - yy)]

        def slot(px, py, pc):
            return self.out_ref.at[4 * px + 2 * py + pc]

        def copy(k, block, to, src=None):
            return pltpu.make_async_remote_copy(
                src_ref=slot(*block) if src is None else src, dst_ref=slot(*block),
                send_sem=self.send_sems.at[k], recv_sem=self.recv_sems.at[k], device_id=to, device_id_type=MESH)

        mine = pltpu.make_async_copy(self.x_ref, slot(*me), self.local_sem)
        first = [copy(0, me, sibling, src=self.x_ref)]
        first += [copy(1 + j, me, (*chip, cc), src=self.x_ref) for j, chip in enumerate(chips)]
        return cc, me, sibling, chips, copy, mine, first

    def start(self):
        _, _, _, _, _, mine, first = self._parts()
        mine.start()
        for cp in first:
            cp.start()

    def finish(self):
        cc, me, sibling, chips, copy, mine, first = self._parts()
        passed = [copy(4 + j, (*chip, cc), sibling) for j, chip in enumerate(chips)]
        for j, chip in enumerate(chips):
            copy(1 + j, (*chip, cc), me).wait_recv()
            passed[j].start()
        copy(0, sibling, me).wait_recv()
        for j, chip in enumerate(chips):
            copy(4 + j, (*chip, 1 - cc), me).wait_recv()
        for cp in first + passed:
            cp.wait_send()
        mine.wait()


class _Scatter:
    def __init__(self, s_ref, r_ref, send_sems, recv_sems, local_sem):
        self.s_ref, self.r_ref, self.send_sems, self.recv_sems, self.local_sem = (
            s_ref, r_ref, send_sems, recv_sems, local_sem)

    def _parts(self):
        xx, yy, cc = _place()
        me = 4 * xx + 2 * yy + cc
        local = pltpu.make_async_copy(self.s_ref.at[me], self.r_ref.at[me], self.local_sem)
        copies = []
        for m in range(1, N_DEV):
            px = 1 - xx if m & 4 else xx
            py = 1 - yy if m & 2 else yy
            pc = 1 - cc if m & 1 else cc
            copies.append(pltpu.make_async_remote_copy(
                src_ref=self.s_ref.at[4 * px + 2 * py + pc], dst_ref=self.r_ref.at[me],
                send_sem=self.send_sems.at[m - 1], recv_sem=self.recv_sems.at[m - 1],
                device_id=(px, py, pc), device_id_type=MESH))
        return local, copies

    def start(self):
        local, copies = self._parts()
        local.start()
        for cp in copies:
            cp.start()

    def finish(self):
        local, copies = self._parts()
        for cp in copies:
            cp.wait_recv()
        for cp in copies:
            cp.wait_send()
        local.wait()


def _exchange_call(kind, x, out_shape, *, name):
    def body(x_ref, out_ref, send_sems, recv_sems, local_sem):
        ex = kind(x_ref, out_ref, send_sems, recv_sems, local_sem)
        ex.start()
        ex.finish()

    return pl.pallas_call(
        body, name=name,
        in_specs=[pl.BlockSpec(memory_space=pl.ANY)], out_specs=pl.BlockSpec(memory_space=pl.ANY),
        out_shape=jax.ShapeDtypeStruct(out_shape, x.dtype), scratch_shapes=EXCHANGE_SEMS,
    )(x)


def _all_gather(x, *, name):
    return _exchange_call(_Gather, x, (N_DEV,) + x.shape, name=name)


def _all_to_all(send, *, name):
    return _exchange_call(_Scatter, send, send.shape, name=name)


def _as_rows(a, lead, dtype):
    a = a.astype(dtype)
    size = 1
    for s in a.shape[len(lead):]:
        size *= s
    if size % PACK_ALIGN:
        a = jnp.pad(a.reshape(lead + (size,)), [(0, 0)] * len(lead) + [(0, (-size) % PACK_ALIGN)])
    return a.reshape(lead + (-1, 128))


def _pack(arrays, dtype):
    buf = jnp.concatenate([_as_rows(a, (), dtype) for a in arrays], axis=0)
    return jnp.pad(buf, ((0, (-buf.shape[0]) % PACK_ROWS), (0, 0)))


def _pack_pieces(arrays, dtype):
    buf = jnp.concatenate([_as_rows(a, (N_DEV,), dtype) for a in arrays], axis=1)
    return jnp.pad(buf, ((0, 0), (0, (-buf.shape[1]) % PACK_ROWS), (0, 0)))


def _unpack(buf, shapes, lead=()):
    out, row = [], 0
    for shp in shapes:
        size = 1
        for s in shp:
            size *= s
        rows = (size + (-size) % PACK_ALIGN) // 128
        part = buf[..., row:row + rows, :]
        if size % PACK_ALIGN:
            part = part.reshape(lead + (rows * 128,))[..., :size]
        out.append(part.reshape(lead + tuple(shp)))
        row += rows
    return out


def _unshard(g, axis):
    g = jnp.moveaxis(g, 0, axis)
    shp = list(g.shape)
    return g.reshape(shp[:axis] + [shp[axis] * shp[axis + 1]] + shp[axis + 2:])


def _pieces(full, axis):
    shp = list(full.shape)
    g = full.reshape(shp[:axis] + [N_DEV, shp[axis] // N_DEV] + shp[axis + 1:])
    return jnp.moveaxis(g, axis, 0)


SHARDED = (("meta", 1), ("w_in", 2), ("conv_dw_w", 2), ("w_conv_out", 2), ("w_pool_grp", 3), ("w_attn_out", 2),
           ("w_o", 1), ("w_up", 2), ("ffn_dw_w", 2), ("w_down", 1))
MATRICES = ("w_in", "w_conv_out", "w_pool_grp", "w_attn_out", "w_o", "w_up", "w_down")
REPLICATED = ("norm1", "conv_dw_b", "conv_ln_g", "conv_ln_b", "b_conv_out", "pool_scale", "norm2", "ffn_dw_b",
              "final_norm")
WEIGHTS = ("meta", "norm1", "w_in", "conv_dw_w", "conv_dw_b", "conv_ln_g", "conv_ln_b", "w_conv_out", "b_conv_out",
           "w_pool_grp", "pool_scale", "w_attn_out", "w_o", "norm2", "w_up", "ffn_dw_w", "ffn_dw_b", "w_down",
           "final_norm")


def _block_diag(w_grp):
    g, gc, od = w_grp.shape
    out = jnp.zeros((g * gc, g * od), w_grp.dtype)
    for i in range(g):
        out = out.at[i * gc:(i + 1) * gc, i * od:(i + 1) * od].set(w_grp[i])
    return out


def _block_diag_grad(gw, g):
    gc, od = gw.shape[0] // g, gw.shape[1] // g
    return jnp.stack([gw[i * gc:(i + 1) * gc, i * od:(i + 1) * od] for i in range(g)])


C_CONV = 2 * CONV_CH
C_POOL = C_CONV + POOL_CH
C_ATT = HEADS * HEAD_DIM
C_QKV = C_POOL + 3 * C_ATT


def _layer_fwd(h, p, tag, gather=None):
    d = h.shape[1]
    w_in = p["w_in"]
    hn = _rms_fwd(h, p["norm1"], name=f"rms1_{tag}")
    pa = _mm(hn, w_in[:, :C_POOL], out_dtype=F32, name=f"proj_a_{tag}")
    q_scale = jnp.concatenate([jnp.full((C_ATT,), Q_SCALE, F32), jnp.ones((2 * C_ATT,), F32)])
    qkv = _mm(hn, w_in[:, C_POOL:C_QKV], out_dtype=BF16, col_scale=q_scale, name=f"proj_qkv_{tag}")
    gates = _mm(hn, w_in[:, C_QKV:], out_dtype=BF16, name=f"proj_g_{tag}")
    s = _conv_fwd(pa, p["conv_dw_w"], p["conv_dw_b"], p["conv_ln_g"], p["conv_ln_b"], name=f"conv_{tag}")
    pooled = _pool_fwd(pa, name=f"pool_{tag}")
    att, att32, *got = _attn_fwd(qkv, name=f"attn_{tag}", gather=gather)
    wb = _block_diag(p["w_pool_grp"])
    mixed = _mix_fwd(s, pooled, att, gates, p["w_conv_out"], wb, p["w_attn_out"], p["b_conv_out"], p["pool_scale"],
                     name=f"mix_{tag}")
    h1 = _mm(mixed, p["w_o"], out_dtype=F32, res=h, mask_rows=True, name=f"wo_{tag}")
    hn2 = _rms_fwd(h1, p["norm2"], name=f"rms2_{tag}")
    f = p["w_up"].shape[1] // 2
    ug = _mm(hn2, p["w_up"][:, :f], out_dtype=BF16, name=f"up_g_{tag}")
    uv = _mm(hn2, p["w_up"][:, f:], out_dtype=BF16, name=f"up_v_{tag}")
    act, gc, vc = _ffn_fwd(ug, uv, p["ffn_dw_w"][:, :f], p["ffn_dw_w"][:, f:], p["ffn_dw_b"][:f], p["ffn_dw_b"][f:],
                           name=f"ffn_{tag}")
    h2 = _mm(act, p["w_down"], out_dtype=F32, res=h1, mask_rows=True, name=f"down_{tag}")
    saved = dict(h=h, hn=hn, pa=pa, qkv=qkv, gates=gates, s=s, pooled=pooled, att=att, att32=att32, wb=wb, mixed=mixed,
                 h1=h1,
                 hn2=hn2, ug=ug, uv=uv, gc=gc, vc=vc, act=act)
    return h2, saved, (got[0] if got else None)


def _layer_bwd(dh2, dh2b, p, sv, tag, scatter=None):
    g = {}
    f = p["w_up"].shape[1] // 2
    dact = _mm(dh2b, p["w_down"].T, out_dtype=BF16, name=f"b_down_{tag}")
    g["w_down"] = _mm_tn(sv["act"], dh2b, name=f"g_down_{tag}")
    dug, duv, gpg, gpv = _ffn_bwd(sv["ug"], sv["uv"], sv["gc"], sv["vc"], p["ffn_dw_w"][:, :f], p["ffn_dw_w"][:, f:],
                                  dact, name=f"b_ffn_{tag}")
    g["ffn_dw_w"] = jnp.concatenate([gpg[0:FFN_K], gpv[0:FFN_K]], axis=1)
    g["ffn_dw_b"] = jnp.concatenate([gpg[FFN_K], gpv[FFN_K]])
    w_up_t = p["w_up"].T
    dhn2 = _mm(dug, w_up_t[:f], out_dtype=F32, name=f"b_up_g_{tag}")
    dhn2 = _mm(duv, w_up_t[f:], out_dtype=F32, res=dhn2, name=f"b_up_v_{tag}")
    g["w_up"] = jnp.concatenate([_mm_tn(sv["hn2"], dug, name=f"g_up_g_{tag}"),
                                 _mm_tn(sv["hn2"], duv, name=f"g_up_v_{tag}")], axis=1)
    dh1, dh1b, g["norm2"] = _rms_bwd(sv["h1"], p["norm2"], dhn2, dh2, name=f"b_rms2_{tag}")
    dmixed = _mm(dh1b, p["w_o"].T, out_dtype=BF16, name=f"b_wo_{tag}")
    g["w_o"] = _mm_tn(sv["mixed"], dh1b, name=f"g_wo_{tag}")
    dgates, dya, dyb, dyc, vec = _mix_bwd(sv["s"], sv["pooled"], sv["att"], sv["gates"], p["w_conv_out"], sv["wb"],
                                          p["w_attn_out"], p["b_conv_out"], p["pool_scale"], dmixed,
                                          name=f"b_mix_{tag}")
    g["b_conv_out"], g["pool_scale"] = vec[0], vec[1]
    ds = _mm(dya, p["w_conv_out"].T, out_dtype=F32, name=f"b_conv_out_{tag}")
    dpooled = _mm(dyb, sv["wb"].T, out_dtype=F32, name=f"b_pool_out_{tag}")
    datt = _mm(dyc, p["w_attn_out"].T, out_dtype=BF16, name=f"b_attn_out_{tag}")
    g["w_conv_out"] = _mm_tn(sv["s"], dya, name=f"g_conv_out_{tag}")
    g["w_pool_grp"] = _block_diag_grad(_mm_tn(sv["pooled"], dyb, name=f"g_pool_{tag}"), len(POOL_WINDOWS))
    g["w_attn_out"] = _mm_tn(sv["att"], dyc, name=f"g_attn_out_{tag}")
    dc, gp = _conv_bwd_ln(sv["pa"], ds, p["conv_dw_w"], p["conv_dw_b"], p["conv_ln_g"], p["conv_ln_b"],
                          name=f"b_conv_ln_{tag}")
    g["conv_dw_w"], g["conv_dw_b"], g["conv_ln_g"], g["conv_ln_b"] = gp[0:CONV_K], gp[32], gp[33], gp[34]
    dconv = _conv_bwd_in(sv["pa"], dc, p["conv_dw_w"], name=f"b_conv_in_{tag}")
    dp = _pool_bwd(dpooled, name=f"b_pool_{tag}")
    dq, dk, dv, *recv = _attn_bwd(sv["qkv"], sv["att32"], datt, name=f"b_attn_{tag}", scatter=scatter)
    dk = jnp.moveaxis(dk, 0, 1).reshape(dq.shape).astype(BF16)
    dv = jnp.moveaxis(dv, 0, 1).reshape(dq.shape).astype(BF16)
    w_in_t = p["w_in"].T
    cols = [(jnp.concatenate([dconv, dp, dq, dk, dv], axis=1), 0, C_QKV), (dgates, C_QKV, w_in_t.shape[0])]
    dhn, gw = None, []
    for n, (dcol, lo, hi) in enumerate(cols):
        dhn = _mm(dcol, w_in_t[lo:hi], out_dtype=F32, res=dhn, name=f"b_in{n}_{tag}")
        gw.append(_mm_tn(sv["hn"], dcol, name=f"g_in{n}_{tag}"))
    g["w_in"] = jnp.concatenate(gw, axis=1)
    dh, dhb, g["norm1"] = _rms_bwd(sv["h"], p["norm1"], dhn, dh1, name=f"b_rms1_{tag}")
    return dh, dhb, g, (recv[0] if recv else None)


def kernel(x, meta, norm1, w_in, conv_dw_w, conv_dw_b, conv_ln_g, conv_ln_b, w_conv_out, b_conv_out, w_pool_grp, pool_scale, w_attn_out, w_o, norm2, w_up, ffn_dw_w, ffn_dw_b, w_down, final_norm, loss_target, m_meta, m_norm1, m_w_in, m_conv_dw_w, m_conv_dw_b, m_conv_ln_g, m_conv_ln_b, m_w_conv_out, m_b_conv_out, m_w_pool_grp, m_pool_scale, m_w_attn_out, m_w_o, m_norm2, m_w_up, m_ffn_dw_w, m_ffn_dw_b, m_w_down, m_final_norm, v_meta, v_norm1, v_w_in, v_conv_dw_w, v_conv_dw_b, v_conv_ln_g, v_conv_ln_b, v_w_conv_out, v_b_conv_out, v_w_pool_grp, v_pool_scale, v_w_attn_out, v_w_o, v_norm2, v_w_up, v_ffn_dw_w, v_ffn_dw_b, v_w_down, v_final_norm):
    given = dict(meta=meta, norm1=norm1, w_in=w_in, conv_dw_w=conv_dw_w, conv_dw_b=conv_dw_b, conv_ln_g=conv_ln_g, conv_ln_b=conv_ln_b, w_conv_out=w_conv_out, b_conv_out=b_conv_out, w_pool_grp=w_pool_grp, pool_scale=pool_scale, w_attn_out=w_attn_out, w_o=w_o, norm2=norm2, w_up=w_up, ffn_dw_w=ffn_dw_w, ffn_dw_b=ffn_dw_b, w_down=w_down, final_norm=final_norm)
    mom_m = dict(meta=m_meta, norm1=m_norm1, w_in=m_w_in, conv_dw_w=m_conv_dw_w, conv_dw_b=m_conv_dw_b, conv_ln_g=m_conv_ln_g, conv_ln_b=m_conv_ln_b, w_conv_out=m_w_conv_out, b_conv_out=m_b_conv_out, w_pool_grp=m_w_pool_grp, pool_scale=m_pool_scale, w_attn_out=m_w_attn_out, w_o=m_w_o, norm2=m_norm2, w_up=m_w_up, ffn_dw_w=m_ffn_dw_w, ffn_dw_b=m_ffn_dw_b, w_down=m_w_down, final_norm=m_final_norm)
    mom_v = dict(meta=v_meta, norm1=v_norm1, w_in=v_w_in, conv_dw_w=v_conv_dw_w, conv_dw_b=v_conv_dw_b, conv_ln_g=v_conv_ln_g, conv_ln_b=v_conv_ln_b, w_conv_out=v_w_conv_out, b_conv_out=v_b_conv_out, w_pool_grp=v_w_pool_grp, pool_scale=v_pool_scale, w_attn_out=v_w_attn_out, w_o=v_w_o, norm2=v_norm2, w_up=v_w_up, ffn_dw_w=v_ffn_dw_w, ffn_dw_b=v_ffn_dw_b, w_down=v_w_down, final_norm=v_final_norm)
    sharded_axis = dict(SHARDED)
    vectors = [n for n, _ in SHARDED if n not in MATRICES]
    depth = norm1.shape[0]

    got_vec = _all_gather(_pack([given[n] for n in vectors], F32), name="gather_vectors")
    full = {n: given[n] for n in REPLICATED}
    for n, a in zip(vectors, _unpack(got_vec, [given[n].shape for n in vectors], (N_DEV,))):
        full[n] = _unshard(a, sharded_axis[n])
    mat_shapes = [given[n].shape[1:] for n in MATRICES]

    def layer_matrices(i):
        return _pack([given[n][i] for n in MATRICES], BF16)

    got_mat = _all_gather(layer_matrices(0), name="gather_matrices_l0")

    xs = x[0]
    d = xs.shape[1]
    h = jnp.concatenate([jnp.zeros((PAD, d), F32), full["meta"], xs], axis=0)
    layers, saved = [], []
    for i in range(depth):
        p = {n: full[n][i] for n in full if n not in ("meta", "final_norm")}
        for n, a in zip(MATRICES, _unpack(got_mat, mat_shapes, (N_DEV,))):
            p[n] = _unshard(a, sharded_axis[n] - 1)
        layers.append(p)
        h, sv, got_mat = _layer_fwd(h, p, f"l{i}", gather=layer_matrices(i + 1) if i + 1 < depth else None)
        saved.append(sv)
    loss_part, dh, dhb, g_final = _loss_head(h, full["final_norm"], loss_target[0], name="loss_head")

    names = [n for n, _ in SHARDED if n != "meta"]

    def layer_pieces(g, extra=()):
        return _pack_pieces([_pieces(g[n], sharded_axis[n] - 1) for n in names] + list(extra), BF16)

    grads, recvs = [None] * depth, [None] * depth
    for i in reversed(range(depth)):
        send = layer_pieces(grads[i + 1]) if i + 1 < depth else None
        dh, dhb, grads[i], got = _layer_bwd(dh, dhb, layers[i], saved[i], f"l{i}", scatter=send)
        if send is not None:
            recvs[i + 1] = got
    grad_x = dh[FRONT:][None]
    recvs[0] = _all_to_all(layer_pieces(grads[0], [_pieces(dh[PAD:FRONT], sharded_axis["meta"])]), name="scatter_grads_l0")
    full_grad = {n: jnp.stack([grads[i][n] for i in range(depth)]) for n in REPLICATED if n != "final_norm"}
    full_grad["final_norm"] = g_final
    rep_shapes = [given[n].shape for n in REPLICATED] + [(1,)]
    rep_parts = _all_gather(_pack([full_grad[n] for n in REPLICATED] + [loss_part.reshape(1)], F32),
                            name="gather_partials")

    out, per_layer = {}, []
    for i in range(depth):
        extra = ["meta"] if i == 0 else []
        pick = lambda src: [src[n][i] for n in names] + [src[n] for n in extra]
        res = _adamw(recvs[i], _pack(pick(given), F32), _pack(pick(mom_m), F32), _pack(pick(mom_v), F32),
                     name=f"adamw_l{i}")
        shapes = [given[n].shape[1:] for n in names] + [given[n].shape for n in extra]
        per_layer.append([_unpack(buf, shapes) for buf in res])
    for k, kind in enumerate(("grad", "delta", "new_m", "new_v")):
        for j, n in enumerate(names):
            out[kind, n] = jnp.stack([per_layer[i][k][j] for i in range(depth)])
        out[kind, "meta"] = per_layer[0][k][len(names)]
    rep_w = [given[n] for n in REPLICATED] + [jnp.zeros((1,), F32)]
    rep_m = [mom_m[n] for n in REPLICATED] + [jnp.zeros((1,), F32)]
    rep_v = [mom_v[n] for n in REPLICATED] + [jnp.ones((1,), F32)]
    res = _adamw(rep_parts, _pack(rep_w, F32), _pack(rep_m, F32), _pack(rep_v, F32), name="adamw_replicated")
    for kind, buf in zip(("grad", "delta", "new_m", "new_v"), res):
        for n, a in zip(list(REPLICATED) + ["loss"], _unpack(buf, rep_shapes)):
            out[kind, n] = a
    loss = out["grad", "loss"][0]
    return (loss, grad_x, *[out["grad", n] for n in WEIGHTS], *[out["delta", n] for n in WEIGHTS],
            *[out["new_m", n] for n in WEIGHTS], *[out["new_v", n] for n in WEIGHTS])
```

```python
import functools

import jax
import jax.numpy as jnp
from jax import lax
from jax.experimental import pallas as pl
from jax.experimental.pallas import tpu as pltpu

F32 = jnp.float32
BF16 = jnp.bfloat16
MESH = pl.DeviceIdType.MESH

N_DEV = 8
N_META = 16
BLOCK = 128
PAD = 240
FRONT = PAD + N_META
HEADS = 4
HEAD_DIM = 128
CONV_CH = 256
CONV_K = 31
POOL_CH = 256
POOL_WINDOWS = (2, 4, 8, 16)
FFN_K = 3
EPS = 1e-6
ADAM_LR, ADAM_B1, ADAM_B2, ADAM_EPS, ADAM_WD, ADAM_STEP = 0.001, 0.9, 0.999, 1e-08, 0.01, 10

VMEM_LIMIT = 56 * 1024 * 1024
CONV_HALO = 32
POOL_HALO = 16
FFN_HALO = 8
PACK_ALIGN = 16 * 128
PACK_ROWS = 512


def _tile(n, cap, unit):
    if n <= cap:
        return n
    best = None
    t = unit
    while t <= cap:
        if n % t == 0:
            best = t
        t += unit
    assert best is not None, (n, cap, unit)
    return best


def _params(sem):
    return pltpu.CompilerParams(dimension_semantics=sem, vmem_limit_bytes=VMEM_LIMIT)


def _sigmoid(x):
    return 0.5 * jnp.tanh(0.5 * x) + 0.5


MM_MAX_K = 3072
MM_RESIDENT_B = 3072 * 1024 * 2


def _mm(a, b, *, out_dtype, name, res=None, col_scale=None, mask_rows=False, tn_cap=768):
    m, k = a.shape
    k2, n = b.shape
    assert k == k2 and k <= MM_MAX_K
    tn = n if k * n * 2 <= MM_RESIDENT_B else _tile(n, tn_cap, 128)
    tm = _tile(m, 1280 if (k <= 1024 and tn <= 1024) else 640, 128)

    def body(*refs):
        refs = list(refs)
        a_ref, b_ref = refs[:2]
        o_ref = refs[-1]
        r_ref = refs[2] if res is not None else None
        c_ref = refs[-2] if col_scale is not None else None
        y = jnp.dot(a_ref[...].astype(BF16), b_ref[...].astype(BF16), preferred_element_type=F32)
        if col_scale is not None:
            y = y * c_ref[...]
        if res is not None:
            y = y + r_ref[...].astype(F32)
        if mask_rows:
            row = pl.program_id(0) * tm + lax.broadcasted_iota(jnp.int32, (tm, 1), 0)
            y = jnp.where(row >= PAD, y, 0.0)
        o_ref[...] = y.astype(out_dtype)

    in_specs = [pl.BlockSpec((tm, k), lambda i, j: (i, 0)), pl.BlockSpec((k, tn), lambda i, j: (0, j))]
    args = [a, b]
    if res is not None:
        in_specs.append(pl.BlockSpec((tm, tn), lambda i, j: (i, j)))
        args.append(res)
    if col_scale is not None:
        in_specs.append(pl.BlockSpec((1, tn), lambda i, j: (0, j)))
        args.append(col_scale.reshape(1, n))
    return pl.pallas_call(
        body, name=name, grid=(m // tm, n // tn),
        in_specs=in_specs, out_specs=pl.BlockSpec((tm, tn), lambda i, j: (i, j)),
        out_shape=jax.ShapeDtypeStruct((m, n), out_dtype),
        compiler_params=_params(("parallel", "parallel")),
    )(*args)


def _mm_tn(a, b, *, name, t1_cap=512, tn_cap=1024, tl_cap=3328):
    l, k1 = a.shape
    l2, n = b.shape
    assert l == l2
    t1, tn, tl = _tile(k1, t1_cap, 128), _tile(n, tn_cap, 128), _tile(l, tl_cap, 128)

    def body(a_ref, b_ref, o_ref):
        @pl.when(pl.program_id(2) == 0)
        def _():
            o_ref[...] = jnp.zeros_like(o_ref)

        o_ref[...] += lax.dot_general(a_ref[...].astype(BF16), b_ref[...].astype(BF16),
                                      (((0,), (0,)), ((), ())), preferred_element_type=F32)

    return pl.pallas_call(
        body, name=name, grid=(k1 // t1, n // tn, l // tl),
        in_specs=[pl.BlockSpec((tl, t1), lambda i, j, ll: (ll, i)), pl.BlockSpec((tl, tn), lambda i, j, ll: (ll, j))],
        out_specs=pl.BlockSpec((t1, tn), lambda i, j, ll: (i, j)),
        out_shape=jax.ShapeDtypeStruct((k1, n), F32),
        compiler_params=_params(("parallel", "parallel", "arbitrary")),
    )(a, b)


def _rms_fwd(x, g, *, name):
    l, d = x.shape
    tm = _tile(l, 640, 128)

    def body(x_ref, g_ref, o_ref):
        xv = x_ref[...]
        r = lax.rsqrt(jnp.mean(xv * xv, axis=-1, keepdims=True) + EPS)
        o_ref[...] = (xv * r * g_ref[...]).astype(BF16)

    return pl.pallas_call(
        body, name=name, grid=(l // tm,),
        in_specs=[pl.BlockSpec((tm, d), lambda i: (i, 0)), pl.BlockSpec((1, d), lambda i: (0, 0))],
        out_specs=pl.BlockSpec((tm, d), lambda i: (i, 0)),
        out_shape=jax.ShapeDtypeStruct((l, d), BF16),
        compiler_params=_params(("parallel",)),
    )(x, g.reshape(1, d))


def _rms_bwd(x, g, dy, dres, *, name):
    l, d = x.shape
    tm = _tile(l, 640, 128)

    def body(x_ref, g_ref, dy_ref, dr_ref, dx_ref, dxb_ref, dg_ref):
        i = pl.program_id(0)

        @pl.when(i == 0)
        def _():
            dg_ref[...] = jnp.zeros_like(dg_ref)

        xv = x_ref[...]
        r = lax.rsqrt(jnp.mean(xv * xv, axis=-1, keepdims=True) + EPS)
        xh = xv * r
        dyv = dy_ref[...].astype(F32)
        dxh = dyv * g_ref[...]
        dx = r * (dxh - xh * jnp.mean(dxh * xh, axis=-1, keepdims=True)) + dr_ref[...]
        row = i * tm + lax.broadcasted_iota(jnp.int32, (tm, 1), 0)
        dx = jnp.where(row >= PAD, dx, 0.0)
        dx_ref[...] = dx
        dxb_ref[...] = dx.astype(BF16)
        dg_ref[0:1, :] += jnp.sum(dyv * xh, axis=0, keepdims=True)

    tile = pl.BlockSpec((tm, d), lambda i: (i, 0))
    dx, dxb, dg = pl.pallas_call(
        body, name=name, grid=(l // tm,),
        in_specs=[tile, pl.BlockSpec((1, d), lambda i: (0, 0)), tile, tile],
        out_specs=[tile, tile, pl.BlockSpec((8, d), lambda i: (0, 0))],
        out_shape=[jax.ShapeDtypeStruct((l, d), F32), jax.ShapeDtypeStruct((l, d), BF16),
                   jax.ShapeDtypeStruct((8, d), F32)],
        compiler_params=_params(("arbitrary",)),
    )(x, g.reshape(1, d), dy, dres)
    return dx, dxb, dg[0]


def _loss_head(h, g, target, *, name):
    l, d = h.shape
    tm = FRONT
    assert l % tm == 0 and target.shape[0] == l - tm

    def body(h_ref, g_ref, t_ref, dh_ref, dhb_ref, loss_ref, dg_ref):
        i = pl.program_id(0)

        @pl.when(i == 0)
        def _():
            loss_ref[...] = jnp.zeros_like(loss_ref)
            dg_ref[...] = jnp.zeros_like(dg_ref)
            dh_ref[...] = jnp.zeros_like(dh_ref)
            dhb_ref[...] = jnp.zeros_like(dhb_ref)

        @pl.when(i > 0)
        def _():
            xv = h_ref[...]
            r = lax.rsqrt(jnp.mean(xv * xv, axis=-1, keepdims=True) + EPS)
            xh = xv * r
            gv = g_ref[...]
            err = xh * gv - t_ref[...]
            loss_ref[...] += 0.5 * jnp.sum(jnp.mean(err * err, axis=-1, keepdims=True))
            dy = err * (1.0 / d)
            dxh = dy * gv
            dh = r * (dxh - xh * jnp.mean(dxh * xh, axis=-1, keepdims=True))
            dh_ref[...] = dh
            dhb_ref[...] = dh.astype(BF16)
            dg_ref[0:1, :] += jnp.sum(dy * xh, axis=0, keepdims=True)

    tile = pl.BlockSpec((tm, d), lambda i: (i, 0))
    dh, dhb, loss, dg = pl.pallas_call(
        body, name=name, grid=(l // tm,),
        in_specs=[tile, pl.BlockSpec((1, d), lambda i: (0, 0)),
                  pl.BlockSpec((tm, d), lambda i: (jnp.maximum(i - 1, 0), 0))],
        out_specs=[tile, tile, pl.BlockSpec((8, 128), lambda i: (0, 0)), pl.BlockSpec((8, d), lambda i: (0, 0))],
        out_shape=[jax.ShapeDtypeStruct((l, d), F32), jax.ShapeDtypeStruct((l, d), BF16),
                   jax.ShapeDtypeStruct((8, 128), F32), jax.ShapeDtypeStruct((8, d), F32)],
        compiler_params=_params(("arbitrary",)),
    )(h, g.reshape(1, d), target)
    return loss[0, 0], dh, dhb, dg[0]


def _conv_tile(l):
    return _tile(l, 640, 128)


def _phase_views(buf, views, tm, max_off):
    for s in range(8):
        n = tm + 8 * ((max_off - s) // 8)
        views[s, 0:n, :] = buf[s:s + n, :]
    return views


def _tap(views, off, tm):
    a, s = divmod(off, 8)
    return views[s, 8 * a:8 * a + tm, :]


def _conv_core(a, gt, buf, views, dw_w, dw_b, first):
    tm = a.shape[0]

    @pl.when(first)
    def _():
        buf[0:CONV_HALO, :] = jnp.zeros((CONV_HALO, CONV_CH), F32)

    @pl.when(jnp.logical_not(first))
    def _():
        buf[0:CONV_HALO, :] = buf[tm:tm + CONV_HALO, :]

    sg = _sigmoid(gt)
    buf[CONV_HALO:CONV_HALO + tm, :] = a * sg
    _phase_views(buf, views, tm, CONV_HALO)
    c = jnp.zeros((tm, CONV_CH), F32) + dw_b
    for k in range(CONV_K):
        c = c + dw_w[k:k + 1, :] * _tap(views, CONV_HALO - (CONV_K - 1) + k, tm)
    return c, sg


def _layer_norm(c, ln_g, ln_b):
    mu = jnp.mean(c, axis=-1, keepdims=True)
    xc = c - mu
    r = lax.rsqrt(jnp.mean(xc * xc, axis=-1, keepdims=True) + EPS)
    xh = xc * r
    return xh, r, xh * ln_g + ln_b


def _conv_fwd(pa, dw_w, dw_b, ln_g, ln_b, *, name):
    l = pa.shape[0]
    tm = _conv_tile(l)

    def body(a_ref, gt_ref, w_ref, b_ref, g_ref, bb_ref, o_ref, buf, views):
        c, _ = _conv_core(a_ref[...], gt_ref[...], buf, views, w_ref[...], b_ref[...], pl.program_id(0) == 0)
        _, _, y = _layer_norm(c, g_ref[...], bb_ref[...])
        o_ref[...] = (y * _sigmoid(y)).astype(BF16)

    vec = pl.BlockSpec((1, CONV_CH), lambda i: (0, 0))
    return pl.pallas_call(
        body, name=name, grid=(l // tm,),
        in_specs=[pl.BlockSpec((tm, CONV_CH), lambda i: (i, 0)), pl.BlockSpec((tm, CONV_CH), lambda i: (i, 1)),
                  pl.BlockSpec((CONV_K, CONV_CH), lambda i: (0, 0)), vec, vec, vec],
        out_specs=pl.BlockSpec((tm, CONV_CH), lambda i: (i, 0)),
        out_shape=jax.ShapeDtypeStruct((l, CONV_CH), BF16),
        scratch_shapes=[pltpu.VMEM((CONV_HALO + tm, CONV_CH), F32), pltpu.VMEM((8, CONV_HALO + tm, CONV_CH), F32)],
        compiler_params=_params(("arbitrary",)),
    )(pa, pa, dw_w, dw_b.reshape(1, -1), ln_g.reshape(1, -1), ln_b.reshape(1, -1))


def _conv_bwd_ln(pa, ds, dw_w, dw_b, ln_g, ln_b, *, name):
    l = pa.shape[0]
    tm = _conv_tile(l)

    def body(a_ref, gt_ref, ds_ref, w_ref, b_ref, g_ref, bb_ref, dc_ref, gp_ref, buf, views):
        i = pl.program_id(0)

        @pl.when(i == 0)
        def _():
            gp_ref[...] = jnp.zeros_like(gp_ref)

        c, _ = _conv_core(a_ref[...], gt_ref[...], buf, views, w_ref[...], b_ref[...], i == 0)
        xh, r, y = _layer_norm(c, g_ref[...], bb_ref[...])
        sy = _sigmoid(y)
        dy = ds_ref[...] * (sy * (1.0 + y * (1.0 - sy)))
        dxh = dy * g_ref[...]
        dc = r * (dxh - jnp.mean(dxh, axis=-1, keepdims=True) - xh * jnp.mean(dxh * xh, axis=-1, keepdims=True))
        dc_ref[...] = dc
        for k in range(CONV_K):
            gp_ref[k:k + 1, :] += jnp.sum(dc * _tap(views, CONV_HALO - (CONV_K - 1) + k, tm), axis=0, keepdims=True)
        gp_ref[32:33, :] += jnp.sum(dc, axis=0, keepdims=True)
        gp_ref[33:34, :] += jnp.sum(dy * xh, axis=0, keepdims=True)
        gp_ref[34:35, :] += jnp.sum(dy, axis=0, keepdims=True)

    vec = pl.BlockSpec((1, CONV_CH), lambda i: (0, 0))
    return pl.pallas_call(
        body, name=name, grid=(l // tm,),
        in_specs=[pl.BlockSpec((tm, CONV_CH), lambda i: (i, 0)), pl.BlockSpec((tm, CONV_CH), lambda i: (i, 1)),
                  pl.BlockSpec((tm, CONV_CH), lambda i: (i, 0)),
                  pl.BlockSpec((CONV_K, CONV_CH), lambda i: (0, 0)), vec, vec, vec],
        out_specs=[pl.BlockSpec((tm, CONV_CH), lambda i: (i, 0)), pl.BlockSpec((40, CONV_CH), lambda i: (0, 0))],
        out_shape=[jax.ShapeDtypeStruct((l, CONV_CH), F32), jax.ShapeDtypeStruct((40, CONV_CH), F32)],
        scratch_shapes=[pltpu.VMEM((CONV_HALO + tm, CONV_CH), F32), pltpu.VMEM((8, CONV_HALO + tm, CONV_CH), F32)],
        compiler_params=_params(("arbitrary",)),
    )(pa, pa, ds, dw_w, dw_b.reshape(1, -1), ln_g.reshape(1, -1), ln_b.reshape(1, -1))


def _conv_bwd_in(pa, dc, dw_w, *, name):
    l = pa.shape[0]
    tm = _conv_tile(l)
    nt = l // tm

    def body(a_ref, gt_ref, dc_ref, w_ref, o_ref, buf, views):
        first = pl.program_id(0) == 0

        @pl.when(first)
        def _():
            buf[tm:tm + CONV_HALO, :] = jnp.zeros((CONV_HALO, CONV_CH), F32)

        @pl.when(jnp.logical_not(first))
        def _():
            buf[tm:tm + CONV_HALO, :] = buf[0:CONV_HALO, :]

        buf[0:tm, :] = dc_ref[...]
        w = w_ref[...]
        _phase_views(buf, views, tm, CONV_K - 1)
        dhc = jnp.zeros((tm, CONV_CH), F32)
        for k in range(CONV_K):
            dhc = dhc + w[k:k + 1, :] * _tap(views, CONV_K - 1 - k, tm)
        a = a_ref[...]
        sg = _sigmoid(gt_ref[...])
        o_ref[:, 0:CONV_CH] = (dhc * sg).astype(BF16)
        o_ref[:, CONV_CH:2 * CONV_CH] = (dhc * a * sg * (1.0 - sg)).astype(BF16)

    return pl.pallas_call(
        body, name=name, grid=(nt,),
        in_specs=[pl.BlockSpec((tm, CONV_CH), lambda i: (nt - 1 - i, 0)),
                  pl.BlockSpec((tm, CONV_CH), lambda i: (nt - 1 - i, 1)),
                  pl.BlockSpec((tm, CONV_CH), lambda i: (nt - 1 - i, 0)),
                  pl.BlockSpec((CONV_K, CONV_CH), lambda i: (0, 0))],
        out_specs=pl.BlockSpec((tm, 2 * CONV_CH), lambda i: (nt - 1 - i, 0)),
        out_shape=jax.ShapeDtypeStruct((l, 2 * CONV_CH), BF16),
        scratch_shapes=[pltpu.VMEM((tm + CONV_HALO, CONV_CH), F32), pltpu.VMEM((8, tm + CONV_HALO, CONV_CH), F32)],
        compiler_params=_params(("arbitrary",)),
    )(pa, pa, dc, dw_w)


def _pool_consts(tm, row0):
    lane = lax.broadcasted_iota(jnp.int32, (1, POOL_CH), 1)
    grp = lane // (POOL_CH // len(POOL_WINDOWS))
    win = jnp.where(grp == 0, 2.0, jnp.where(grp == 1, 4.0, jnp.where(grp == 2, 8.0, 16.0))).astype(F32)
    pos = (row0 + lax.broadcasted_iota(jnp.int32, (tm, 1), 0) - PAD).astype(F32)
    cnt = jnp.maximum(jnp.minimum(pos + 1.0, win), 1.0)
    return grp, cnt


def _pool_select(grp, s2, s4, s8, s16):
    return jnp.where(grp == 0, s2, jnp.where(grp == 1, s4, jnp.where(grp == 2, s8, s16)))


def _pool_fwd(pa, *, name):
    l = pa.shape[0]
    tm = _conv_tile(l)
    ext = POOL_HALO + tm

    def body(p_ref, o_ref, buf):
        i = pl.program_id(0)

        @pl.when(i == 0)
        def _():
            buf[0:POOL_HALO, :] = jnp.zeros((POOL_HALO, POOL_CH), F32)

        @pl.when(i > 0)
        def _():
            buf[0:POOL_HALO, :] = buf[tm:tm + POOL_HALO, :]

        p = p_ref[...]
        buf[POOL_HALO:ext, :] = p
        x = buf[...]
        s2 = x + pltpu.roll(x, 1, 0)
        s4 = s2 + pltpu.roll(s2, 2, 0)
        s8 = s4 + pltpu.roll(s4, 4, 0)
        s16 = s8 + pltpu.roll(s8, 8, 0)
        grp, cnt = _pool_consts(tm, i * tm)
        s = _pool_select(grp, s2, s4, s8, s16)[POOL_HALO:ext, :]
        o_ref[...] = (s / cnt - p).astype(BF16)

    return pl.pallas_call(
        body, name=name, grid=(l // tm,),
        in_specs=[pl.BlockSpec((tm, POOL_CH), lambda i: (i, 2))],
        out_specs=pl.BlockSpec((tm, POOL_CH), lambda i: (i, 0)),
        out_shape=jax.ShapeDtypeStruct((l, POOL_CH), BF16),
        scratch_shapes=[pltpu.VMEM((ext, POOL_CH), F32)],
        compiler_params=_params(("arbitrary",)),
    )(pa)


def _pool_bwd(dpooled, *, name):
    l = dpooled.shape[0]
    tm = _conv_tile(l)
    nt = l // tm
    ext = tm + POOL_HALO

    def body(d_ref, o_ref, buf):
        i = pl.program_id(0)

        @pl.when(i == 0)
        def _():
            buf[tm:ext, :] = jnp.zeros((POOL_HALO, POOL_CH), F32)

        @pl.when(i > 0)
        def _():
            buf[tm:ext, :] = buf[0:POOL_HALO, :]

        d = d_ref[...]
        grp, cnt = _pool_consts(tm, (nt - 1 - i) * tm)
        buf[0:tm, :] = d / cnt
        x = buf[...]
        s2 = x + pltpu.roll(x, ext - 1, 0)
        s4 = s2 + pltpu.roll(s2, ext - 2, 0)
        s8 = s4 + pltpu.roll(s4, ext - 4, 0)
        s16 = s8 + pltpu.roll(s8, ext - 8, 0)
        s = _pool_select(grp, s2, s4, s8, s16)[0:tm, :]
        o_ref[...] = (s - d).astype(BF16)

    return pl.pallas_call(
        body, name=name, grid=(nt,),
        in_specs=[pl.BlockSpec((tm, POOL_CH), lambda i: (nt - 1 - i, 0))],
        out_specs=pl.BlockSpec((tm, POOL_CH), lambda i: (nt - 1 - i, 0)),
        out_shape=jax.ShapeDtypeStruct((l, POOL_CH), BF16),
        scratch_shapes=[pltpu.VMEM((ext, POOL_CH), F32)],
        compiler_params=_params(("arbitrary",)),
    )(dpooled)


ATT_TQ = 256
ATT_TK = 2 * BLOCK
ATT_SUB = ATT_TK // BLOCK
LOG2E = 1.4426950408889634
LN2 = 0.6931471805599453
Q_SCALE = HEAD_DIM ** -0.5 * LOG2E
ATT_CUT = 160.0


def _tri_ones():
    r = lax.broadcasted_iota(jnp.int32, (2 * BLOCK, 2 * BLOCK), 0) % BLOCK
    c = lax.broadcasted_iota(jnp.int32, (2 * BLOCK, 2 * BLOCK), 1)
    return jnp.where((c >= BLOCK) | (r > c), 1.0, 0.0).astype(BF16)


def _split_dot(x, rhs):
    hi = x.astype(BF16)
    lo = (x - hi.astype(F32)).astype(BF16)
    return jnp.dot(jnp.concatenate([hi, lo], axis=1), rhs, preferred_element_type=F32)


def _scores(q, kt, qpos, base, masked):
    z = lax.dot_general(q, kt, (((1,), (1,)), ((), ())), preferred_element_type=F32)
    sp = jnp.log2(1.0 + jnp.exp2(-jnp.abs(z)))
    lb = jnp.minimum(z, 0.0) - sp
    lk = lb - z
    valid = None
    if masked:
        kpos = base + lax.broadcasted_iota(jnp.int32, (1, z.shape[1]), 1)
        valid = (kpos < qpos) & (kpos >= PAD)
        lk = jnp.where(valid, lk, 0.0)
    return lk, lb, valid


def _suffix(x, tri, carry):
    wts = [_split_dot(x[:, b * BLOCK:(b + 1) * BLOCK], tri) for b in range(ATT_SUB)]
    offs = [None] * ATT_SUB
    s = carry
    for b in reversed(range(ATT_SUB)):
        offs[b] = wts[b][:, :BLOCK] + s
        s = s + wts[b][:, BLOCK:]
    return jnp.concatenate(offs, axis=1), s


def _walk_tiles(i, tq, step):
    t_top = ((i + 1) * tq - 1) // ATT_TK
    t_diag = (i * tq) // ATT_TK
    n_plain = jnp.maximum(t_diag - 1, 0)

    def masked(jj, top):
        return step(t_top - jj, True)

    def live(carry):
        return (carry[0] < n_plain) & (carry[1] > -ATT_CUT)

    def plain(carry):
        return carry[0] + 1, step(t_diag - 1 - carry[0], False)

    top = lax.fori_loop(0, t_top - t_diag + 1, masked, jnp.float32(0.0))
    _, top = lax.while_loop(live, plain, (jnp.int32(0), top))

    @pl.when((t_diag > 0) & (top > -ATT_CUT))
    def _():
        step(0, True)


def _tile_base(t):
    base = t * ATT_TK
    return base if isinstance(base, int) else pl.multiple_of(base, BLOCK)


def _attn_fwd(qkv, *, name, gather=None):
    l = qkv.shape[0]
    tq = ATT_TQ
    nq = l // tq
    assert l % tq == 0 and l % ATT_TK == 0

    def body(*refs):
        if gather is None:
            q_ref, k_ref, v_ref, o_ref, o32_ref, acc_ref, r_ref = refs
        else:
            q_ref, k_ref, v_ref, x_hbm, o_ref, o32_ref, got_hbm, acc_ref, r_ref = refs[:9]
            exchange = _Gather(x_hbm, got_hbm, *refs[9:])
            first = (pl.program_id(0) == 0) & (pl.program_id(1) == 0)
            last = (pl.program_id(0) == HEADS - 1) & (pl.program_id(1) == nq - 1)
            pl.when(first)(exchange.start)
        i = pl.program_id(1)
        acc_ref[...] = jnp.zeros_like(acc_ref)
        r_ref[...] = jnp.zeros_like(r_ref)
        q = q_ref[...]
        qpos = i * tq + lax.broadcasted_iota(jnp.int32, (tq, 1), 0)
        tri = _tri_ones()

        def step(t, masked):
            base = _tile_base(t)
            lk, lb, valid = _scores(q, k_ref[pl.ds(base, ATT_TK), :], qpos, base, masked)
            off, r_new = _suffix(lk, tri, r_ref[...])
            a = jnp.exp2(lb + off)
            if masked:
                a = jnp.where(valid, a, 0.0)
            acc_ref[...] += jnp.dot(a.astype(BF16), v_ref[pl.ds(base, ATT_TK), :], preferred_element_type=F32)
            r_ref[...] = r_new
            return jnp.max(r_new)

        _walk_tiles(i, tq, step)
        o_ref[...] = acc_ref[...].astype(BF16)
        o32_ref[...] = acc_ref[...]
        if gather is not None:
            pl.when(last)(exchange.finish)

    tile = pl.BlockSpec((tq, HEAD_DIM), lambda h, i: (i, h))
    hbm = pl.BlockSpec(memory_space=pl.ANY)
    hosted = gather is not None
    return pl.pallas_call(
        body, name=name, grid=(HEADS, nq),
        in_specs=[tile,
                  pl.BlockSpec((l, HEAD_DIM), lambda h, i: (0, HEADS + h)),
                  pl.BlockSpec((l, HEAD_DIM), lambda h, i: (0, 2 * HEADS + h))] + [hbm] * hosted,
        out_specs=[tile, tile] + [hbm] * hosted,
        out_shape=[jax.ShapeDtypeStruct((l, HEADS * HEAD_DIM), BF16),
                   jax.ShapeDtypeStruct((l, HEADS * HEAD_DIM), F32)]
        + ([jax.ShapeDtypeStruct((N_DEV,) + gather.shape, gather.dtype)] if hosted else []),
        scratch_shapes=[pltpu.VMEM((tq, HEAD_DIM), F32), pltpu.VMEM((tq, BLOCK), F32)] + EXCHANGE_SEMS * hosted,
        compiler_params=_params(("arbitrary", "arbitrary")),
    )(qkv, qkv, qkv, *([gather] * hosted))


def _attn_bwd(qkv, att, datt, *, name, scatter=None):
    l = qkv.shape[0]
    tq = ATT_TQ
    nq = l // tq
    assert l % tq == 0 and l % ATT_TK == 0

    def body(*refs):
        if scatter is None:
            q_ref, k_ref, v_ref, o_ref, do_ref, dq_ref, dk_hbm, dv_hbm, dk_acc, dv_acc, dq_acc, r_ref, s_ref, sem = refs
        else:
            (q_ref, k_ref, v_ref, o_ref, do_ref, send_hbm, dq_ref, dk_hbm, dv_hbm, recv_hbm,
             dk_acc, dv_acc, dq_acc, r_ref, s_ref, sem) = refs[:16]
            exchange = _Scatter(send_hbm, recv_hbm, *refs[16:])
            pl.when((pl.program_id(0) == 0) & (pl.program_id(1) == 0))(exchange.start)
        h = pl.program_id(0)
        i = pl.program_id(1)

        @pl.when(i == 0)
        def _():
            dk_acc[...] = jnp.zeros_like(dk_acc)
            dv_acc[...] = jnp.zeros_like(dv_acc)

        dq_acc[...] = jnp.zeros_like(dq_acc)
        r_ref[...] = jnp.zeros_like(r_ref)
        s_ref[...] = jnp.zeros_like(s_ref)
        q = q_ref[...]
        do = do_ref[...]
        ptot = jnp.sum(do.astype(F32) * o_ref[...], axis=-1, keepdims=True)
        qpos = i * tq + lax.broadcasted_iota(jnp.int32, (tq, 1), 0)
        tri = _tri_ones()

        def step(t, masked):
            base = _tile_base(t)
            kt = k_ref[pl.ds(base, ATT_TK), :]
            vt = v_ref[pl.ds(base, ATT_TK), :]
            lk, lb, valid = _scores(q, kt, qpos, base, masked)
            off, r_new = _suffix(lk, tri, r_ref[...])
            a = jnp.exp2(lb + off)
            if masked:
                a = jnp.where(valid, a, 0.0)
            ab = a.astype(BF16)
            da = lax.dot_general(do, vt, (((1,), (1,)), ((), ())), preferred_element_type=F32)
            p = ab.astype(F32) * da
            poff, s_new = _suffix(p, tri, s_ref[...])
            dz = (p - jnp.exp2(lb) * (ptot - poff)) * LN2
            if masked:
                dz = jnp.where(valid, dz, 0.0)
            dzb = dz.astype(BF16)
            dq_acc[...] += jnp.dot(dzb, kt, preferred_element_type=F32)
            dk_acc[pl.ds(base, ATT_TK), :] += lax.dot_general(dzb, q, (((0,), (0,)), ((), ())),
                                                              preferred_element_type=F32)
            dv_acc[pl.ds(base, ATT_TK), :] += lax.dot_general(ab, do, (((0,), (0,)), ((), ())),
                                                              preferred_element_type=F32)
            r_ref[...] = r_new
            s_ref[...] = s_new
            return jnp.max(r_new)

        _walk_tiles(i, tq, step)
        dq_ref[...] = (dq_acc[...] * Q_SCALE).astype(BF16)

        @pl.when(i == nq - 1)
        def _():
            ck = pltpu.make_async_copy(dk_acc, dk_hbm.at[h], sem.at[0])
            cv = pltpu.make_async_copy(dv_acc, dv_hbm.at[h], sem.at[1])
            ck.start()
            cv.start()
            ck.wait()
            cv.wait()

        if scatter is not None:
            pl.when((h == HEADS - 1) & (i == nq - 1))(exchange.finish)

    tile = pl.BlockSpec((tq, HEAD_DIM), lambda h, i: (i, h))
    hbm = pl.BlockSpec(memory_space=pl.ANY)
    hosted = scatter is not None
    return pl.pallas_call(
        body, name=name, grid=(HEADS, nq),
        in_specs=[tile,
                  pl.BlockSpec((l, HEAD_DIM), lambda h, i: (0, HEADS + h)),
                  pl.BlockSpec((l, HEAD_DIM), lambda h, i: (0, 2 * HEADS + h)),
                  tile, tile] + [hbm] * hosted,
        out_specs=[tile, hbm, hbm] + [hbm] * hosted,
        out_shape=[jax.ShapeDtypeStruct((l, HEADS * HEAD_DIM), BF16),
                   jax.ShapeDtypeStruct((HEADS, l, HEAD_DIM), F32), jax.ShapeDtypeStruct((HEADS, l, HEAD_DIM), F32)]
        + ([jax.ShapeDtypeStruct(scatter.shape, scatter.dtype)] if hosted else []),
        scratch_shapes=[pltpu.VMEM((l, HEAD_DIM), F32), pltpu.VMEM((l, HEAD_DIM), F32),
                        pltpu.VMEM((tq, HEAD_DIM), F32), pltpu.VMEM((tq, BLOCK), F32), pltpu.VMEM((tq, BLOCK), F32),
                        pltpu.SemaphoreType.DMA((2,))] + EXCHANGE_SEMS * hosted,
        compiler_params=_params(("arbitrary", "arbitrary")),
    )(qkv, qkv, qkv, att, datt, *([scatter] * hosted))


MIX_TM = 256


def _mix_branches(s_ref, p_ref, t_ref, g_ref, wa_ref, wb_ref, wc_ref, ba_ref, sc_ref, d):
    ya = jnp.dot(s_ref[...], wa_ref[...], preferred_element_type=F32) + ba_ref[...]
    yb0 = jnp.dot(p_ref[...], wb_ref[...], preferred_element_type=F32)
    yc = jnp.dot(t_ref[...], wc_ref[...], preferred_element_type=F32)
    g0 = _sigmoid(g_ref[:, 0:d].astype(F32))
    g1 = _sigmoid(g_ref[:, d:2 * d].astype(F32))
    g2 = _sigmoid(g_ref[:, 2 * d:3 * d].astype(F32))
    return ya, yb0, yc, g0, g1, g2


def _mix_specs(tm, d):
    row = lambda w: pl.BlockSpec((tm, w), lambda i: (i, 0))
    full = lambda r: pl.BlockSpec((r, d), lambda i: (0, 0))
    return [row(CONV_CH), row(POOL_CH), row(HEADS * HEAD_DIM), row(3 * d),
            full(CONV_CH), full(POOL_CH), full(HEADS * HEAD_DIM), full(1), full(1)]


def _mix_fwd(s, pooled, att, gates, wa, wb, wc, ba, scale, *, name):
    l, d = s.shape[0], wa.shape[1]
    tm = _tile(l, MIX_TM, 128)

    def body(s_ref, p_ref, t_ref, g_ref, wa_ref, wb_ref, wc_ref, ba_ref, sc_ref, o_ref):
        ya, yb0, yc, g0, g1, g2 = _mix_branches(s_ref, p_ref, t_ref, g_ref, wa_ref, wb_ref, wc_ref, ba_ref, sc_ref, d)
        o_ref[...] = (g0 * ya + g1 * (yb0 * sc_ref[...]) + g2 * yc).astype(BF16)

    return pl.pallas_call(
        body, name=name, grid=(l // tm,), in_specs=_mix_specs(tm, d),
        out_specs=pl.BlockSpec((tm, d), lambda i: (i, 0)),
        out_shape=jax.ShapeDtypeStruct((l, d), BF16),
        compiler_params=_params(("parallel",)),
    )(s, pooled, att, gates, wa, wb, wc, ba.reshape(1, d), scale.reshape(1, d))


def _mix_bwd(s, pooled, att, gates, wa, wb, wc, ba, scale, dmixed, *, name):
    l, d = s.shape[0], wa.shape[1]
    tm = _tile(l, MIX_TM, 128)

    def body(s_ref, p_ref, t_ref, g_ref, wa_ref, wb_ref, wc_ref, ba_ref, sc_ref, dm_ref,
             dg_ref, dya_ref, dyb_ref, dyc_ref, vec_ref):
        @pl.when(pl.program_id(0) == 0)
        def _():
            vec_ref[...] = jnp.zeros_like(vec_ref)

        ya, yb0, yc, g0, g1, g2 = _mix_branches(s_ref, p_ref, t_ref, g_ref, wa_ref, wb_ref, wc_ref, ba_ref, sc_ref, d)
        dm = dm_ref[...].astype(F32)
        sc = sc_ref[...]
        dg_ref[:, 0:d] = (dm * ya * g0 * (1.0 - g0)).astype(BF16)
        dg_ref[:, d:2 * d] = (dm * (yb0 * sc) * g1 * (1.0 - g1)).astype(BF16)
        dg_ref[:, 2 * d:3 * d] = (dm * yc * g2 * (1.0 - g2)).astype(BF16)
        dya = dm * g0
        dyb = dm * g1
        dya_ref[...] = dya.astype(BF16)
        dyb_ref[...] = (dyb * sc).astype(BF16)
        dyc_ref[...] = (dm * g2).astype(BF16)
        vec_ref[0:1, :] += jnp.sum(dya, axis=0, keepdims=True)
        vec_ref[1:2, :] += jnp.sum(dyb * yb0, axis=0, keepdims=True)

    row = lambda w: pl.BlockSpec((tm, w), lambda i: (i, 0))
    outs = pl.pallas_call(
        body, name=name, grid=(l // tm,), in_specs=_mix_specs(tm, d) + [row(d)],
        out_specs=[row(3 * d), row(d), row(d), row(d), pl.BlockSpec((8, d), lambda i: (0, 0))],
        out_shape=[jax.ShapeDtypeStruct((l, 3 * d), BF16), jax.ShapeDtypeStruct((l, d), BF16),
                   jax.ShapeDtypeStruct((l, d), BF16), jax.ShapeDtypeStruct((l, d), BF16),
                   jax.ShapeDtypeStruct((8, d), F32)],
        compiler_params=_params(("arbitrary",)),
    )(s, pooled, att, gates, wa, wb, wc, ba.reshape(1, d), scale.reshape(1, d), dmixed)
    return outs


FFN_TC = 512
_GELU_C = 0.7978845608028654
_GELU_A = 0.044715


def _gelu(x):
    th = jnp.tanh(_GELU_C * (x + _GELU_A * x * x * x))
    return 0.5 * x * (1.0 + th), th


def _gelu_grad(x, th):
    return 0.5 * (1.0 + th) + 0.5 * x * (1.0 - th * th) * _GELU_C * (1.0 + 3.0 * _GELU_A * x * x)


FFN_CH = 32


def _ffn_taps(win):
    return (pltpu.roll(win, 2, 0)[FFN_HALO:, :], pltpu.roll(win, 1, 0)[FFN_HALO:, :], win[FFN_HALO:, :])


def _ffn_conv(taps, w, b):
    return b + w[0:1, :] * taps[0] + w[1:2, :] * taps[1] + w[2:3, :] * taps[2]


def _fold8(x):
    acc = x[0:8, :]
    for r in range(8, x.shape[0], 8):
        acc = acc + x[r:r + 8, :]
    return acc


def _ffn_fwd(ug, uv, wg, wv, bg, bv, *, name):
    l, f = ug.shape
    tm = _conv_tile(l)
    tc = _tile(f, FFN_TC, 128)
    ext = FFN_HALO + tm

    def body(ug_ref, uv_ref, wg_ref, wv_ref, bg_ref, bv_ref, o_ref, gc_ref, vc_ref, bufg, bufv):
        i = pl.program_id(1)
        for buf, u_ref in ((bufg, ug_ref), (bufv, uv_ref)):
            @pl.when(i == 0)
            def _():
                buf[0:FFN_HALO, :] = jnp.zeros((FFN_HALO, tc), F32)

            @pl.when(i > 0)
            def _():
                buf[0:FFN_HALO, :] = buf[tm:ext, :]

            buf[FFN_HALO:ext, :] = u_ref[...].astype(F32)
        wg, wv, bg_, bv_ = wg_ref[...], wv_ref[...], bg_ref[...], bv_ref[...]

        def chunk(c, carry):
            r0 = pl.multiple_of(c * FFN_CH, FFN_CH)
            gc = _ffn_conv(_ffn_taps(bufg[pl.ds(r0, FFN_HALO + FFN_CH), :]), wg, bg_)
            vc = _ffn_conv(_ffn_taps(bufv[pl.ds(r0, FFN_HALO + FFN_CH), :]), wv, bv_)
            o_ref[pl.ds(r0, FFN_CH), :] = (_gelu(gc)[0] * vc).astype(BF16)
            gc_ref[pl.ds(r0, FFN_CH), :] = gc.astype(BF16)
            vc_ref[pl.ds(r0, FFN_CH), :] = vc.astype(BF16)
            return carry

        lax.fori_loop(0, tm // FFN_CH, chunk, 0)

    assert tm % FFN_CH == 0
    tile = pl.BlockSpec((tm, tc), lambda j, i: (i, j))
    wspec = pl.BlockSpec((FFN_K, tc), lambda j, i: (0, j))
    bspec = pl.BlockSpec((1, tc), lambda j, i: (0, j))
    return pl.pallas_call(
        body, name=name, grid=(f // tc, l // tm),
        in_specs=[tile, tile, wspec, wspec, bspec, bspec], out_specs=[tile, tile, tile],
        out_shape=[jax.ShapeDtypeStruct((l, f), BF16)] * 3,
        scratch_shapes=[pltpu.VMEM((ext, tc), F32), pltpu.VMEM((ext, tc), F32)],
        compiler_params=_params(("parallel", "arbitrary")),
    )(ug, uv, wg, wv, bg.reshape(1, f), bv.reshape(1, f))


def _ffn_bwd(ug, uv, gc, vc, wg, wv, dact, *, name):
    l, f = ug.shape
    tm = _conv_tile(l)
    tc = _tile(f, FFN_TC, 128)
    nt = l // tm
    ext = FFN_HALO + tm
    win_rows = FFN_CH + FFN_HALO

    def body(ug_ref, uv_ref, gc_ref, vc_ref, wg_ref, wv_ref, da_ref,
             dug_ref, duv_ref, gg_ref, gv_ref, dbufg, dbufv, gaccg, gaccv):
        i = pl.program_id(1)

        @pl.when(i == 0)
        def _():
            gg_ref[...] = jnp.zeros_like(gg_ref)
            gv_ref[...] = jnp.zeros_like(gv_ref)

        for dbuf, gacc in ((dbufg, gaccg), (dbufv, gaccv)):
            @pl.when(i == 0)
            def _():
                dbuf[tm:ext, :] = jnp.zeros((FFN_HALO, tc), F32)

            @pl.when(i > 0)
            def _():
                dbuf[tm:ext, :] = dbuf[0:FFN_HALO, :]

            gacc[...] = jnp.zeros_like(gacc)
        wg, wv = wg_ref[...], wv_ref[...]

        def chunk(cc, carry):
            r0 = pl.multiple_of((tm // FFN_CH - 1 - cc) * FFN_CH, FFN_CH)
            rows = pl.ds(r0, FFN_CH)
            gcv = gc_ref[rows, :].astype(F32)
            ge, th = _gelu(gcv)
            da = da_ref[rows, :].astype(F32)
            for dc, u_ref, w, dbuf, du_ref, gacc in (
                    (da * vc_ref[rows, :].astype(F32) * _gelu_grad(gcv, th), ug_ref, wg, dbufg, dug_ref, gaccg),
                    (da * ge, uv_ref, wv, dbufv, duv_ref, gaccv)):
                dbuf[rows, :] = dc
                dwin = dbuf[pl.ds(r0, win_rows), :]
                d1 = pltpu.roll(dwin, win_rows - 1, 0)[0:FFN_CH, :]
                d2 = pltpu.roll(dwin, win_rows - 2, 0)[0:FFN_CH, :]
                du_ref[rows, :] = (w[2:3, :] * dc + w[1:2, :] * d1 + w[0:1, :] * d2).astype(BF16)
                u = u_ref[rows, :].astype(F32)
                gacc[0:8, :] += _fold8(d2 * u)
                gacc[8:16, :] += _fold8(d1 * u)
                gacc[16:24, :] += _fold8(dc * u)
                gacc[24:32, :] += _fold8(dc)
            return carry

        lax.fori_loop(0, tm // FFN_CH, chunk, 0)
        for gacc, gp_ref in ((gaccg, gg_ref), (gaccv, gv_ref)):
            for k in range(FFN_K + 1):
                gp_ref[k:k + 1, :] += jnp.sum(gacc[8 * k:8 * k + 8, :], axis=0, keepdims=True)

    assert tm % FFN_CH == 0
    tile = pl.BlockSpec((tm, tc), lambda j, i: (nt - 1 - i, j))
    wspec = pl.BlockSpec((FFN_K, tc), lambda j, i: (0, j))
    gspec = pl.BlockSpec((8, tc), lambda j, i: (0, j))
    return pl.pallas_call(
        body, name=name, grid=(f // tc, nt),
        in_specs=[tile, tile, tile, tile, wspec, wspec, tile],
        out_specs=[tile, tile, gspec, gspec],
        out_shape=[jax.ShapeDtypeStruct((l, f), BF16), jax.ShapeDtypeStruct((l, f), BF16),
                   jax.ShapeDtypeStruct((8, f), F32), jax.ShapeDtypeStruct((8, f), F32)],
        scratch_shapes=[pltpu.VMEM((ext, tc), F32), pltpu.VMEM((ext, tc), F32),
                        pltpu.VMEM((32, tc), F32), pltpu.VMEM((32, tc), F32)],
        compiler_params=_params(("parallel", "arbitrary")),
    )(ug, uv, gc, vc, wg, wv, dact)


def _adamw(parts, w, m, v, *, name):
    r = w.shape[0]
    tr = _tile(r, PACK_ROWS, 16)
    c1 = 1.0 / (1.0 - ADAM_B1 ** ADAM_STEP)
    c2 = 1.0 / (1.0 - ADAM_B2 ** ADAM_STEP)

    def body(p_ref, w_ref, m_ref, v_ref, g_ref, d_ref, nm_ref, nv_ref):
        g = p_ref[0].astype(F32)
        for k in range(1, N_DEV):
            g = g + p_ref[k].astype(F32)
        nm = ADAM_B1 * m_ref[...] + (1.0 - ADAM_B1) * g
        nv = ADAM_B2 * v_ref[...] + (1.0 - ADAM_B2) * (g * g)
        g_ref[...] = g
        nm_ref[...] = nm
        nv_ref[...] = nv
        d_ref[...] = -ADAM_LR * ((nm * c1) / (jnp.sqrt(nv * c2) + ADAM_EPS) + ADAM_WD * w_ref[...])

    tile = pl.BlockSpec((tr, 128), lambda i: (i, 0))
    return pl.pallas_call(
        body, name=name, grid=(r // tr,),
        in_specs=[pl.BlockSpec((N_DEV, tr, 128), lambda i: (0, i, 0)), tile, tile, tile],
        out_specs=[tile, tile, tile, tile],
        out_shape=[jax.ShapeDtypeStruct((r, 128), F32)] * 4,
        compiler_params=_params(("parallel",)),
    )(parts, w, m, v)


def _place():
    return lax.axis_index("x"), lax.axis_index("y"), lax.axis_index("c")


EXCHANGE_SEMS = [pltpu.SemaphoreType.DMA((7,)), pltpu.SemaphoreType.DMA((7,)), pltpu.SemaphoreType.DMA]


class _Gather:
    def __init__(self, x_ref, out_ref, send_sems, recv_sems, local_sem):
        self.x_ref, self.out_ref, self.send_sems, self.recv_sems, self.local_sem = (
            x_ref, out_ref, send_sems, recv_sems, local_sem)

    def _parts(self):
        xx, yy, cc = _place()
        me, sibling = (xx, yy, cc), (xx, yy, 1 - cc)
        chips = [(1 - xx, yy), (xx, 1 - yy), (1 - xx, 1 - yy)]

        def slot(px, py, pc):
            return self.out_ref.at[4 * px + 2 * py + pc]

        def copy(k, block, to, src=None):
            return pltpu.make_async_remote_copy(
                src_ref=slot(*block) if src is None else src, dst_ref=slot(*block),
                send_sem=self.send_sems.at[k], recv_sem=self.recv_sems.at[k], device_id=to, device_id_type=MESH)

        mine = pltpu.make_async_copy(self.x_ref, slot(*me), self.local_sem)
        first = [copy(0, me, sibling, src=self.x_ref)]
        first += [copy(1 + j, me, (*chip, cc), src=self.x_ref) for j, chip in enumerate(chips)]
        return cc, me, sibling, chips, copy, mine, first

    def start(self):
        _, _, _, _, _, mine, first = self._parts()
        mine.start()
        for cp in first:
            cp.start()

    def finish(self):
        cc, me, sibling, chips, copy, mine, first = self._parts()
        passed = [copy(4 + j, (*chip, cc), sibling) for j, chip in enumerate(chips)]
        for j, chip in enumerate(chips):
            copy(1 + j, (*chip, cc), me).wait_recv()
            passed[j].start()
        copy(0, sibling, me).wait_recv()
        for j, chip in enumerate(chips):
            copy(4 + j, (*chip, 1 - cc), me).wait_recv()
        for cp in first + passed:
            cp.wait_send()
        mine.wait()


class _Scatter:
    def __init__(self, s_ref, r_ref, send_sems, recv_sems, local_sem):
        self.s_ref, self.r_ref, self.send_sems, self.recv_sems, self.local_sem = (
            s_ref, r_ref, send_sems, recv_sems, local_sem)

    def _parts(self):
        xx, yy, cc = _place()
        me = 4 * xx + 2 * yy + cc
        local = pltpu.make_async_copy(self.s_ref.at[me], self.r_ref.at[me], self.local_sem)
        copies = []
        for m in range(1, N_DEV):
            px = 1 - xx if m & 4 else xx
            py = 1 - yy if m & 2 else yy
            pc = 1 - cc if m & 1 else cc
            copies.append(pltpu.make_async_remote_copy(
                src_ref=self.s_ref.at[4 * px + 2 * py + pc], dst_ref=self.r_ref.at[me],
                send_sem=self.send_sems.at[m - 1], recv_sem=self.recv_sems.at[m - 1],
                device_id=(px, py, pc), device_id_type=MESH))
        return local, copies

    def start(self):
        local, copies = self._parts()
        local.start()
        for cp in copies:
            cp.start()

    def finish(self):
        local, copies = self._parts()
        for cp in copies:
            cp.wait_recv()
        for cp in copies:
            cp.wait_send()
        local.wait()


def _exchange_call(kind, x, out_shape, *, name):
    def body(x_ref, out_ref, send_sems, recv_sems, local_sem):
        ex = kind(x_ref, out_ref, send_sems, recv_sems, local_sem)
        ex.start()
        ex.finish()

    return pl.pallas_call(
        body, name=name,
        in_specs=[pl.BlockSpec(memory_space=pl.ANY)], out_specs=pl.BlockSpec(memory_space=pl.ANY),
        out_shape=jax.ShapeDtypeStruct(out_shape, x.dtype), scratch_shapes=EXCHANGE_SEMS,
    )(x)


def _all_gather(x, *, name):
    return _exchange_call(_Gather, x, (N_DEV,) + x.shape, name=name)


def _all_to_all(send, *, name):
    return _exchange_call(_Scatter, send, send.shape, name=name)


def _as_rows(a, lead, dtype):
    a = a.astype(dtype)
    size = 1
    for s in a.shape[len(lead):]:
        size *= s
    if size % PACK_ALIGN:
        a = jnp.pad(a.reshape(lead + (size,)), [(0, 0)] * len(lead) + [(0, (-size) % PACK_ALIGN)])
    return a.reshape(lead + (-1, 128))


def _pack(arrays, dtype):
    buf = jnp.concatenate([_as_rows(a, (), dtype) for a in arrays], axis=0)
    return jnp.pad(buf, ((0, (-buf.shape[0]) % PACK_ROWS), (0, 0)))


def _pack_pieces(arrays, dtype):
    buf = jnp.concatenate([_as_rows(a, (N_DEV,), dtype) for a in arrays], axis=1)
    return jnp.pad(buf, ((0, 0), (0, (-buf.shape[1]) % PACK_ROWS), (0, 0)))


def _unpack(buf, shapes, lead=()):
    out, row = [], 0
    for shp in shapes:
        size = 1
        for s in shp:
            size *= s
        rows = (size + (-size) % PACK_ALIGN) // 128
        part = buf[..., row:row + rows, :]
        if size % PACK_ALIGN:
            part = part.reshape(lead + (rows * 128,))[..., :size]
        out.append(part.reshape(lead + tuple(shp)))
        row += rows
    return out


def _unshard(g, axis):
    g = jnp.moveaxis(g, 0, axis)
    shp = list(g.shape)
    return g.reshape(shp[:axis] + [shp[axis] * shp[axis + 1]] + shp[axis + 2:])


def _pieces(full, axis):
    shp = list(full.shape)
    g = full.reshape(shp[:axis] + [N_DEV, shp[axis] // N_DEV] + shp[axis + 1:])
    return jnp.moveaxis(g, axis, 0)


SHARDED = (("meta", 1), ("w_in", 2), ("conv_dw_w", 2), ("w_conv_out", 2), ("w_pool_grp", 3), ("w_attn_out", 2),
           ("w_o", 1), ("w_up", 2), ("ffn_dw_w", 2), ("w_down", 1))
MATRICES = ("w_in", "w_conv_out", "w_pool_grp", "w_attn_out", "w_o", "w_up", "w_down")
REPLICATED = ("norm1", "conv_dw_b", "conv_ln_g", "conv_ln_b", "b_conv_out", "pool_scale", "norm2", "ffn_dw_b",
              "final_norm")
WEIGHTS = ("meta", "norm1", "w_in", "conv_dw_w", "conv_dw_b", "conv_ln_g", "conv_ln_b", "w_conv_out", "b_conv_out",
           "w_pool_grp", "pool_scale", "w_attn_out", "w_o", "norm2", "w_up", "ffn_dw_w", "ffn_dw_b", "w_down",
           "final_norm")


def _block_diag(w_grp):
    g, gc, od = w_grp.shape
    out = jnp.zeros((g * gc, g * od), w_grp.dtype)
    for i in range(g):
        out = out.at[i * gc:(i + 1) * gc, i * od:(i + 1) * od].set(w_grp[i])
    return out


def _block_diag_grad(gw, g):
    gc, od = gw.shape[0] // g, gw.shape[1] // g
    return jnp.stack([gw[i * gc:(i + 1) * gc, i * od:(i + 1) * od] for i in range(g)])


C_CONV = 2 * CONV_CH
C_POOL = C_CONV + POOL_CH
C_ATT = HEADS * HEAD_DIM
C_QKV = C_POOL + 3 * C_ATT


def _layer_fwd(h, p, tag, gather=None):
    d = h.shape[1]
    w_in = p["w_in"]
    hn = _rms_fwd(h, p["norm1"], name=f"rms1_{tag}")
    pa = _mm(hn, w_in[:, :C_POOL], out_dtype=F32, name=f"proj_a_{tag}")
    q_scale = jnp.concatenate([jnp.full((C_ATT,), Q_SCALE, F32), jnp.ones((2 * C_ATT,), F32)])
    qkv = _mm(hn, w_in[:, C_POOL:C_QKV], out_dtype=BF16, col_scale=q_scale, name=f"proj_qkv_{tag}")
    gates = _mm(hn, w_in[:, C_QKV:], out_dtype=BF16, name=f"proj_g_{tag}")
    s = _conv_fwd(pa, p["conv_dw_w"], p["conv_dw_b"], p["conv_ln_g"], p["conv_ln_b"], name=f"conv_{tag}")
    pooled = _pool_fwd(pa, name=f"pool_{tag}")
    att, att32, *got = _attn_fwd(qkv, name=f"attn_{tag}", gather=gather)
    wb = _block_diag(p["w_pool_grp"])
    mixed = _mix_fwd(s, pooled, att, gates, p["w_conv_out"], wb, p["w_attn_out"], p["b_conv_out"], p["pool_scale"],
                     name=f"mix_{tag}")
    h1 = _mm(mixed, p["w_o"], out_dtype=F32, res=h, mask_rows=True, name=f"wo_{tag}")
    hn2 = _rms_fwd(h1, p["norm2"], name=f"rms2_{tag}")
    f = p["w_up"].shape[1] // 2
    ug = _mm(hn2, p["w_up"][:, :f], out_dtype=BF16, name=f"up_g_{tag}")
    uv = _mm(hn2, p["w_up"][:, f:], out_dtype=BF16, name=f"up_v_{tag}")
    act, gc, vc = _ffn_fwd(ug, uv, p["ffn_dw_w"][:, :f], p["ffn_dw_w"][:, f:], p["ffn_dw_b"][:f], p["ffn_dw_b"][f:],
                           name=f"ffn_{tag}")
    h2 = _mm(act, p["w_down"], out_dtype=F32, res=h1, mask_rows=True, name=f"down_{tag}")
    saved = dict(h=h, hn=hn, pa=pa, qkv=qkv, gates=gates, s=s, pooled=pooled, att=att, att32=att32, wb=wb, mixed=mixed,
                 h1=h1,
                 hn2=hn2, ug=ug, uv=uv, gc=gc, vc=vc, act=act)
    return h2, saved, (got[0] if got else None)


def _layer_bwd(dh2, dh2b, p, sv, tag, scatter=None):
    g = {}
    f = p["w_up"].shape[1] // 2
    dact = _mm(dh2b, p["w_down"].T, out_dtype=BF16, name=f"b_down_{tag}")
    g["w_down"] = _mm_tn(sv["act"], dh2b, name=f"g_down_{tag}")
    dug, duv, gpg, gpv = _ffn_bwd(sv["ug"], sv["uv"], sv["gc"], sv["vc"], p["ffn_dw_w"][:, :f], p["ffn_dw_w"][:, f:],
                                  dact, name=f"b_ffn_{tag}")
    g["ffn_dw_w"] = jnp.concatenate([gpg[0:FFN_K], gpv[0:FFN_K]], axis=1)
    g["ffn_dw_b"] = jnp.concatenate([gpg[FFN_K], gpv[FFN_K]])
    w_up_t = p["w_up"].T
    dhn2 = _mm(dug, w_up_t[:f], out_dtype=F32, name=f"b_up_g_{tag}")
    dhn2 = _mm(duv, w_up_t[f:], out_dtype=BF16, res=dhn2, name=f"b_up_v_{tag}")
    g["w_up"] = jnp.concatenate([_mm_tn(sv["hn2"], dug, name=f"g_up_g_{tag}"),
                                 _mm_tn(sv["hn2"], duv, name=f"g_up_v_{tag}")], axis=1)
    dh1, dh1b, g["norm2"] = _rms_bwd(sv["h1"], p["norm2"], dhn2, dh2, name=f"b_rms2_{tag}")
    dmixed = _mm(dh1b, p["w_o"].T, out_dtype=BF16, name=f"b_wo_{tag}")
    g["w_o"] = _mm_tn(sv["mixed"], dh1b, name=f"g_wo_{tag}")
    dgates, dya, dyb, dyc, vec = _mix_bwd(sv["s"], sv["pooled"], sv["att"], sv["gates"], p["w_conv_out"], sv["wb"],
                                          p["w_attn_out"], p["b_conv_out"], p["pool_scale"], dmixed,
                                          name=f"b_mix_{tag}")
    g["b_conv_out"], g["pool_scale"] = vec[0], vec[1]
    ds = _mm(dya, p["w_conv_out"].T, out_dtype=F32, name=f"b_conv_out_{tag}")
    dpooled = _mm(dyb, sv["wb"].T, out_dtype=F32, name=f"b_pool_out_{tag}")
    datt = _mm(dyc, p["w_attn_out"].T, out_dtype=BF16, name=f"b_attn_out_{tag}")
    g["w_conv_out"] = _mm_tn(sv["s"], dya, name=f"g_conv_out_{tag}")
    g["w_pool_grp"] = _block_diag_grad(_mm_tn(sv["pooled"], dyb, name=f"g_pool_{tag}"), len(POOL_WINDOWS))
    g["w_attn_out"] = _mm_tn(sv["att"], dyc, name=f"g_attn_out_{tag}")
    dc, gp = _conv_bwd_ln(sv["pa"], ds, p["conv_dw_w"], p["conv_dw_b"], p["conv_ln_g"], p["conv_ln_b"],
                          name=f"b_conv_ln_{tag}")
    g["conv_dw_w"], g["conv_dw_b"], g["conv_ln_g"], g["conv_ln_b"] = gp[0:CONV_K], gp[32], gp[33], gp[34]
    dconv = _conv_bwd_in(sv["pa"], dc, p["conv_dw_w"], name=f"b_conv_in_{tag}")
    dp = _pool_bwd(dpooled, name=f"b_pool_{tag}")
    dq, dk, dv, *recv = _attn_bwd(sv["qkv"], sv["att32"], datt, name=f"b_attn_{tag}", scatter=scatter)
    dk = jnp.moveaxis(dk, 0, 1).reshape(dq.shape).astype(BF16)
    dv = jnp.moveaxis(dv, 0, 1).reshape(dq.shape).astype(BF16)
    w_in_t = p["w_in"].T
    cols = [(jnp.concatenate([dconv, dp, dq, dk, dv], axis=1), 0, C_QKV), (dgates, C_QKV, w_in_t.shape[0])]
    dhn, gw = None, []
    for n, (dcol, lo, hi) in enumerate(cols):
        dhn = _mm(dcol, w_in_t[lo:hi], out_dtype=BF16 if n + 1 == len(cols) else F32, res=dhn, name=f"b_in{n}_{tag}")
        gw.append(_mm_tn(sv["hn"], dcol, name=f"g_in{n}_{tag}"))
    g["w_in"] = jnp.concatenate(gw, axis=1)
    dh, dhb, g["norm1"] = _rms_bwd(sv["h"], p["norm1"], dhn, dh1, name=f"b_rms1_{tag}")
    return dh, dhb, g, (recv[0] if recv else None)


def kernel(x, meta, norm1, w_in, conv_dw_w, conv_dw_b, conv_ln_g, conv_ln_b, w_conv_out, b_conv_out, w_pool_grp, pool_scale, w_attn_out, w_o, norm2, w_up, ffn_dw_w, ffn_dw_b, w_down, final_norm, loss_target, m_meta, m_norm1, m_w_in, m_conv_dw_w, m_conv_dw_b, m_conv_ln_g, m_conv_ln_b, m_w_conv_out, m_b_conv_out, m_w_pool_grp, m_pool_scale, m_w_attn_out, m_w_o, m_norm2, m_w_up, m_ffn_dw_w, m_ffn_dw_b, m_w_down, m_final_norm, v_meta, v_norm1, v_w_in, v_conv_dw_w, v_conv_dw_b, v_conv_ln_g, v_conv_ln_b, v_w_conv_out, v_b_conv_out, v_w_pool_grp, v_pool_scale, v_w_attn_out, v_w_o, v_norm2, v_w_up, v_ffn_dw_w, v_ffn_dw_b, v_w_down, v_final_norm):
    given = dict(meta=meta, norm1=norm1, w_in=w_in, conv_dw_w=conv_dw_w, conv_dw_b=conv_dw_b, conv_ln_g=conv_ln_g, conv_ln_b=conv_ln_b, w_conv_out=w_conv_out, b_conv_out=b_conv_out, w_pool_grp=w_pool_grp, pool_scale=pool_scale, w_attn_out=w_attn_out, w_o=w_o, norm2=norm2, w_up=w_up, ffn_dw_w=ffn_dw_w, ffn_dw_b=ffn_dw_b, w_down=w_down, final_norm=final_norm)
    mom_m = dict(meta=m_meta, norm1=m_norm1, w_in=m_w_in, conv_dw_w=m_conv_dw_w, conv_dw_b=m_conv_dw_b, conv_ln_g=m_conv_ln_g, conv_ln_b=m_conv_ln_b, w_conv_out=m_w_conv_out, b_conv_out=m_b_conv_out, w_pool_grp=m_w_pool_grp, pool_scale=m_pool_scale, w_attn_out=m_w_attn_out, w_o=m_w_o, norm2=m_norm2, w_up=m_w_up, ffn_dw_w=m_ffn_dw_w, ffn_dw_b=m_ffn_dw_b, w_down=m_w_down, final_norm=m_final_norm)
    mom_v = dict(meta=v_meta, norm1=v_norm1, w_in=v_w_in, conv_dw_w=v_conv_dw_w, conv_dw_b=v_conv_dw_b, conv_ln_g=v_conv_ln_g, conv_ln_b=v_conv_ln_b, w_conv_out=v_w_conv_out, b_conv_out=v_b_conv_out, w_pool_grp=v_w_pool_grp, pool_scale=v_pool_scale, w_attn_out=v_w_attn_out, w_o=v_w_o, norm2=v_norm2, w_up=v_w_up, ffn_dw_w=v_ffn_dw_w, ffn_dw_b=v_ffn_dw_b, w_down=v_w_down, final_norm=v_final_norm)
    sharded_axis = dict(SHARDED)
    vectors = [n for n, _ in SHARDED if n not in MATRICES]
    depth = norm1.shape[0]

    got_vec = _all_gather(_pack([given[n] for n in vectors], F32), name="gather_vectors")
    full = {n: given[n] for n in REPLICATED}
    for n, a in zip(vectors, _unpack(got_vec, [given[n].shape for n in vectors], (N_DEV,))):
        full[n] = _unshard(a, sharded_axis[n])
    mat_shapes = [given[n].shape[1:] for n in MATRICES]

    def layer_matrices(i):
        return _pack([given[n][i] for n in MATRICES], BF16)

    got_mat = _all_gather(layer_matrices(0), name="gather_matrices_l0")

    xs = x[0]
    d = xs.shape[1]
    h = jnp.concatenate([jnp.zeros((PAD, d), F32), full["meta"], xs], axis=0)
    layers, saved = [], []
    for i in range(depth):
        p = {n: full[n][i] for n in full if n not in ("meta", "final_norm")}
        for n, a in zip(MATRICES, _unpack(got_mat, mat_shapes, (N_DEV,))):
            p[n] = _unshard(a, sharded_axis[n] - 1)
        layers.append(p)
        h, sv, got_mat = _layer_fwd(h, p, f"l{i}", gather=layer_matrices(i + 1) if i + 1 < depth else None)
        saved.append(sv)
    loss_part, dh, dhb, g_final = _loss_head(h, full["final_norm"], loss_target[0], name="loss_head")

    names = [n for n, _ in SHARDED if n != "meta"]

    def layer_pieces(g, extra=()):
        return _pack_pieces([_pieces(g[n], sharded_axis[n] - 1) for n in names] + list(extra), BF16)

    grads, recvs = [None] * depth, [None] * depth
    for i in reversed(range(depth)):
        send = layer_pieces(grads[i + 1]) if i + 1 < depth else None
        dh, dhb, grads[i], got = _layer_bwd(dh, dhb, layers[i], saved[i], f"l{i}", scatter=send)
        if send is not None:
            recvs[i + 1] = got
    grad_x = dh[FRONT:][None]
    recvs[0] = _all_to_all(layer_pieces(grads[0], [_pieces(dh[PAD:FRONT], sharded_axis["meta"])]), name="scatter_grads_l0")
    full_grad = {n: jnp.stack([grads[i][n] for i in range(depth)]) for n in REPLICATED if n != "final_norm"}
    full_grad["final_norm"] = g_final
    rep_shapes = [given[n].shape for n in REPLICATED] + [(1,)]
    rep_parts = _all_gather(_pack([full_grad[n] for n in REPLICATED] + [loss_part.reshape(1)], F32),
                            name="gather_partials")

    out, per_layer = {}, []
    for i in range(depth):
        extra = ["meta"] if i == 0 else []
        pick = lambda src: [src[n][i] for n in names] + [src[n] for n in extra]
        res = _adamw(recvs[i], _pack(pick(given), F32), _pack(pick(mom_m), F32), _pack(pick(mom_v), F32),
                     name=f"adamw_l{i}")
        shapes = [given[n].shape[1:] for n in names] + [given[n].shape for n in extra]
        per_layer.append([_unpack(buf, shapes) for buf in res])
    for k, kind in enumerate(("grad", "delta", "new_m", "new_v")):
        for j, n in enumerate(names):
            out[kind, n] = jnp.stack([per_layer[i][k][j] for i in range(depth)])
        out[kind, "meta"] = per_layer[0][k][len(names)]
    rep_w = [given[n] for n in REPLICATED] + [jnp.zeros((1,), F32)]
    rep_m = [mom_m[n] for n in REPLICATED] + [jnp.zeros((1,), F32)]
    rep_v = [mom_v[n] for n in REPLICATED] + [jnp.ones((1,), F32)]
    res = _adamw(rep_parts, _pack(rep_w, F32), _pack(rep_m, F32), _pack(rep_v, F32), name="adamw_replicated")
    for kind, buf in zip(("grad", "delta", "new_m", "new_v"), res):
        for n, a in zip(list(REPLICATED) + ["loss"], _unpack(buf, rep_shapes)):
            out[kind, n] = a
    loss = out["grad", "loss"][0]
    return (loss, grad_x, *[out["grad", n] for n in WEIGHTS], *[out["delta", n] for n in WEIGHTS],
            *[out["new_m", n] for n in WEIGHTS], *[out["new_v", n] for n in WEIGHTS])
```
